```python
import jax, jax.numpy as jnp
from jax import lax
import numpy as np

D_MODEL = 1024
BATCH = 8
SEQ = 2048
DEPTH = 1
DEC_BATCH = 128
DEC_SEQ = 4
PAST_LEN = 16384
PAGE_SIZE = 128

HGRN_HEADS = 8
HGRN_DK = 128
HGRN_DV = 128
HGRN_FDIM = HGRN_HEADS * HGRN_DK
HGRN_WIDTH = HGRN_HEADS * HGRN_DV
MLSTM_HEADS = 4
MLSTM_DH = 256
MLSTM_WIDTH = MLSTM_HEADS * MLSTM_DH
D_MIX = HGRN_WIDTH + MLSTM_WIDTH
CONV_WIDTH = 4
QKV_BLOCK = 4
CHUNK = 64
EPS = 1e-6
IN_SIZES = (HGRN_FDIM, HGRN_FDIM, HGRN_WIDTH, HGRN_WIDTH, MLSTM_WIDTH, MLSTM_WIDTH, MLSTM_WIDTH, MLSTM_HEADS, MLSTM_HEADS)
W_IN_COLS = sum(IN_SIZES)
SPLIT_IDX = tuple(int(s) for s in np.cumsum(IN_SIZES)[:-1])
FGATE_OFFSET = W_IN_COLS - MLSTM_HEADS

kernel_name = "hymba_hgrn2_mlstm_adaln_step"

F32 = jnp.float32


def _rms(x, g):
    xf = x.astype(F32)
    y = xf * lax.rsqrt(jnp.mean(xf * xf, -1, keepdims=True) + EPS)
    return (y * g.astype(F32)).astype(x.dtype)


def _head_rms(h, g):
    B, T, H, D = h.shape
    y = h * lax.rsqrt(jnp.mean(h * h, -1, keepdims=True) + EPS)
    return y.reshape(B, T, H * D) * g.astype(F32)


def _to_chunks(a, chunk):
    B, T = a.shape[:2]
    return a.reshape((B, T // chunk, chunk) + a.shape[2:]).swapaxes(0, 1)


def _from_chunks(a):
    n, B, L = a.shape[:3]
    return a.swapaxes(0, 1).reshape((B, n * L) + a.shape[3:])


def _hgrn2(q, logf, v, S0, chunk):
    k = -jnp.expm1(logf)
    L = chunk
    causal = jnp.tril(jnp.ones((L, L), bool))

    def step(S, inp):
        qc, lfc, kc, vc = inp
        G = jnp.cumsum(lfc, axis=1)
        diff = G[:, :, None] - G[:, None, :]
        decay = jnp.where(causal[None, :, :, None, None], jnp.exp(jnp.minimum(diff, 0.0)), 0.0)
        A = jnp.einsum('bthk,btjhk,bjhk->bhtj', qc, decay, kc)
        o_intra = jnp.einsum('bhtj,bjhv->bthv', A, vc)
        o_inter = jnp.einsum('bthk,bhkv->bthv', qc * jnp.exp(G), S)
        GL = G[:, -1]
        kdec = kc * jnp.exp(GL[:, None] - G)
        S_new = jnp.exp(GL)[..., None] * S + jnp.einsum('bjhk,bjhv->bhkv', kdec, vc)
        return S_new, o_intra + o_inter

    S_fin, o = lax.scan(step, S0, (_to_chunks(q, L), _to_chunks(logf, L), _to_chunks(k, L), _to_chunks(v, L)))
    return _from_chunks(o), S_fin


def _mlstm(q, k, v, ig, logf, C0, n0, m0, chunk):
    L = chunk
    causal = jnp.tril(jnp.ones((L, L), bool))

    def step(carry, inp):
        C, nv, m = carry
        qc, kc, vc, ic, fc = inp
        b = jnp.cumsum(fc, axis=1)
        logD = b[:, :, None, :] - b[:, None, :, :] + ic[:, None, :, :]
        logD = jnp.where(causal[None, :, :, None], logD, -jnp.inf)
        m_inter = b + m[:, None, :]
        m_t = jnp.maximum(m_inter, jnp.max(logD, axis=2))
        Dm = jnp.exp(logD - m_t[:, :, None, :])
        inter = jnp.exp(m_inter - m_t)
        s = jnp.einsum('bthd,bjhd->btjh', qc, kc) * Dm
        num = inter[..., None] * jnp.einsum('bthk,bhkv->bthv', qc, C) + jnp.einsum('btjh,bjhv->bthv', s, vc)
        den = inter * jnp.einsum('bthk,bhk->bth', qc, nv) + jnp.sum(s, axis=2)
        h = num / jnp.maximum(jnp.abs(den), jnp.exp(-m_t))[..., None]
        m_new = m_t[:, -1]
        wC = jnp.exp(b[:, -1:] - b + ic - m_new[:, None])
        dec = jnp.exp(b[:, -1] + m - m_new)
        C_new = dec[..., None, None] * C + jnp.einsum('bjh,bjhk,bjhv->bhkv', wC, kc, vc)
        n_new = dec[..., None] * nv + jnp.einsum('bjh,bjhk->bhk', wC, kc)
        return (C_new, n_new, m_new), h

    (C, nv, m), h = lax.scan(step, (C0, n0, m0), (_to_chunks(q, L), _to_chunks(k, L), _to_chunks(v, L), _to_chunks(ig, L), _to_chunks(logf, L)))
    return _from_chunks(h), C, nv, m


def _blockdiag(x, w):
    B, T, W = x.shape
    xb = x.reshape(B, T, W // QKV_BLOCK, QKV_BLOCK)
    return jnp.einsum('btgi,gij->btgj', xb, w).reshape(B, T, W)


def _layer(x, c, S_h, C_m, n_m, m_m, buf, chunk, lb, w_ada, b_ada, norm_g, w_in, b_in, hgrn_norm_g,
           conv_w, conv_b, wq, wk, wv, mnorm_g, skip, w_out):
    B, T, _ = x.shape
    mod = jnp.dot(jax.nn.silu(c), w_ada) + b_ada
    shift, scale, gate = jnp.split(mod, 3, axis=-1)
    h = _rms(x, norm_g) * (1 + scale[:, None]) + shift[:, None]
    proj = jnp.dot(h, w_in) + b_in
    hq, hf, hi, hz, mu, mz, mo, mi, mf = jnp.split(proj, SPLIT_IDX, axis=-1)

    f = lb + (1.0 - lb) * jax.nn.sigmoid(hf.astype(F32))
    logf = jnp.log(f).reshape(B, T, HGRN_HEADS, HGRN_DK)
    qh = hq.astype(F32).reshape(B, T, HGRN_HEADS, HGRN_DK)
    vh = hi.astype(F32).reshape(B, T, HGRN_HEADS, HGRN_DV)
    o_h, S_new = _hgrn2(qh, logf, vh, S_h.astype(F32), chunk)
    y_h = _head_rms(o_h, hgrn_norm_g) * jax.nn.silu(hz.astype(F32))

    ext = jnp.concatenate([buf.astype(mu.dtype), mu], axis=1)
    new_buf = ext[:, -(CONV_WIDTH - 1):]
    conv = lax.conv_general_dilated(ext, conv_w[:, None, :].astype(mu.dtype), (1,), 'VALID',
                                    dimension_numbers=('NWC', 'WIO', 'NWC'),
                                    feature_group_count=MLSTM_WIDTH) + conv_b
    xc = jax.nn.silu(conv)
    q = _blockdiag(xc, wq).astype(F32).reshape(B, T, MLSTM_HEADS, MLSTM_DH)
    k = (_blockdiag(xc, wk).astype(F32) * (MLSTM_DH ** -0.5)).reshape(B, T, MLSTM_HEADS, MLSTM_DH)
    v = _blockdiag(mu, wv).astype(F32).reshape(B, T, MLSTM_HEADS, MLSTM_DH)
    ig = mi.astype(F32)
    lfm = jax.nn.log_sigmoid(mf.astype(F32))
    h_m, C_new, n_new, m_new = _mlstm(q, k, v, ig, lfm, C_m.astype(F32), n_m.astype(F32), m_m.astype(F32), chunk)
    h_m = jax.nn.sigmoid(mo.astype(F32)).reshape(B, T, MLSTM_HEADS, MLSTM_DH) * h_m
    y_m = (_head_rms(h_m, mnorm_g) + skip.astype(F32) * xc.astype(F32)) * jax.nn.silu(mz.astype(F32))

    mix = jnp.concatenate([y_h, y_m], axis=-1).astype(x.dtype)
    out = x + gate[:, None] * jnp.dot(mix, w_out)
    return out, (S_new, C_new, n_new, m_new, new_buf)


def setup_inputs(seed: int = 0) -> dict:
    key = jax.random.key(seed)
    ks = jax.random.split(key, 32)
    nrm = jax.random.normal
    d = {}
    d['x_prompt'] = nrm(ks[0], (BATCH, SEQ, D_MODEL), F32)
    d['x_sample'] = nrm(ks[1], (DEC_BATCH, DEC_SEQ, D_MODEL), F32)
    d['c_prompt'] = nrm(ks[2], (BATCH, D_MODEL), F32)
    d['c_sample'] = nrm(ks[3], (DEC_BATCH, D_MODEL), F32)
    d['state_hgrn'] = 0.5 * nrm(ks[4], (DEPTH, DEC_BATCH, HGRN_HEADS, HGRN_DK, HGRN_DV), F32)
    d['state_mlstm_C'] = 0.1 * nrm(ks[5], (DEPTH, DEC_BATCH, MLSTM_HEADS, MLSTM_DH, MLSTM_DH), F32)
    d['state_mlstm_n'] = 0.5 * nrm(ks[6], (DEPTH, DEC_BATCH, MLSTM_HEADS, MLSTM_DH), F32)
    d['state_mlstm_m'] = 0.5 * nrm(ks[7], (DEPTH, DEC_BATCH, MLSTM_HEADS), F32)
    d['state_mlstm_conv'] = nrm(ks[8], (DEPTH, DEC_BATCH, CONV_WIDTH - 1, MLSTM_WIDTH), F32)
    d['w_ada'] = 0.5 * (D_MODEL ** -0.5) * nrm(ks[9], (DEPTH, D_MODEL, 3 * D_MODEL), F32)
    d['b_ada'] = 0.01 * nrm(ks[10], (DEPTH, 3 * D_MODEL), F32)
    d['norm_g'] = 1.0 + 0.01 * nrm(ks[11], (DEPTH, D_MODEL), F32)
    d['w_in'] = (D_MODEL ** -0.5) * nrm(ks[12], (DEPTH, D_MODEL, W_IN_COLS), F32)
    b_in = 0.01 * nrm(ks[13], (DEPTH, W_IN_COLS), F32)
    d['b_in'] = b_in.at[:, FGATE_OFFSET:].add(jnp.linspace(3.0, 6.0, MLSTM_HEADS, dtype=F32))
    d['hgrn_lb_logits'] = 0.1 * nrm(ks[14], (DEPTH + 1, HGRN_FDIM), F32)
    d['hgrn_norm_g'] = 1.0 + 0.01 * nrm(ks[15], (DEPTH, HGRN_WIDTH), F32)
    d['mlstm_conv_w'] = (CONV_WIDTH ** -0.5) * nrm(ks[16], (DEPTH, CONV_WIDTH, MLSTM_WIDTH), F32)
    d['mlstm_conv_b'] = 0.01 * nrm(ks[17], (DEPTH, MLSTM_WIDTH), F32)
    nb = MLSTM_WIDTH // QKV_BLOCK
    d['mlstm_wq'] = (QKV_BLOCK ** -0.5) * nrm(ks[18], (DEPTH, nb, QKV_BLOCK, QKV_BLOCK), F32)
    d['mlstm_wk'] = (QKV_BLOCK ** -0.5) * nrm(ks[19], (DEPTH, nb, QKV_BLOCK, QKV_BLOCK), F32)
    d['mlstm_wv'] = (QKV_BLOCK ** -0.5) * nrm(ks[20], (DEPTH, nb, QKV_BLOCK, QKV_BLOCK), F32)
    d['mlstm_norm_g'] = 1.0 + 0.01 * nrm(ks[21], (DEPTH, MLSTM_WIDTH), F32)
    d['mlstm_skip'] = 1.0 + 0.01 * nrm(ks[22], (DEPTH, MLSTM_WIDTH), F32)
    d['w_out'] = (D_MIX ** -0.5) * nrm(ks[23], (DEPTH, D_MIX, D_MODEL), F32)
    d['final_g'] = 1.0 + 0.01 * nrm(ks[24], (D_MODEL,), F32)
    return d


def reference(x_prompt, x_sample, c_prompt, c_sample, state_hgrn, state_mlstm_C, state_mlstm_n,
              state_mlstm_m, state_mlstm_conv, w_ada, b_ada, norm_g, w_in, b_in, hgrn_lb_logits,
              hgrn_norm_g, mlstm_conv_w, mlstm_conv_b, mlstm_wq, mlstm_wk, mlstm_wv, mlstm_norm_g,
              mlstm_skip, w_out, final_g):
    lb_all = jnp.cumsum(jax.nn.softmax(hgrn_lb_logits.astype(F32), axis=0), axis=0)
    xp, xs = x_prompt, x_sample
    Bp, Tp = xp.shape[0], xp.shape[1]
    Ts = xs.shape[1]
    chunk_p = CHUNK if Tp % CHUNK == 0 else Tp
    sp_all = [[], [], [], [], []]
    ss_all = [[], [], [], [], []]
    for l in range(DEPTH):
        wts = (lb_all[l], w_ada[l], b_ada[l], norm_g[l], w_in[l], b_in[l], hgrn_norm_g[l],
               mlstm_conv_w[l], mlstm_conv_b[l], mlstm_wq[l], mlstm_wk[l], mlstm_wv[l],
               mlstm_norm_g[l], mlstm_skip[l], w_out[l])
        zS = jnp.zeros((Bp, HGRN_HEADS, HGRN_DK, HGRN_DV), F32)
        zC = jnp.zeros((Bp, MLSTM_HEADS, MLSTM_DH, MLSTM_DH), F32)
        zn = jnp.zeros((Bp, MLSTM_HEADS, MLSTM_DH), F32)
        zm = jnp.zeros((Bp, MLSTM_HEADS), F32)
        zb = jnp.zeros((Bp, CONV_WIDTH - 1, MLSTM_WIDTH), xp.dtype)
        xp, sp = _layer(xp, c_prompt, zS, zC, zn, zm, zb, chunk_p, *wts)
        xs, ss = _layer(xs, c_sample, state_hgrn[l], state_mlstm_C[l], state_mlstm_n[l],
                        state_mlstm_m[l], state_mlstm_conv[l], Ts, *wts)
        for i in range(5):
            sp_all[i].append(sp[i])
            ss_all[i].append(ss[i])
    y_prompt = _rms(xp, final_g)
    y_sample = _rms(xs, final_g)
    hgrn_p = jnp.stack(sp_all[0]); C_p = jnp.stack(sp_all[1]); n_p = jnp.stack(sp_all[2])
    m_p = jnp.stack(sp_all[3]); conv_p = jnp.stack(sp_all[4])
    hgrn_s = jnp.stack(ss_all[0]); C_s = jnp.stack(ss_all[1]); n_s = jnp.stack(ss_all[2])
    m_s = jnp.stack(ss_all[3]); conv_s = jnp.stack(ss_all[4])
    return (y_prompt, y_sample, hgrn_p, C_p, n_p, m_p, conv_p, hgrn_s, C_s, n_s, m_s, conv_s)
```

```python
import functools

import jax
import jax.numpy as jnp
from jax import lax
from jax.experimental import pallas as pl
from jax.experimental.pallas import tpu as pltpu

F32 = jnp.float32
BF16 = jnp.bfloat16

D_MODEL = 1024
HGRN_HEADS = 8
HGRN_D = 128
MLSTM_HEADS = 4
MLSTM_DH = 256
CONV_WIDTH = 4
QKV_BLOCK = 4
EPS = 1e-6
N_PROJ = 7176
N_PROJ_PAD = 7296
GATE_TILE = 56
N_COL_TILES = 3
COL_TILE = N_PROJ_PAD // N_COL_TILES
PROMPT_CHUNK = 64
SAMPLE_ROWS = 8
VMEM_LIMIT = 56 * 1024 * 1024


def _dot(a, b):
    return jnp.dot(a, b, preferred_element_type=F32)


def _dot_nt(a, b):
    return lax.dot_general(a, b, (((1,), (1,)), ((), ())), preferred_element_type=F32)


def _dot_tn(a, b):
    return lax.dot_general(a, b, (((0,), (0,)), ((), ())), preferred_element_type=F32)


def _sigmoid(x):
    return 1.0 / (1.0 + jnp.exp(-x))


def _levels(L):
    out, s = [], 1
    while s < L:
        out.append(s)
        s *= 2
    return out


def _seg_bcast(W, s, L, r):
    n = W.shape[1]
    if s == 1:
        return jnp.where((r & 1) != 0, pltpu.roll(W, 1, 0), W)
    if s == 2:
        m = r & 3
        return jnp.where(m == 0, pltpu.roll(W, L - 1, 0),
                         jnp.where(m == 1, W,
                                   jnp.where(m == 2, pltpu.roll(W, 1, 0), pltpu.roll(W, 2, 0))))
    pieces = [jnp.broadcast_to(W[b * 2 * s + s - 1:b * 2 * s + s, :], (2 * s, n))
              for b in range(L // (2 * s))]
    return pieces[0] if len(pieces) == 1 else jnp.concatenate(pieces, axis=0)


def _segmented_scan(x, L, r, want_factors):
    W = x
    factors = []
    for s in _levels(L):
        Tb = _seg_bcast(W, s, L, r)
        sec = (r & s) != 0
        if want_factors:
            factors.append(jnp.exp(jnp.where(sec, W, Tb - W)))
        W = W + jnp.where(sec, Tb, 0.0)
    return W, factors


def _mod_kernel(c_ref, w_ref, b_ref, o_ref):
    c = c_ref[...]
    a = c * _sigmoid(c)
    o_ref[...] = _dot(a.astype(BF16), w_ref[...].astype(BF16)) + b_ref[...]


def _modulation(c_all, w_ada, b_ada):
    m = c_all.shape[0]
    n = w_ada.shape[1]
    tn = 512
    return pl.pallas_call(
        _mod_kernel,
        grid=(n // tn,),
        in_specs=[pl.BlockSpec((m, D_MODEL), lambda j: (0, 0)),
                  pl.BlockSpec((D_MODEL, tn), lambda j: (0, j)),
                  pl.BlockSpec((1, tn), lambda j: (0, j))],
        out_specs=pl.BlockSpec((m, tn), lambda j: (0, j)),
        out_shape=jax.ShapeDtypeStruct((m, n), F32),
        name="modulation",
    )(c_all, w_ada, b_ada)


def _inproj_kernel(x_ref, scale_ref, shift_ref, g_ref, w_ref, b_ref, o_ref, h_scr):
    @pl.when(pl.program_id(1) == 0)
    def _():
        x = x_ref[...]
        ms = jnp.mean(x * x, axis=-1, keepdims=True)
        y = x * lax.rsqrt(ms + EPS) * g_ref[...]
        h = y * (1.0 + scale_ref[0]) + shift_ref[0]
        h_scr[...] = h.astype(BF16)

    o_ref[...] = _dot(h_scr[...], w_ref[...]) + b_ref[...]


def _in_projection(x2d, scale3, shift3, norm_g, w_pad, b_pad, rb):
    rows = x2d.shape[0]
    mrows = scale3.shape[1]
    return pl.pallas_call(
        _inproj_kernel,
        grid=(rows // rb, N_COL_TILES),
        in_specs=[pl.BlockSpec((rb, D_MODEL), lambda i, j: (i, 0)),
                  pl.BlockSpec((1, mrows, D_MODEL), lambda i, j: (i * scale3.shape[0] // (rows // rb), 0, 0)),
                  pl.BlockSpec((1, mrows, D_MODEL), lambda i, j: (i * shift3.shape[0] // (rows // rb), 0, 0)),
                  pl.BlockSpec((1, D_MODEL), lambda i, j: (0, 0)),
                  pl.BlockSpec((D_MODEL, COL_TILE), lambda i, j: (0, j)),
                  pl.BlockSpec((1, COL_TILE), lambda i, j: (0, j))],
        out_specs=pl.BlockSpec((rb, COL_TILE), lambda i, j: (i, j)),
        out_shape=jax.ShapeDtypeStruct((rows, N_PROJ_PAD), F32),
        scratch_shapes=[pltpu.VMEM((rb, D_MODEL), BF16)],
        compiler_params=pltpu.CompilerParams(vmem_limit_bytes=VMEM_LIMIT),
        name="in_projection",
    )(x2d, scale3, shift3, norm_g, w_pad, b_pad)


def _hgrn_kernel(q_ref, f_ref, i_ref, z_ref, s0_ref, lbl_ref, g_ref, y_ref, s_ref, st_scr,
                 *, NB, TB, L, Tv):
    t = pl.program_id(1)
    n_chunks = TB // L

    @pl.when(t == 0)
    def _():
        for nb in range(NB):
            for h in range(HGRN_HEADS):
                st_scr[nb, h] = s0_ref[nb, h].T

    lg = lbl_ref[...]
    mx = jnp.max(lg, axis=0, keepdims=True)
    e = jnp.exp(lg - mx)
    lb = e[0:1, :] / jnp.sum(e, axis=0, keepdims=True)
    g_norm = g_ref[...]

    r = lax.broadcasted_iota(jnp.int32, (L, 1), 0)
    ti = lax.broadcasted_iota(jnp.int32, (L, L), 0)
    ji = lax.broadcasted_iota(jnp.int32, (L, L), 1)
    xo = ti ^ ji
    diag_mask = ti == ji
    lvl_masks = [(ji < ti) & (xo >= s) & (xo < 2 * s) for s in _levels(L)]

    def chunk(idx, carry):
        if NB == 1:
            nb, r0 = 0, pl.multiple_of(idx * L, L)
        else:
            nb, r0 = idx, pl.multiple_of(idx * TB, TB)
        rows = pl.ds(r0, L)
        hq = q_ref[rows, :]
        hf = f_ref[rows, :]
        hv = i_ref[rows, :]
        hz = z_ref[rows, :]
        sig = _sigmoid(hf)
        f = lb + (1.0 - lb) * sig
        logf = jnp.log(f)
        kk = (1.0 - lb) * (1.0 - sig)
        if Tv < L:
            valid = r < Tv
            logf = jnp.where(valid, logf, 0.0)
            kk = jnp.where(valid, kk, 0.0)
        G, factors = _segmented_scan(logf, L, r, True)
        GL = G[L - 1:L, :]
        qg = (hq * jnp.exp(G)).astype(BF16)
        kd = (kk * jnp.exp(GL - G)).astype(BF16)
        dS = jnp.exp(GL)
        vb = hv.astype(BF16)
        for h in range(HGRN_HEADS):
            hs = slice(h * HGRN_D, (h + 1) * HGRN_D)
            qh = hq[:, hs]
            kh = kk[:, hs]
            A = jnp.where(diag_mask, _dot_nt(qh.astype(BF16), kh.astype(BF16)), 0.0)
            for lvl in range(len(factors)):
                E = factors[lvl][:, hs]
                A = A + jnp.where(lvl_masks[lvl],
                                  _dot_nt((qh * E).astype(BF16), (kh * E).astype(BF16)), 0.0)
            st = st_scr[nb, h]
            o = _dot(A.astype(BF16), vb[:, hs]) + _dot_nt(qg[:, hs], st.astype(BF16))
            st_scr[nb, h] = st * dS[:, hs] + _dot_tn(vb[:, hs], kd[:, hs])
            ms = jnp.mean(o * o, axis=-1, keepdims=True)
            zh = hz[:, hs]
            y = o * lax.rsqrt(ms + EPS) * g_norm[:, hs] * (zh * _sigmoid(zh))
            y_ref[rows, hs] = y.astype(y_ref.dtype)
        return carry

    lax.fori_loop(0, NB * n_chunks, chunk, 0)

    @pl.when(t == pl.num_programs(1) - 1)
    def _():
        for nb in range(NB):
            for h in range(HGRN_HEADS):
                s_ref[nb, h] = st_scr[nb, h].T


def _hgrn(proj, s0, lb_logits, norm_g, *, B, T, TB, NB, L, Tv):
    nt = T // TB
    nbg = B // NB
    assert NB == 1 or nt == 1
    rb = NB * TB

    def col(c):
        return pl.BlockSpec((rb, D_MODEL), lambda bg, t: (bg * nt + t, c))

    kern = functools.partial(_hgrn_kernel, NB=NB, TB=TB, L=L, Tv=Tv)
    return pl.pallas_call(
        kern,
        grid=(nbg, nt),
        in_specs=[col(0), col(1), col(2), col(3),
                  pl.BlockSpec((NB, HGRN_HEADS, HGRN_D, HGRN_D), lambda bg, t: (bg, 0, 0, 0)),
                  pl.BlockSpec((2, D_MODEL), lambda bg, t: (0, 0)),
                  pl.BlockSpec((1, D_MODEL), lambda bg, t: (0, 0))],
        out_specs=[pl.BlockSpec((rb, D_MODEL), lambda bg, t: (bg * nt + t, 0)),
                   pl.BlockSpec((NB, HGRN_HEADS, HGRN_D, HGRN_D), lambda bg, t: (bg, 0, 0, 0))],
        out_shape=[jax.ShapeDtypeStruct((B * T, D_MODEL), BF16),
                   jax.ShapeDtypeStruct((B, HGRN_HEADS, HGRN_D, HGRN_D), F32)],
        scratch_shapes=[pltpu.VMEM((NB, HGRN_HEADS, HGRN_D, HGRN_D), F32)],
        compiler_params=pltpu.CompilerParams(vmem_limit_bytes=VMEM_LIMIT),
        name="hgrn2",
    )(proj, proj, proj, proj, s0, lb_logits, norm_g)


def _mlstm_kernel(u_ref, z_ref, o_ref, gate_ref, c0_ref, n0_ref, m0_ref, tail0_ref,
                  cw_ref, cb_ref, wq_ref, wk_ref, wv_ref, ng_ref, skip_ref,
                  y_ref, c_ref, n_ref, m_ref, tail_scr, ext_scr, xc_scr,
                  *, NB, TB, L, Tv):
    t = pl.program_id(1)
    n_chunks = TB // L

    @pl.when(t == 0)
    def _():
        c_ref[...] = c0_ref[...]
        n_ref[...] = n0_ref[...]
        m_ref[...] = m0_ref[...]
        tail_scr[...] = tail0_ref[...]

    cw = cw_ref[...]
    cb = cb_ref[...]
    ng = ng_ref[...]
    skip = skip_ref[...]

    r = lax.broadcasted_iota(jnp.int32, (L, 1), 0)
    lane = lax.broadcasted_iota(jnp.int32, (1, 128), 1)
    ti = lax.broadcasted_iota(jnp.int32, (L, L), 0)
    ji = lax.broadcasted_iota(jnp.int32, (L, L), 1)
    causal = ji <= ti

    for nb in range(NB):
        rs = slice(nb * TB, (nb + 1) * TB)
        ext_scr[0:8, :] = tail_scr[nb]
        ext_scr[8:8 + TB, :] = u_ref[rs, :]
        conv = cb
        for i in range(CONV_WIDTH):
            lo = 8 - (CONV_WIDTH - 1) + i
            conv = conv + cw[i:i + 1, :] * ext_scr[lo:lo + TB, :]
        xc_scr[rs, :] = conv * _sigmoid(conv)
        tail_scr[nb] = u_ref[nb * TB + TB - 8:(nb + 1) * TB, :]

    def chunk(idx, carry):
        if NB == 1:
            nb, r0 = 0, pl.multiple_of(idx * L, L)
        else:
            nb, r0 = idx, pl.multiple_of(idx * TB, TB)
        rows = pl.ds(r0, L)
        mu = u_ref[rows, :]
        xc = xc_scr[rows, :]
        xcb = xc.astype(BF16)
        mub = mu.astype(BF16)
        mz = z_ref[rows, :]
        mo = o_ref[rows, :]

        gt = gate_ref[rows, :]
        lf = jnp.minimum(gt, 0.0) - jnp.log1p(jnp.exp(-jnp.abs(gt)))
        ig = gt
        if Tv < L:
            valid = r < Tv
            lf = jnp.where(valid, lf, 0.0)
            ig = jnp.where(valid, ig, -jnp.inf)
        bcum, _ = _segmented_scan(lf, L, r, False)
        comb = jnp.where(lane < MLSTM_HEADS, ig, bcum)
        if L < 128:
            comb_sq = jnp.concatenate([comb, jnp.zeros((128 - L, 128), F32)], axis=0)
        else:
            comb_sq = comb
        combT = comb_sq.T
        m_row = m_ref[nb]
        n_row = n_ref[nb]
        for h in range(MLSTM_HEADS):
            hs = slice(h * MLSTM_DH, (h + 1) * MLSTM_DH)
            icol = ig[:, h:h + 1]
            bcol = bcum[:, MLSTM_HEADS + h:MLSTM_HEADS + h + 1]
            irow = combT[h:h + 1, 0:L]
            brow = combT[MLSTM_HEADS + h:MLSTM_HEADS + h + 1, 0:L]
            m_prev = m_row[:, h:h + 1]
            logD = jnp.where(causal, (bcol - brow) + irow, -jnp.inf)
            m_intra = jnp.max(logD, axis=-1, keepdims=True)
            m_inter = bcol + m_prev
            m_t = jnp.maximum(m_inter, m_intra)
            Dm = jnp.exp(logD - m_t)
            inter = jnp.exp(m_inter - m_t)
            q = _dot(xcb[:, hs], wq_ref[h])
            k = _dot(xcb[:, hs], wk_ref[h]) * (MLSTM_DH ** -0.5)
            v = _dot(mub[:, hs], wv_ref[h])
            qb = q.astype(BF16)
            vb = v.astype(BF16)
            s = _dot_nt(qb, k.astype(BF16)) * Dm
            C = c_ref[nb, h]
            nh = n_row[:, hs]
            num = inter * _dot(qb, C.astype(BF16)) + _dot(s.astype(BF16), vb)
            den = inter * jnp.sum(q * nh, axis=-1, keepdims=True) + jnp.sum(s, axis=-1, keepdims=True)
            hh = num / jnp.maximum(jnp.abs(den), jnp.exp(-m_t))
            m_new = m_t[L - 1:L, :]
            b_last = bcol[L - 1:L, :]
            wC = jnp.exp((b_last - bcol) + icol - m_new)
            dec = jnp.exp(b_last + m_prev - m_new)
            kw = wC * k
            c_ref[nb, h] = dec * C + _dot_tn(kw.astype(BF16), vb)
            n_ref[nb, :, hs] = dec * nh + jnp.sum(kw, axis=0, keepdims=True)
            m_row = jnp.where(lane == h, m_new, m_row)
            moh = mo[:, hs]
            hm = _sigmoid(moh) * hh
            ms = jnp.mean(hm * hm, axis=-1, keepdims=True)
            mzh = mz[:, hs]
            y = (hm * lax.rsqrt(ms + EPS) * ng[:, hs] + skip[:, hs] * xc[:, hs]) * (mzh * _sigmoid(mzh))
            y_ref[rows, hs] = y.astype(y_ref.dtype)
        m_ref[nb] = m_row
        return carry

    lax.fori_loop(0, NB * n_chunks, chunk, 0)


def _mlstm(proj, c0, n0, m0, tail0, conv_w, conv_b, wq, wk, wv, norm_g, skip, *, B, T, TB, NB, L, Tv):
    nt = T // TB
    nbg = B // NB
    assert NB == 1 or nt == 1
    rb = NB * TB

    def col(c, w=D_MODEL):
        return pl.BlockSpec((rb, w), lambda bg, t: (bg * nt + t, c))

    def full(shape):
        return pl.BlockSpec(shape, lambda bg, t: (0,) * len(shape))

    c_spec = pl.BlockSpec((NB, MLSTM_HEADS, MLSTM_DH, MLSTM_DH), lambda bg, t: (bg, 0, 0, 0))
    n_spec = pl.BlockSpec((NB, 1, D_MODEL), lambda bg, t: (bg, 0, 0))
    m_spec = pl.BlockSpec((NB, 1, 128), lambda bg, t: (bg, 0, 0))
    kern = functools.partial(_mlstm_kernel, NB=NB, TB=TB, L=L, Tv=Tv)
    return pl.pallas_call(
        kern,
        grid=(nbg, nt),
        in_specs=[col(4), col(5), col(6), col(GATE_TILE, 128),
                  c_spec, n_spec, m_spec,
                  pl.BlockSpec((NB, 8, D_MODEL), lambda bg, t: (bg, 0, 0)),
                  full((CONV_WIDTH, D_MODEL)), full((1, D_MODEL)),
                  full((MLSTM_HEADS, MLSTM_DH, MLSTM_DH)), full((MLSTM_HEADS, MLSTM_DH, MLSTM_DH)),
                  full((MLSTM_HEADS, MLSTM_DH, MLSTM_DH)),
                  full((1, D_MODEL)), full((1, D_MODEL))],
        out_specs=[pl.BlockSpec((rb, D_MODEL), lambda bg, t: (bg * nt + t, 0)), c_spec, n_spec, m_spec],
        out_shape=[jax.ShapeDtypeStruct((B * T, D_MODEL), BF16),
                   jax.ShapeDtypeStruct((B, MLSTM_HEADS, MLSTM_DH, MLSTM_DH), F32),
                   jax.ShapeDtypeStruct((B, 1, D_MODEL), F32),
                   jax.ShapeDtypeStruct((B, 1, 128), F32)],
        scratch_shapes=[pltpu.VMEM((NB, 8, D_MODEL), F32),
                        pltpu.VMEM((8 + TB, D_MODEL), F32),
                        pltpu.VMEM((NB * TB, D_MODEL), F32)],
        compiler_params=pltpu.CompilerParams(vmem_limit_bytes=VMEM_LIMIT),
        name="mlstm",
    )(proj, proj, proj, proj, c0, n0, m0, tail0, conv_w, conv_b, wq, wk, wv, norm_g, skip)


def _out_kernel(yh_ref, ym_ref, x_ref, gate_ref, w_ref, fg_ref, o_ref):
    acc = _dot(yh_ref[...], w_ref[0:D_MODEL, :]) + _dot(ym_ref[...], w_ref[D_MODEL:2 * D_MODEL, :])
    out = x_ref[...] + gate_ref[0] * acc
    ms = jnp.mean(out * out, axis=-1, keepdims=True)
    o_ref[...] = out * lax.rsqrt(ms + EPS) * fg_ref[...]


def _out_projection(yh, ym, x2d, gate3, w_out, final_g, rb):
    rows = x2d.shape[0]
    mrows = gate3.shape[1]
    nblk = rows // rb
    return pl.pallas_call(
        _out_kernel,
        grid=(nblk,),
        in_specs=[pl.BlockSpec((rb, D_MODEL), lambda i: (i, 0)),
                  pl.BlockSpec((rb, D_MODEL), lambda i: (i, 0)),
                  pl.BlockSpec((rb, D_MODEL), lambda i: (i, 0)),
                  pl.BlockSpec((1, mrows, D_MODEL), lambda i: (i * gate3.shape[0] // nblk, 0, 0)),
                  pl.BlockSpec((2 * D_MODEL, D_MODEL), lambda i: (0, 0)),
                  pl.BlockSpec((1, D_MODEL), lambda i: (0, 0))],
        out_specs=pl.BlockSpec((rb, D_MODEL), lambda i: (i, 0)),
        out_shape=jax.ShapeDtypeStruct((rows, D_MODEL), F32),
        compiler_params=pltpu.CompilerParams(vmem_limit_bytes=VMEM_LIMIT),
        name="out_projection",
    )(yh, ym, x2d, gate3, w_out, final_g)


def _block_diag_heads(w):
    groups = MLSTM_DH // QKV_BLOCK
    wh = w.reshape(MLSTM_HEADS, groups, QKV_BLOCK, QKV_BLOCK)
    eye = jnp.eye(groups, dtype=w.dtype)
    bd = jnp.einsum('hgij,gk->hgikj', wh, eye)
    return bd.reshape(MLSTM_HEADS, MLSTM_DH, MLSTM_DH).astype(BF16)


def _layer(x, mod, s0, c0, n0, m0, conv0, wts, *, L, TB, NB, Tv, rb):
    (norm_g, w_pad, b_pad, lb_logits, hgrn_norm_g, conv_w, conv_b, wq, wk, wv, mnorm_g, skip, w_out, final_g) = wts
    B, T, _ = x.shape
    rows = B * T
    x2d = x.reshape(rows, D_MODEL)
    shift, scale, gate = mod[:, :D_MODEL], mod[:, D_MODEL:2 * D_MODEL], mod[:, 2 * D_MODEL:]
    if T >= rb:
        def per_block(a):
            return a.reshape(B, 1, D_MODEL)
    else:
        def per_block(a):
            return jnp.repeat(a, T, axis=0).reshape(rows // rb, rb, D_MODEL)
    proj = _in_projection(x2d, per_block(scale), per_block(shift), norm_g, w_pad, b_pad, rb)
    yh, s_new = _hgrn(proj, s0, lb_logits, hgrn_norm_g, B=B, T=T, TB=TB, NB=NB, L=L, Tv=Tv)
    tail0 = jnp.pad(conv0, ((0, 0), (8 - (CONV_WIDTH - 1), 0), (0, 0)))
    m0p = jnp.pad(m0, ((0, 0), (0, 128 - MLSTM_HEADS))).reshape(B, 1, 128)
    ym, c_new, n_new, m_new = _mlstm(proj, c0, n0.reshape(B, 1, D_MODEL), m0p, tail0, conv_w, conv_b,
                                     wq, wk, wv, mnorm_g, skip, B=B, T=T, TB=TB, NB=min(NB, 4), L=L, Tv=Tv)
    y = _out_projection(yh, ym, x2d, per_block(gate), w_out, final_g, rb)
    return (y.reshape(B, T, D_MODEL), proj.reshape(B, T, N_PROJ_PAD), s_new, c_new,
            n_new.reshape(B, MLSTM_HEADS, MLSTM_DH), m_new.reshape(B, 128)[:, :MLSTM_HEADS])


def kernel(x_prompt, x_sample, c_prompt, c_sample, state_hgrn, state_mlstm_C, state_mlstm_n, state_mlstm_m, state_mlstm_conv, w_ada, b_ada, norm_g, w_in, b_in, hgrn_lb_logits, hgrn_norm_g, mlstm_conv_w, mlstm_conv_b, mlstm_wq, mlstm_wk, mlstm_wv, mlstm_norm_g, mlstm_skip, w_out, final_g):
    assert w_in.shape == (1, D_MODEL, N_PROJ) and hgrn_lb_logits.shape == (2, D_MODEL)
    Bp, Tp, _ = x_prompt.shape
    Bs, Ts, _ = x_sample.shape
    assert Tp % PROMPT_CHUNK == 0 and Ts <= SAMPLE_ROWS and Ts >= CONV_WIDTH - 1

    mod = _modulation(jnp.concatenate([c_prompt, c_sample], axis=0), w_ada[0], b_ada[0].reshape(1, -1))
    pad_cols = N_PROJ_PAD - N_PROJ
    wts = (norm_g[0].reshape(1, -1),
           jnp.pad(w_in[0], ((0, 0), (0, pad_cols))).astype(BF16),
           jnp.pad(b_in[0], (0, pad_cols)).reshape(1, -1),
           hgrn_lb_logits, hgrn_norm_g[0].reshape(1, -1),
           mlstm_conv_w[0], mlstm_conv_b[0].reshape(1, -1),
           _block_diag_heads(mlstm_wq[0]), _block_diag_heads(mlstm_wk[0]), _block_diag_heads(mlstm_wv[0]),
           mlstm_norm_g[0].reshape(1, -1), mlstm_skip[0].reshape(1, -1),
           w_out[0].astype(BF16), final_g.reshape(1, -1))

    yp, proj_p, hg_p, c_p, n_p, m_p = _layer(
        x_prompt, mod[:Bp],
        jnp.zeros((Bp, HGRN_HEADS, HGRN_D, HGRN_D), F32),
        jnp.zeros((Bp, MLSTM_HEADS, MLSTM_DH, MLSTM_DH), F32),
        jnp.zeros((Bp, MLSTM_HEADS, MLSTM_DH), F32),
        jnp.zeros((Bp, MLSTM_HEADS), F32),
        jnp.zeros((Bp, CONV_WIDTH - 1, D_MODEL), F32),
        wts, L=PROMPT_CHUNK, TB=256, NB=1, Tv=PROMPT_CHUNK, rb=512)
    conv_p = proj_p[:, Tp - (CONV_WIDTH - 1):, 4 * D_MODEL:5 * D_MODEL]

    xs = jnp.pad(x_sample, ((0, 0), (0, SAMPLE_ROWS - Ts), (0, 0)))
    ys, proj_s, hg_s, c_s, n_s, m_s = _layer(
        xs, mod[Bp:], state_hgrn[0], state_mlstm_C[0], state_mlstm_n[0], state_mlstm_m[0],
        state_mlstm_conv[0], wts, L=SAMPLE_ROWS, TB=SAMPLE_ROWS, NB=8, Tv=Ts, rb=512)
    conv_s = proj_s[:, Ts - (CONV_WIDTH - 1):Ts, 4 * D_MODEL:5 * D_MODEL]

    return (yp, ys[:, :Ts], hg_p[None], c_p[None], n_p[None], m_p[None], conv_p[None],
            hg_s[None], c_s[None], n_s[None], m_s[None], conv_s[None])
```

```python
import functools

import jax
import jax.numpy as jnp
from jax import lax
from jax.experimental import pallas as pl
from jax.experimental.pallas import tpu as pltpu

F32 = jnp.float32
BF16 = jnp.bfloat16

D_MODEL = 1024
HGRN_HEADS = 8
HGRN_D = 128
MLSTM_HEADS = 4
MLSTM_DH = 256
CONV_WIDTH = 4
QKV_BLOCK = 4
EPS = 1e-6
N_PROJ = 7176
N_PROJ_PAD = 7296
GATE_TILE = 56
N_COL_TILES = 3
COL_TILE = N_PROJ_PAD // N_COL_TILES
PROMPT_CHUNK = 64
SAMPLE_ROWS = 8
SUBLANES = 8
LANES = 128
VMEM_LIMIT = 56 * 1024 * 1024
FAST_PATH_MIN_LOG_DECAY = -80.0


def _dot(a, b):
    return jnp.dot(a, b, preferred_element_type=F32)


def _dot_nt(a, b):
    return lax.dot_general(a, b, (((1,), (1,)), ((), ())), preferred_element_type=F32)


def _dot_tn(a, b):
    return lax.dot_general(a, b, (((0,), (0,)), ((), ())), preferred_element_type=F32)


def _sigmoid(x):
    return 0.5 * jnp.tanh(0.5 * x) + 0.5


def _levels(L):
    out, s = [], 1
    while s < L:
        out.append(s)
        s *= 2
    return out


def _seg_bcast(W, s, L, r):
    n = W.shape[1]
    if s == 1:
        return jnp.where((r & 1) != 0, pltpu.roll(W, 1, 0), W)
    if s == 2:
        m = r & 3
        return jnp.where(m == 0, pltpu.roll(W, L - 1, 0),
                         jnp.where(m == 1, W,
                                   jnp.where(m == 2, pltpu.roll(W, 1, 0), pltpu.roll(W, 2, 0))))
    pieces = [jnp.broadcast_to(W[b * 2 * s + s - 1:b * 2 * s + s, :], (2 * s, n))
              for b in range(L // (2 * s))]
    return pieces[0] if len(pieces) == 1 else jnp.concatenate(pieces, axis=0)


def _level_factors(x, L, r):
    W = x
    factors = []
    for s in _levels(L):
        Tb = _seg_bcast(W, s, L, r)
        sec = (r & s) != 0
        factors.append(jnp.exp(jnp.where(sec, W, Tb - W)))
        W = W + jnp.where(sec, Tb, 0.0)
    return factors


def _prefix8(x):
    sub = lax.broadcasted_iota(jnp.int32, (SUBLANES, 1), 0)
    y = x + jnp.where(sub >= 1, pltpu.roll(x, 1, 0), 0.0)
    y = y + jnp.where(sub >= 2, pltpu.roll(y, 2, 0), 0.0)
    return y + jnp.where(sub >= 4, pltpu.roll(y, 4, 0), 0.0)


def _chunk_cumsum(x, L):
    outs = []
    for c in range(x.shape[0] // L):
        total = None
        for g in range(L // SUBLANES):
            lo = c * L + g * SUBLANES
            p = _prefix8(x[lo:lo + SUBLANES, :])
            if total is not None:
                p = p + total
            outs.append(p)
            total = p[SUBLANES - 1:SUBLANES, :]
    return outs[0] if len(outs) == 1 else jnp.concatenate(outs, axis=0)


def _rows_bcast(x, row_in_chunk, L):
    n = x.shape[1]
    pieces = [jnp.broadcast_to(x[c * L + row_in_chunk:c * L + row_in_chunk + 1, :], (L, n))
              for c in range(x.shape[0] // L)]
    return pieces[0] if len(pieces) == 1 else jnp.concatenate(pieces, axis=0)


def _mod_kernel(c_ref, w_ref, b_ref, o_ref):
    c = c_ref[...]
    a = c * _sigmoid(c)
    o_ref[...] = _dot(a.astype(BF16), w_ref[...].astype(BF16)) + b_ref[...]


def _modulation(c_all, w_ada, b_ada):
    m = c_all.shape[0]
    n = w_ada.shape[1]
    tn = 512
    return pl.pallas_call(
        _mod_kernel,
        grid=(n // tn,),
        in_specs=[pl.BlockSpec((m, D_MODEL), lambda j: (0, 0)),
                  pl.BlockSpec((D_MODEL, tn), lambda j: (0, j)),
                  pl.BlockSpec((1, tn), lambda j: (0, j))],
        out_specs=pl.BlockSpec((m, tn), lambda j: (0, j)),
        out_shape=jax.ShapeDtypeStruct((m, n), F32),
        name="modulation",
    )(c_all, w_ada, b_ada)


def _inproj_kernel(x_ref, scale_ref, shift_ref, g_ref, w_ref, b_ref, o_ref, h_scr):
    @pl.when(pl.program_id(1) == 0)
    def _():
        x = x_ref[...]
        ms = jnp.mean(x * x, axis=-1, keepdims=True)
        y = x * lax.rsqrt(ms + EPS) * g_ref[...]
        h = y * (1.0 + scale_ref[0]) + shift_ref[0]
        h_scr[...] = h.astype(BF16)

    o_ref[...] = _dot(h_scr[...], w_ref[...]) + b_ref[...]


def _in_projection(x2d, scale3, shift3, norm_g, w_pad, b_pad, rb):
    rows = x2d.shape[0]
    mrows = scale3.shape[1]
    nblk = rows // rb
    return pl.pallas_call(
        _inproj_kernel,
        grid=(nblk, N_COL_TILES),
        in_specs=[pl.BlockSpec((rb, D_MODEL), lambda i, j: (i, 0)),
                  pl.BlockSpec((1, mrows, D_MODEL), lambda i, j: (i * scale3.shape[0] // nblk, 0, 0)),
                  pl.BlockSpec((1, mrows, D_MODEL), lambda i, j: (i * shift3.shape[0] // nblk, 0, 0)),
                  pl.BlockSpec((1, D_MODEL), lambda i, j: (0, 0)),
                  pl.BlockSpec((D_MODEL, COL_TILE), lambda i, j: (0, j)),
                  pl.BlockSpec((1, COL_TILE), lambda i, j: (0, j))],
        out_specs=pl.BlockSpec((rb, COL_TILE), lambda i, j: (i, j)),
        out_shape=jax.ShapeDtypeStruct((rows, N_PROJ_PAD), F32),
        scratch_shapes=[pltpu.VMEM((rb, D_MODEL), BF16)],
        compiler_params=pltpu.CompilerParams(vmem_limit_bytes=VMEM_LIMIT),
        name="in_projection",
    )(x2d, scale3, shift3, norm_g, w_pad, b_pad)


def _hgrn_kernel(q_ref, f_ref, i_ref, z_ref, s0_ref, lbl_ref, g_ref, y_ref, s_ref, st_scr, a_scr,
                 *, NB, TB, L, Tv):
    H = HGRN_HEADS
    rb = NB * TB
    spn = TB // L
    nseg = rb // L
    t = pl.program_id(1)

    @pl.when(t == 0)
    def _():
        for nb in range(NB):
            for h in range(H):
                st_scr[nb, h] = s0_ref[nb, h].T

    lg = lbl_ref[...]
    mx = jnp.max(lg, axis=0, keepdims=True)
    e = jnp.exp(lg - mx)
    lb = e[0:1, :] / jnp.sum(e, axis=0, keepdims=True)
    g_norm = g_ref[...]

    hq = q_ref[...]
    sig = _sigmoid(f_ref[...])
    logf = jnp.log(lb + (1.0 - lb) * sig)
    kk = (1.0 - lb) * (1.0 - sig)
    if Tv < L:
        valid = (lax.broadcasted_iota(jnp.int32, (rb, 1), 0) & (L - 1)) < Tv
        logf = jnp.where(valid, logf, 0.0)
        kk = jnp.where(valid, kk, 0.0)
    G = _chunk_cumsum(logf, L)
    GL = [G[s * L + L - 1:s * L + L, :] for s in range(nseg)]
    fast_ok = jnp.min(functools.reduce(jnp.minimum, GL)) >= FAST_PATH_MIN_LOG_DECAY

    ti = lax.broadcasted_iota(jnp.int32, (L, L), 0)
    ji = lax.broadcasted_iota(jnp.int32, (L, L), 1)

    @pl.when(fast_ok)
    def _():
        d = G - _rows_bcast(G, L // 2 - 1, L)
        qt = (hq * jnp.exp(d)).astype(BF16)
        kt = (kk * jnp.exp(-d)).astype(BF16)
        causal = ji <= ti
        for s in range(nseg):
            rows = slice(s * L, (s + 1) * L)
            for h in range(H):
                hs = slice(h * HGRN_D, (h + 1) * HGRN_D)
                a_scr[s * H + h] = jnp.where(causal, _dot_nt(qt[rows, hs], kt[rows, hs]), 0.0)

    @pl.when(jnp.logical_not(fast_ok))
    def _():
        r = lax.broadcasted_iota(jnp.int32, (L, 1), 0)
        xo = ti ^ ji
        diag_mask = ti == ji
        lvl_masks = [(ji < ti) & (xo >= s) & (xo < 2 * s) for s in _levels(L)]
        for s in range(nseg):
            rows = slice(s * L, (s + 1) * L)
            factors = _level_factors(logf[rows, :], L, r)
            for h in range(H):
                hs = slice(h * HGRN_D, (h + 1) * HGRN_D)
                qh = hq[rows, hs]
                kh = kk[rows, hs]
                A = jnp.where(diag_mask, _dot_nt(qh.astype(BF16), kh.astype(BF16)), 0.0)
                for lvl in range(len(factors)):
                    E = factors[lvl][:, hs]
                    A = A + jnp.where(lvl_masks[lvl],
                                      _dot_nt((qh * E).astype(BF16), (kh * E).astype(BF16)), 0.0)
                a_scr[s * H + h] = A

    qg = (hq * jnp.exp(G)).astype(BF16)
    GLb = GL[0] if nseg == 1 and L == rb else jnp.concatenate(
        [jnp.broadcast_to(gl, (L, D_MODEL)) for gl in GL], axis=0)
    kd = (kk * jnp.exp(GLb - G)).astype(BF16)
    vb = i_ref[...].astype(BF16)
    hz = z_ref[...]
    zgate = hz * _sigmoid(hz)
    for nb in range(NB):
        st = [st_scr[nb, h] for h in range(H)]
        for c in range(spn):
            s = nb * spn + c
            rows = slice(s * L, (s + 1) * L)
            dS = jnp.exp(GL[s])
            for h in range(H):
                hs = slice(h * HGRN_D, (h + 1) * HGRN_D)
                A = a_scr[s * H + h].astype(BF16)
                o = _dot(A, vb[rows, hs]) + _dot_nt(qg[rows, hs], st[h].astype(BF16))
                st[h] = st[h] * dS[:, hs] + _dot_tn(vb[rows, hs], kd[rows, hs])
                ms = jnp.mean(o * o, axis=-1, keepdims=True)
                y = o * lax.rsqrt(ms + EPS) * g_norm[:, hs] * zgate[rows, hs]
                y_ref[rows, hs] = y.astype(y_ref.dtype)
        for h in range(H):
            st_scr[nb, h] = st[h]

    @pl.when(t == pl.num_programs(1) - 1)
    def _():
        for nb in range(NB):
            for h in range(H):
                s_ref[nb, h] = st_scr[nb, h].T


def _hgrn(proj, s0, lb_logits, norm_g, *, B, T, TB, NB, L, Tv):
    nt = T // TB
    nbg = B // NB
    assert NB == 1 or nt == 1
    rb = NB * TB

    def col(c):
        return pl.BlockSpec((rb, D_MODEL), lambda bg, t: (bg * nt + t, c))

    kern = functools.partial(_hgrn_kernel, NB=NB, TB=TB, L=L, Tv=Tv)
    return pl.pallas_call(
        kern,
        grid=(nbg, nt),
        in_specs=[col(0), col(1), col(2), col(3),
                  pl.BlockSpec((NB, HGRN_HEADS, HGRN_D, HGRN_D), lambda bg, t: (bg, 0, 0, 0)),
                  pl.BlockSpec((2, D_MODEL), lambda bg, t: (0, 0)),
                  pl.BlockSpec((1, D_MODEL), lambda bg, t: (0, 0))],
        out_specs=[pl.BlockSpec((rb, D_MODEL), lambda bg, t: (bg * nt + t, 0)),
                   pl.BlockSpec((NB, HGRN_HEADS, HGRN_D, HGRN_D), lambda bg, t: (bg, 0, 0, 0))],
        out_shape=[jax.ShapeDtypeStruct((B * T, D_MODEL), BF16),
                   jax.ShapeDtypeStruct((B, HGRN_HEADS, HGRN_D, HGRN_D), F32)],
        scratch_shapes=[pltpu.VMEM((NB, HGRN_HEADS, HGRN_D, HGRN_D), F32),
                        pltpu.VMEM((rb // L * HGRN_HEADS, L, L), F32)],
        compiler_params=pltpu.CompilerParams(vmem_limit_bytes=VMEM_LIMIT),
        name="hgrn2",
    )(proj, proj, proj, proj, s0, lb_logits, norm_g)


def _mlstm_kernel(u_ref, z_ref, o_ref, gate_ref, c0_ref, n0_ref, m0_ref, tail0_ref,
                  cw_ref, cb_ref, wq_ref, wk_ref, wv_ref, ng_ref, skip_ref,
                  y_ref, c_ref, n_ref, m_ref, tail_scr,
                  *, NB, TB, L, Tv):
    H = MLSTM_HEADS
    rb = NB * TB
    spn = TB // L
    t = pl.program_id(1)

    @pl.when(t == 0)
    def _():
        c_ref[...] = c0_ref[...]
        n_ref[...] = n0_ref[...]
        m_ref[...] = m0_ref[...]
        tail_scr[...] = tail0_ref[...]

    cw = cw_ref[...]
    ng = ng_ref[...]
    skip = skip_ref[...]
    lane = lax.broadcasted_iota(jnp.int32, (1, LANES), 1)
    ti = lax.broadcasted_iota(jnp.int32, (L, L), 0)
    ji = lax.broadcasted_iota(jnp.int32, (L, L), 1)
    causal = ji <= ti

    mu = u_ref[...]
    xcs = []
    for nb in range(NB):
        u_nb = mu[nb * TB:(nb + 1) * TB, :]
        ext = jnp.concatenate([tail_scr[nb], u_nb], axis=0)
        conv = cb_ref[...] + cw[CONV_WIDTH - 1:CONV_WIDTH, :] * u_nb
        for i in range(1, CONV_WIDTH):
            conv = conv + cw[CONV_WIDTH - 1 - i:CONV_WIDTH - i, :] * pltpu.roll(ext, i, 0)[SUBLANES:, :]
        xcs.append(conv * _sigmoid(conv))
        tail_scr[nb] = u_nb[TB - SUBLANES:, :]
    xc = xcs[0] if NB == 1 else jnp.concatenate(xcs, axis=0)
    xcb = xc.astype(BF16)
    mub = mu.astype(BF16)
    mz = z_ref[...]
    zgate = mz * _sigmoid(mz)
    ogate = _sigmoid(o_ref[...])

    gt = gate_ref[...]
    lf = jnp.minimum(gt, 0.0) - jnp.log1p(jnp.exp(-jnp.abs(gt)))
    ig = gt
    if Tv < L:
        valid = (lax.broadcasted_iota(jnp.int32, (rb, 1), 0) & (L - 1)) < Tv
        lf = jnp.where(valid, lf, 0.0)
        ig = jnp.where(valid, ig, -jnp.inf)
    bcum = _chunk_cumsum(lf, L)
    comb = jnp.where(lane < H, ig, bcum)
    rpad = -rb % LANES
    if rpad:
        comb = jnp.concatenate([comb, jnp.zeros((rpad, LANES), F32)], axis=0)
    combT = comb.T

    for h in range(H):
        hs = slice(h * MLSTM_DH, (h + 1) * MLSTM_DH)
        q_all = _dot(xcb[:, hs], wq_ref[h])
        k_all = _dot(xcb[:, hs], wk_ref[h]) * (MLSTM_DH ** -0.5)
        vb_all = _dot(mub[:, hs], wv_ref[h]).astype(BF16)
        qb_all = q_all.astype(BF16)
        kb_all = k_all.astype(BF16)
        for nb in range(NB):
            C = c_ref[nb, h]
            nh = n_ref[nb, :, hs]
            m_prev = m_ref[nb][:, h:h + 1]
            for c in range(spn):
                s = nb * spn + c
                rows = slice(s * L, (s + 1) * L)
                icol = ig[rows, h:h + 1]
                bcol = bcum[rows, H + h:H + h + 1]
                irow = combT[h:h + 1, rows]
                brow = combT[H + h:H + h + 1, rows]
                logD = jnp.where(causal, (bcol - brow) + irow, -jnp.inf)
                m_intra = jnp.max(logD, axis=-1, keepdims=True)
                sc = _dot_nt(qb_all[rows], kb_all[rows]) * jnp.exp(logD - m_intra)
                rs = jnp.sum(sc, axis=-1, keepdims=True)
                sv = _dot(sc.astype(BF16), vb_all[rows])
                m_loc = m_intra[L - 1:L, :]
                b_last = bcol[L - 1:L, :]
                kw = jnp.exp((b_last - bcol) + icol - m_loc) * k_all[rows]
                U = _dot_tn(kw.astype(BF16), vb_all[rows])
                ks = jnp.sum(kw, axis=0, keepdims=True)
                m_inter = bcol + m_prev
                m_t = jnp.maximum(m_inter, m_intra)
                inter = jnp.exp(m_inter - m_t)
                scl = jnp.exp(m_intra - m_t)
                q = q_all[rows]
                num = inter * _dot(qb_all[rows], C.astype(BF16)) + scl * sv
                den = inter * jnp.sum(q * nh, axis=-1, keepdims=True) + scl * rs
                hh = num * (1.0 / jnp.maximum(jnp.abs(den), jnp.exp(-m_t)))
                m_new = m_t[L - 1:L, :]
                dec = jnp.exp(b_last + m_prev - m_new)
                scu = jnp.exp(m_loc - m_new)
                C = dec * C + scu * U
                nh = dec * nh + scu * ks
                m_prev = m_new
                hm = ogate[rows, hs] * hh
                ms = jnp.mean(hm * hm, axis=-1, keepdims=True)
                y = (hm * lax.rsqrt(ms + EPS) * ng[:, hs] + skip[:, hs] * xc[rows, hs]) * zgate[rows, hs]
                y_ref[rows, hs] = y.astype(y_ref.dtype)
            c_ref[nb, h] = C
            n_ref[nb, :, hs] = nh
            m_ref[nb] = jnp.where(lane == h, m_prev, m_ref[nb])


def _mlstm(proj, c0, n0, m0, tail0, conv_w, conv_b, wq, wk, wv, norm_g, skip, *, B, T, TB, NB, L, Tv):
    nt = T // TB
    nbg = B // NB
    assert NB == 1 or nt == 1
    rb = NB * TB

    def col(c, w=D_MODEL):
        return pl.BlockSpec((rb, w), lambda bg, t: (bg * nt + t, c))

    def full(shape):
        return pl.BlockSpec(shape, lambda bg, t: (0,) * len(shape))

    c_spec = pl.BlockSpec((NB, MLSTM_HEADS, MLSTM_DH, MLSTM_DH), lambda bg, t: (bg, 0, 0, 0))
    n_spec = pl.BlockSpec((NB, 1, D_MODEL), lambda bg, t: (bg, 0, 0))
    m_spec = pl.BlockSpec((NB, 1, LANES), lambda bg, t: (bg, 0, 0))
    kern = functools.partial(_mlstm_kernel, NB=NB, TB=TB, L=L, Tv=Tv)
    return pl.pallas_call(
        kern,
        grid=(nbg, nt),
        in_specs=[col(4), col(5), col(6), col(GATE_TILE, LANES),
                  c_spec, n_spec, m_spec,
                  pl.BlockSpec((NB, SUBLANES, D_MODEL), lambda bg, t: (bg, 0, 0)),
                  full((CONV_WIDTH, D_MODEL)), full((1, D_MODEL)),
                  full((MLSTM_HEADS, MLSTM_DH, MLSTM_DH)), full((MLSTM_HEADS, MLSTM_DH, MLSTM_DH)),
                  full((MLSTM_HEADS, MLSTM_DH, MLSTM_DH)),
                  full((1, D_MODEL)), full((1, D_MODEL))],
        out_specs=[pl.BlockSpec((rb, D_MODEL), lambda bg, t: (bg * nt + t, 0)), c_spec, n_spec, m_spec],
        out_shape=[jax.ShapeDtypeStruct((B * T, D_MODEL), BF16),
                   jax.ShapeDtypeStruct((B, MLSTM_HEADS, MLSTM_DH, MLSTM_DH), F32),
                   jax.ShapeDtypeStruct((B, 1, D_MODEL), F32),
                   jax.ShapeDtypeStruct((B, 1, LANES), F32)],
        scratch_shapes=[pltpu.VMEM((NB, SUBLANES, D_MODEL), F32)],
        compiler_params=pltpu.CompilerParams(vmem_limit_bytes=VMEM_LIMIT),
        name="mlstm",
    )(proj, proj, proj, proj, c0, n0, m0, tail0, conv_w, conv_b, wq, wk, wv, norm_g, skip)


def _out_kernel(yh_ref, ym_ref, x_ref, gate_ref, w_ref, fg_ref, o_ref):
    acc = _dot(yh_ref[...], w_ref[0:D_MODEL, :]) + _dot(ym_ref[...], w_ref[D_MODEL:2 * D_MODEL, :])
    out = x_ref[...] + gate_ref[0] * acc
    ms = jnp.mean(out * out, axis=-1, keepdims=True)
    o_ref[...] = out * lax.rsqrt(ms + EPS) * fg_ref[...]


def _out_projection(yh, ym, x2d, gate3, w_out, final_g, rb):
    rows = x2d.shape[0]
    mrows = gate3.shape[1]
    nblk = rows // rb
    return pl.pallas_call(
        _out_kernel,
        grid=(nblk,),
        in_specs=[pl.BlockSpec((rb, D_MODEL), lambda i: (i, 0)),
                  pl.BlockSpec((rb, D_MODEL), lambda i: (i, 0)),
                  pl.BlockSpec((rb, D_MODEL), lambda i: (i, 0)),
                  pl.BlockSpec((1, mrows, D_MODEL), lambda i: (i * gate3.shape[0] // nblk, 0, 0)),
                  pl.BlockSpec((2 * D_MODEL, D_MODEL), lambda i: (0, 0)),
                  pl.BlockSpec((1, D_MODEL), lambda i: (0, 0))],
        out_specs=pl.BlockSpec((rb, D_MODEL), lambda i: (i, 0)),
        out_shape=jax.ShapeDtypeStruct((rows, D_MODEL), F32),
        compiler_params=pltpu.CompilerParams(vmem_limit_bytes=VMEM_LIMIT),
        name="out_projection",
    )(yh, ym, x2d, gate3, w_out, final_g)


def _block_diag_heads(w):
    groups = MLSTM_DH // QKV_BLOCK
    wh = w.reshape(MLSTM_HEADS, groups, QKV_BLOCK, QKV_BLOCK)
    eye = jnp.eye(groups, dtype=w.dtype)
    bd = jnp.einsum('hgij,gk->hgikj', wh, eye)
    return bd.reshape(MLSTM_HEADS, MLSTM_DH, MLSTM_DH).astype(BF16)


def _layer(x, mod, s0, c0, n0, m0, conv0, wts, *, L, TB, NB_H, NB_M, Tv, rb):
    (norm_g, w_pad, b_pad, lb_logits, hgrn_norm_g, conv_w, conv_b, wq, wk, wv, mnorm_g, skip, w_out, final_g) = wts
    B, T, _ = x.shape
    rows = B * T
    x2d = x.reshape(rows, D_MODEL)
    shift, scale, gate = mod[:, :D_MODEL], mod[:, D_MODEL:2 * D_MODEL], mod[:, 2 * D_MODEL:]
    if T >= rb:
        def per_block(a):
            return a.reshape(B, 1, D_MODEL)
    else:
        def per_block(a):
            return jnp.repeat(a, T, axis=0).reshape(rows // rb, rb, D_MODEL)
    proj = _in_projection(x2d, per_block(scale), per_block(shift), norm_g, w_pad, b_pad, rb)
    yh, s_new = _hgrn(proj, s0, lb_logits, hgrn_norm_g, B=B, T=T, TB=TB, NB=NB_H, L=L, Tv=Tv)
    tail0 = jnp.pad(conv0, ((0, 0), (SUBLANES - (CONV_WIDTH - 1), 0), (0, 0)))
    m0p = jnp.pad(m0, ((0, 0), (0, LANES - MLSTM_HEADS))).reshape(B, 1, LANES)
    ym, c_new, n_new, m_new = _mlstm(proj, c0, n0.reshape(B, 1, D_MODEL), m0p, tail0, conv_w, conv_b,
                                     wq, wk, wv, mnorm_g, skip, B=B, T=T, TB=TB, NB=NB_M, L=L, Tv=Tv)
    y = _out_projection(yh, ym, x2d, per_block(gate), w_out, final_g, rb)
    return (y.reshape(B, T, D_MODEL), proj.reshape(B, T, N_PROJ_PAD), s_new, c_new,
            n_new.reshape(B, MLSTM_HEADS, MLSTM_DH), m_new.reshape(B, LANES)[:, :MLSTM_HEADS])


def kernel(x_prompt, x_sample, c_prompt, c_sample, state_hgrn, state_mlstm_C, state_mlstm_n, state_mlstm_m, state_mlstm_conv, w_ada, b_ada, norm_g, w_in, b_in, hgrn_lb_logits, hgrn_norm_g, mlstm_conv_w, mlstm_conv_b, mlstm_wq, mlstm_wk, mlstm_wv, mlstm_norm_g, mlstm_skip, w_out, final_g):
    assert w_in.shape == (1, D_MODEL, N_PROJ) and hgrn_lb_logits.shape == (2, D_MODEL)
    Bp, Tp, _ = x_prompt.shape
    Bs, Ts, _ = x_sample.shape
    assert Tp % PROMPT_CHUNK == 0 and Ts <= SAMPLE_ROWS and Ts >= CONV_WIDTH - 1

    mod = _modulation(jnp.concatenate([c_prompt, c_sample], axis=0), w_ada[0], b_ada[0].reshape(1, -1))
    pad_cols = N_PROJ_PAD - N_PROJ
    wts = (norm_g[0].reshape(1, -1),
           jnp.pad(w_in[0], ((0, 0), (0, pad_cols))).astype(BF16),
           jnp.pad(b_in[0], (0, pad_cols)).reshape(1, -1),
           hgrn_lb_logits, hgrn_norm_g[0].reshape(1, -1),
           mlstm_conv_w[0], mlstm_conv_b[0].reshape(1, -1),
           _block_diag_heads(mlstm_wq[0]), _block_diag_heads(mlstm_wk[0]), _block_diag_heads(mlstm_wv[0]),
           mlstm_norm_g[0].reshape(1, -1), mlstm_skip[0].reshape(1, -1),
           w_out[0].astype(BF16), final_g.reshape(1, -1))

    yp, proj_p, hg_p, c_p, n_p, m_p = _layer(
        x_prompt, mod[:Bp],
        jnp.zeros((Bp, HGRN_HEADS, HGRN_D, HGRN_D), F32),
        jnp.zeros((Bp, MLSTM_HEADS, MLSTM_DH, MLSTM_DH), F32),
        jnp.zeros((Bp, MLSTM_HEADS, MLSTM_DH), F32),
        jnp.zeros((Bp, MLSTM_HEADS), F32),
        jnp.zeros((Bp, CONV_WIDTH - 1, D_MODEL), F32),
        wts, L=PROMPT_CHUNK, TB=256, NB_H=1, NB_M=1, Tv=PROMPT_CHUNK, rb=512)
    conv_p = proj_p[:, Tp - (CONV_WIDTH - 1):, 4 * D_MODEL:5 * D_MODEL]

    xs = jnp.pad(x_sample, ((0, 0), (0, SAMPLE_ROWS - Ts), (0, 0)))
    ys, proj_s, hg_s, c_s, n_s, m_s = _layer(
        xs, mod[Bp:], state_hgrn[0], state_mlstm_C[0], state_mlstm_n[0], state_mlstm_m[0],
        state_mlstm_conv[0], wts, L=SAMPLE_ROWS, TB=SAMPLE_ROWS, NB_H=8, NB_M=4, Tv=Ts, rb=512)
    conv_s = proj_s[:, Ts - (CONV_WIDTH - 1):Ts, 4 * D_MODEL:5 * D_MODEL]

    return (yp, ys[:, :Ts], hg_p[None], c_p[None], n_p[None], m_p[None], conv_p[None],
            hg_s[None], c_s[None], n_s[None], m_s[None], conv_s[None])
```

```python
import functools

import jax
import jax.numpy as jnp
from jax import lax
from jax.experimental import pallas as pl
from jax.experimental.pallas import tpu as pltpu

F32 = jnp.float32
BF16 = jnp.bfloat16

D_MODEL = 1024
HGRN_HEADS = 8
HGRN_D = 128
MLSTM_HEADS = 4
MLSTM_DH = 256
CONV_WIDTH = 4
QKV_BLOCK = 4
EPS = 1e-6
N_PROJ = 7176
N_PROJ_PAD = 7296
GATE_TILE = 56
N_COL_TILES = 3
COL_TILE = N_PROJ_PAD // N_COL_TILES
PROMPT_CHUNK = 64
PROMPT_BLOCK = 256
SAMPLE_ROWS = 8
SUBLANES = 8
LANES = 128
VMEM_LIMIT = 56 * 1024 * 1024
FAST_PATH_MIN_LOG_DECAY = -80.0


def _dot(a, b):
    return jnp.dot(a, b, preferred_element_type=F32)


def _dot_nt(a, b):
    return lax.dot_general(a, b, (((1,), (1,)), ((), ())), preferred_element_type=F32)


def _dot_tn(a, b):
    return lax.dot_general(a, b, (((0,), (0,)), ((), ())), preferred_element_type=F32)


def _sigmoid(x):
    return 0.5 * jnp.tanh(0.5 * x) + 0.5


def _levels(L):
    out, s = [], 1
    while s < L:
        out.append(s)
        s *= 2
    return out


def _seg_bcast(W, s, L, r):
    n = W.shape[1]
    if s == 1:
        return jnp.where((r & 1) != 0, pltpu.roll(W, 1, 0), W)
    if s == 2:
        m = r & 3
        return jnp.where(m == 0, pltpu.roll(W, L - 1, 0),
                         jnp.where(m == 1, W,
                                   jnp.where(m == 2, pltpu.roll(W, 1, 0), pltpu.roll(W, 2, 0))))
    pieces = [jnp.broadcast_to(W[b * 2 * s + s - 1:b * 2 * s + s, :], (2 * s, n))
              for b in range(L // (2 * s))]
    return pieces[0] if len(pieces) == 1 else jnp.concatenate(pieces, axis=0)


def _level_factors(x, L, r):
    W = x
    factors = []
    for s in _levels(L):
        Tb = _seg_bcast(W, s, L, r)
        sec = (r & s) != 0
        factors.append(jnp.exp(jnp.where(sec, W, Tb - W)))
        W = W + jnp.where(sec, Tb, 0.0)
    return factors


def _prefix8(x):
    sub = lax.broadcasted_iota(jnp.int32, (SUBLANES, 1), 0)
    y = x + jnp.where(sub >= 1, pltpu.roll(x, 1, 0), 0.0)
    y = y + jnp.where(sub >= 2, pltpu.roll(y, 2, 0), 0.0)
    return y + jnp.where(sub >= 4, pltpu.roll(y, 4, 0), 0.0)


def _chunk_cumsum(x, L):
    outs = []
    for c in range(x.shape[0] // L):
        total = None
        for g in range(L // SUBLANES):
            lo = c * L + g * SUBLANES
            p = _prefix8(x[lo:lo + SUBLANES, :])
            if total is not None:
                p = p + total
            outs.append(p)
            total = p[SUBLANES - 1:SUBLANES, :]
    return outs[0] if len(outs) == 1 else jnp.concatenate(outs, axis=0)


def _rows_bcast(x, row_in_chunk, L):
    n = x.shape[1]
    pieces = [jnp.broadcast_to(x[c * L + row_in_chunk:c * L + row_in_chunk + 1, :], (L, n))
              for c in range(x.shape[0] // L)]
    return pieces[0] if len(pieces) == 1 else jnp.concatenate(pieces, axis=0)


def _mod_kernel(c_ref, w_ref, b_ref, o_ref):
    c = c_ref[...]
    a = c * _sigmoid(c)
    o_ref[...] = _dot(a.astype(BF16), w_ref[...].astype(BF16)) + b_ref[...]


def _modulation(c_all, w_ada, b_ada):
    m = c_all.shape[0]
    n = w_ada.shape[1]
    tn = 512
    return pl.pallas_call(
        _mod_kernel,
        grid=(n // tn,),
        in_specs=[pl.BlockSpec((m, D_MODEL), lambda j: (0, 0)),
                  pl.BlockSpec((D_MODEL, tn), lambda j: (0, j)),
                  pl.BlockSpec((1, tn), lambda j: (0, j))],
        out_specs=pl.BlockSpec((m, tn), lambda j: (0, j)),
        out_shape=jax.ShapeDtypeStruct((m, n), F32),
        name="modulation",
    )(c_all, w_ada, b_ada)


def _inproj_kernel(x_ref, scale_ref, shift_ref, g_ref, w_ref, b_ref, o_ref, h_scr):
    @pl.when(pl.program_id(1) == 0)
    def _():
        x = x_ref[...]
        ms = jnp.mean(x * x, axis=-1, keepdims=True)
        y = x * lax.rsqrt(ms + EPS) * g_ref[...]
        h = y * (1.0 + scale_ref[0]) + shift_ref[0]
        h_scr[...] = h.astype(BF16)

    o_ref[...] = _dot(h_scr[...], w_ref[...]) + b_ref[...]


def _in_projection(x2d, scale3, shift3, norm_g, w_pad, b_pad, rb):
    rows = x2d.shape[0]
    mrows = scale3.shape[1]
    nblk = rows // rb
    return pl.pallas_call(
        _inproj_kernel,
        grid=(nblk, N_COL_TILES),
        in_specs=[pl.BlockSpec((rb, D_MODEL), lambda i, j: (i, 0)),
                  pl.BlockSpec((1, mrows, D_MODEL), lambda i, j: (i * scale3.shape[0] // nblk, 0, 0)),
                  pl.BlockSpec((1, mrows, D_MODEL), lambda i, j: (i * shift3.shape[0] // nblk, 0, 0)),
                  pl.BlockSpec((1, D_MODEL), lambda i, j: (0, 0)),
                  pl.BlockSpec((D_MODEL, COL_TILE), lambda i, j: (0, j)),
                  pl.BlockSpec((1, COL_TILE), lambda i, j: (0, j))],
        out_specs=pl.BlockSpec((rb, COL_TILE), lambda i, j: (i, j)),
        out_shape=jax.ShapeDtypeStruct((rows, N_PROJ_PAD), F32),
        scratch_shapes=[pltpu.VMEM((rb, D_MODEL), BF16)],
        compiler_params=pltpu.CompilerParams(vmem_limit_bytes=VMEM_LIMIT),
        name="in_projection",
    )(x2d, scale3, shift3, norm_g, w_pad, b_pad)


def _hgrn_kernel(q_ref, f_ref, i_ref, z_ref, s0_ref, lbl_ref, g_ref, y_ref, s_ref, st_scr, a_scr,
                 *, NB, TB, L, Tv):
    H = HGRN_HEADS
    rb = NB * TB
    spn = TB // L
    nseg = rb // L
    t = pl.program_id(1)

    @pl.when(t == 0)
    def _():
        for nb in range(NB):
            for h in range(H):
                st_scr[nb, h] = s0_ref[nb, h].T

    lg = lbl_ref[...]
    mx = jnp.max(lg, axis=0, keepdims=True)
    e = jnp.exp(lg - mx)
    lb = e[0:1, :] / jnp.sum(e, axis=0, keepdims=True)
    g_norm = g_ref[...]

    hq = q_ref[...]
    sig = _sigmoid(f_ref[...])
    logf = jnp.log(lb + (1.0 - lb) * sig)
    kk = (1.0 - lb) * (1.0 - sig)
    if Tv < L:
        valid = (lax.broadcasted_iota(jnp.int32, (rb, 1), 0) & (L - 1)) < Tv
        logf = jnp.where(valid, logf, 0.0)
        kk = jnp.where(valid, kk, 0.0)
    G = _chunk_cumsum(logf, L)
    GL = [G[s * L + L - 1:s * L + L, :] for s in range(nseg)]
    fast_ok = jnp.min(functools.reduce(jnp.minimum, GL)) >= FAST_PATH_MIN_LOG_DECAY

    ti = lax.broadcasted_iota(jnp.int32, (L, L), 0)
    ji = lax.broadcasted_iota(jnp.int32, (L, L), 1)

    @pl.when(fast_ok)
    def _():
        d = G - _rows_bcast(G, L // 2 - 1, L)
        qt = (hq * jnp.exp(d)).astype(BF16)
        kt = (kk * jnp.exp(-d)).astype(BF16)
        causal = ji <= ti
        for s in range(nseg):
            rows = slice(s * L, (s + 1) * L)
            for h in range(H):
                hs = slice(h * HGRN_D, (h + 1) * HGRN_D)
                a_scr[s * H + h] = jnp.where(causal, _dot_nt(qt[rows, hs], kt[rows, hs]), 0.0)

    @pl.when(jnp.logical_not(fast_ok))
    def _():
        r = lax.broadcasted_iota(jnp.int32, (L, 1), 0)
        xo = ti ^ ji
        diag_mask = ti == ji
        lvl_masks = [(ji < ti) & (xo >= s) & (xo < 2 * s) for s in _levels(L)]
        for s in range(nseg):
            rows = slice(s * L, (s + 1) * L)
            factors = _level_factors(logf[rows, :], L, r)
            for h in range(H):
                hs = slice(h * HGRN_D, (h + 1) * HGRN_D)
                qh = hq[rows, hs]
                kh = kk[rows, hs]
                A = jnp.where(diag_mask, _dot_nt(qh.astype(BF16), kh.astype(BF16)), 0.0)
                for lvl in range(len(factors)):
                    E = factors[lvl][:, hs]
                    A = A + jnp.where(lvl_masks[lvl],
                                      _dot_nt((qh * E).astype(BF16), (kh * E).astype(BF16)), 0.0)
                a_scr[s * H + h] = A

    qg = (hq * jnp.exp(G)).astype(BF16)
    GLb = GL[0] if nseg == 1 and L == rb else jnp.concatenate(
        [jnp.broadcast_to(gl, (L, D_MODEL)) for gl in GL], axis=0)
    kd = (kk * jnp.exp(GLb - G)).astype(BF16)
    vb = i_ref[...].astype(BF16)
    hz = z_ref[...]
    zgate = hz * _sigmoid(hz)
    for nb in range(NB):
        st = [st_scr[nb, h] for h in range(H)]
        for c in range(spn):
            s = nb * spn + c
            rows = slice(s * L, (s + 1) * L)
            dS = jnp.exp(GL[s])
            for h in range(H):
                hs = slice(h * HGRN_D, (h + 1) * HGRN_D)
                A = a_scr[s * H + h].astype(BF16)
                o = _dot(A, vb[rows, hs]) + _dot_nt(qg[rows, hs], st[h].astype(BF16))
                st[h] = st[h] * dS[:, hs] + _dot_tn(vb[rows, hs], kd[rows, hs])
                ms = jnp.mean(o * o, axis=-1, keepdims=True)
                y = o * lax.rsqrt(ms + EPS) * g_norm[:, hs] * zgate[rows, hs]
                y_ref[rows, hs] = y.astype(y_ref.dtype)
        for h in range(H):
            st_scr[nb, h] = st[h]

    @pl.when(t == pl.num_programs(1) - 1)
    def _():
        for nb in range(NB):
            for h in range(H):
                s_ref[nb, h] = st_scr[nb, h].T


def _hgrn(proj, s0, lb_logits, norm_g, *, B, T, TB, NB, L, Tv):
    nt = T // TB
    nbg = B // NB
    assert NB == 1 or nt == 1
    rb = NB * TB

    def col(c):
        return pl.BlockSpec((rb, D_MODEL), lambda bg, t: (bg * nt + t, c))

    kern = functools.partial(_hgrn_kernel, NB=NB, TB=TB, L=L, Tv=Tv)
    return pl.pallas_call(
        kern,
        grid=(nbg, nt),
        in_specs=[col(0), col(1), col(2), col(3),
                  pl.BlockSpec((NB, HGRN_HEADS, HGRN_D, HGRN_D), lambda bg, t: (bg, 0, 0, 0)),
                  pl.BlockSpec((2, D_MODEL), lambda bg, t: (0, 0)),
                  pl.BlockSpec((1, D_MODEL), lambda bg, t: (0, 0))],
        out_specs=[pl.BlockSpec((rb, D_MODEL), lambda bg, t: (bg * nt + t, 0)),
                   pl.BlockSpec((NB, HGRN_HEADS, HGRN_D, HGRN_D), lambda bg, t: (bg, 0, 0, 0))],
        out_shape=[jax.ShapeDtypeStruct((B * T, D_MODEL), BF16),
                   jax.ShapeDtypeStruct((B, HGRN_HEADS, HGRN_D, HGRN_D), F32)],
        scratch_shapes=[pltpu.VMEM((NB, HGRN_HEADS, HGRN_D, HGRN_D), F32),
                        pltpu.VMEM((rb // L * HGRN_HEADS, L, L), F32)],
        compiler_params=pltpu.CompilerParams(vmem_limit_bytes=VMEM_LIMIT),
        name="hgrn2",
    )(proj, proj, proj, proj, s0, lb_logits, norm_g)


def _mlstm_kernel(u_ref, z_ref, o_ref, gate_ref, c0_ref, n0_ref, m0_ref, tail0_ref,
                  cw_ref, cb_ref, wq_ref, wk_ref, wkt_ref, wv_ref, ng_ref, skip_ref,
                  y_ref, c_ref, n_ref, m_ref, tail_scr,
                  *, NB, TB, L, Tv):
    H = MLSTM_HEADS
    rb = NB * TB
    spn = TB // L
    t = pl.program_id(1)

    @pl.when(t == 0)
    def _():
        c_ref[...] = c0_ref[...]
        n_ref[...] = n0_ref[...]
        m_ref[...] = m0_ref[...]
        tail_scr[...] = tail0_ref[...]

    cw = cw_ref[...]
    ng = ng_ref[...]
    skip = skip_ref[...]
    lane = lax.broadcasted_iota(jnp.int32, (1, LANES), 1)
    ti = lax.broadcasted_iota(jnp.int32, (L, L), 0)
    ji = lax.broadcasted_iota(jnp.int32, (L, L), 1)
    causal = ji <= ti

    mu = u_ref[...]
    xcs = []
    for nb in range(NB):
        u_nb = mu[nb * TB:(nb + 1) * TB, :]
        ext = jnp.concatenate([tail_scr[nb], u_nb], axis=0)
        conv = cb_ref[...] + cw[CONV_WIDTH - 1:CONV_WIDTH, :] * u_nb
        for i in range(1, CONV_WIDTH):
            conv = conv + cw[CONV_WIDTH - 1 - i:CONV_WIDTH - i, :] * pltpu.roll(ext, i, 0)[SUBLANES:, :]
        xcs.append(conv * _sigmoid(conv))
        tail_scr[nb] = u_nb[TB - SUBLANES:, :]
    xc = xcs[0] if NB == 1 else jnp.concatenate(xcs, axis=0)
    xcb = xc.astype(BF16)
    mub = mu.astype(BF16)
    mz = z_ref[...]
    zgate = mz * _sigmoid(mz)
    ogate = _sigmoid(o_ref[...])

    gt = gate_ref[...]
    lf = jnp.minimum(gt, 0.0) - jnp.log1p(jnp.exp(-jnp.abs(gt)))
    ig = gt
    if Tv < L:
        valid = (lax.broadcasted_iota(jnp.int32, (rb, 1), 0) & (L - 1)) < Tv
        lf = jnp.where(valid, lf, 0.0)
        ig = jnp.where(valid, ig, -jnp.inf)
    bcum = _chunk_cumsum(lf, L)
    comb = jnp.where(lane < H, ig, bcum)
    rpad = -rb % LANES
    if rpad:
        comb = jnp.concatenate([comb, jnp.zeros((rpad, LANES), F32)], axis=0)
    combT = comb.T

    for h in range(H):
        hs = slice(h * MLSTM_DH, (h + 1) * MLSTM_DH)
        q_all = _dot(xcb[:, hs], wq_ref[h])
        kb_all = (_dot(xcb[:, hs], wk_ref[h]) * (MLSTM_DH ** -0.5)).astype(BF16)
        kT_all = _dot_nt(wkt_ref[h], xcb[:, hs]) * (MLSTM_DH ** -0.5)
        vb_all = _dot(mub[:, hs], wv_ref[h]).astype(BF16)
        qb_all = q_all.astype(BF16)
        for nb in range(NB):
            C = c_ref[nb, h]
            nh = n_ref[nb, :, hs]
            m_prev = m_ref[nb][:, h:h + 1]
            for c in range(spn):
                s = nb * spn + c
                rows = slice(s * L, (s + 1) * L)
                bcol = bcum[rows, H + h:H + h + 1]
                irow = combT[h:h + 1, rows]
                brow = combT[H + h:H + h + 1, rows]
                logD = jnp.where(causal, (bcol - brow) + irow, -jnp.inf)
                m_intra = jnp.max(logD, axis=-1, keepdims=True)
                sc = _dot_nt(qb_all[rows], kb_all[rows]) * jnp.exp(logD - m_intra)
                rs = jnp.sum(sc, axis=-1, keepdims=True)
                sv = _dot(sc.astype(BF16), vb_all[rows])
                m_loc = m_intra[L - 1:L, :]
                b_last = bcol[L - 1:L, :]
                wrow = jnp.exp((b_last - brow) + irow - m_loc)
                U = _dot((kT_all[:, rows] * wrow).astype(BF16), vb_all[rows])
                ks = _dot(jnp.broadcast_to(wrow, (SUBLANES, L)).astype(BF16), kb_all[rows])[0:1, :]
                m_inter = bcol + m_prev
                m_t = jnp.maximum(m_inter, m_intra)
                inter = jnp.exp(m_inter - m_t)
                scl = jnp.exp(m_intra - m_t)
                den = inter * jnp.sum(q_all[rows] * nh, axis=-1, keepdims=True) + scl * rs
                rden = 1.0 / jnp.maximum(jnp.abs(den), jnp.exp(-m_t))
                hh = (inter * rden) * _dot(qb_all[rows], C.astype(BF16)) + (scl * rden) * sv
                m_new = m_t[L - 1:L, :]
                dec = jnp.exp(b_last + m_prev - m_new)
                scu = jnp.exp(m_loc - m_new)
                C = dec * C + scu * U
                nh = dec * nh + scu * ks
                m_prev = m_new
                hm = ogate[rows, hs] * hh
                ms = jnp.mean(hm * hm, axis=-1, keepdims=True)
                y = (hm * lax.rsqrt(ms + EPS) * ng[:, hs] + skip[:, hs] * xc[rows, hs]) * zgate[rows, hs]
                y_ref[rows, hs] = y.astype(y_ref.dtype)
            c_ref[nb, h] = C
            n_ref[nb, :, hs] = nh
            m_ref[nb] = jnp.where(lane == h, m_prev, m_ref[nb])


def _mlstm(proj, c0, n0, m0, tail0, conv_w, conv_b, wq, wk, wkt, wv, norm_g, skip, *, B, T, TB, NB, L, Tv):
    nt = T // TB
    nbg = B // NB
    assert NB == 1 or nt == 1
    rb = NB * TB

    def col(c, w=D_MODEL):
        return pl.BlockSpec((rb, w), lambda bg, t: (bg * nt + t, c))

    def full(shape):
        return pl.BlockSpec(shape, lambda bg, t: (0,) * len(shape))

    c_spec = pl.BlockSpec((NB, MLSTM_HEADS, MLSTM_DH, MLSTM_DH), lambda bg, t: (bg, 0, 0, 0))
    n_spec = pl.BlockSpec((NB, 1, D_MODEL), lambda bg, t: (bg, 0, 0))
    m_spec = pl.BlockSpec((NB, 1, LANES), lambda bg, t: (bg, 0, 0))
    kern = functools.partial(_mlstm_kernel, NB=NB, TB=TB, L=L, Tv=Tv)
    return pl.pallas_call(
        kern,
        grid=(nbg, nt),
        in_specs=[col(4), col(5), col(6), col(GATE_TILE, LANES),
                  c_spec, n_spec, m_spec,
                  pl.BlockSpec((NB, SUBLANES, D_MODEL), lambda bg, t: (bg, 0, 0)),
                  full((CONV_WIDTH, D_MODEL)), full((1, D_MODEL)),
                  full((MLSTM_HEADS, MLSTM_DH, MLSTM_DH)), full((MLSTM_HEADS, MLSTM_DH, MLSTM_DH)),
                  full((MLSTM_HEADS, MLSTM_DH, MLSTM_DH)), full((MLSTM_HEADS, MLSTM_DH, MLSTM_DH)),
                  full((1, D_MODEL)), full((1, D_MODEL))],
        out_specs=[pl.BlockSpec((rb, D_MODEL), lambda bg, t: (bg * nt + t, 0)), c_spec, n_spec, m_spec],
        out_shape=[jax.ShapeDtypeStruct((B * T, D_MODEL), BF16),
                   jax.ShapeDtypeStruct((B, MLSTM_HEADS, MLSTM_DH, MLSTM_DH), F32),
                   jax.ShapeDtypeStruct((B, 1, D_MODEL), F32),
                   jax.ShapeDtypeStruct((B, 1, LANES), F32)],
        scratch_shapes=[pltpu.VMEM((NB, SUBLANES, D_MODEL), F32)],
        compiler_params=pltpu.CompilerParams(vmem_limit_bytes=VMEM_LIMIT),
        name="mlstm",
    )(proj, proj, proj, proj, c0, n0, m0, tail0, conv_w, conv_b, wq, wk, wkt, wv, norm_g, skip)


def _out_kernel(yh_ref, ym_ref, x_ref, gate_ref, w_ref, fg_ref, o_ref):
    acc = _dot(yh_ref[...], w_ref[0:D_MODEL, :]) + _dot(ym_ref[...], w_ref[D_MODEL:2 * D_MODEL, :])
    out = x_ref[...] + gate_ref[0] * acc
    ms = jnp.mean(out * out, axis=-1, keepdims=True)
    o_ref[...] = out * lax.rsqrt(ms + EPS) * fg_ref[...]


def _out_projection(yh, ym, x2d, gate3, w_out, final_g, rb):
    rows = x2d.shape[0]
    mrows = gate3.shape[1]
    nblk = rows // rb
    return pl.pallas_call(
        _out_kernel,
        grid=(nblk,),
        in_specs=[pl.BlockSpec((rb, D_MODEL), lambda i: (i, 0)),
                  pl.BlockSpec((rb, D_MODEL), lambda i: (i, 0)),
                  pl.BlockSpec((rb, D_MODEL), lambda i: (i, 0)),
                  pl.BlockSpec((1, mrows, D_MODEL), lambda i: (i * gate3.shape[0] // nblk, 0, 0)),
                  pl.BlockSpec((2 * D_MODEL, D_MODEL), lambda i: (0, 0)),
                  pl.BlockSpec((1, D_MODEL), lambda i: (0, 0))],
        out_specs=pl.BlockSpec((rb, D_MODEL), lambda i: (i, 0)),
        out_shape=jax.ShapeDtypeStruct((rows, D_MODEL), F32),
        compiler_params=pltpu.CompilerParams(vmem_limit_bytes=VMEM_LIMIT),
        name="out_projection",
    )(yh, ym, x2d, gate3, w_out, final_g)


def _block_diag_heads(w):
    rows = w.reshape(MLSTM_HEADS, MLSTM_DH, QKV_BLOCK)
    tiled = jnp.tile(rows, (1, 1, MLSTM_DH // QKV_BLOCK))
    rg = lax.broadcasted_iota(jnp.int32, (MLSTM_DH, MLSTM_DH), 0) // QKV_BLOCK
    cg = lax.broadcasted_iota(jnp.int32, (MLSTM_DH, MLSTM_DH), 1) // QKV_BLOCK
    return jnp.where(rg == cg, tiled, 0.0).astype(BF16)


def _layer(x, mod, s0, c0, n0, m0, conv0, wts, *, L_H, L_M, TB, NB_H, NB_M, Tv, rb):
    (norm_g, w_pad, b_pad, lb_logits, hgrn_norm_g, conv_w, conv_b, wq, wk, wkt, wv, mnorm_g, skip, w_out, final_g) = wts
    B, T, _ = x.shape
    rows = B * T
    x2d = x.reshape(rows, D_MODEL)
    shift, scale, gate = mod[:, :D_MODEL], mod[:, D_MODEL:2 * D_MODEL], mod[:, 2 * D_MODEL:]
    if T >= rb:
        def per_block(a):
            return a.reshape(B, 1, D_MODEL)
    else:
        def per_block(a):
            return jnp.repeat(a, T, axis=0).reshape(rows // rb, rb, D_MODEL)
    proj = _in_projection(x2d, per_block(scale), per_block(shift), norm_g, w_pad, b_pad, rb)
    yh, s_new = _hgrn(proj, s0, lb_logits, hgrn_norm_g, B=B, T=T, TB=TB, NB=NB_H, L=L_H, Tv=Tv)
    tail0 = jnp.pad(conv0, ((0, 0), (SUBLANES - (CONV_WIDTH - 1), 0), (0, 0)))
    m0p = jnp.pad(m0, ((0, 0), (0, LANES - MLSTM_HEADS))).reshape(B, 1, LANES)
    ym, c_new, n_new, m_new = _mlstm(proj, c0, n0.reshape(B, 1, D_MODEL), m0p, tail0, conv_w, conv_b,
                                     wq, wk, wkt, wv, mnorm_g, skip, B=B, T=T, TB=TB, NB=NB_M, L=L_M, Tv=Tv)
    y = _out_projection(yh, ym, x2d, per_block(gate), w_out, final_g, rb)
    return (y.reshape(B, T, D_MODEL), proj.reshape(B, T, N_PROJ_PAD), s_new, c_new,
            n_new.reshape(B, MLSTM_HEADS, MLSTM_DH), m_new.reshape(B, LANES)[:, :MLSTM_HEADS])


def kernel(x_prompt, x_sample, c_prompt, c_sample, state_hgrn, state_mlstm_C, state_mlstm_n, state_mlstm_m, state_mlstm_conv, w_ada, b_ada, norm_g, w_in, b_in, hgrn_lb_logits, hgrn_norm_g, mlstm_conv_w, mlstm_conv_b, mlstm_wq, mlstm_wk, mlstm_wv, mlstm_norm_g, mlstm_skip, w_out, final_g):
    assert w_in.shape == (1, D_MODEL, N_PROJ) and hgrn_lb_logits.shape == (2, D_MODEL)
    Bp, Tp, _ = x_prompt.shape
    Bs, Ts, _ = x_sample.shape
    assert Tp % PROMPT_CHUNK == 0 and Ts <= SAMPLE_ROWS and Ts >= CONV_WIDTH - 1

    mod = _modulation(jnp.concatenate([c_prompt, c_sample], axis=0), w_ada[0], b_ada[0].reshape(1, -1))
    pad_cols = N_PROJ_PAD - N_PROJ
    wts = (norm_g[0].reshape(1, -1),
           jnp.pad(w_in[0], ((0, 0), (0, pad_cols))).astype(BF16),
           jnp.pad(b_in[0], (0, pad_cols)).reshape(1, -1),
           hgrn_lb_logits, hgrn_norm_g[0].reshape(1, -1),
           mlstm_conv_w[0], mlstm_conv_b[0].reshape(1, -1),
           _block_diag_heads(mlstm_wq[0]), _block_diag_heads(mlstm_wk[0]),
           _block_diag_heads(jnp.swapaxes(mlstm_wk[0], -1, -2)), _block_diag_heads(mlstm_wv[0]),
           mlstm_norm_g[0].reshape(1, -1), mlstm_skip[0].reshape(1, -1),
           w_out[0].astype(BF16), final_g.reshape(1, -1))

    yp, proj_p, hg_p, c_p, n_p, m_p = _layer(
        x_prompt, mod[:Bp],
        jnp.zeros((Bp, HGRN_HEADS, HGRN_D, HGRN_D), F32),
        jnp.zeros((Bp, MLSTM_HEADS, MLSTM_DH, MLSTM_DH), F32),
        jnp.zeros((Bp, MLSTM_HEADS, MLSTM_DH), F32),
        jnp.zeros((Bp, MLSTM_HEADS), F32),
        jnp.zeros((Bp, CONV_WIDTH - 1, D_MODEL), F32),
        wts, L_H=PROMPT_CHUNK, L_M=PROMPT_BLOCK, TB=PROMPT_BLOCK, NB_H=1, NB_M=1, Tv=PROMPT_BLOCK, rb=1024)
    conv_p = proj_p[:, Tp - (CONV_WIDTH - 1):, 4 * D_MODEL:5 * D_MODEL]

    xs = jnp.pad(x_sample, ((0, 0), (0, SAMPLE_ROWS - Ts), (0, 0)))
    ys, proj_s, hg_s, c_s, n_s, m_s = _layer(
        xs, mod[Bp:], state_hgrn[0], state_mlstm_C[0], state_mlstm_n[0], state_mlstm_m[0],
        state_mlstm_conv[0], wts, L_H=SAMPLE_ROWS, L_M=SAMPLE_ROWS, TB=SAMPLE_ROWS, NB_H=8, NB_M=4, Tv=Ts,
        rb=Bs * SAMPLE_ROWS)
    conv_s = proj_s[:, Ts - (CONV_WIDTH - 1):Ts, 4 * D_MODEL:5 * D_MODEL]

    return (yp, ys[:, :Ts], hg_p[None], c_p[None], n_p[None], m_p[None], conv_p[None],
            hg_s[None], c_s[None], n_s[None], m_s[None], conv_s[None])
```

```python
import functools

import jax
import jax.numpy as jnp
from jax import lax
from jax.experimental import pallas as pl
from jax.experimental.pallas import tpu as pltpu

F32 = jnp.float32
BF16 = jnp.bfloat16

D_MODEL = 1024
HGRN_HEADS = 8
HGRN_D = 128
MLSTM_HEADS = 4
MLSTM_DH = 256
CONV_WIDTH = 4
QKV_BLOCK = 4
EPS = 1e-6
N_PROJ = 7176
N_PROJ_PAD = 7296
N_HGRN = 4 * D_MODEL
N_MLSTM_PAD = N_PROJ_PAD - N_HGRN
GATE_TILE = 56
N_COL_TILES = 3
COL_TILE = N_PROJ_PAD // N_COL_TILES
MXU_WIDTH = 256
PROMPT_CHUNK = 64
PROMPT_BLOCK = 256
SAMPLE_ROWS = 8
SUBLANES = 8
LANES = 128
VMEM_LIMIT = 56 * 1024 * 1024
FAST_PATH_MIN_LOG_DECAY = -80.0


def _dot(a, b):
    return jnp.dot(a, b, preferred_element_type=F32)


def _dot_nt(a, b):
    return lax.dot_general(a, b, (((1,), (1,)), ((), ())), preferred_element_type=F32)


def _dot_tn(a, b):
    return lax.dot_general(a, b, (((0,), (0,)), ((), ())), preferred_element_type=F32)


def _sigmoid(x):
    return 0.5 * jnp.tanh(0.5 * x) + 0.5


def _levels(L):
    out, s = [], 1
    while s < L:
        out.append(s)
        s *= 2
    return out


def _seg_bcast(W, s, L, r):
    n = W.shape[1]
    if s == 1:
        return jnp.where((r & 1) != 0, pltpu.roll(W, 1, 0), W)
    if s == 2:
        m = r & 3
        return jnp.where(m == 0, pltpu.roll(W, L - 1, 0),
                         jnp.where(m == 1, W,
                                   jnp.where(m == 2, pltpu.roll(W, 1, 0), pltpu.roll(W, 2, 0))))
    pieces = [jnp.broadcast_to(W[b * 2 * s + s - 1:b * 2 * s + s, :], (2 * s, n))
              for b in range(L // (2 * s))]
    return pieces[0] if len(pieces) == 1 else jnp.concatenate(pieces, axis=0)


def _level_factors(x, L, r):
    W = x
    factors = []
    for s in _levels(L):
        Tb = _seg_bcast(W, s, L, r)
        sec = (r & s) != 0
        factors.append(jnp.exp(jnp.where(sec, W, Tb - W)))
        W = W + jnp.where(sec, Tb, 0.0)
    return factors


def _prefix8(x):
    sub = lax.broadcasted_iota(jnp.int32, (SUBLANES, 1), 0)
    y = x + jnp.where(sub >= 1, pltpu.roll(x, 1, 0), 0.0)
    y = y + jnp.where(sub >= 2, pltpu.roll(y, 2, 0), 0.0)
    return y + jnp.where(sub >= 4, pltpu.roll(y, 4, 0), 0.0)


def _chunk_cumsum(x, L):
    outs = []
    for c in range(x.shape[0] // L):
        total = None
        for g in range(L // SUBLANES):
            lo = c * L + g * SUBLANES
            p = _prefix8(x[lo:lo + SUBLANES, :])
            if total is not None:
                p = p + total
            outs.append(p)
            total = p[SUBLANES - 1:SUBLANES, :]
    return outs[0] if len(outs) == 1 else jnp.concatenate(outs, axis=0)


def _rows_bcast(x, row_in_chunk, L):
    n = x.shape[1]
    pieces = [jnp.broadcast_to(x[c * L + row_in_chunk:c * L + row_in_chunk + 1, :], (L, n))
              for c in range(x.shape[0] // L)]
    return pieces[0] if len(pieces) == 1 else jnp.concatenate(pieces, axis=0)


def _rms_mod_bf16(x, scale, shift, g):
    ms = jnp.mean(x * x, axis=-1, keepdims=True)
    return ((x * lax.rsqrt(ms + EPS) * g) * (1.0 + scale) + shift).astype(BF16)


def _project_into(dst_ref, h, w_ref, b_ref):
    n = w_ref.shape[1]
    bounds = list(range(0, n - n % D_MODEL, D_MODEL)) or [0]
    for k, lo in enumerate(bounds):
        hi = n if k == len(bounds) - 1 else lo + D_MODEL
        dst_ref[:, lo:hi] = _dot(h, w_ref[:, lo:hi]) + b_ref[:, lo:hi]


def _mod_kernel(c_ref, w_ref, b_ref, o_ref):
    c = c_ref[...]
    a = c * _sigmoid(c)
    o_ref[...] = _dot(a.astype(BF16), w_ref[...].astype(BF16)) + b_ref[...]


def _modulation(c_all, w_ada, b_ada):
    m = c_all.shape[0]
    n = w_ada.shape[1]
    tn = 512
    return pl.pallas_call(
        _mod_kernel,
        grid=(n // tn,),
        in_specs=[pl.BlockSpec((m, D_MODEL), lambda j: (0, 0)),
                  pl.BlockSpec((D_MODEL, tn), lambda j: (0, j)),
                  pl.BlockSpec((1, tn), lambda j: (0, j))],
        out_specs=pl.BlockSpec((m, tn), lambda j: (0, j)),
        out_shape=jax.ShapeDtypeStruct((m, n), F32),
        name="modulation",
    )(c_all, w_ada, b_ada)


def _inproj_kernel(x_ref, scale_ref, shift_ref, g_ref, w_ref, b_ref, o_ref, h_scr):
    @pl.when(pl.program_id(1) == 0)
    def _():
        h_scr[...] = _rms_mod_bf16(x_ref[...], scale_ref[0], shift_ref[0], g_ref[...])

    o_ref[...] = _dot(h_scr[...], w_ref[...]) + b_ref[...]


def _in_projection(x2d, scale3, shift3, norm_g, w_pad, b_pad, rb):
    rows = x2d.shape[0]
    mrows = scale3.shape[1]
    nblk = rows // rb
    return pl.pallas_call(
        _inproj_kernel,
        grid=(nblk, N_COL_TILES),
        in_specs=[pl.BlockSpec((rb, D_MODEL), lambda i, j: (i, 0)),
                  pl.BlockSpec((1, mrows, D_MODEL), lambda i, j: (i * scale3.shape[0] // nblk, 0, 0)),
                  pl.BlockSpec((1, mrows, D_MODEL), lambda i, j: (i * shift3.shape[0] // nblk, 0, 0)),
                  pl.BlockSpec((1, D_MODEL), lambda i, j: (0, 0)),
                  pl.BlockSpec((D_MODEL, COL_TILE), lambda i, j: (0, j)),
                  pl.BlockSpec((1, COL_TILE), lambda i, j: (0, j))],
        out_specs=pl.BlockSpec((rb, COL_TILE), lambda i, j: (i, j)),
        out_shape=jax.ShapeDtypeStruct((rows, N_PROJ_PAD), F32),
        scratch_shapes=[pltpu.VMEM((rb, D_MODEL), BF16)],
        compiler_params=pltpu.CompilerParams(vmem_limit_bytes=VMEM_LIMIT),
        name="in_projection",
    )(x2d, scale3, shift3, norm_g, w_pad, b_pad)


class _SideWork:
    def __init__(self, pieces=()):
        self._pieces = list(pieces)

    def run(self, n=1):
        for _ in range(n):
            if self._pieces:
                self._pieces.pop(0)()

    def flush(self):
        self.run(len(self._pieces))


def _fused_projection_schedule(i, proj_a, proj_b, x0_ref, sc0_ref, sh0_ref, xn_ref, scn_ref, shn_ref,
                               ng_ref, w_ref, b_ref, step):
    g = ng_ref[...]
    n = w_ref.shape[1]

    @pl.when(i == 0)
    def _():
        _project_into(proj_a, _rms_mod_bf16(x0_ref[...], sc0_ref[0], sh0_ref[0], g), w_ref, b_ref)

    def run(cur, nxt):
        h = []

        def norm_piece():
            h.append(_rms_mod_bf16(xn_ref[...], scn_ref[0], shn_ref[0], g))

        def tile_piece(lo):
            hi = min(lo + MXU_WIDTH, n)

            def piece():
                nxt[:, lo:hi] = _dot(h[0], w_ref[:, lo:hi]) + b_ref[:, lo:hi]
            return piece

        step(cur, _SideWork([norm_piece] + [tile_piece(lo) for lo in range(0, n, MXU_WIDTH)]))

    parity = lax.rem(i, 2)
    pl.when(parity == 0)(lambda: run(proj_a, proj_b))
    pl.when(parity == 1)(lambda: run(proj_b, proj_a))


def _fused_projection_specs(n_blocks, nt, tb, n_cols):
    def nxt(i):
        return jnp.minimum(i + 1, n_blocks - 1)
    vec = (1, 1, D_MODEL)
    return [pl.BlockSpec((tb, D_MODEL), lambda i: (0, 0)),
            pl.BlockSpec(vec, lambda i: (0, 0, 0)), pl.BlockSpec(vec, lambda i: (0, 0, 0)),
            pl.BlockSpec((tb, D_MODEL), lambda i: (nxt(i), 0)),
            pl.BlockSpec(vec, lambda i: (nxt(i) // nt, 0, 0)), pl.BlockSpec(vec, lambda i: (nxt(i) // nt, 0, 0)),
            pl.BlockSpec((1, D_MODEL), lambda i: (0, 0)),
            pl.BlockSpec((D_MODEL, n_cols), lambda i: (0, 0)),
            pl.BlockSpec((1, n_cols), lambda i: (0, 0))]


def _hgrn_step(load, t, nt, s0_ref, lbl_ref, g_ref, y_ref, s_ref, st_scr, a_scr,
               *, NB, TB, L, Tv, side_work=None):
    H = HGRN_HEADS
    rb = NB * TB
    spn = TB // L
    nseg = rb // L

    @pl.when(t == 0)
    def _():
        for nb in range(NB):
            for h in range(H):
                st_scr[nb, h] = s0_ref[nb, h].T

    sw = side_work if side_work is not None else _SideWork()
    sw.run(1)

    lg = lbl_ref[...]
    mx = jnp.max(lg, axis=0, keepdims=True)
    e = jnp.exp(lg - mx)
    lb = e[0:1, :] / jnp.sum(e, axis=0, keepdims=True)
    g_norm = g_ref[...]

    hq = load(0)
    sw.run(2)
    sig = _sigmoid(load(1))
    sw.run(2)
    logf = jnp.log(lb + (1.0 - lb) * sig)
    sw.run(2)
    kk = (1.0 - lb) * (1.0 - sig)
    sw.run(2)
    if Tv < L:
        valid = (lax.broadcasted_iota(jnp.int32, (rb, 1), 0) & (L - 1)) < Tv
        logf = jnp.where(valid, logf, 0.0)
        kk = jnp.where(valid, kk, 0.0)
    G = _chunk_cumsum(logf, L)
    sw.run(3)
    GL = [G[s * L + L - 1:s * L + L, :] for s in range(nseg)]
    fast_ok = jnp.min(functools.reduce(jnp.minimum, GL)) >= FAST_PATH_MIN_LOG_DECAY

    ti = lax.broadcasted_iota(jnp.int32, (L, L), 0)
    ji = lax.broadcasted_iota(jnp.int32, (L, L), 1)

    @pl.when(fast_ok)
    def _():
        d = G - _rows_bcast(G, L // 2 - 1, L)
        qt = (hq * jnp.exp(d)).astype(BF16)
        kt = (kk * jnp.exp(-d)).astype(BF16)
        causal = ji <= ti
        for s in range(nseg):
            rows = slice(s * L, (s + 1) * L)
            for h in range(H):
                hs = slice(h * HGRN_D, (h + 1) * HGRN_D)
                a_scr[s * H + h] = jnp.where(causal, _dot_nt(qt[rows, hs], kt[rows, hs]), 0.0)

    @pl.when(jnp.logical_not(fast_ok))
    def _():
        r = lax.broadcasted_iota(jnp.int32, (L, 1), 0)
        xo = ti ^ ji
        diag_mask = ti == ji
        lvl_masks = [(ji < ti) & (xo >= s) & (xo < 2 * s) for s in _levels(L)]
        for s in range(nseg):
            rows = slice(s * L, (s + 1) * L)
            factors = _level_factors(logf[rows, :], L, r)
            for h in range(H):
                hs = slice(h * HGRN_D, (h + 1) * HGRN_D)
                qh = hq[rows, hs]
                kh = kk[rows, hs]
                A = jnp.where(diag_mask, _dot_nt(qh.astype(BF16), kh.astype(BF16)), 0.0)
                for lvl in range(len(factors)):
                    E = factors[lvl][:, hs]
                    A = A + jnp.where(lvl_masks[lvl],
                                      _dot_nt((qh * E).astype(BF16), (kh * E).astype(BF16)), 0.0)
                a_scr[s * H + h] = A

    qg = (hq * jnp.exp(G)).astype(BF16)
    sw.run(2)
    GLb = GL[0] if nseg == 1 and L == rb else jnp.concatenate(
        [jnp.broadcast_to(gl, (L, D_MODEL)) for gl in GL], axis=0)
    kd = (kk * jnp.exp(GLb - G)).astype(BF16)
    sw.run(2)
    vb = load(2).astype(BF16)
    hz = load(3)
    zgate = hz * _sigmoid(hz)
    sw.flush()
    for nb in range(NB):
        st = [st_scr[nb, h] for h in range(H)]
        for c in range(spn):
            s = nb * spn + c
            rows = slice(s * L, (s + 1) * L)
            dS = jnp.exp(GL[s])
            for h in range(H):
                hs = slice(h * HGRN_D, (h + 1) * HGRN_D)
                A = a_scr[s * H + h].astype(BF16)
                o = _dot(A, vb[rows, hs]) + _dot_nt(qg[rows, hs], st[h].astype(BF16))
                st[h] = st[h] * dS[:, hs] + _dot_tn(vb[rows, hs], kd[rows, hs])
                ms = jnp.mean(o * o, axis=-1, keepdims=True)
                y = o * lax.rsqrt(ms + EPS) * g_norm[:, hs] * zgate[rows, hs]
                y_ref[rows, hs] = y.astype(y_ref.dtype)
        for h in range(H):
            st_scr[nb, h] = st[h]

    @pl.when(t == nt - 1)
    def _():
        for nb in range(NB):
            for h in range(H):
                s_ref[nb, h] = st_scr[nb, h].T


def _hgrn_kernel(q_ref, f_ref, i_ref, z_ref, s0_ref, lbl_ref, g_ref, y_ref, s_ref, st_scr, a_scr,
                 *, nt, **static):
    cols = (q_ref, f_ref, i_ref, z_ref)
    _hgrn_step(lambda c: cols[c][...], lax.rem(pl.program_id(0), nt), nt,
               s0_ref, lbl_ref, g_ref, y_ref, s_ref, st_scr, a_scr, **static)


def _hgrn_fused_kernel(x0_ref, sc0_ref, sh0_ref, xn_ref, scn_ref, shn_ref, ng_ref, w_ref, b_ref,
                       s0_ref, lbl_ref, g_ref, y_ref, s_ref, st_scr, a_scr, proj_a, proj_b,
                       *, nt, **static):
    i = pl.program_id(0)

    def step(cur, side_work):
        _hgrn_step(lambda c: cur[:, c * D_MODEL:(c + 1) * D_MODEL], lax.rem(i, nt), nt,
                   s0_ref, lbl_ref, g_ref, y_ref, s_ref, st_scr, a_scr, side_work=side_work, **static)

    _fused_projection_schedule(i, proj_a, proj_b, x0_ref, sc0_ref, sh0_ref, xn_ref, scn_ref, shn_ref,
                               ng_ref, w_ref, b_ref, step)


def _hgrn_specs(B, T, TB, NB, L):
    nt = T // TB
    rb = NB * TB
    s_spec = pl.BlockSpec((NB, HGRN_HEADS, HGRN_D, HGRN_D), lambda i: (i // nt, 0, 0, 0))
    in_tail = [s_spec, pl.BlockSpec((2, D_MODEL), lambda i: (0, 0)), pl.BlockSpec((1, D_MODEL), lambda i: (0, 0))]
    out_specs = [pl.BlockSpec((rb, D_MODEL), lambda i: (i, 0)), s_spec]
    out_shape = [jax.ShapeDtypeStruct((B * T, D_MODEL), BF16),
                 jax.ShapeDtypeStruct((B, HGRN_HEADS, HGRN_D, HGRN_D), F32)]
    scratch = [pltpu.VMEM((NB, HGRN_HEADS, HGRN_D, HGRN_D), F32),
               pltpu.VMEM((rb // L * HGRN_HEADS, L, L), F32)]
    return nt, rb, in_tail, out_specs, out_shape, scratch


def _hgrn(proj, s0, lb_logits, norm_g, *, B, T, TB, NB, L, Tv):
    nt, rb, in_tail, out_specs, out_shape, scratch = _hgrn_specs(B, T, TB, NB, L)
    assert NB == 1 or nt == 1
    cols = [pl.BlockSpec((rb, D_MODEL), lambda i, c=c: (i, c)) for c in range(4)]
    return pl.pallas_call(
        functools.partial(_hgrn_kernel, nt=nt, NB=NB, TB=TB, L=L, Tv=Tv),
        grid=(B // NB * nt,),
        in_specs=cols + in_tail, out_specs=out_specs, out_shape=out_shape, scratch_shapes=scratch,
        compiler_params=pltpu.CompilerParams(vmem_limit_bytes=VMEM_LIMIT),
        name="hgrn2",
    )(proj, proj, proj, proj, s0, lb_logits, norm_g)


def _hgrn_fused(x2d, scale3, shift3, norm_g, w, b, s0, lb_logits, hgrn_norm_g, *, B, T, TB, L):
    nt, rb, in_tail, out_specs, out_shape, scratch = _hgrn_specs(B, T, TB, 1, L)
    n_blocks = B * nt
    return pl.pallas_call(
        functools.partial(_hgrn_fused_kernel, nt=nt, NB=1, TB=TB, L=L, Tv=L),
        grid=(n_blocks,),
        in_specs=_fused_projection_specs(n_blocks, nt, TB, N_HGRN) + in_tail,
        out_specs=out_specs, out_shape=out_shape,
        scratch_shapes=scratch + [pltpu.VMEM((TB, N_HGRN), F32), pltpu.VMEM((TB, N_HGRN), F32)],
        compiler_params=pltpu.CompilerParams(vmem_limit_bytes=VMEM_LIMIT),
        name="hgrn2_fused",
    )(x2d, scale3, shift3, x2d, scale3, shift3, norm_g, w, b, s0, lb_logits, hgrn_norm_g)


def _mlstm_step(load, t, nt, c0_ref, n0_ref, m0_ref, tail0_ref,
                cw_ref, cb_ref, wq_ref, wk_ref, wkt_ref, wv_ref, ng_ref, skip_ref,
                y_ref, c_ref, n_ref, m_ref, tail_scr, conv_ref,
                *, NB, TB, L, Tv, side_work=None):
    H = MLSTM_HEADS
    rb = NB * TB
    spn = TB // L
    if nt == 1:
        c_in, n_in, m_in, tail_in = c0_ref, n0_ref, m0_ref, tail0_ref
    else:
        c_in, n_in, m_in, tail_in = c_ref, n_ref, m_ref, tail_scr

        @pl.when(t == 0)
        def _():
            c_ref[...] = c0_ref[...]
            n_ref[...] = n0_ref[...]
            m_ref[...] = m0_ref[...]
            tail_scr[...] = tail0_ref[...]

    sw = side_work if side_work is not None else _SideWork()
    sw.run(1)

    cw = cw_ref[...]
    ng = ng_ref[...]
    skip = skip_ref[...]
    lane = lax.broadcasted_iota(jnp.int32, (1, LANES), 1)
    ti = lax.broadcasted_iota(jnp.int32, (L, L), 0)
    ji = lax.broadcasted_iota(jnp.int32, (L, L), 1)
    causal = ji <= ti

    mu = load(0)
    xcs = []
    for nb in range(NB):
        u_nb = mu[nb * TB:(nb + 1) * TB, :]
        ext = jnp.concatenate([tail_in[nb], u_nb], axis=0)
        conv = cb_ref[...] + cw[CONV_WIDTH - 1:CONV_WIDTH, :] * u_nb
        for i in range(1, CONV_WIDTH):
            conv = conv + cw[CONV_WIDTH - 1 - i:CONV_WIDTH - i, :] * pltpu.roll(ext, i, 0)[SUBLANES:, :]
            sw.run(1)
        xcs.append(conv * _sigmoid(conv))
        sw.run(1)
        if nt > 1:
            tail_scr[nb] = u_nb[TB - SUBLANES:, :]
    if conv_ref is not None:
        @pl.when(t == nt - 1)
        def _():
            for nb in range(NB):
                conv_ref[nb] = mu[(nb + 1) * TB - SUBLANES:(nb + 1) * TB, :]
    xc = xcs[0] if NB == 1 else jnp.concatenate(xcs, axis=0)
    xcb = xc.astype(BF16)
    mub = mu.astype(BF16)
    mz = load(1)
    zgate = mz * _sigmoid(mz)
    sw.run(2)
    ogate = _sigmoid(load(2))
    sw.run(2)

    gt = load(3)
    lf = jnp.minimum(gt, 0.0) - jnp.log1p(jnp.exp(-jnp.abs(gt)))
    ig = gt
    if Tv < L:
        valid = (lax.broadcasted_iota(jnp.int32, (rb, 1), 0) & (L - 1)) < Tv
        lf = jnp.where(valid, lf, 0.0)
        ig = jnp.where(valid, ig, -jnp.inf)
    bcum = _chunk_cumsum(lf, L)
    sw.run(1)
    comb = jnp.where(lane < H, ig, bcum)
    rpad = -rb % LANES
    if rpad:
        comb = jnp.concatenate([comb, jnp.zeros((rpad, LANES), F32)], axis=0)
    combT = comb.T

    for h in range(H):
        hs = slice(h * MLSTM_DH, (h + 1) * MLSTM_DH)
        q_all = _dot(xcb[:, hs], wq_ref[h])
        kb_all = (_dot(xcb[:, hs], wk_ref[h]) * (MLSTM_DH ** -0.5)).astype(BF16)
        kT_all = _dot_nt(wkt_ref[h], xcb[:, hs]) * (MLSTM_DH ** -0.5)
        vb_all = _dot(mub[:, hs], wv_ref[h]).astype(BF16)
        qb_all = q_all.astype(BF16)
        sw.run(1)
        for nb in range(NB):
            C = c_in[nb, h]
            nh = n_in[nb, :, hs]
            m_prev = m_in[nb][:, h:h + 1]
            for c in range(spn):
                s = nb * spn + c
                rows = slice(s * L, (s + 1) * L)
                bcol = bcum[rows, H + h:H + h + 1]
                irow = combT[h:h + 1, rows]
                brow = combT[H + h:H + h + 1, rows]
                logD = jnp.where(causal, (bcol - brow) + irow, -jnp.inf)
                m_intra = jnp.max(logD, axis=-1, keepdims=True)
                sc = _dot_nt(qb_all[rows], kb_all[rows]) * jnp.exp(logD - m_intra)
                rs = jnp.sum(sc, axis=-1, keepdims=True)
                sv = _dot(sc.astype(BF16), vb_all[rows])
                m_loc = m_intra[L - 1:L, :]
                b_last = bcol[L - 1:L, :]
                wrow = jnp.exp((b_last - brow) + irow - m_loc)
                U = _dot((kT_all[:, rows] * wrow).astype(BF16), vb_all[rows])
                ks = _dot(jnp.broadcast_to(wrow, (SUBLANES, L)).astype(BF16), kb_all[rows])[0:1, :]
                m_inter = bcol + m_prev
                m_t = jnp.maximum(m_inter, m_intra)
                inter = jnp.exp(m_inter - m_t)
                scl = jnp.exp(m_intra - m_t)
                den = inter * jnp.sum(q_all[rows] * nh, axis=-1, keepdims=True) + scl * rs
                rden = 1.0 / jnp.maximum(jnp.abs(den), jnp.exp(-m_t))
                hh = (inter * rden) * _dot(qb_all[rows], C.astype(BF16)) + (scl * rden) * sv
                m_new = m_t[L - 1:L, :]
                dec = jnp.exp(b_last + m_prev - m_new)
                scu = jnp.exp(m_loc - m_new)
                C = dec * C + scu * U
                nh = dec * nh + scu * ks
                m_prev = m_new
                hm = ogate[rows, hs] * hh
                ms = jnp.mean(hm * hm, axis=-1, keepdims=True)
                y = (hm * lax.rsqrt(ms + EPS) * ng[:, hs] + skip[:, hs] * xc[rows, hs]) * zgate[rows, hs]
                y_ref[rows, hs] = y.astype(y_ref.dtype)
            c_ref[nb, h] = C
            n_ref[nb, :, hs] = nh
            m_old = m_in[nb] if h == 0 else m_ref[nb]
            m_ref[nb] = jnp.where(lane == h, m_prev, m_old)
    sw.flush()


def _mlstm_kernel(u_ref, z_ref, o_ref, gate_ref, *rest, nt, **static):
    cols = (u_ref, z_ref, o_ref, gate_ref)
    ins, (y_ref, c_ref, n_ref, m_ref, tail_scr) = rest[:12], rest[12:]
    _mlstm_step(lambda c: cols[c][...], lax.rem(pl.program_id(0), nt), nt, *ins,
                y_ref, c_ref, n_ref, m_ref, tail_scr, None, **static)


def _mlstm_fused_kernel(x0_ref, sc0_ref, sh0_ref, xn_ref, scn_ref, shn_ref, ng_ref, w_ref, b_ref,
                        *rest, nt, **static):
    i = pl.program_id(0)
    ins, (y_ref, c_ref, n_ref, m_ref, conv_ref, tail_scr, proj_a, proj_b) = rest[:12], rest[12:]
    widths = (D_MODEL, D_MODEL, D_MODEL, LANES)

    def step(cur, side_work):
        _mlstm_step(lambda c: cur[:, c * D_MODEL:c * D_MODEL + widths[c]], lax.rem(i, nt), nt, *ins,
                    y_ref, c_ref, n_ref, m_ref, tail_scr, conv_ref, side_work=side_work, **static)

    _fused_projection_schedule(i, proj_a, proj_b, x0_ref, sc0_ref, sh0_ref, xn_ref, scn_ref, shn_ref,
                               ng_ref, w_ref, b_ref, step)


def _mlstm_specs(B, T, TB, NB):
    nt = T // TB
    rb = NB * TB

    def full(shape):
        return pl.BlockSpec(shape, lambda i: (0,) * len(shape))

    c_spec = pl.BlockSpec((NB, MLSTM_HEADS, MLSTM_DH, MLSTM_DH), lambda i: (i // nt, 0, 0, 0))
    n_spec = pl.BlockSpec((NB, 1, D_MODEL), lambda i: (i // nt, 0, 0))
    m_spec = pl.BlockSpec((NB, 1, LANES), lambda i: (i // nt, 0, 0))
    t_spec = pl.BlockSpec((NB, SUBLANES, D_MODEL), lambda i: (i // nt, 0, 0))
    head_w = full((MLSTM_HEADS, MLSTM_DH, MLSTM_DH))
    in_tail = [c_spec, n_spec, m_spec, t_spec, full((CONV_WIDTH, D_MODEL)), full((1, D_MODEL)),
               head_w, head_w, head_w, head_w, full((1, D_MODEL)), full((1, D_MODEL))]
    out_specs = [pl.BlockSpec((rb, D_MODEL), lambda i: (i, 0)), c_spec, n_spec, m_spec]
    out_shape = [jax.ShapeDtypeStruct((B * T, D_MODEL), BF16),
                 jax.ShapeDtypeStruct((B, MLSTM_HEADS, MLSTM_DH, MLSTM_DH), F32),
                 jax.ShapeDtypeStruct((B, 1, D_MODEL), F32),
                 jax.ShapeDtypeStruct((B, 1, LANES), F32)]
    scratch = [pltpu.VMEM((NB, SUBLANES, D_MODEL), F32)]
    return nt, rb, in_tail, out_specs, out_shape, scratch, t_spec


def _mlstm(proj, states_and_weights, *, B, T, TB, NB, L, Tv):
    nt, rb, in_tail, out_specs, out_shape, scratch, _ = _mlstm_specs(B, T, TB, NB)
    assert NB == 1 or nt == 1
    cols = [pl.BlockSpec((rb, D_MODEL), lambda i, c=c: (i, N_HGRN // D_MODEL + c)) for c in range(3)]
    cols.append(pl.BlockSpec((rb, LANES), lambda i: (i, GATE_TILE)))
    return pl.pallas_call(
        functools.partial(_mlstm_kernel, nt=nt, NB=NB, TB=TB, L=L, Tv=Tv),
        grid=(B // NB * nt,),
        in_specs=cols + in_tail, out_specs=out_specs, out_shape=out_shape, scratch_shapes=scratch,
        compiler_params=pltpu.CompilerParams(vmem_limit_bytes=VMEM_LIMIT),
        name="mlstm",
    )(proj, proj, proj, proj, *states_and_weights)


def _mlstm_fused(x2d, scale3, shift3, norm_g, w, b, states_and_weights, *, B, T, TB, L):
    nt, rb, in_tail, out_specs, out_shape, scratch, t_spec = _mlstm_specs(B, T, TB, 1)
    n_blocks = B * nt
    return pl.pallas_call(
        functools.partial(_mlstm_fused_kernel, nt=nt, NB=1, TB=TB, L=L, Tv=L),
        grid=(n_blocks,),
        in_specs=_fused_projection_specs(n_blocks, nt, TB, N_MLSTM_PAD) + in_tail,
        out_specs=out_specs + [t_spec],
        out_shape=out_shape + [jax.ShapeDtypeStruct((B, SUBLANES, D_MODEL), F32)],
        scratch_shapes=scratch + [pltpu.VMEM((TB, N_MLSTM_PAD), F32), pltpu.VMEM((TB, N_MLSTM_PAD), F32)],
        compiler_params=pltpu.CompilerParams(vmem_limit_bytes=VMEM_LIMIT),
        name="mlstm_fused",
    )(x2d, scale3, shift3, x2d, scale3, shift3, norm_g, w, b, *states_and_weights)


def _out_kernel(yh_ref, ym_ref, x_ref, gate_ref, w_ref, fg_ref, o_ref):
    acc = _dot(yh_ref[...], w_ref[0:D_MODEL, :]) + _dot(ym_ref[...], w_ref[D_MODEL:2 * D_MODEL, :])
    out = x_ref[...] + gate_ref[0] * acc
    ms = jnp.mean(out * out, axis=-1, keepdims=True)
    o_ref[...] = out * lax.rsqrt(ms + EPS) * fg_ref[...]


def _out_projection(yh, ym, x2d, gate3, w_out, final_g, rb):
    rows = x2d.shape[0]
    mrows = gate3.shape[1]
    nblk = rows // rb
    return pl.pallas_call(
        _out_kernel,
        grid=(nblk,),
        in_specs=[pl.BlockSpec((rb, D_MODEL), lambda i: (i, 0)),
                  pl.BlockSpec((rb, D_MODEL), lambda i: (i, 0)),
                  pl.BlockSpec((rb, D_MODEL), lambda i: (i, 0)),
                  pl.BlockSpec((1, mrows, D_MODEL), lambda i: (i * gate3.shape[0] // nblk, 0, 0)),
                  pl.BlockSpec((2 * D_MODEL, D_MODEL), lambda i: (0, 0)),
                  pl.BlockSpec((1, D_MODEL), lambda i: (0, 0))],
        out_specs=pl.BlockSpec((rb, D_MODEL), lambda i: (i, 0)),
        out_shape=jax.ShapeDtypeStruct((rows, D_MODEL), F32),
        compiler_params=pltpu.CompilerParams(vmem_limit_bytes=VMEM_LIMIT),
        name="out_projection",
    )(yh, ym, x2d, gate3, w_out, final_g)


def _block_diag_heads(w):
    rows = w.reshape(MLSTM_HEADS, MLSTM_DH, QKV_BLOCK)
    tiled = jnp.tile(rows, (1, 1, MLSTM_DH // QKV_BLOCK))
    rg = lax.broadcasted_iota(jnp.int32, (MLSTM_DH, MLSTM_DH), 0) // QKV_BLOCK
    cg = lax.broadcasted_iota(jnp.int32, (MLSTM_DH, MLSTM_DH), 1) // QKV_BLOCK
    return jnp.where(rg == cg, tiled, 0.0).astype(BF16)


def _mlstm_operands(c0, n0, m0, conv0, mw):
    B = c0.shape[0]
    tail0 = jnp.pad(conv0, ((0, 0), (SUBLANES - (CONV_WIDTH - 1), 0), (0, 0)))
    m0p = jnp.pad(m0, ((0, 0), (0, LANES - MLSTM_HEADS))).reshape(B, 1, LANES)
    return (c0, n0.reshape(B, 1, D_MODEL), m0p, tail0) + mw


def _unpack_mlstm_state(B, n_new, m_new):
    return n_new.reshape(B, MLSTM_HEADS, MLSTM_DH), m_new.reshape(B, LANES)[:, :MLSTM_HEADS]


def kernel(x_prompt, x_sample, c_prompt, c_sample, state_hgrn, state_mlstm_C, state_mlstm_n, state_mlstm_m, state_mlstm_conv, w_ada, b_ada, norm_g, w_in, b_in, hgrn_lb_logits, hgrn_norm_g, mlstm_conv_w, mlstm_conv_b, mlstm_wq, mlstm_wk, mlstm_wv, mlstm_norm_g, mlstm_skip, w_out, final_g):
    assert w_in.shape == (1, D_MODEL, N_PROJ) and hgrn_lb_logits.shape == (2, D_MODEL)
    Bp, Tp, _ = x_prompt.shape
    Bs, Ts, _ = x_sample.shape
    assert Tp % PROMPT_BLOCK == 0 and Ts <= SAMPLE_ROWS and Ts >= CONV_WIDTH - 1

    mod = _modulation(jnp.concatenate([c_prompt, c_sample], axis=0), w_ada[0], b_ada[0].reshape(1, -1))
    shift, scale, gate = mod[:, :D_MODEL], mod[:, D_MODEL:2 * D_MODEL], mod[:, 2 * D_MODEL:]
    pad_cols = N_PROJ_PAD - N_PROJ
    ng = norm_g[0].reshape(1, -1)
    w_pad = jnp.pad(w_in[0], ((0, 0), (0, pad_cols))).astype(BF16)
    b_pad = jnp.pad(b_in[0], (0, pad_cols)).reshape(1, -1)
    hg = hgrn_norm_g[0].reshape(1, -1)
    mw = (mlstm_conv_w[0], mlstm_conv_b[0].reshape(1, -1),
          _block_diag_heads(mlstm_wq[0]), _block_diag_heads(mlstm_wk[0]),
          _block_diag_heads(jnp.swapaxes(mlstm_wk[0], -1, -2)), _block_diag_heads(mlstm_wv[0]),
          mlstm_norm_g[0].reshape(1, -1), mlstm_skip[0].reshape(1, -1))
    wo = w_out[0].astype(BF16)
    fg = final_g.reshape(1, -1)

    xp2 = x_prompt.reshape(Bp * Tp, D_MODEL)
    per_seq = lambda a: a.reshape(-1, 1, D_MODEL)
    sc_p, sh_p = per_seq(scale[:Bp]), per_seq(shift[:Bp])
    yh_p, hg_p = _hgrn_fused(xp2, sc_p, sh_p, ng, w_pad[:, :N_HGRN], b_pad[:, :N_HGRN],
                             jnp.zeros((Bp, HGRN_HEADS, HGRN_D, HGRN_D), F32), hgrn_lb_logits, hg,
                             B=Bp, T=Tp, TB=PROMPT_BLOCK, L=PROMPT_CHUNK)
    ops_p = _mlstm_operands(jnp.zeros((Bp, MLSTM_HEADS, MLSTM_DH, MLSTM_DH), F32),
                            jnp.zeros((Bp, MLSTM_HEADS, MLSTM_DH), F32), jnp.zeros((Bp, MLSTM_HEADS), F32),
                            jnp.zeros((Bp, CONV_WIDTH - 1, D_MODEL), F32), mw)
    ym_p, c_p, n_p, m_p, tail_p = _mlstm_fused(xp2, sc_p, sh_p, ng, w_pad[:, N_HGRN:], b_pad[:, N_HGRN:], ops_p,
                                               B=Bp, T=Tp, TB=PROMPT_BLOCK, L=PROMPT_BLOCK)
    n_p, m_p = _unpack_mlstm_state(Bp, n_p, m_p)
    conv_p = tail_p[:, SUBLANES - (CONV_WIDTH - 1):]
    yp = _out_projection(yh_p, ym_p, xp2, per_seq(gate[:Bp]), wo, fg, 1024).reshape(Bp, Tp, D_MODEL)

    rows_s = Bs * SAMPLE_ROWS
    xs2 = jnp.pad(x_sample, ((0, 0), (0, SAMPLE_ROWS - Ts), (0, 0))).reshape(rows_s, D_MODEL)
    per_row = lambda a: jnp.repeat(a, SAMPLE_ROWS, axis=0).reshape(1, rows_s, D_MODEL)
    proj_s = _in_projection(xs2, per_row(scale[Bp:]), per_row(shift[Bp:]), ng, w_pad, b_pad, rows_s)
    yh_s, hg_s = _hgrn(proj_s, state_hgrn[0], hgrn_lb_logits, hg,
                       B=Bs, T=SAMPLE_ROWS, TB=SAMPLE_ROWS, NB=8, L=SAMPLE_ROWS, Tv=Ts)
    ops_s = _mlstm_operands(state_mlstm_C[0], state_mlstm_n[0], state_mlstm_m[0], state_mlstm_conv[0], mw)
    ym_s, c_s, n_s, m_s = _mlstm(proj_s, ops_s, B=Bs, T=SAMPLE_ROWS, TB=SAMPLE_ROWS, NB=4, L=SAMPLE_ROWS, Tv=Ts)
    n_s, m_s = _unpack_mlstm_state(Bs, n_s, m_s)
    conv_s = proj_s.reshape(Bs, SAMPLE_ROWS, N_PROJ_PAD)[:, Ts - (CONV_WIDTH - 1):Ts, N_HGRN:N_HGRN + D_MODEL]
    ys = _out_projection(yh_s, ym_s, xs2, per_row(gate[Bp:]), wo, fg, rows_s).reshape(Bs, SAMPLE_ROWS, D_MODEL)

    return (yp, ys[:, :Ts], hg_p[None], c_p[None], n_p[None], m_p[None], conv_p[None],
            hg_s[None], c_s[None], n_s[None], m_s[None], conv_s[None])
```

```python
import functools

import jax
import jax.numpy as jnp
from jax import lax
from jax.experimental import pallas as pl
from jax.experimental.pallas import tpu as pltpu

F32 = jnp.float32
BF16 = jnp.bfloat16

D_MODEL = 1024
HGRN_HEADS = 8
HGRN_D = 128
MLSTM_HEADS = 4
MLSTM_DH = 256
CONV_WIDTH = 4
QKV_BLOCK = 4
EPS = 1e-6
N_PROJ = 7176
N_PROJ_PAD = 7296
N_HGRN = 4 * D_MODEL
N_MLSTM_PAD = N_PROJ_PAD - N_HGRN
GATE_TILE = 56
N_COL_TILES = 3
COL_TILE = N_PROJ_PAD // N_COL_TILES
MXU_WIDTH = 256
PROMPT_CHUNK = 64
PROMPT_BLOCK = 256
SAMPLE_ROWS = 8
SUBLANES = 8
LANES = 128
VMEM_LIMIT = 56 * 1024 * 1024
FAST_PATH_MIN_LOG_DECAY = -80.0


def _dot(a, b):
    return jnp.dot(a, b, preferred_element_type=F32)


def _dot_nt(a, b):
    return lax.dot_general(a, b, (((1,), (1,)), ((), ())), preferred_element_type=F32)


def _dot_tn(a, b):
    return lax.dot_general(a, b, (((0,), (0,)), ((), ())), preferred_element_type=F32)


def _sigmoid(x):
    return 0.5 * jnp.tanh(0.5 * x) + 0.5


def _levels(L):
    out, s = [], 1
    while s < L:
        out.append(s)
        s *= 2
    return out


def _seg_bcast(W, s, L, r):
    n = W.shape[1]
    if s == 1:
        return jnp.where((r & 1) != 0, pltpu.roll(W, 1, 0), W)
    if s == 2:
        m = r & 3
        return jnp.where(m == 0, pltpu.roll(W, L - 1, 0),
                         jnp.where(m == 1, W,
                                   jnp.where(m == 2, pltpu.roll(W, 1, 0), pltpu.roll(W, 2, 0))))
    pieces = [jnp.broadcast_to(W[b * 2 * s + s - 1:b * 2 * s + s, :], (2 * s, n))
              for b in range(L // (2 * s))]
    return pieces[0] if len(pieces) == 1 else jnp.concatenate(pieces, axis=0)


def _level_factors(x, L, r):
    W = x
    factors = []
    for s in _levels(L):
        Tb = _seg_bcast(W, s, L, r)
        sec = (r & s) != 0
        factors.append(jnp.exp(jnp.where(sec, W, Tb - W)))
        W = W + jnp.where(sec, Tb, 0.0)
    return factors


def _prefix8(x):
    sub = lax.broadcasted_iota(jnp.int32, (SUBLANES, 1), 0)
    y = x + jnp.where(sub >= 1, pltpu.roll(x, 1, 0), 0.0)
    y = y + jnp.where(sub >= 2, pltpu.roll(y, 2, 0), 0.0)
    return y + jnp.where(sub >= 4, pltpu.roll(y, 4, 0), 0.0)


def _chunk_cumsum(x, L):
    outs = []
    for c in range(x.shape[0] // L):
        total = None
        for g in range(L // SUBLANES):
            lo = c * L + g * SUBLANES
            p = _prefix8(x[lo:lo + SUBLANES, :])
            if total is not None:
                p = p + total
            outs.append(p)
            total = p[SUBLANES - 1:SUBLANES, :]
    return outs[0] if len(outs) == 1 else jnp.concatenate(outs, axis=0)


def _rows_bcast(x, row_in_chunk, L):
    n = x.shape[1]
    pieces = [jnp.broadcast_to(x[c * L + row_in_chunk:c * L + row_in_chunk + 1, :], (L, n))
              for c in range(x.shape[0] // L)]
    return pieces[0] if len(pieces) == 1 else jnp.concatenate(pieces, axis=0)


def _rms_mod_bf16(x, scale, shift, g):
    ms = jnp.mean(x * x, axis=-1, keepdims=True)
    return ((x * lax.rsqrt(ms + EPS) * g) * (1.0 + scale) + shift).astype(BF16)


def _project_into(dst_ref, h, w_ref, b_ref):
    n = w_ref.shape[1]
    bounds = list(range(0, n - n % D_MODEL, D_MODEL)) or [0]
    for k, lo in enumerate(bounds):
        hi = n if k == len(bounds) - 1 else lo + D_MODEL
        dst_ref[:, lo:hi] = _dot(h, w_ref[:, lo:hi]) + b_ref[:, lo:hi]


def _mod_kernel(c_ref, w_ref, b_ref, o_ref):
    c = c_ref[...]
    a = c * _sigmoid(c)
    o_ref[...] = _dot(a.astype(BF16), w_ref[...].astype(BF16)) + b_ref[...]


def _modulation(c_all, w_ada, b_ada):
    m = c_all.shape[0]
    n = w_ada.shape[1]
    tn = 512
    return pl.pallas_call(
        _mod_kernel,
        grid=(n // tn,),
        in_specs=[pl.BlockSpec((m, D_MODEL), lambda j: (0, 0)),
                  pl.BlockSpec((D_MODEL, tn), lambda j: (0, j)),
                  pl.BlockSpec((1, tn), lambda j: (0, j))],
        out_specs=pl.BlockSpec((m, tn), lambda j: (0, j)),
        out_shape=jax.ShapeDtypeStruct((m, n), F32),
        name="modulation",
    )(c_all, w_ada, b_ada)


def _inproj_kernel(x_ref, scale_ref, shift_ref, g_ref, w_ref, b_ref, o_ref, h_scr):
    @pl.when(pl.program_id(1) == 0)
    def _():
        h_scr[...] = _rms_mod_bf16(x_ref[...], scale_ref[0], shift_ref[0], g_ref[...])

    o_ref[...] = _dot(h_scr[...], w_ref[...]) + b_ref[...]


def _in_projection(x2d, scale3, shift3, norm_g, w_pad, b_pad, rb):
    rows = x2d.shape[0]
    mrows = scale3.shape[1]
    nblk = rows // rb
    return pl.pallas_call(
        _inproj_kernel,
        grid=(nblk, N_COL_TILES),
        in_specs=[pl.BlockSpec((rb, D_MODEL), lambda i, j: (i, 0)),
                  pl.BlockSpec((1, mrows, D_MODEL), lambda i, j: (i * scale3.shape[0] // nblk, 0, 0)),
                  pl.BlockSpec((1, mrows, D_MODEL), lambda i, j: (i * shift3.shape[0] // nblk, 0, 0)),
                  pl.BlockSpec((1, D_MODEL), lambda i, j: (0, 0)),
                  pl.BlockSpec((D_MODEL, COL_TILE), lambda i, j: (0, j)),
                  pl.BlockSpec((1, COL_TILE), lambda i, j: (0, j))],
        out_specs=pl.BlockSpec((rb, COL_TILE), lambda i, j: (i, j)),
        out_shape=jax.ShapeDtypeStruct((rows, N_PROJ_PAD), F32),
        scratch_shapes=[pltpu.VMEM((rb, D_MODEL), BF16)],
        compiler_params=pltpu.CompilerParams(vmem_limit_bytes=VMEM_LIMIT),
        name="in_projection",
    )(x2d, scale3, shift3, norm_g, w_pad, b_pad)


class _SideWork:
    def __init__(self, pieces=()):
        self._pieces = list(pieces)

    def run(self, n=1):
        for _ in range(n):
            if self._pieces:
                self._pieces.pop(0)()

    def flush(self):
        self.run(len(self._pieces))


def _fused_projection_schedule(i, proj_a, proj_b, x0_ref, sc0_ref, sh0_ref, xn_ref, scn_ref, shn_ref,
                               ng_ref, w_ref, b_ref, step):
    g = ng_ref[...]
    n = w_ref.shape[1]

    @pl.when(i == 0)
    def _():
        _project_into(proj_a, _rms_mod_bf16(x0_ref[...], sc0_ref[0], sh0_ref[0], g), w_ref, b_ref)

    def run(cur, nxt):
        h = []

        def norm_piece():
            h.append(_rms_mod_bf16(xn_ref[...], scn_ref[0], shn_ref[0], g))

        def tile_piece(lo):
            hi = min(lo + MXU_WIDTH, n)

            def piece():
                nxt[:, lo:hi] = _dot(h[0], w_ref[:, lo:hi]) + b_ref[:, lo:hi]
            return piece

        step(cur, _SideWork([norm_piece] + [tile_piece(lo) for lo in range(0, n, MXU_WIDTH)]))

    parity = lax.rem(i, 2)
    pl.when(parity == 0)(lambda: run(proj_a, proj_b))
    pl.when(parity == 1)(lambda: run(proj_b, proj_a))


def _fused_projection_specs(n_blocks, nt, tb, n_cols):
    def nxt(i):
        return jnp.minimum(i + 1, n_blocks - 1)
    vec = (1, 1, D_MODEL)
    return [pl.BlockSpec((tb, D_MODEL), lambda i: (0, 0)),
            pl.BlockSpec(vec, lambda i: (0, 0, 0)), pl.BlockSpec(vec, lambda i: (0, 0, 0)),
            pl.BlockSpec((tb, D_MODEL), lambda i: (nxt(i), 0)),
            pl.BlockSpec(vec, lambda i: (nxt(i) // nt, 0, 0)), pl.BlockSpec(vec, lambda i: (nxt(i) // nt, 0, 0)),
            pl.BlockSpec((1, D_MODEL), lambda i: (0, 0)),
            pl.BlockSpec((D_MODEL, n_cols), lambda i: (0, 0)),
            pl.BlockSpec((1, n_cols), lambda i: (0, 0))]


def _hgrn_step(load, t, nt, s0_ref, lbl_ref, g_ref, y_ref, s_ref, st_scr, a_scr,
               *, NB, TB, L, Tv, side_work=None):
    H = HGRN_HEADS
    rb = NB * TB
    spn = TB // L
    nseg = rb // L

    @pl.when(t == 0)
    def _():
        for nb in range(NB):
            for h in range(H):
                st_scr[nb, h] = s0_ref[nb, h].T

    sw = side_work if side_work is not None else _SideWork()
    sw.run(1)

    lg = lbl_ref[...]
    mx = jnp.max(lg, axis=0, keepdims=True)
    e = jnp.exp(lg - mx)
    lb = e[0:1, :] / jnp.sum(e, axis=0, keepdims=True)
    g_norm = g_ref[...]

    hq = load(0)
    sw.run(2)
    c1 = 0.5 * (1.0 - lb)
    p = c1 * jnp.tanh(0.5 * load(1))
    sw.run(2)
    logf = jnp.log((lb + c1) + p)
    sw.run(2)
    kk = c1 - p
    sw.run(2)
    if Tv < L:
        valid = (lax.broadcasted_iota(jnp.int32, (rb, 1), 0) & (L - 1)) < Tv
        logf = jnp.where(valid, logf, 0.0)
        kk = jnp.where(valid, kk, 0.0)
    G = _chunk_cumsum(logf, L)
    sw.run(3)
    GL = [G[s * L + L - 1:s * L + L, :] for s in range(nseg)]
    fast_ok = jnp.min(functools.reduce(jnp.minimum, GL)) >= FAST_PATH_MIN_LOG_DECAY

    ti = lax.broadcasted_iota(jnp.int32, (L, L), 0)
    ji = lax.broadcasted_iota(jnp.int32, (L, L), 1)

    @pl.when(fast_ok)
    def _():
        d = G - _rows_bcast(G, L // 2 - 1, L)
        qt = (hq * jnp.exp(d)).astype(BF16)
        kt = (kk * jnp.exp(-d)).astype(BF16)
        causal = ji <= ti
        for s in range(nseg):
            rows = slice(s * L, (s + 1) * L)
            for h in range(H):
                hs = slice(h * HGRN_D, (h + 1) * HGRN_D)
                a_scr[s * H + h] = jnp.where(causal, _dot_nt(qt[rows, hs], kt[rows, hs]), 0.0)

    @pl.when(jnp.logical_not(fast_ok))
    def _():
        r = lax.broadcasted_iota(jnp.int32, (L, 1), 0)
        xo = ti ^ ji
        diag_mask = ti == ji
        lvl_masks = [(ji < ti) & (xo >= s) & (xo < 2 * s) for s in _levels(L)]
        for s in range(nseg):
            rows = slice(s * L, (s + 1) * L)
            factors = _level_factors(logf[rows, :], L, r)
            for h in range(H):
                hs = slice(h * HGRN_D, (h + 1) * HGRN_D)
                qh = hq[rows, hs]
                kh = kk[rows, hs]
                A = jnp.where(diag_mask, _dot_nt(qh.astype(BF16), kh.astype(BF16)), 0.0)
                for lvl in range(len(factors)):
                    E = factors[lvl][:, hs]
                    A = A + jnp.where(lvl_masks[lvl],
                                      _dot_nt((qh * E).astype(BF16), (kh * E).astype(BF16)), 0.0)
                a_scr[s * H + h] = A

    qg = (hq * jnp.exp(G)).astype(BF16)
    sw.run(2)
    GLb = GL[0] if nseg == 1 and L == rb else jnp.concatenate(
        [jnp.broadcast_to(gl, (L, D_MODEL)) for gl in GL], axis=0)
    kd = (kk * jnp.exp(GLb - G)).astype(BF16)
    sw.run(2)
    vb = load(2).astype(BF16)
    hz = load(3)
    zgate = hz * _sigmoid(hz)
    sw.flush()
    for nb in range(NB):
        st = [st_scr[nb, h] for h in range(H)]
        for c in range(spn):
            s = nb * spn + c
            rows = slice(s * L, (s + 1) * L)
            dS = jnp.exp(GL[s])
            for h in range(H):
                hs = slice(h * HGRN_D, (h + 1) * HGRN_D)
                A = a_scr[s * H + h].astype(BF16)
                o = _dot(A, vb[rows, hs]) + _dot_nt(qg[rows, hs], st[h].astype(BF16))
                st[h] = st[h] * dS[:, hs] + _dot_tn(vb[rows, hs], kd[rows, hs])
                ms = jnp.mean(o * o, axis=-1, keepdims=True)
                y = o * lax.rsqrt(ms + EPS) * g_norm[:, hs] * zgate[rows, hs]
                y_ref[rows, hs] = y.astype(y_ref.dtype)
        for h in range(H):
            st_scr[nb, h] = st[h]

    @pl.when(t == nt - 1)
    def _():
        for nb in range(NB):
            for h in range(H):
                s_ref[nb, h] = st_scr[nb, h].T


def _hgrn_kernel(q_ref, f_ref, i_ref, z_ref, s0_ref, lbl_ref, g_ref, y_ref, s_ref, st_scr, a_scr,
                 *, nt, **static):
    cols = (q_ref, f_ref, i_ref, z_ref)
    _hgrn_step(lambda c: cols[c][...], lax.rem(pl.program_id(0), nt), nt,
               s0_ref, lbl_ref, g_ref, y_ref, s_ref, st_scr, a_scr, **static)


def _hgrn_fused_kernel(x0_ref, sc0_ref, sh0_ref, xn_ref, scn_ref, shn_ref, ng_ref, w_ref, b_ref,
                       s0_ref, lbl_ref, g_ref, y_ref, s_ref, st_scr, a_scr, proj_a, proj_b,
                       *, nt, **static):
    i = pl.program_id(0)

    def step(cur, side_work):
        _hgrn_step(lambda c: cur[:, c * D_MODEL:(c + 1) * D_MODEL], lax.rem(i, nt), nt,
                   s0_ref, lbl_ref, g_ref, y_ref, s_ref, st_scr, a_scr, side_work=side_work, **static)

    _fused_projection_schedule(i, proj_a, proj_b, x0_ref, sc0_ref, sh0_ref, xn_ref, scn_ref, shn_ref,
                               ng_ref, w_ref, b_ref, step)


def _hgrn_specs(B, T, TB, NB, L):
    nt = T // TB
    rb = NB * TB
    s_spec = pl.BlockSpec((NB, HGRN_HEADS, HGRN_D, HGRN_D), lambda i: (i // nt, 0, 0, 0))
    in_tail = [s_spec, pl.BlockSpec((2, D_MODEL), lambda i: (0, 0)), pl.BlockSpec((1, D_MODEL), lambda i: (0, 0))]
    out_specs = [pl.BlockSpec((rb, D_MODEL), lambda i: (i, 0)), s_spec]
    out_shape = [jax.ShapeDtypeStruct((B * T, D_MODEL), BF16),
                 jax.ShapeDtypeStruct((B, HGRN_HEADS, HGRN_D, HGRN_D), F32)]
    scratch = [pltpu.VMEM((NB, HGRN_HEADS, HGRN_D, HGRN_D), F32),
               pltpu.VMEM((rb // L * HGRN_HEADS, L, L), F32)]
    return nt, rb, in_tail, out_specs, out_shape, scratch


def _hgrn(proj, s0, lb_logits, norm_g, *, B, T, TB, NB, L, Tv):
    nt, rb, in_tail, out_specs, out_shape, scratch = _hgrn_specs(B, T, TB, NB, L)
    assert NB == 1 or nt == 1
    cols = [pl.BlockSpec((rb, D_MODEL), lambda i, c=c: (i, c)) for c in range(4)]
    return pl.pallas_call(
        functools.partial(_hgrn_kernel, nt=nt, NB=NB, TB=TB, L=L, Tv=Tv),
        grid=(B // NB * nt,),
        in_specs=cols + in_tail, out_specs=out_specs, out_shape=out_shape, scratch_shapes=scratch,
        compiler_params=pltpu.CompilerParams(vmem_limit_bytes=VMEM_LIMIT),
        name="hgrn2",
    )(proj, proj, proj, proj, s0, lb_logits, norm_g)


def _hgrn_fused(x2d, scale3, shift3, norm_g, w, b, s0, lb_logits, hgrn_norm_g, *, B, T, TB, L):
    nt, rb, in_tail, out_specs, out_shape, scratch = _hgrn_specs(B, T, TB, 1, L)
    n_blocks = B * nt
    return pl.pallas_call(
        functools.partial(_hgrn_fused_kernel, nt=nt, NB=1, TB=TB, L=L, Tv=L),
        grid=(n_blocks,),
        in_specs=_fused_projection_specs(n_blocks, nt, TB, N_HGRN) + in_tail,
        out_specs=out_specs, out_shape=out_shape,
        scratch_shapes=scratch + [pltpu.VMEM((TB, N_HGRN), F32), pltpu.VMEM((TB, N_HGRN), F32)],
        compiler_params=pltpu.CompilerParams(vmem_limit_bytes=VMEM_LIMIT),
        name="hgrn2_fused",
    )(x2d, scale3, shift3, x2d, scale3, shift3, norm_g, w, b, s0, lb_logits, hgrn_norm_g)


def _mlstm_step(load, t, nt, c0_ref, n0_ref, m0_ref, tail0_ref,
                cw_ref, cb_ref, wq_ref, wk_ref, wkt_ref, wv_ref, ng_ref, skip_ref,
                y_ref, c_ref, n_ref, m_ref, tail_scr, conv_ref,
                *, NB, TB, L, Tv, side_work=None):
    H = MLSTM_HEADS
    rb = NB * TB
    spn = TB // L
    if nt == 1:
        c_in, n_in, m_in, tail_in = c0_ref, n0_ref, m0_ref, tail0_ref
    else:
        c_in, n_in, m_in, tail_in = c_ref, n_ref, m_ref, tail_scr

        @pl.when(t == 0)
        def _():
            c_ref[...] = c0_ref[...]
            n_ref[...] = n0_ref[...]
            m_ref[...] = m0_ref[...]
            tail_scr[...] = tail0_ref[...]

    sw = side_work if side_work is not None else _SideWork()
    sw.run(1)

    cw = cw_ref[...]
    ng = ng_ref[...]
    skip = skip_ref[...]
    lane = lax.broadcasted_iota(jnp.int32, (1, LANES), 1)
    ti = lax.broadcasted_iota(jnp.int32, (L, L), 0)
    ji = lax.broadcasted_iota(jnp.int32, (L, L), 1)
    causal = ji <= ti

    mu = load(0)
    xcs = []
    for nb in range(NB):
        u_nb = mu[nb * TB:(nb + 1) * TB, :]
        ext = jnp.concatenate([tail_in[nb], u_nb], axis=0)
        conv = cb_ref[...] + cw[CONV_WIDTH - 1:CONV_WIDTH, :] * u_nb
        for i in range(1, CONV_WIDTH):
            conv = conv + cw[CONV_WIDTH - 1 - i:CONV_WIDTH - i, :] * pltpu.roll(ext, i, 0)[SUBLANES:, :]
            sw.run(1)
        xcs.append(conv * _sigmoid(conv))
        sw.run(1)
        if nt > 1:
            tail_scr[nb] = u_nb[TB - SUBLANES:, :]
    if conv_ref is not None:
        @pl.when(t == nt - 1)
        def _():
            for nb in range(NB):
                conv_ref[nb] = mu[(nb + 1) * TB - SUBLANES:(nb + 1) * TB, :]
    xc = xcs[0] if NB == 1 else jnp.concatenate(xcs, axis=0)
    xcb = xc.astype(BF16)
    mub = mu.astype(BF16)
    mz = load(1)
    zgate = mz * _sigmoid(mz)
    sw.run(2)
    ogate = _sigmoid(load(2))
    sw.run(2)

    gt = load(3)
    lf = jnp.minimum(gt, 0.0) - jnp.log1p(jnp.exp(-jnp.abs(gt)))
    ig = gt
    if Tv < L:
        valid = (lax.broadcasted_iota(jnp.int32, (rb, 1), 0) & (L - 1)) < Tv
        lf = jnp.where(valid, lf, 0.0)
        ig = jnp.where(valid, ig, -jnp.inf)
    bcum = _chunk_cumsum(lf, L)
    sw.run(1)
    comb = jnp.where(lane < H, ig, bcum)
    rpad = -rb % LANES
    if rpad:
        comb = jnp.concatenate([comb, jnp.zeros((rpad, LANES), F32)], axis=0)
    combT = comb.T

    assert spn == 1
    heads = {}

    def project_head(h):
        hs = slice(h * MLSTM_DH, (h + 1) * MLSTM_DH)
        q_all = _dot(xcb[:, hs], wq_ref[h])
        kb_all = (_dot(xcb[:, hs], wk_ref[h]) * (MLSTM_DH ** -0.5)).astype(BF16)
        kT_all = _dot_nt(wkt_ref[h], xcb[:, hs]) * (MLSTM_DH ** -0.5)
        vb_all = _dot(mub[:, hs], wv_ref[h]).astype(BF16)
        heads[h] = (hs, q_all, q_all.astype(BF16), kb_all, kT_all, vb_all)
        sw.run(1)

    pairs = [(h, nb) for h in range(H) for nb in range(NB)]
    groups = [pairs] if NB > 1 else [[p] for p in pairs]
    rows_of = lambda nb: slice(nb * L, (nb + 1) * L)

    def run_group(pairs):
        for h in sorted({h for h, _ in pairs}):
            project_head(h)

        gates = []
        for h, nb in pairs:
            rows = rows_of(nb)
            bcol = bcum[rows, H + h:H + h + 1]
            irow = combT[h:h + 1, rows]
            brow = combT[H + h:H + h + 1, rows]
            logD = jnp.where(causal, (bcol - brow) + irow, -jnp.inf)
            m_intra = jnp.max(logD, axis=-1, keepdims=True)
            gates.append((bcol, irow, brow, m_intra, jnp.exp(logD - m_intra)))

        scores = []
        for (h, nb), g in zip(pairs, gates):
            _, _, qb_all, kb_all, _, _ = heads[h]
            rows = rows_of(nb)
            scores.append(_dot_nt(qb_all[rows], kb_all[rows]) * g[4])

        intra = []
        for (h, nb), sc in zip(pairs, scores):
            vb_all = heads[h][5]
            intra.append((jnp.sum(sc, axis=-1, keepdims=True), _dot(sc.astype(BF16), vb_all[rows_of(nb)])))

        writes = []
        for (h, nb), g in zip(pairs, gates):
            _, _, _, kb_all, kT_all, vb_all = heads[h]
            rows = rows_of(nb)
            bcol, irow, brow, m_intra, _ = g
            m_loc = m_intra[L - 1:L, :]
            b_last = bcol[L - 1:L, :]
            wrow = jnp.exp((b_last - brow) + irow - m_loc)
            U = _dot((kT_all[:, rows] * wrow).astype(BF16), vb_all[rows])
            ks = _dot(jnp.broadcast_to(wrow, (SUBLANES, L)).astype(BF16), kb_all[rows])[0:1, :]
            writes.append((m_loc, b_last, U, ks))

        outs = []
        for (h, nb), g, (rs, sv), (m_loc, b_last, U, ks) in zip(pairs, gates, intra, writes):
            hs, q_all, qb_all, _, _, _ = heads[h]
            rows = rows_of(nb)
            bcol, _, _, m_intra, _ = g
            C = c_in[nb, h]
            nh = n_in[nb, :, hs]
            m_prev = m_in[nb][:, h:h + 1]
            m_inter = bcol + m_prev
            m_t = jnp.maximum(m_inter, m_intra)
            inter = jnp.exp(m_inter - m_t)
            scl = jnp.exp(m_intra - m_t)
            den = inter * jnp.sum(q_all[rows] * nh, axis=-1, keepdims=True) + scl * rs
            rden = 1.0 / jnp.maximum(jnp.abs(den), jnp.exp(-m_t))
            hh = (inter * rden) * _dot(qb_all[rows], C.astype(BF16)) + (scl * rden) * sv
            m_new = m_t[L - 1:L, :]
            dec = jnp.exp(b_last + m_prev - m_new)
            scu = jnp.exp(m_loc - m_new)
            c_ref[nb, h] = dec * C + scu * U
            n_ref[nb, :, hs] = dec * nh + scu * ks
            outs.append((hh, m_new))

        for (h, nb), (hh, _) in zip(pairs, outs):
            hs = heads[h][0]
            rows = rows_of(nb)
            hm = ogate[rows, hs] * hh
            ms = jnp.mean(hm * hm, axis=-1, keepdims=True)
            y = (hm * lax.rsqrt(ms + EPS) * ng[:, hs] + skip[:, hs] * xc[rows, hs]) * zgate[rows, hs]
            y_ref[rows, hs] = y.astype(y_ref.dtype)
        return outs

    outs = [o for grp in groups for o in run_group(grp)]

    for nb in range(NB):
        m_row = m_in[nb]
        for h in range(H):
            m_row = jnp.where(lane == h, outs[h * NB + nb][1], m_row)
        m_ref[nb] = m_row
    sw.flush()


def _mlstm_kernel(u_ref, z_ref, o_ref, gate_ref, *rest, nt, **static):
    cols = (u_ref, z_ref, o_ref, gate_ref)
    ins, (y_ref, c_ref, n_ref, m_ref, tail_scr) = rest[:12], rest[12:]
    _mlstm_step(lambda c: cols[c][...], lax.rem(pl.program_id(0), nt), nt, *ins,
                y_ref, c_ref, n_ref, m_ref, tail_scr, None, **static)


def _mlstm_fused_kernel(x0_ref, sc0_ref, sh0_ref, xn_ref, scn_ref, shn_ref, ng_ref, w_ref, b_ref,
                        *rest, nt, **static):
    i = pl.program_id(0)
    ins, (y_ref, c_ref, n_ref, m_ref, conv_ref, tail_scr, proj_a, proj_b) = rest[:12], rest[12:]
    widths = (D_MODEL, D_MODEL, D_MODEL, LANES)

    def step(cur, side_work):
        _mlstm_step(lambda c: cur[:, c * D_MODEL:c * D_MODEL + widths[c]], lax.rem(i, nt), nt, *ins,
                    y_ref, c_ref, n_ref, m_ref, tail_scr, conv_ref, side_work=side_work, **static)

    _fused_projection_schedule(i, proj_a, proj_b, x0_ref, sc0_ref, sh0_ref, xn_ref, scn_ref, shn_ref,
                               ng_ref, w_ref, b_ref, step)


def _mlstm_specs(B, T, TB, NB):
    nt = T // TB
    rb = NB * TB

    def full(shape):
        return pl.BlockSpec(shape, lambda i: (0,) * len(shape))

    c_spec = pl.BlockSpec((NB, MLSTM_HEADS, MLSTM_DH, MLSTM_DH), lambda i: (i // nt, 0, 0, 0))
    n_spec = pl.BlockSpec((NB, 1, D_MODEL), lambda i: (i // nt, 0, 0))
    m_spec = pl.BlockSpec((NB, 1, LANES), lambda i: (i // nt, 0, 0))
    t_spec = pl.BlockSpec((NB, SUBLANES, D_MODEL), lambda i: (i // nt, 0, 0))
    head_w = full((MLSTM_HEADS, MLSTM_DH, MLSTM_DH))
    in_tail = [c_spec, n_spec, m_spec, t_spec, full((CONV_WIDTH, D_MODEL)), full((1, D_MODEL)),
               head_w, head_w, head_w, head_w, full((1, D_MODEL)), full((1, D_MODEL))]
    out_specs = [pl.BlockSpec((rb, D_MODEL), lambda i: (i, 0)), c_spec, n_spec, m_spec]
    out_shape = [jax.ShapeDtypeStruct((B * T, D_MODEL), BF16),
                 jax.ShapeDtypeStruct((B, MLSTM_HEADS, MLSTM_DH, MLSTM_DH), F32),
                 jax.ShapeDtypeStruct((B, 1, D_MODEL), F32),
                 jax.ShapeDtypeStruct((B, 1, LANES), F32)]
    scratch = [pltpu.VMEM((NB, SUBLANES, D_MODEL), F32)]
    return nt, rb, in_tail, out_specs, out_shape, scratch, t_spec


def _mlstm(proj, states_and_weights, *, B, T, TB, NB, L, Tv):
    nt, rb, in_tail, out_specs, out_shape, scratch, _ = _mlstm_specs(B, T, TB, NB)
    assert NB == 1 or nt == 1
    cols = [pl.BlockSpec((rb, D_MODEL), lambda i, c=c: (i, N_HGRN // D_MODEL + c)) for c in range(3)]
    cols.append(pl.BlockSpec((rb, LANES), lambda i: (i, GATE_TILE)))
    return pl.pallas_call(
        functools.partial(_mlstm_kernel, nt=nt, NB=NB, TB=TB, L=L, Tv=Tv),
        grid=(B // NB * nt,),
        in_specs=cols + in_tail, out_specs=out_specs, out_shape=out_shape, scratch_shapes=scratch,
        compiler_params=pltpu.CompilerParams(vmem_limit_bytes=VMEM_LIMIT),
        name="mlstm",
    )(proj, proj, proj, proj, *states_and_weights)


def _mlstm_fused(x2d, scale3, shift3, norm_g, w, b, states_and_weights, *, B, T, TB, L):
    nt, rb, in_tail, out_specs, out_shape, scratch, t_spec = _mlstm_specs(B, T, TB, 1)
    n_blocks = B * nt
    return pl.pallas_call(
        functools.partial(_mlstm_fused_kernel, nt=nt, NB=1, TB=TB, L=L, Tv=L),
        grid=(n_blocks,),
        in_specs=_fused_projection_specs(n_blocks, nt, TB, N_MLSTM_PAD) + in_tail,
        out_specs=out_specs + [t_spec],
        out_shape=out_shape + [jax.ShapeDtypeStruct((B, SUBLANES, D_MODEL), F32)],
        scratch_shapes=scratch + [pltpu.VMEM((TB, N_MLSTM_PAD), F32), pltpu.VMEM((TB, N_MLSTM_PAD), F32)],
        compiler_params=pltpu.CompilerParams(vmem_limit_bytes=VMEM_LIMIT),
        name="mlstm_fused",
    )(x2d, scale3, shift3, x2d, scale3, shift3, norm_g, w, b, *states_and_weights)


def _out_kernel(yh_ref, ym_ref, x_ref, gate_ref, w_ref, fg_ref, o_ref):
    acc = _dot(yh_ref[...], w_ref[0:D_MODEL, :]) + _dot(ym_ref[...], w_ref[D_MODEL:2 * D_MODEL, :])
    out = x_ref[...] + gate_ref[0] * acc
    ms = jnp.mean(out * out, axis=-1, keepdims=True)
    o_ref[...] = out * lax.rsqrt(ms + EPS) * fg_ref[...]


def _out_projection(yh, ym, x2d, gate3, w_out, final_g, rb):
    rows = x2d.shape[0]
    mrows = gate3.shape[1]
    nblk = rows // rb
    return pl.pallas_call(
        _out_kernel,
        grid=(nblk,),
        in_specs=[pl.BlockSpec((rb, D_MODEL), lambda i: (i, 0)),
                  pl.BlockSpec((rb, D_MODEL), lambda i: (i, 0)),
                  pl.BlockSpec((rb, D_MODEL), lambda i: (i, 0)),
                  pl.BlockSpec((1, mrows, D_MODEL), lambda i: (i * gate3.shape[0] // nblk, 0, 0)),
                  pl.BlockSpec((2 * D_MODEL, D_MODEL), lambda i: (0, 0)),
                  pl.BlockSpec((1, D_MODEL), lambda i: (0, 0))],
        out_specs=pl.BlockSpec((rb, D_MODEL), lambda i: (i, 0)),
        out_shape=jax.ShapeDtypeStruct((rows, D_MODEL), F32),
        compiler_params=pltpu.CompilerParams(vmem_limit_bytes=VMEM_LIMIT),
        name="out_projection",
    )(yh, ym, x2d, gate3, w_out, final_g)


def _block_diag_heads(w):
    rows = w.reshape(MLSTM_HEADS, MLSTM_DH, QKV_BLOCK)
    tiled = jnp.tile(rows, (1, 1, MLSTM_DH // QKV_BLOCK))
    rg = lax.broadcasted_iota(jnp.int32, (MLSTM_DH, MLSTM_DH), 0) // QKV_BLOCK
    cg = lax.broadcasted_iota(jnp.int32, (MLSTM_DH, MLSTM_DH), 1) // QKV_BLOCK
    return jnp.where(rg == cg, tiled, 0.0).astype(BF16)


def _mlstm_operands(c0, n0, m0, conv0, mw):
    B = c0.shape[0]
    tail0 = jnp.pad(conv0, ((0, 0), (SUBLANES - (CONV_WIDTH - 1), 0), (0, 0)))
    m0p = jnp.pad(m0, ((0, 0), (0, LANES - MLSTM_HEADS))).reshape(B, 1, LANES)
    return (c0, n0.reshape(B, 1, D_MODEL), m0p, tail0) + mw


def _unpack_mlstm_state(B, n_new, m_new):
    return n_new.reshape(B, MLSTM_HEADS, MLSTM_DH), m_new.reshape(B, LANES)[:, :MLSTM_HEADS]


def kernel(x_prompt, x_sample, c_prompt, c_sample, state_hgrn, state_mlstm_C, state_mlstm_n, state_mlstm_m, state_mlstm_conv, w_ada, b_ada, norm_g, w_in, b_in, hgrn_lb_logits, hgrn_norm_g, mlstm_conv_w, mlstm_conv_b, mlstm_wq, mlstm_wk, mlstm_wv, mlstm_norm_g, mlstm_skip, w_out, final_g):
    assert w_in.shape == (1, D_MODEL, N_PROJ) and hgrn_lb_logits.shape == (2, D_MODEL)
    Bp, Tp, _ = x_prompt.shape
    Bs, Ts, _ = x_sample.shape
    assert Tp % PROMPT_BLOCK == 0 and Ts <= SAMPLE_ROWS and Ts >= CONV_WIDTH - 1

    mod = _modulation(jnp.concatenate([c_prompt, c_sample], axis=0), w_ada[0], b_ada[0].reshape(1, -1))
    shift, scale, gate = mod[:, :D_MODEL], mod[:, D_MODEL:2 * D_MODEL], mod[:, 2 * D_MODEL:]
    pad_cols = N_PROJ_PAD - N_PROJ
    ng = norm_g[0].reshape(1, -1)
    w_pad = jnp.pad(w_in[0], ((0, 0), (0, pad_cols))).astype(BF16)
    b_pad = jnp.pad(b_in[0], (0, pad_cols)).reshape(1, -1)
    hg = hgrn_norm_g[0].reshape(1, -1)
    mw = (mlstm_conv_w[0], mlstm_conv_b[0].reshape(1, -1),
          _block_diag_heads(mlstm_wq[0]), _block_diag_heads(mlstm_wk[0]),
          _block_diag_heads(jnp.swapaxes(mlstm_wk[0], -1, -2)), _block_diag_heads(mlstm_wv[0]),
          mlstm_norm_g[0].reshape(1, -1), mlstm_skip[0].reshape(1, -1))
    wo = w_out[0].astype(BF16)
    fg = final_g.reshape(1, -1)

    xp2 = x_prompt.reshape(Bp * Tp, D_MODEL)
    per_seq = lambda a: a.reshape(-1, 1, D_MODEL)
    sc_p, sh_p = per_seq(scale[:Bp]), per_seq(shift[:Bp])
    yh_p, hg_p = _hgrn_fused(xp2, sc_p, sh_p, ng, w_pad[:, :N_HGRN], b_pad[:, :N_HGRN],
                             jnp.zeros((Bp, HGRN_HEADS, HGRN_D, HGRN_D), F32), hgrn_lb_logits, hg,
                             B=Bp, T=Tp, TB=PROMPT_BLOCK, L=PROMPT_CHUNK)
    ops_p = _mlstm_operands(jnp.zeros((Bp, MLSTM_HEADS, MLSTM_DH, MLSTM_DH), F32),
                            jnp.zeros((Bp, MLSTM_HEADS, MLSTM_DH), F32), jnp.zeros((Bp, MLSTM_HEADS), F32),
                            jnp.zeros((Bp, CONV_WIDTH - 1, D_MODEL), F32), mw)
    ym_p, c_p, n_p, m_p, tail_p = _mlstm_fused(xp2, sc_p, sh_p, ng, w_pad[:, N_HGRN:], b_pad[:, N_HGRN:], ops_p,
                                               B=Bp, T=Tp, TB=PROMPT_BLOCK, L=PROMPT_BLOCK)
    n_p, m_p = _unpack_mlstm_state(Bp, n_p, m_p)
    conv_p = tail_p[:, SUBLANES - (CONV_WIDTH - 1):]
    yp = _out_projection(yh_p, ym_p, xp2, per_seq(gate[:Bp]), wo, fg, 1024).reshape(Bp, Tp, D_MODEL)

    rows_s = Bs * SAMPLE_ROWS
    xs2 = jnp.pad(x_sample, ((0, 0), (0, SAMPLE_ROWS - Ts), (0, 0))).reshape(rows_s, D_MODEL)
    per_row = lambda a: jnp.repeat(a, SAMPLE_ROWS, axis=0).reshape(1, rows_s, D_MODEL)
    proj_s = _in_projection(xs2, per_row(scale[Bp:]), per_row(shift[Bp:]), ng, w_pad, b_pad, rows_s)
    yh_s, hg_s = _hgrn(proj_s, state_hgrn[0], hgrn_lb_logits, hg,
                       B=Bs, T=SAMPLE_ROWS, TB=SAMPLE_ROWS, NB=8, L=SAMPLE_ROWS, Tv=Ts)
    ops_s = _mlstm_operands(state_mlstm_C[0], state_mlstm_n[0], state_mlstm_m[0], state_mlstm_conv[0], mw)
    ym_s, c_s, n_s, m_s = _mlstm(proj_s, ops_s, B=Bs, T=SAMPLE_ROWS, TB=SAMPLE_ROWS, NB=8, L=SAMPLE_ROWS, Tv=Ts)
    n_s, m_s = _unpack_mlstm_state(Bs, n_s, m_s)
    conv_s = proj_s.reshape(Bs, SAMPLE_ROWS, N_PROJ_PAD)[:, Ts - (CONV_WIDTH - 1):Ts, N_HGRN:N_HGRN + D_MODEL]
    ys = _out_projection(yh_s, ym_s, xs2, per_row(gate[Bp:]), wo, fg, rows_s).reshape(Bs, SAMPLE_ROWS, D_MODEL)

    return (yp, ys[:, :Ts], hg_p[None], c_p[None], n_p[None], m_p[None], conv_p[None],
            hg_s[None], c_s[None], n_s[None], m_s[None], conv_s[None])
```

```python
import functools

import jax
import jax.numpy as jnp
from jax import lax
from jax.experimental import pallas as pl
from jax.experimental.pallas import tpu as pltpu

F32 = jnp.float32
BF16 = jnp.bfloat16

D_MODEL = 1024
HGRN_HEADS = 8
HGRN_D = 128
MLSTM_HEADS = 4
MLSTM_DH = 256
CONV_WIDTH = 4
QKV_BLOCK = 4
EPS = 1e-6
N_PROJ = 7176
N_HGRN = 4 * D_MODEL
N_MLSTM = N_PROJ - N_HGRN
N_MLSTM_PAD = 3 * D_MODEL + 128
GATE_TILE = 3 * D_MODEL // 128
MXU_WIDTH = 256
PROMPT_CHUNK = 64
PROMPT_BLOCK = 256
SAMPLE_ROWS = 8
SUBLANES = 8
LANES = 128
VMEM_LIMIT = 56 * 1024 * 1024
FAST_PATH_MIN_LOG_DECAY = -80.0


def _dot(a, b):
    return jnp.dot(a, b, preferred_element_type=F32)


def _dot_nt(a, b):
    return lax.dot_general(a, b, (((1,), (1,)), ((), ())), preferred_element_type=F32)


def _dot_tn(a, b):
    return lax.dot_general(a, b, (((0,), (0,)), ((), ())), preferred_element_type=F32)


def _sigmoid(x):
    return 0.5 * jnp.tanh(0.5 * x) + 0.5


def _levels(L):
    out, s = [], 1
    while s < L:
        out.append(s)
        s *= 2
    return out


def _seg_bcast(W, s, L, r):
    n = W.shape[1]
    if s == 1:
        return jnp.where((r & 1) != 0, pltpu.roll(W, 1, 0), W)
    if s == 2:
        m = r & 3
        return jnp.where(m == 0, pltpu.roll(W, L - 1, 0),
                         jnp.where(m == 1, W,
                                   jnp.where(m == 2, pltpu.roll(W, 1, 0), pltpu.roll(W, 2, 0))))
    pieces = [jnp.broadcast_to(W[b * 2 * s + s - 1:b * 2 * s + s, :], (2 * s, n))
              for b in range(L // (2 * s))]
    return pieces[0] if len(pieces) == 1 else jnp.concatenate(pieces, axis=0)


def _level_factors(x, L, r):
    W = x
    factors = []
    for s in _levels(L):
        Tb = _seg_bcast(W, s, L, r)
        sec = (r & s) != 0
        factors.append(jnp.exp(jnp.where(sec, W, Tb - W)))
        W = W + jnp.where(sec, Tb, 0.0)
    return factors


def _prefix8(x):
    sub = lax.broadcasted_iota(jnp.int32, (SUBLANES, 1), 0)
    y = x + jnp.where(sub >= 1, pltpu.roll(x, 1, 0), 0.0)
    y = y + jnp.where(sub >= 2, pltpu.roll(y, 2, 0), 0.0)
    return y + jnp.where(sub >= 4, pltpu.roll(y, 4, 0), 0.0)


def _chunk_cumsum(x, L):
    outs = []
    for c in range(x.shape[0] // L):
        total = None
        for g in range(L // SUBLANES):
            lo = c * L + g * SUBLANES
            p = _prefix8(x[lo:lo + SUBLANES, :])
            if total is not None:
                p = p + total
            outs.append(p)
            total = p[SUBLANES - 1:SUBLANES, :]
    return outs[0] if len(outs) == 1 else jnp.concatenate(outs, axis=0)


def _rows_bcast(x, row_in_chunk, L):
    n = x.shape[1]
    pieces = [jnp.broadcast_to(x[c * L + row_in_chunk:c * L + row_in_chunk + 1, :], (L, n))
              for c in range(x.shape[0] // L)]
    return pieces[0] if len(pieces) == 1 else jnp.concatenate(pieces, axis=0)


def _rms_mod_bf16(x, scale, shift, g):
    ms = jnp.mean(x * x, axis=-1, keepdims=True)
    return ((x * lax.rsqrt(ms + EPS)) * (g * (1.0 + scale)) + shift).astype(BF16)


def _project_into(dst_ref, h, w_ref, b_ref):
    n = w_ref.shape[1]
    bounds = list(range(0, n - n % D_MODEL, D_MODEL)) or [0]
    for k, lo in enumerate(bounds):
        hi = n if k == len(bounds) - 1 else lo + D_MODEL
        dst_ref[:, lo:hi] = _dot(h, w_ref[:, lo:hi]) + b_ref[:, lo:hi]


def _mod_kernel(c_ref, w_ref, b_ref, o_ref):
    c = c_ref[...]
    a = c * _sigmoid(c)
    o_ref[...] = _dot(a.astype(BF16), w_ref[...].astype(BF16)) + b_ref[...]


def _modulation(c_all, w_ada, b_ada):
    m = c_all.shape[0]
    n = w_ada.shape[1]
    tn = 512
    return pl.pallas_call(
        _mod_kernel,
        grid=(n // tn,),
        in_specs=[pl.BlockSpec((m, D_MODEL), lambda j: (0, 0)),
                  pl.BlockSpec((D_MODEL, tn), lambda j: (0, j)),
                  pl.BlockSpec((1, tn), lambda j: (0, j))],
        out_specs=pl.BlockSpec((m, tn), lambda j: (0, j)),
        out_shape=jax.ShapeDtypeStruct((m, n), F32),
        name="modulation",
    )(c_all, w_ada, b_ada)


def _inproj_kernel(x_ref, scale_ref, shift_ref, g_ref, w_ref, b_ref, o_ref, h_scr):
    @pl.when(pl.program_id(1) == 0)
    def _():
        h_scr[...] = _rms_mod_bf16(x_ref[...], scale_ref[0], shift_ref[0], g_ref[...])

    o_ref[...] = _dot(h_scr[...], w_ref[...]) + b_ref[...]


def _in_projection(x2d, scale3, shift3, norm_g, w, b, rb, col_tile):
    rows = x2d.shape[0]
    mrows = scale3.shape[1]
    nblk = rows // rb
    n_cols = w.shape[1]
    return pl.pallas_call(
        _inproj_kernel,
        grid=(nblk, n_cols // col_tile),
        in_specs=[pl.BlockSpec((rb, D_MODEL), lambda i, j: (i, 0)),
                  pl.BlockSpec((1, mrows, D_MODEL), lambda i, j: (i * scale3.shape[0] // nblk, 0, 0)),
                  pl.BlockSpec((1, mrows, D_MODEL), lambda i, j: (i * shift3.shape[0] // nblk, 0, 0)),
                  pl.BlockSpec((1, D_MODEL), lambda i, j: (0, 0)),
                  pl.BlockSpec((D_MODEL, col_tile), lambda i, j: (0, j)),
                  pl.BlockSpec((1, col_tile), lambda i, j: (0, j))],
        out_specs=pl.BlockSpec((rb, col_tile), lambda i, j: (i, j)),
        out_shape=jax.ShapeDtypeStruct((rows, n_cols), F32),
        scratch_shapes=[pltpu.VMEM((rb, D_MODEL), BF16)],
        compiler_params=pltpu.CompilerParams(vmem_limit_bytes=VMEM_LIMIT),
        name="in_projection",
    )(x2d, scale3, shift3, norm_g, w, b)


class _SideWork:
    def __init__(self, pieces=()):
        self._pieces = list(pieces)

    def run(self, n=1):
        for _ in range(n):
            if self._pieces:
                self._pieces.pop(0)()

    def flush(self):
        self.run(len(self._pieces))


def _fused_projection_schedule(i, proj_a, proj_b, x0_ref, sc0_ref, sh0_ref, ng_ref, w_ref, b_ref, next_h, step):
    n = w_ref.shape[1]

    @pl.when(i == 0)
    def _():
        _project_into(proj_a, _rms_mod_bf16(x0_ref[...], sc0_ref[0], sh0_ref[0], ng_ref[...]), w_ref, b_ref)

    def run(cur, nxt):
        h = []

        def norm_piece():
            h.append(next_h())

        def tile_piece(lo):
            hi = min(lo + MXU_WIDTH, n)

            def piece():
                nxt[:, lo:hi] = _dot(h[0], w_ref[:, lo:hi]) + b_ref[:, lo:hi]
            return piece

        step(cur, _SideWork([norm_piece] + [tile_piece(lo) for lo in range(0, n, MXU_WIDTH)]))

    parity = lax.rem(i, 2)
    pl.when(parity == 0)(lambda: run(proj_a, proj_b))
    pl.when(parity == 1)(lambda: run(proj_b, proj_a))


def _fused_projection_specs(n_blocks, nt, tb, n_cols, next_h_given):
    def nxt(i):
        return jnp.minimum(i + 1, n_blocks - 1)
    vec = (1, 1, D_MODEL)
    specs = [pl.BlockSpec((tb, D_MODEL), lambda i: (0, 0)),
             pl.BlockSpec(vec, lambda i: (0, 0, 0)), pl.BlockSpec(vec, lambda i: (0, 0, 0)),
             pl.BlockSpec((1, D_MODEL), lambda i: (0, 0)),
             pl.BlockSpec((D_MODEL, n_cols), lambda i: (0, 0)),
             pl.BlockSpec((1, n_cols), lambda i: (0, 0))]
    if next_h_given:
        return specs + [pl.BlockSpec((tb, D_MODEL), lambda i: (i, 0))]
    return specs + [pl.BlockSpec((tb, D_MODEL), lambda i: (nxt(i), 0)),
                    pl.BlockSpec(vec, lambda i: (nxt(i) // nt, 0, 0)),
                    pl.BlockSpec(vec, lambda i: (nxt(i) // nt, 0, 0))]


def _hgrn_step(load, t, nt, s0_ref, lbl_ref, g_ref, y_ref, s_ref, st_scr, a_scr,
               *, NB, TB, L, Tv, side_work=None):
    H = HGRN_HEADS
    rb = NB * TB
    spn = TB // L
    nseg = rb // L

    @pl.when(t == 0)
    def _():
        for nb in range(NB):
            for h in range(H):
                st_scr[nb, h] = jnp.zeros((HGRN_D, HGRN_D), F32) if s0_ref is None else s0_ref[nb, h].T

    sw = side_work if side_work is not None else _SideWork()
    sw.run(1)

    lg = lbl_ref[...]
    mx = jnp.max(lg, axis=0, keepdims=True)
    e = jnp.exp(lg - mx)
    lb = e[0:1, :] / jnp.sum(e, axis=0, keepdims=True)
    g_norm = g_ref[...]

    hq = load(0)
    sw.run(2)
    c1 = 0.5 * (1.0 - lb)
    p = c1 * jnp.tanh(0.5 * load(1))
    sw.run(2)
    logf = jnp.log((lb + c1) + p)
    sw.run(2)
    kk = c1 - p
    sw.run(2)
    if Tv < L:
        valid = (lax.broadcasted_iota(jnp.int32, (rb, 1), 0) & (L - 1)) < Tv
        logf = jnp.where(valid, logf, 0.0)
        kk = jnp.where(valid, kk, 0.0)
    G = _chunk_cumsum(logf, L)
    sw.run(3)
    GL = [G[s * L + L - 1:s * L + L, :] for s in range(nseg)]
    fast_ok = jnp.min(functools.reduce(jnp.minimum, GL)) >= FAST_PATH_MIN_LOG_DECAY

    ti = lax.broadcasted_iota(jnp.int32, (L, L), 0)
    ji = lax.broadcasted_iota(jnp.int32, (L, L), 1)

    @pl.when(fast_ok)
    def _():
        d = G - _rows_bcast(G, L // 2 - 1, L)
        qt = (hq * jnp.exp(d)).astype(BF16)
        kt = (kk * jnp.exp(-d)).astype(BF16)
        causal = ji <= ti
        for s in range(nseg):
            rows = slice(s * L, (s + 1) * L)
            for h in range(H):
                hs = slice(h * HGRN_D, (h + 1) * HGRN_D)
                a_scr[s * H + h] = jnp.where(causal, _dot_nt(qt[rows, hs], kt[rows, hs]), 0.0)

    @pl.when(jnp.logical_not(fast_ok))
    def _():
        r = lax.broadcasted_iota(jnp.int32, (L, 1), 0)
        xo = ti ^ ji
        diag_mask = ti == ji
        lvl_masks = [(ji < ti) & (xo >= s) & (xo < 2 * s) for s in _levels(L)]
        for s in range(nseg):
            rows = slice(s * L, (s + 1) * L)
            factors = _level_factors(logf[rows, :], L, r)
            for h in range(H):
                hs = slice(h * HGRN_D, (h + 1) * HGRN_D)
                qh = hq[rows, hs]
                kh = kk[rows, hs]
                A = jnp.where(diag_mask, _dot_nt(qh.astype(BF16), kh.astype(BF16)), 0.0)
                for lvl in range(len(factors)):
                    E = factors[lvl][:, hs]
                    A = A + jnp.where(lvl_masks[lvl],
                                      _dot_nt((qh * E).astype(BF16), (kh * E).astype(BF16)), 0.0)
                a_scr[s * H + h] = A

    qg = (hq * jnp.exp(G)).astype(BF16)
    sw.run(2)
    GLb = GL[0] if nseg == 1 and L == rb else jnp.concatenate(
        [jnp.broadcast_to(gl, (L, D_MODEL)) for gl in GL], axis=0)
    kd = (kk * jnp.exp(GLb - G)).astype(BF16)
    sw.run(2)
    vb = load(2).astype(BF16)
    hz = load(3)
    zgate = hz * _sigmoid(hz)
    sw.flush()
    for nb in range(NB):
        st = [st_scr[nb, h] for h in range(H)]
        for c in range(spn):
            s = nb * spn + c
            rows = slice(s * L, (s + 1) * L)
            dS = jnp.exp(GL[s])
            for h in range(H):
                hs = slice(h * HGRN_D, (h + 1) * HGRN_D)
                A = a_scr[s * H + h].astype(BF16)
                o = _dot(A, vb[rows, hs]) + _dot_nt(qg[rows, hs], st[h].astype(BF16))
                st[h] = st[h] * dS[:, hs] + _dot_tn(vb[rows, hs], kd[rows, hs])
                ms = jnp.mean(o * o, axis=-1, keepdims=True)
                y = o * lax.rsqrt(ms + EPS) * g_norm[:, hs] * zgate[rows, hs]
                y_ref[rows, hs] = y.astype(y_ref.dtype)
        for h in range(H):
            st_scr[nb, h] = st[h]

    @pl.when(t == nt - 1)
    def _():
        for nb in range(NB):
            for h in range(H):
                s_ref[nb, h] = st_scr[nb, h].T


def _hgrn_kernel(q_ref, f_ref, i_ref, z_ref, s0_ref, lbl_ref, g_ref, y_ref, s_ref, st_scr, a_scr,
                 *, nt, **static):
    cols = (q_ref, f_ref, i_ref, z_ref)
    _hgrn_step(lambda c: cols[c][...], lax.rem(pl.program_id(0), nt), nt,
               s0_ref, lbl_ref, g_ref, y_ref, s_ref, st_scr, a_scr, **static)


def _hgrn_fused_kernel(x0_ref, sc0_ref, sh0_ref, ng_ref, w_ref, b_ref, xn_ref, scn_ref, shn_ref,
                       lbl_ref, g_ref, y_ref, s_ref, hn_ref, st_scr, a_scr, proj_a, proj_b,
                       *, nt, **static):
    i = pl.program_id(0)

    def next_h():
        h = _rms_mod_bf16(xn_ref[...], scn_ref[0], shn_ref[0], ng_ref[...])
        hn_ref[...] = h
        return h

    def step(cur, side_work):
        _hgrn_step(lambda c: cur[:, c * D_MODEL:(c + 1) * D_MODEL], lax.rem(i, nt), nt,
                   None, lbl_ref, g_ref, y_ref, s_ref, st_scr, a_scr, side_work=side_work, **static)

    _fused_projection_schedule(i, proj_a, proj_b, x0_ref, sc0_ref, sh0_ref, ng_ref, w_ref, b_ref, next_h, step)


def _hgrn_specs(B, T, TB, NB, L):
    nt = T // TB
    rb = NB * TB
    s_spec = pl.BlockSpec((NB, HGRN_HEADS, HGRN_D, HGRN_D), lambda i: (i // nt, 0, 0, 0))
    in_tail = [pl.BlockSpec((2, D_MODEL), lambda i: (0, 0)), pl.BlockSpec((1, D_MODEL), lambda i: (0, 0))]
    out_specs = [pl.BlockSpec((rb, D_MODEL), lambda i: (i, 0)), s_spec]
    out_shape = [jax.ShapeDtypeStruct((B * T, D_MODEL), BF16),
                 jax.ShapeDtypeStruct((B, HGRN_HEADS, HGRN_D, HGRN_D), F32)]
    scratch = [pltpu.VMEM((NB, HGRN_HEADS, HGRN_D, HGRN_D), F32),
               pltpu.VMEM((rb // L * HGRN_HEADS, L, L), F32)]
    return nt, rb, s_spec, in_tail, out_specs, out_shape, scratch


def _hgrn(proj, s0, lb_logits, norm_g, *, B, T, TB, NB, L, Tv):
    nt, rb, s_spec, in_tail, out_specs, out_shape, scratch = _hgrn_specs(B, T, TB, NB, L)
    assert NB == 1 or nt == 1
    cols = [pl.BlockSpec((rb, D_MODEL), lambda i, c=c: (i, c)) for c in range(4)]
    return pl.pallas_call(
        functools.partial(_hgrn_kernel, nt=nt, NB=NB, TB=TB, L=L, Tv=Tv),
        grid=(B // NB * nt,),
        in_specs=cols + [s_spec] + in_tail, out_specs=out_specs, out_shape=out_shape, scratch_shapes=scratch,
        compiler_params=pltpu.CompilerParams(vmem_limit_bytes=VMEM_LIMIT),
        name="hgrn2",
    )(proj, proj, proj, proj, s0, lb_logits, norm_g)


def _hgrn_fused(x2d, scale3, shift3, norm_g, w, b, lb_logits, hgrn_norm_g, *, B, T, TB, L):
    nt, rb, _, in_tail, out_specs, out_shape, scratch = _hgrn_specs(B, T, TB, 1, L)
    n_blocks = B * nt
    return pl.pallas_call(
        functools.partial(_hgrn_fused_kernel, nt=nt, NB=1, TB=TB, L=L, Tv=L),
        grid=(n_blocks,),
        in_specs=_fused_projection_specs(n_blocks, nt, TB, N_HGRN, False) + in_tail,
        out_specs=out_specs + [pl.BlockSpec((TB, D_MODEL), lambda i: (i, 0))],
        out_shape=out_shape + [jax.ShapeDtypeStruct((B * T, D_MODEL), BF16)],
        scratch_shapes=scratch + [pltpu.VMEM((TB, N_HGRN), F32), pltpu.VMEM((TB, N_HGRN), F32)],
        compiler_params=pltpu.CompilerParams(vmem_limit_bytes=VMEM_LIMIT),
        name="hgrn2_fused",
    )(x2d, scale3, shift3, norm_g, w, b, x2d, scale3, shift3, lb_logits, hgrn_norm_g)


def _mlstm_step(load, t, nt, c0_ref, n0_ref, m0_ref, tail0_ref,
                cw_ref, cb_ref, wq_ref, wk_ref, wkt_ref, wv_ref, ng_ref, skip_ref,
                y_ref, c_ref, n_ref, m_ref, tail_scr, conv_ref,
                *, NB, TB, L, Tv, side_work=None):
    H = MLSTM_HEADS
    rb = NB * TB
    spn = TB // L
    if nt == 1:
        c_in, n_in, m_in, tail_in = c0_ref, n0_ref, m0_ref, tail0_ref
    else:
        c_in, n_in, m_in, tail_in = c_ref, n_ref, m_ref, tail_scr

        @pl.when(t == 0)
        def _():
            for dst, src in ((c_ref, c0_ref), (n_ref, n0_ref), (m_ref, m0_ref), (tail_scr, tail0_ref)):
                dst[...] = jnp.zeros(dst.shape, F32) if src is None else src[...]

    sw = side_work if side_work is not None else _SideWork()
    sw.run(1)

    cw = cw_ref[...]
    ng = ng_ref[...]
    skip = skip_ref[...]
    lane = lax.broadcasted_iota(jnp.int32, (1, LANES), 1)
    ti = lax.broadcasted_iota(jnp.int32, (L, L), 0)
    ji = lax.broadcasted_iota(jnp.int32, (L, L), 1)
    causal = ji <= ti

    mu = load(0)
    xcs = []
    for nb in range(NB):
        u_nb = mu[nb * TB:(nb + 1) * TB, :]
        ext = jnp.concatenate([tail_in[nb], u_nb], axis=0)
        conv = cb_ref[...] + cw[CONV_WIDTH - 1:CONV_WIDTH, :] * u_nb
        for i in range(1, CONV_WIDTH):
            conv = conv + cw[CONV_WIDTH - 1 - i:CONV_WIDTH - i, :] * pltpu.roll(ext, i, 0)[SUBLANES:, :]
            sw.run(1)
        xcs.append(conv * _sigmoid(conv))
        sw.run(1)
        if nt > 1:
            tail_scr[nb] = u_nb[TB - SUBLANES:, :]
    if conv_ref is not None:
        @pl.when(t == nt - 1)
        def _():
            for nb in range(NB):
                conv_ref[nb] = mu[(nb + 1) * TB - SUBLANES:(nb + 1) * TB, :]
    xc = xcs[0] if NB == 1 else jnp.concatenate(xcs, axis=0)
    xcb = xc.astype(BF16)
    mub = mu.astype(BF16)
    mz = load(1)
    zgate = mz * _sigmoid(mz)
    sw.run(2)
    ogate = _sigmoid(load(2))
    sw.run(2)

    gt = load(3)
    lf = jnp.minimum(gt, 0.0) - jnp.log1p(jnp.exp(-jnp.abs(gt)))
    ig = gt
    if Tv < L:
        valid = (lax.broadcasted_iota(jnp.int32, (rb, 1), 0) & (L - 1)) < Tv
        lf = jnp.where(valid, lf, 0.0)
        ig = jnp.where(valid, ig, -jnp.inf)
    bcum = _chunk_cumsum(lf, L)
    sw.run(1)
    comb = jnp.where(lane < H, ig, bcum)
    rpad = -rb % LANES
    if rpad:
        comb = jnp.concatenate([comb, jnp.zeros((rpad, LANES), F32)], axis=0)
    combT = comb.T

    assert spn == 1
    heads = {}

    def project_head(h):
        hs = slice(h * MLSTM_DH, (h + 1) * MLSTM_DH)
        q_all = _dot(xcb[:, hs], wq_ref[h])
        kb_all = (_dot(xcb[:, hs], wk_ref[h]) * (MLSTM_DH ** -0.5)).astype(BF16)
        kT_all = _dot_nt(wkt_ref[h], xcb[:, hs]) * (MLSTM_DH ** -0.5)
        vb_all = _dot(mub[:, hs], wv_ref[h]).astype(BF16)
        heads[h] = (hs, q_all, q_all.astype(BF16), kb_all, kT_all, vb_all)
        sw.run(1)

    pairs = [(h, nb) for h in range(H) for nb in range(NB)]
    groups = [pairs] if NB > 1 else [[p] for p in pairs]
    rows_of = lambda nb: slice(nb * L, (nb + 1) * L)

    def run_group(pairs):
        for h in sorted({h for h, _ in pairs}):
            project_head(h)

        gates = []
        for h, nb in pairs:
            rows = rows_of(nb)
            bcol = bcum[rows, H + h:H + h + 1]
            irow = combT[h:h + 1, rows]
            brow = combT[H + h:H + h + 1, rows]
            logD = jnp.where(causal, (bcol - brow) + irow, -jnp.inf)
            m_intra = jnp.max(logD, axis=-1, keepdims=True)
            gates.append((bcol, irow, brow, m_intra, jnp.exp(logD - m_intra)))

        scores = []
        for (h, nb), g in zip(pairs, gates):
            _, _, qb_all, kb_all, _, _ = heads[h]
            rows = rows_of(nb)
            scores.append(_dot_nt(qb_all[rows], kb_all[rows]) * g[4])

        intra = []
        for (h, nb), sc in zip(pairs, scores):
            vb_all = heads[h][5]
            intra.append((jnp.sum(sc, axis=-1, keepdims=True), _dot(sc.astype(BF16), vb_all[rows_of(nb)])))

        writes = []
        for (h, nb), g in zip(pairs, gates):
            _, _, _, kb_all, kT_all, vb_all = heads[h]
            rows = rows_of(nb)
            bcol, irow, brow, m_intra, _ = g
            m_loc = m_intra[L - 1:L, :]
            b_last = bcol[L - 1:L, :]
            wrow = jnp.exp((b_last - brow) + irow - m_loc)
            U = _dot((kT_all[:, rows] * wrow).astype(BF16), vb_all[rows])
            ks = _dot(jnp.broadcast_to(wrow, (SUBLANES, L)).astype(BF16), kb_all[rows])[0:1, :]
            writes.append((m_loc, b_last, U, ks))

        outs = []
        for (h, nb), g, (rs, sv), (m_loc, b_last, U, ks) in zip(pairs, gates, intra, writes):
            hs, q_all, qb_all, _, _, _ = heads[h]
            rows = rows_of(nb)
            bcol, _, _, m_intra, _ = g
            C = c_in[nb, h]
            nh = n_in[nb, :, hs]
            m_prev = m_in[nb][:, h:h + 1]
            m_inter = bcol + m_prev
            m_t = jnp.maximum(m_inter, m_intra)
            inter = jnp.exp(m_inter - m_t)
            scl = jnp.exp(m_intra - m_t)
            den = inter * jnp.sum(q_all[rows] * nh, axis=-1, keepdims=True) + scl * rs
            rden = 1.0 / jnp.maximum(jnp.abs(den), jnp.exp(-m_t))
            hh = (inter * rden) * _dot(qb_all[rows], C.astype(BF16)) + (scl * rden) * sv
            m_new = m_t[L - 1:L, :]
            dec = jnp.exp(b_last + m_prev - m_new)
            scu = jnp.exp(m_loc - m_new)
            c_ref[nb, h] = dec * C + scu * U
            n_ref[nb, :, hs] = dec * nh + scu * ks
            outs.append((hh, m_new))

        for (h, nb), (hh, _) in zip(pairs, outs):
            hs = heads[h][0]
            rows = rows_of(nb)
            hm = ogate[rows, hs] * hh
            ms = jnp.mean(hm * hm, axis=-1, keepdims=True)
            y = (hm * lax.rsqrt(ms + EPS) * ng[:, hs] + skip[:, hs] * xc[rows, hs]) * zgate[rows, hs]
            y_ref[rows, hs] = y.astype(y_ref.dtype)
        return outs

    outs = [o for grp in groups for o in run_group(grp)]

    for nb in range(NB):
        m_row = m_in[nb]
        for h in range(H):
            m_row = jnp.where(lane == h, outs[h * NB + nb][1], m_row)
        m_ref[nb] = m_row
    sw.flush()


N_MLSTM_WEIGHTS = 8


def _mlstm_kernel(u_ref, z_ref, o_ref, gate_ref, *rest, nt, **static):
    cols = (u_ref, z_ref, o_ref, gate_ref)
    ins, (y_ref, c_ref, n_ref, m_ref, tail_scr) = rest[:4 + N_MLSTM_WEIGHTS], rest[4 + N_MLSTM_WEIGHTS:]
    _mlstm_step(lambda c: cols[c][...], lax.rem(pl.program_id(0), nt), nt, *ins,
                y_ref, c_ref, n_ref, m_ref, tail_scr, None, **static)


def _mlstm_fused_kernel(x0_ref, sc0_ref, sh0_ref, ng_ref, w_ref, b_ref, hn_ref, *rest, nt, **static):
    i = pl.program_id(0)
    weights = rest[:N_MLSTM_WEIGHTS]
    y_ref, c_ref, n_ref, m_ref, conv_ref, tail_scr, proj_a, proj_b = rest[N_MLSTM_WEIGHTS:]
    widths = (D_MODEL, D_MODEL, D_MODEL, LANES)

    def step(cur, side_work):
        _mlstm_step(lambda c: cur[:, c * D_MODEL:c * D_MODEL + widths[c]], lax.rem(i, nt), nt,
                    None, None, None, None, *weights,
                    y_ref, c_ref, n_ref, m_ref, tail_scr, conv_ref, side_work=side_work, **static)

    _fused_projection_schedule(i, proj_a, proj_b, x0_ref, sc0_ref, sh0_ref, ng_ref, w_ref, b_ref,
                               lambda: hn_ref[...], step)


def _mlstm_specs(B, T, TB, NB):
    nt = T // TB
    rb = NB * TB

    def full(shape):
        return pl.BlockSpec(shape, lambda i: (0,) * len(shape))

    c_spec = pl.BlockSpec((NB, MLSTM_HEADS, MLSTM_DH, MLSTM_DH), lambda i: (i // nt, 0, 0, 0))
    n_spec = pl.BlockSpec((NB, 1, D_MODEL), lambda i: (i // nt, 0, 0))
    m_spec = pl.BlockSpec((NB, 1, LANES), lambda i: (i // nt, 0, 0))
    t_spec = pl.BlockSpec((NB, SUBLANES, D_MODEL), lambda i: (i // nt, 0, 0))
    head_w = full((MLSTM_HEADS, MLSTM_DH, MLSTM_DH))
    state_specs = [c_spec, n_spec, m_spec, t_spec]
    weight_specs = [full((CONV_WIDTH, D_MODEL)), full((1, D_MODEL)),
                    head_w, head_w, head_w, head_w, full((1, D_MODEL)), full((1, D_MODEL))]
    out_specs = [pl.BlockSpec((rb, D_MODEL), lambda i: (i, 0)), c_spec, n_spec, m_spec]
    out_shape = [jax.ShapeDtypeStruct((B * T, D_MODEL), BF16),
                 jax.ShapeDtypeStruct((B, MLSTM_HEADS, MLSTM_DH, MLSTM_DH), F32),
                 jax.ShapeDtypeStruct((B, 1, D_MODEL), F32),
                 jax.ShapeDtypeStruct((B, 1, LANES), F32)]
    scratch = [pltpu.VMEM((NB, SUBLANES, D_MODEL), F32)]
    return nt, rb, state_specs, weight_specs, out_specs, out_shape, scratch, t_spec


def _mlstm(proj, states, weights, *, B, T, TB, NB, L, Tv):
    nt, rb, state_specs, weight_specs, out_specs, out_shape, scratch, _ = _mlstm_specs(B, T, TB, NB)
    assert NB == 1 or nt == 1
    cols = [pl.BlockSpec((rb, D_MODEL), lambda i, c=c: (i, c)) for c in range(3)]
    cols.append(pl.BlockSpec((rb, LANES), lambda i: (i, GATE_TILE)))
    return pl.pallas_call(
        functools.partial(_mlstm_kernel, nt=nt, NB=NB, TB=TB, L=L, Tv=Tv),
        grid=(B // NB * nt,),
        in_specs=cols + state_specs + weight_specs,
        out_specs=out_specs, out_shape=out_shape, scratch_shapes=scratch,
        compiler_params=pltpu.CompilerParams(vmem_limit_bytes=VMEM_LIMIT),
        name="mlstm",
    )(proj, proj, proj, proj, *states, *weights)


def _mlstm_fused(x2d, scale3, shift3, norm_g, w, b, h_next, weights, *, B, T, TB, L):
    nt, rb, _, weight_specs, out_specs, out_shape, scratch, t_spec = _mlstm_specs(B, T, TB, 1)
    n_blocks = B * nt
    return pl.pallas_call(
        functools.partial(_mlstm_fused_kernel, nt=nt, NB=1, TB=TB, L=L, Tv=L),
        grid=(n_blocks,),
        in_specs=_fused_projection_specs(n_blocks, nt, TB, N_MLSTM_PAD, True) + weight_specs,
        out_specs=out_specs + [t_spec],
        out_shape=out_shape + [jax.ShapeDtypeStruct((B, SUBLANES, D_MODEL), F32)],
        scratch_shapes=scratch + [pltpu.VMEM((TB, N_MLSTM_PAD), F32), pltpu.VMEM((TB, N_MLSTM_PAD), F32)],
        compiler_params=pltpu.CompilerParams(vmem_limit_bytes=VMEM_LIMIT),
        name="mlstm_fused",
    )(x2d, scale3, shift3, norm_g, w, b, h_next, *weights)


def _out_kernel(yh_ref, ym_ref, x_ref, gate_ref, w_ref, fg_ref, o_ref):
    acc = _dot(yh_ref[...], w_ref[0:D_MODEL, :]) + _dot(ym_ref[...], w_ref[D_MODEL:2 * D_MODEL, :])
    out = x_ref[...] + gate_ref[0] * acc
    ms = jnp.mean(out * out, axis=-1, keepdims=True)
    o_ref[...] = out * lax.rsqrt(ms + EPS) * fg_ref[...]


def _out_projection(yh, ym, x2d, gate3, w_out, final_g, rb):
    rows = x2d.shape[0]
    mrows = gate3.shape[1]
    nblk = rows // rb
    return pl.pallas_call(
        _out_kernel,
        grid=(nblk,),
        in_specs=[pl.BlockSpec((rb, D_MODEL), lambda i: (i, 0)),
                  pl.BlockSpec((rb, D_MODEL), lambda i: (i, 0)),
                  pl.BlockSpec((rb, D_MODEL), lambda i: (i, 0)),
                  pl.BlockSpec((1, mrows, D_MODEL), lambda i: (i * gate3.shape[0] // nblk, 0, 0)),
                  pl.BlockSpec((2 * D_MODEL, D_MODEL), lambda i: (0, 0)),
                  pl.BlockSpec((1, D_MODEL), lambda i: (0, 0))],
        out_specs=pl.BlockSpec((rb, D_MODEL), lambda i: (i, 0)),
        out_shape=jax.ShapeDtypeStruct((rows, D_MODEL), F32),
        compiler_params=pltpu.CompilerParams(vmem_limit_bytes=VMEM_LIMIT),
        name="out_projection",
    )(yh, ym, x2d, gate3, w_out, final_g)


def _block_diag_heads(w):
    rows = w.reshape(MLSTM_HEADS, MLSTM_DH, QKV_BLOCK)
    tiled = jnp.tile(rows, (1, 1, MLSTM_DH // QKV_BLOCK))
    rg = lax.broadcasted_iota(jnp.int32, (MLSTM_DH, MLSTM_DH), 0) // QKV_BLOCK
    cg = lax.broadcasted_iota(jnp.int32, (MLSTM_DH, MLSTM_DH), 1) // QKV_BLOCK
    return jnp.where(rg == cg, tiled, 0.0).astype(BF16)


def _mlstm_state_operands(c0, n0, m0, conv0):
    B = c0.shape[0]
    tail0 = jnp.pad(conv0, ((0, 0), (SUBLANES - (CONV_WIDTH - 1), 0), (0, 0)))
    m0p = jnp.pad(m0, ((0, 0), (0, LANES - MLSTM_HEADS))).reshape(B, 1, LANES)
    return (c0, n0.reshape(B, 1, D_MODEL), m0p, tail0)


def _unpack_mlstm_state(B, n_new, m_new):
    return n_new.reshape(B, MLSTM_HEADS, MLSTM_DH), m_new.reshape(B, LANES)[:, :MLSTM_HEADS]


def kernel(x_prompt, x_sample, c_prompt, c_sample, state_hgrn, state_mlstm_C, state_mlstm_n, state_mlstm_m, state_mlstm_conv, w_ada, b_ada, norm_g, w_in, b_in, hgrn_lb_logits, hgrn_norm_g, mlstm_conv_w, mlstm_conv_b, mlstm_wq, mlstm_wk, mlstm_wv, mlstm_norm_g, mlstm_skip, w_out, final_g):
    assert w_in.shape == (1, D_MODEL, N_PROJ) and hgrn_lb_logits.shape == (2, D_MODEL)
    Bp, Tp, _ = x_prompt.shape
    Bs, Ts, _ = x_sample.shape
    assert Tp % PROMPT_BLOCK == 0 and Ts <= SAMPLE_ROWS and Ts >= CONV_WIDTH - 1

    mod = _modulation(jnp.concatenate([c_prompt, c_sample], axis=0), w_ada[0], b_ada[0].reshape(1, -1))
    shift, scale, gate = mod[:, :D_MODEL], mod[:, D_MODEL:2 * D_MODEL], mod[:, 2 * D_MODEL:]
    pad_cols = N_MLSTM_PAD - N_MLSTM
    ng = norm_g[0].reshape(1, -1)
    w_h = w_in[0, :, :N_HGRN].astype(BF16)
    w_m = jnp.pad(w_in[0, :, N_HGRN:].astype(BF16), ((0, 0), (0, pad_cols)))
    b_h = b_in[0, :N_HGRN].reshape(1, -1)
    b_m = jnp.pad(b_in[0, N_HGRN:], (0, pad_cols)).reshape(1, -1)
    hg = hgrn_norm_g[0].reshape(1, -1)
    mw = (mlstm_conv_w[0], mlstm_conv_b[0].reshape(1, -1),
          _block_diag_heads(mlstm_wq[0]), _block_diag_heads(mlstm_wk[0]),
          _block_diag_heads(jnp.swapaxes(mlstm_wk[0], -1, -2)), _block_diag_heads(mlstm_wv[0]),
          mlstm_norm_g[0].reshape(1, -1), mlstm_skip[0].reshape(1, -1))
    wo = w_out[0].astype(BF16)
    fg = final_g.reshape(1, -1)

    xp2 = x_prompt.reshape(Bp * Tp, D_MODEL)
    per_seq = lambda a: a.reshape(-1, 1, D_MODEL)
    sc_p, sh_p = per_seq(scale[:Bp]), per_seq(shift[:Bp])
    yh_p, hg_p, h_next = _hgrn_fused(xp2, sc_p, sh_p, ng, w_h, b_h, hgrn_lb_logits, hg,
                                     B=Bp, T=Tp, TB=PROMPT_BLOCK, L=PROMPT_CHUNK)
    ym_p, c_p, n_p, m_p, tail_p = _mlstm_fused(xp2, sc_p, sh_p, ng, w_m, b_m, h_next, mw,
                                               B=Bp, T=Tp, TB=PROMPT_BLOCK, L=PROMPT_BLOCK)
    n_p, m_p = _unpack_mlstm_state(Bp, n_p, m_p)
    conv_p = tail_p[:, SUBLANES - (CONV_WIDTH - 1):]
    yp = _out_projection(yh_p, ym_p, xp2, per_seq(gate[:Bp]), wo, fg, 1024).reshape(Bp, Tp, D_MODEL)

    rows_s = Bs * SAMPLE_ROWS
    xs2 = jnp.pad(x_sample, ((0, 0), (0, SAMPLE_ROWS - Ts), (0, 0))).reshape(rows_s, D_MODEL)
    per_row = lambda a: jnp.repeat(a, SAMPLE_ROWS, axis=0).reshape(1, rows_s, D_MODEL)
    sc_s, sh_s = per_row(scale[Bp:]), per_row(shift[Bp:])
    proj_h = _in_projection(xs2, sc_s, sh_s, ng, w_h, b_h, rows_s, N_HGRN // 2)
    proj_m = _in_projection(xs2, sc_s, sh_s, ng, w_m, b_m, rows_s, N_MLSTM_PAD // 5)
    yh_s, hg_s = _hgrn(proj_h, state_hgrn[0], hgrn_lb_logits, hg,
                       B=Bs, T=SAMPLE_ROWS, TB=SAMPLE_ROWS, NB=8, L=SAMPLE_ROWS, Tv=Ts)
    states_s = _mlstm_state_operands(state_mlstm_C[0], state_mlstm_n[0], state_mlstm_m[0], state_mlstm_conv[0])
    ym_s, c_s, n_s, m_s = _mlstm(proj_m, states_s, mw,
                                 B=Bs, T=SAMPLE_ROWS, TB=SAMPLE_ROWS, NB=8, L=SAMPLE_ROWS, Tv=Ts)
    n_s, m_s = _unpack_mlstm_state(Bs, n_s, m_s)
    conv_s = proj_m.reshape(Bs, SAMPLE_ROWS, N_MLSTM_PAD)[:, Ts - (CONV_WIDTH - 1):Ts, :D_MODEL]
    ys = _out_projection(yh_s, ym_s, xs2, per_row(gate[Bp:]), wo, fg, rows_s).reshape(Bs, SAMPLE_ROWS, D_MODEL)

    return (yp, ys[:, :Ts], hg_p[None], c_p[None], n_p[None], m_p[None], conv_p[None],
            hg_s[None], c_s[None], n_s[None], m_s[None], conv_s[None])
```

```python
import functools

import jax
import jax.numpy as jnp
from jax import lax
from jax.experimental import pallas as pl
from jax.experimental.pallas import tpu as pltpu

F32 = jnp.float32
BF16 = jnp.bfloat16

D_MODEL = 1024
HGRN_HEADS = 8
HGRN_D = 128
MLSTM_HEADS = 4
MLSTM_DH = 256
KEY_SCALE = MLSTM_DH ** -0.5
CONV_WIDTH = 4
QKV_BLOCK = 4
EPS = 1e-6
N_PROJ = 7176
N_HGRN = 4 * D_MODEL
N_MLSTM = N_PROJ - N_HGRN
N_MLSTM_PAD = 3 * D_MODEL + 128
GATE_TILE = 3 * D_MODEL // 128
MXU_WIDTH = 256
PROMPT_CHUNK = 128
PROMPT_BLOCK = 256
SAMPLE_ROWS = 8
SUBLANES = 8
LANES = 128
VMEM_LIMIT = 56 * 1024 * 1024
FAST_PATH_MIN_LOG_DECAY = -80.0


def _dot(a, b):
    return jnp.dot(a, b, preferred_element_type=F32)


def _dot_nt(a, b):
    return lax.dot_general(a, b, (((1,), (1,)), ((), ())), preferred_element_type=F32)


def _dot_tn(a, b):
    return lax.dot_general(a, b, (((0,), (0,)), ((), ())), preferred_element_type=F32)


def _sigmoid(x):
    return 0.5 * jnp.tanh(0.5 * x) + 0.5


def _silu(x):
    u = 0.5 * x
    return u * jnp.tanh(u) + u


def _levels(L):
    out, s = [], 1
    while s < L:
        out.append(s)
        s *= 2
    return out


def _seg_bcast(W, s, L, r):
    n = W.shape[1]
    if s == 1:
        return jnp.where((r & 1) != 0, pltpu.roll(W, 1, 0), W)
    if s == 2:
        m = r & 3
        return jnp.where(m == 0, pltpu.roll(W, L - 1, 0),
                         jnp.where(m == 1, W,
                                   jnp.where(m == 2, pltpu.roll(W, 1, 0), pltpu.roll(W, 2, 0))))
    pieces = [jnp.broadcast_to(W[b * 2 * s + s - 1:b * 2 * s + s, :], (2 * s, n))
              for b in range(L // (2 * s))]
    return pieces[0] if len(pieces) == 1 else jnp.concatenate(pieces, axis=0)


def _level_factors(x, L, r):
    W = x
    factors = []
    for s in _levels(L):
        Tb = _seg_bcast(W, s, L, r)
        sec = (r & s) != 0
        factors.append(jnp.exp(jnp.where(sec, W, Tb - W)))
        W = W + jnp.where(sec, Tb, 0.0)
    return factors


def _prefix8(x):
    sub = lax.broadcasted_iota(jnp.int32, (SUBLANES, 1), 0)
    y = x + jnp.where(sub >= 1, pltpu.roll(x, 1, 0), 0.0)
    y = y + jnp.where(sub >= 2, pltpu.roll(y, 2, 0), 0.0)
    return y + jnp.where(sub >= 4, pltpu.roll(y, 4, 0), 0.0)


def _chunk_cumsum(x, L):
    outs = []
    for c in range(x.shape[0] // L):
        total = None
        for g in range(L // SUBLANES):
            lo = c * L + g * SUBLANES
            p = _prefix8(x[lo:lo + SUBLANES, :])
            if total is not None:
                p = p + total
            outs.append(p)
            total = p[SUBLANES - 1:SUBLANES, :]
    return outs[0] if len(outs) == 1 else jnp.concatenate(outs, axis=0)


def _rows_bcast(x, row_in_chunk, L):
    n = x.shape[1]
    pieces = [jnp.broadcast_to(x[c * L + row_in_chunk:c * L + row_in_chunk + 1, :], (L, n))
              for c in range(x.shape[0] // L)]
    return pieces[0] if len(pieces) == 1 else jnp.concatenate(pieces, axis=0)


def _rms_mod_bf16(x, scale, shift, g):
    ms = jnp.mean(x * x, axis=-1, keepdims=True)
    return ((x * lax.rsqrt(ms + EPS)) * (g * (1.0 + scale)) + shift).astype(BF16)


def _project_into(dst_ref, h, w_ref, b_ref):
    n = w_ref.shape[1]
    bounds = list(range(0, n - n % D_MODEL, D_MODEL)) or [0]
    for k, lo in enumerate(bounds):
        hi = n if k == len(bounds) - 1 else lo + D_MODEL
        dst_ref[:, lo:hi] = _dot(h, w_ref[:, lo:hi]) + b_ref[:, lo:hi]


def _mod_kernel(c_ref, w_ref, b_ref, o_ref):
    c = c_ref[...]
    a = _silu(c)
    o_ref[...] = _dot(a.astype(BF16), w_ref[...].astype(BF16)) + b_ref[...]


def _modulation(c_all, w_ada, b_ada):
    m = c_all.shape[0]
    n = w_ada.shape[1]
    tn = 512
    return pl.pallas_call(
        _mod_kernel,
        grid=(n // tn,),
        in_specs=[pl.BlockSpec((m, D_MODEL), lambda j: (0, 0)),
                  pl.BlockSpec((D_MODEL, tn), lambda j: (0, j)),
                  pl.BlockSpec((1, tn), lambda j: (0, j))],
        out_specs=pl.BlockSpec((m, tn), lambda j: (0, j)),
        out_shape=jax.ShapeDtypeStruct((m, n), F32),
        name="modulation",
    )(c_all, w_ada, b_ada)


def _inproj_kernel(x_ref, scale_ref, shift_ref, g_ref, w_ref, b_ref, o_ref, h_scr):
    @pl.when(pl.program_id(1) == 0)
    def _():
        h_scr[...] = _rms_mod_bf16(x_ref[...], scale_ref[0], shift_ref[0], g_ref[...])

    o_ref[...] = _dot(h_scr[...], w_ref[...]) + b_ref[...]


def _in_projection(x2d, scale3, shift3, norm_g, w, b, rb, col_tile):
    rows = x2d.shape[0]
    mrows = scale3.shape[1]
    nblk = rows // rb
    n_cols = w.shape[1]
    return pl.pallas_call(
        _inproj_kernel,
        grid=(nblk, n_cols // col_tile),
        in_specs=[pl.BlockSpec((rb, D_MODEL), lambda i, j: (i, 0)),
                  pl.BlockSpec((1, mrows, D_MODEL), lambda i, j: (i * scale3.shape[0] // nblk, 0, 0)),
                  pl.BlockSpec((1, mrows, D_MODEL), lambda i, j: (i * shift3.shape[0] // nblk, 0, 0)),
                  pl.BlockSpec((1, D_MODEL), lambda i, j: (0, 0)),
                  pl.BlockSpec((D_MODEL, col_tile), lambda i, j: (0, j)),
                  pl.BlockSpec((1, col_tile), lambda i, j: (0, j))],
        out_specs=pl.BlockSpec((rb, col_tile), lambda i, j: (i, j)),
        out_shape=jax.ShapeDtypeStruct((rows, n_cols), F32),
        scratch_shapes=[pltpu.VMEM((rb, D_MODEL), BF16)],
        compiler_params=pltpu.CompilerParams(vmem_limit_bytes=VMEM_LIMIT),
        name="in_projection",
    )(x2d, scale3, shift3, norm_g, w, b)


class _SideWork:
    def __init__(self, pieces=()):
        self._pieces = list(pieces)

    def run(self, n=1):
        for _ in range(n):
            if self._pieces:
                self._pieces.pop(0)()

    def flush(self):
        self.run(len(self._pieces))


def _fused_projection_schedule(i, proj_a, proj_b, x0_ref, sc0_ref, sh0_ref, ng_ref, w_ref, b_ref, next_h, step):
    n = w_ref.shape[1]

    @pl.when(i == 0)
    def _():
        _project_into(proj_a, _rms_mod_bf16(x0_ref[...], sc0_ref[0], sh0_ref[0], ng_ref[...]), w_ref, b_ref)

    def run(cur, nxt):
        h = []

        def norm_piece():
            h.append(next_h())

        def tile_piece(lo):
            hi = min(lo + MXU_WIDTH, n)

            def piece():
                nxt[:, lo:hi] = _dot(h[0], w_ref[:, lo:hi]) + b_ref[:, lo:hi]
            return piece

        step(cur, _SideWork([norm_piece] + [tile_piece(lo) for lo in range(0, n, MXU_WIDTH)]))

    parity = lax.rem(i, 2)
    pl.when(parity == 0)(lambda: run(proj_a, proj_b))
    pl.when(parity == 1)(lambda: run(proj_b, proj_a))


def _fused_projection_specs(n_blocks, nt, tb, n_cols, next_h_given):
    def nxt(i):
        return jnp.minimum(i + 1, n_blocks - 1)
    vec = (1, 1, D_MODEL)
    specs = [pl.BlockSpec((tb, D_MODEL), lambda i: (0, 0)),
             pl.BlockSpec(vec, lambda i: (0, 0, 0)), pl.BlockSpec(vec, lambda i: (0, 0, 0)),
             pl.BlockSpec((1, D_MODEL), lambda i: (0, 0)),
             pl.BlockSpec((D_MODEL, n_cols), lambda i: (0, 0)),
             pl.BlockSpec((1, n_cols), lambda i: (0, 0))]
    if next_h_given:
        return specs + [pl.BlockSpec((tb, D_MODEL), lambda i: (i, 0))]
    return specs + [pl.BlockSpec((tb, D_MODEL), lambda i: (nxt(i), 0)),
                    pl.BlockSpec(vec, lambda i: (nxt(i) // nt, 0, 0)),
                    pl.BlockSpec(vec, lambda i: (nxt(i) // nt, 0, 0))]


def _hgrn_step(load, t, nt, s0_ref, lbl_ref, g_ref, y_ref, s_ref, st_scr, a_scr,
               *, NB, TB, L, Tv, side_work=None):
    H = HGRN_HEADS
    rb = NB * TB
    spn = TB // L
    nseg = rb // L

    @pl.when(t == 0)
    def _():
        for nb in range(NB):
            for h in range(H):
                st_scr[nb, h] = jnp.zeros((HGRN_D, HGRN_D), F32) if s0_ref is None else s0_ref[nb, h].T

    sw = side_work if side_work is not None else _SideWork()
    sw.run(1)

    lg = lbl_ref[...]
    mx = jnp.max(lg, axis=0, keepdims=True)
    e = jnp.exp(lg - mx)
    lb = e[0:1, :] / jnp.sum(e, axis=0, keepdims=True)
    g_norm = g_ref[...]

    hq = load(0)
    sw.run(2)
    c1 = 0.5 * (1.0 - lb)
    p = c1 * jnp.tanh(0.5 * load(1))
    sw.run(2)
    logf = jnp.log((lb + c1) + p)
    sw.run(2)
    kk = c1 - p
    sw.run(2)
    if Tv < L:
        valid = (lax.broadcasted_iota(jnp.int32, (rb, 1), 0) & (L - 1)) < Tv
        logf = jnp.where(valid, logf, 0.0)
        kk = jnp.where(valid, kk, 0.0)
    G = _chunk_cumsum(logf, L)
    sw.run(3)
    GL = [G[s * L + L - 1:s * L + L, :] for s in range(nseg)]
    fast_ok = jnp.min(functools.reduce(jnp.minimum, GL)) >= FAST_PATH_MIN_LOG_DECAY

    ti = lax.broadcasted_iota(jnp.int32, (L, L), 0)
    ji = lax.broadcasted_iota(jnp.int32, (L, L), 1)

    @pl.when(fast_ok)
    def _():
        d = G - _rows_bcast(G, L // 2 - 1, L)
        qt = (hq * jnp.exp(d)).astype(BF16)
        kt = (kk * jnp.exp(-d)).astype(BF16)
        causal = ji <= ti
        for s in range(nseg):
            rows = slice(s * L, (s + 1) * L)
            for h in range(H):
                hs = slice(h * HGRN_D, (h + 1) * HGRN_D)
                a_scr[s * H + h] = jnp.where(causal, _dot_nt(qt[rows, hs], kt[rows, hs]), 0.0)

    @pl.when(jnp.logical_not(fast_ok))
    def _():
        r = lax.broadcasted_iota(jnp.int32, (L, 1), 0)
        xo = ti ^ ji
        diag_mask = ti == ji
        lvl_masks = [(ji < ti) & (xo >= s) & (xo < 2 * s) for s in _levels(L)]
        for s in range(nseg):
            rows = slice(s * L, (s + 1) * L)
            factors = _level_factors(logf[rows, :], L, r)
            for h in range(H):
                hs = slice(h * HGRN_D, (h + 1) * HGRN_D)
                qh = hq[rows, hs]
                kh = kk[rows, hs]
                A = jnp.where(diag_mask, _dot_nt(qh.astype(BF16), kh.astype(BF16)), 0.0)
                for lvl in range(len(factors)):
                    E = factors[lvl][:, hs]
                    A = A + jnp.where(lvl_masks[lvl],
                                      _dot_nt((qh * E).astype(BF16), (kh * E).astype(BF16)), 0.0)
                a_scr[s * H + h] = A

    qg = (hq * jnp.exp(G)).astype(BF16)
    sw.run(2)
    GLb = GL[0] if nseg == 1 and L == rb else jnp.concatenate(
        [jnp.broadcast_to(gl, (L, D_MODEL)) for gl in GL], axis=0)
    kd = (kk * jnp.exp(GLb - G)).astype(BF16)
    sw.run(2)
    hv = load(2)
    vb = hv.astype(BF16)
    hz = load(3)
    zgate = _silu(hz)
    sw.flush()
    merged = L % LANES == 0
    for nb in range(NB):
        st = [st_scr[nb, h] for h in range(H)]
        for c in range(spn):
            s = nb * spn + c
            rows = slice(s * L, (s + 1) * L)
            dS = jnp.exp(GL[s])
            for h in range(H):
                hs = slice(h * HGRN_D, (h + 1) * HGRN_D)
                A = a_scr[s * H + h].astype(BF16)
                if merged:
                    vT = hv[rows, hs].T.astype(BF16)
                    o = _dot_nt(jnp.concatenate([qg[rows, hs], A], axis=1),
                                jnp.concatenate([st[h].astype(BF16), vT], axis=1))
                    st[h] = st[h] * dS[:, hs] + _dot(vT, kd[rows, hs])
                else:
                    o = _dot(A, vb[rows, hs]) + _dot_nt(qg[rows, hs], st[h].astype(BF16))
                    st[h] = st[h] * dS[:, hs] + _dot_tn(vb[rows, hs], kd[rows, hs])
                ms = jnp.mean(o * o, axis=-1, keepdims=True)
                y = o * lax.rsqrt(ms + EPS) * g_norm[:, hs] * zgate[rows, hs]
                y_ref[rows, hs] = y.astype(y_ref.dtype)
        for h in range(H):
            st_scr[nb, h] = st[h]

    @pl.when(t == nt - 1)
    def _():
        for nb in range(NB):
            for h in range(H):
                s_ref[nb, h] = st_scr[nb, h].T


def _hgrn_kernel(q_ref, f_ref, i_ref, z_ref, s0_ref, lbl_ref, g_ref, y_ref, s_ref, st_scr, a_scr,
                 *, nt, **static):
    cols = (q_ref, f_ref, i_ref, z_ref)
    _hgrn_step(lambda c: cols[c][...], lax.rem(pl.program_id(0), nt), nt,
               s0_ref, lbl_ref, g_ref, y_ref, s_ref, st_scr, a_scr, **static)


def _hgrn_fused_kernel(x0_ref, sc0_ref, sh0_ref, ng_ref, w_ref, b_ref, xn_ref, scn_ref, shn_ref,
                       lbl_ref, g_ref, y_ref, s_ref, hn_ref, st_scr, a_scr, proj_a, proj_b,
                       *, nt, **static):
    i = pl.program_id(0)

    def next_h():
        h = _rms_mod_bf16(xn_ref[...], scn_ref[0], shn_ref[0], ng_ref[...])
        hn_ref[...] = h
        return h

    def step(cur, side_work):
        _hgrn_step(lambda c: cur[:, c * D_MODEL:(c + 1) * D_MODEL], lax.rem(i, nt), nt,
                   None, lbl_ref, g_ref, y_ref, s_ref, st_scr, a_scr, side_work=side_work, **static)

    _fused_projection_schedule(i, proj_a, proj_b, x0_ref, sc0_ref, sh0_ref, ng_ref, w_ref, b_ref, next_h, step)


def _hgrn_specs(B, T, TB, NB, L):
    nt = T // TB
    rb = NB * TB
    s_spec = pl.BlockSpec((NB, HGRN_HEADS, HGRN_D, HGRN_D), lambda i: (i // nt, 0, 0, 0))
    in_tail = [pl.BlockSpec((2, D_MODEL), lambda i: (0, 0)), pl.BlockSpec((1, D_MODEL), lambda i: (0, 0))]
    out_specs = [pl.BlockSpec((rb, D_MODEL), lambda i: (i, 0)), s_spec]
    out_shape = [jax.ShapeDtypeStruct((B * T, D_MODEL), BF16),
                 jax.ShapeDtypeStruct((B, HGRN_HEADS, HGRN_D, HGRN_D), F32)]
    scratch = [pltpu.VMEM((NB, HGRN_HEADS, HGRN_D, HGRN_D), F32),
               pltpu.VMEM((rb // L * HGRN_HEADS, L, L), F32)]
    return nt, rb, s_spec, in_tail, out_specs, out_shape, scratch


def _hgrn(proj, s0, lb_logits, norm_g, *, B, T, TB, NB, L, Tv):
    nt, rb, s_spec, in_tail, out_specs, out_shape, scratch = _hgrn_specs(B, T, TB, NB, L)
    assert NB == 1 or nt == 1
    cols = [pl.BlockSpec((rb, D_MODEL), lambda i, c=c: (i, c)) for c in range(4)]
    return pl.pallas_call(
        functools.partial(_hgrn_kernel, nt=nt, NB=NB, TB=TB, L=L, Tv=Tv),
        grid=(B // NB * nt,),
        in_specs=cols + [s_spec] + in_tail, out_specs=out_specs, out_shape=out_shape, scratch_shapes=scratch,
        compiler_params=pltpu.CompilerParams(vmem_limit_bytes=VMEM_LIMIT),
        name="hgrn2",
    )(proj, proj, proj, proj, s0, lb_logits, norm_g)


def _hgrn_fused(x2d, scale3, shift3, norm_g, w, b, lb_logits, hgrn_norm_g, *, B, T, TB, L):
    nt, rb, _, in_tail, out_specs, out_shape, scratch = _hgrn_specs(B, T, TB, 1, L)
    n_blocks = B * nt
    return pl.pallas_call(
        functools.partial(_hgrn_fused_kernel, nt=nt, NB=1, TB=TB, L=L, Tv=L),
        grid=(n_blocks,),
        in_specs=_fused_projection_specs(n_blocks, nt, TB, N_HGRN, False) + in_tail,
        out_specs=out_specs + [pl.BlockSpec((TB, D_MODEL), lambda i: (i, 0))],
        out_shape=out_shape + [jax.ShapeDtypeStruct((B * T, D_MODEL), BF16)],
        scratch_shapes=scratch + [pltpu.VMEM((TB, N_HGRN), F32), pltpu.VMEM((TB, N_HGRN), F32)],
        compiler_params=pltpu.CompilerParams(vmem_limit_bytes=VMEM_LIMIT),
        name="hgrn2_fused",
    )(x2d, scale3, shift3, norm_g, w, b, x2d, scale3, shift3, lb_logits, hgrn_norm_g)


def _mlstm_step(load, t, nt, c0_ref, n0_ref, m0_ref, tail0_ref,
                cw_ref, cb_ref, wq_ref, wk_ref, wkt_ref, wv_ref, ng_ref, skip_ref,
                y_ref, c_ref, n_ref, m_ref, tail_scr, conv_ref,
                *, NB, TB, L, Tv, side_work=None):
    H = MLSTM_HEADS
    rb = NB * TB
    spn = TB // L
    if nt == 1:
        c_in, n_in, m_in, tail_in = c0_ref, n0_ref, m0_ref, tail0_ref
    else:
        c_in, n_in, m_in, tail_in = c_ref, n_ref, m_ref, tail_scr

        @pl.when(t == 0)
        def _():
            for dst, src in ((c_ref, c0_ref), (n_ref, n0_ref), (m_ref, m0_ref), (tail_scr, tail0_ref)):
                dst[...] = jnp.zeros(dst.shape, F32) if src is None else src[...]

    sw = side_work if side_work is not None else _SideWork()
    sw.run(1)

    cw = cw_ref[...]
    ng = ng_ref[...]
    skip = skip_ref[...]
    lane = lax.broadcasted_iota(jnp.int32, (1, LANES), 1)
    ti = lax.broadcasted_iota(jnp.int32, (L, L), 0)
    ji = lax.broadcasted_iota(jnp.int32, (L, L), 1)
    causal = ji <= ti

    mu = load(0)
    xcs = []
    for nb in range(NB):
        u_nb = mu[nb * TB:(nb + 1) * TB, :]
        ext = jnp.concatenate([tail_in[nb], u_nb], axis=0)
        conv = cb_ref[...] + cw[CONV_WIDTH - 1:CONV_WIDTH, :] * u_nb
        for i in range(1, CONV_WIDTH):
            conv = conv + cw[CONV_WIDTH - 1 - i:CONV_WIDTH - i, :] * pltpu.roll(ext, i, 0)[SUBLANES:, :]
            sw.run(1)
        xcs.append(_silu(conv))
        sw.run(1)
        if nt > 1:
            tail_scr[nb] = u_nb[TB - SUBLANES:, :]
    if conv_ref is not None:
        @pl.when(t == nt - 1)
        def _():
            for nb in range(NB):
                conv_ref[nb] = mu[(nb + 1) * TB - SUBLANES:(nb + 1) * TB, :]
    xc = xcs[0] if NB == 1 else jnp.concatenate(xcs, axis=0)
    xcb = xc.astype(BF16)
    mub = mu.astype(BF16)
    mz = load(1)
    zgate = _silu(mz)
    sw.run(2)
    ogate = _sigmoid(load(2))
    sw.run(2)

    gt = load(3)
    lf = jnp.minimum(gt, 0.0) - jnp.log1p(jnp.exp(-jnp.abs(gt)))
    ig = gt
    if Tv < L:
        valid = (lax.broadcasted_iota(jnp.int32, (rb, 1), 0) & (L - 1)) < Tv
        lf = jnp.where(valid, lf, 0.0)
        ig = jnp.where(valid, ig, -jnp.inf)
    bcum = _chunk_cumsum(lf, L)
    sw.run(1)
    comb = jnp.where(lane < H, ig, bcum)
    rpad = -rb % LANES
    if rpad:
        comb = jnp.concatenate([comb, jnp.zeros((rpad, LANES), F32)], axis=0)
    combT = comb.T

    assert spn == 1
    heads = {}

    def project_head(h):
        hs = slice(h * MLSTM_DH, (h + 1) * MLSTM_DH)
        q_all = _dot(xcb[:, hs], wq_ref[h])
        kb_all = _dot(xcb[:, hs], wk_ref[h]).astype(BF16)
        kT_all = _dot_nt(wkt_ref[h], xcb[:, hs])
        vb_all = _dot(mub[:, hs], wv_ref[h]).astype(BF16)
        heads[h] = (hs, q_all, q_all.astype(BF16), kb_all, kT_all, vb_all)
        sw.run(1)

    pairs = [(h, nb) for h in range(H) for nb in range(NB)]
    groups = [pairs] if NB > 1 else [[p] for p in pairs]
    rows_of = lambda nb: slice(nb * L, (nb + 1) * L)

    def run_group(pairs):
        for h in sorted({h for h, _ in pairs}):
            project_head(h)

        gates = []
        for h, nb in pairs:
            rows = rows_of(nb)
            bcol = bcum[rows, H + h:H + h + 1]
            irow = combT[h:h + 1, rows]
            brow = combT[H + h:H + h + 1, rows]
            logD = jnp.where(causal, (bcol - brow) + irow, -jnp.inf)
            m_intra = jnp.max(logD, axis=-1, keepdims=True)
            gates.append((bcol, irow, brow, m_intra, jnp.exp(logD - m_intra)))

        scores = []
        for (h, nb), g in zip(pairs, gates):
            _, _, qb_all, kb_all, _, _ = heads[h]
            rows = rows_of(nb)
            scores.append(_dot_nt(qb_all[rows], kb_all[rows]) * g[4])

        intra = []
        for (h, nb), sc in zip(pairs, scores):
            vb_all = heads[h][5]
            intra.append((jnp.sum(sc, axis=-1, keepdims=True), _dot(sc.astype(BF16), vb_all[rows_of(nb)])))

        writes = []
        for (h, nb), g in zip(pairs, gates):
            _, _, _, kb_all, kT_all, vb_all = heads[h]
            rows = rows_of(nb)
            bcol, irow, brow, m_intra, _ = g
            m_loc = m_intra[L - 1:L, :]
            b_last = bcol[L - 1:L, :]
            wrow = jnp.exp((b_last - brow) + irow - m_loc)
            U = _dot((kT_all[:, rows] * wrow).astype(BF16), vb_all[rows])
            ks = _dot(jnp.broadcast_to(wrow, (SUBLANES, L)).astype(BF16), kb_all[rows])[0:1, :]
            writes.append((m_loc, b_last, U, ks))

        outs = []
        for (h, nb), g, (rs, sv), (m_loc, b_last, U, ks) in zip(pairs, gates, intra, writes):
            hs, q_all, qb_all, _, _, _ = heads[h]
            rows = rows_of(nb)
            bcol, _, _, m_intra, _ = g
            C = c_in[nb, h]
            nh = n_in[nb, :, hs]
            m_prev = m_in[nb][:, h:h + 1]
            m_inter = bcol + m_prev
            m_t = jnp.maximum(m_inter, m_intra)
            inter = jnp.exp(m_inter - m_t)
            scl = jnp.exp(m_intra - m_t)
            den = inter * jnp.sum(q_all[rows] * nh, axis=-1, keepdims=True) + scl * rs
            rden = 1.0 / jnp.maximum(jnp.abs(den), jnp.exp(-m_t))
            hh = (inter * rden) * _dot(qb_all[rows], C.astype(BF16)) + (scl * rden) * sv
            m_new = m_t[L - 1:L, :]
            dec = jnp.exp(b_last + m_prev - m_new)
            scu = jnp.exp(m_loc - m_new)
            c_ref[nb, h] = dec * C + scu * U
            n_ref[nb, :, hs] = dec * nh + scu * ks
            outs.append((hh, m_new))

        for (h, nb), (hh, _) in zip(pairs, outs):
            hs = heads[h][0]
            rows = rows_of(nb)
            hm = ogate[rows, hs] * hh
            ms = jnp.mean(hm * hm, axis=-1, keepdims=True)
            y = (hm * lax.rsqrt(ms + EPS) * ng[:, hs] + skip[:, hs] * xc[rows, hs]) * zgate[rows, hs]
            y_ref[rows, hs] = y.astype(y_ref.dtype)
        return outs

    outs = [o for grp in groups for o in run_group(grp)]

    for nb in range(NB):
        m_row = m_in[nb]
        for h in range(H):
            m_row = jnp.where(lane == h, outs[h * NB + nb][1], m_row)
        m_ref[nb] = m_row
    sw.flush()


N_MLSTM_WEIGHTS = 8


def _mlstm_kernel(u_ref, z_ref, o_ref, gate_ref, *rest, nt, **static):
    cols = (u_ref, z_ref, o_ref, gate_ref)
    ins, (y_ref, c_ref, n_ref, m_ref, tail_scr) = rest[:4 + N_MLSTM_WEIGHTS], rest[4 + N_MLSTM_WEIGHTS:]
    _mlstm_step(lambda c: cols[c][...], lax.rem(pl.program_id(0), nt), nt, *ins,
                y_ref, c_ref, n_ref, m_ref, tail_scr, None, **static)


def _mlstm_fused_kernel(x0_ref, sc0_ref, sh0_ref, ng_ref, w_ref, b_ref, hn_ref, *rest, nt, **static):
    i = pl.program_id(0)
    weights = rest[:N_MLSTM_WEIGHTS]
    y_ref, c_ref, n_ref, m_ref, conv_ref, tail_scr, proj_a, proj_b = rest[N_MLSTM_WEIGHTS:]
    widths = (D_MODEL, D_MODEL, D_MODEL, LANES)

    def step(cur, side_work):
        _mlstm_step(lambda c: cur[:, c * D_MODEL:c * D_MODEL + widths[c]], lax.rem(i, nt), nt,
                    None, None, None, None, *weights,
                    y_ref, c_ref, n_ref, m_ref, tail_scr, conv_ref, side_work=side_work, **static)

    _fused_projection_schedule(i, proj_a, proj_b, x0_ref, sc0_ref, sh0_ref, ng_ref, w_ref, b_ref,
                               lambda: hn_ref[...], step)


def _mlstm_specs(B, T, TB, NB):
    nt = T // TB
    rb = NB * TB

    def full(shape):
        return pl.BlockSpec(shape, lambda i: (0,) * len(shape))

    c_spec = pl.BlockSpec((NB, MLSTM_HEADS, MLSTM_DH, MLSTM_DH), lambda i: (i // nt, 0, 0, 0))
    n_spec = pl.BlockSpec((NB, 1, D_MODEL), lambda i: (i // nt, 0, 0))
    m_spec = pl.BlockSpec((NB, 1, LANES), lambda i: (i // nt, 0, 0))
    t_spec = pl.BlockSpec((NB, SUBLANES, D_MODEL), lambda i: (i // nt, 0, 0))
    head_w = full((MLSTM_HEADS, MLSTM_DH, MLSTM_DH))
    state_specs = [c_spec, n_spec, m_spec, t_spec]
    weight_specs = [full((CONV_WIDTH, D_MODEL)), full((1, D_MODEL)),
                    head_w, head_w, head_w, head_w, full((1, D_MODEL)), full((1, D_MODEL))]
    out_specs = [pl.BlockSpec((rb, D_MODEL), lambda i: (i, 0)), c_spec, n_spec, m_spec]
    out_shape = [jax.ShapeDtypeStruct((B * T, D_MODEL), BF16),
                 jax.ShapeDtypeStruct((B, MLSTM_HEADS, MLSTM_DH, MLSTM_DH), F32),
                 jax.ShapeDtypeStruct((B, 1, D_MODEL), F32),
                 jax.ShapeDtypeStruct((B, 1, LANES), F32)]
    scratch = [pltpu.VMEM((NB, SUBLANES, D_MODEL), F32)]
    return nt, rb, state_specs, weight_specs, out_specs, out_shape, scratch, t_spec


def _mlstm(proj, states, weights, *, B, T, TB, NB, L, Tv):
    nt, rb, state_specs, weight_specs, out_specs, out_shape, scratch, _ = _mlstm_specs(B, T, TB, NB)
    assert NB == 1 or nt == 1
    cols = [pl.BlockSpec((rb, D_MODEL), lambda i, c=c: (i, c)) for c in range(3)]
    cols.append(pl.BlockSpec((rb, LANES), lambda i: (i, GATE_TILE)))
    return pl.pallas_call(
        functools.partial(_mlstm_kernel, nt=nt, NB=NB, TB=TB, L=L, Tv=Tv),
        grid=(B // NB * nt,),
        in_specs=cols + state_specs + weight_specs,
        out_specs=out_specs, out_shape=out_shape, scratch_shapes=scratch,
        compiler_params=pltpu.CompilerParams(vmem_limit_bytes=VMEM_LIMIT),
        name="mlstm",
    )(proj, proj, proj, proj, *states, *weights)


def _mlstm_fused(x2d, scale3, shift3, norm_g, w, b, h_next, weights, *, B, T, TB, L):
    nt, rb, _, weight_specs, out_specs, out_shape, scratch, t_spec = _mlstm_specs(B, T, TB, 1)
    n_blocks = B * nt
    return pl.pallas_call(
        functools.partial(_mlstm_fused_kernel, nt=nt, NB=1, TB=TB, L=L, Tv=L),
        grid=(n_blocks,),
        in_specs=_fused_projection_specs(n_blocks, nt, TB, N_MLSTM_PAD, True) + weight_specs,
        out_specs=out_specs + [t_spec],
        out_shape=out_shape + [jax.ShapeDtypeStruct((B, SUBLANES, D_MODEL), F32)],
        scratch_shapes=scratch + [pltpu.VMEM((TB, N_MLSTM_PAD), F32), pltpu.VMEM((TB, N_MLSTM_PAD), F32)],
        compiler_params=pltpu.CompilerParams(vmem_limit_bytes=VMEM_LIMIT),
        name="mlstm_fused",
    )(x2d, scale3, shift3, norm_g, w, b, h_next, *weights)


def _out_kernel(yh_ref, ym_ref, x_ref, gate_ref, w_ref, fg_ref, o_ref):
    acc = _dot(yh_ref[...], w_ref[0:D_MODEL, :]) + _dot(ym_ref[...], w_ref[D_MODEL:2 * D_MODEL, :])
    out = x_ref[...] + gate_ref[0] * acc
    ms = jnp.mean(out * out, axis=-1, keepdims=True)
    o_ref[...] = out * lax.rsqrt(ms + EPS) * fg_ref[...]


def _out_projection(yh, ym, x2d, gate3, w_out, final_g, rb):
    rows = x2d.shape[0]
    mrows = gate3.shape[1]
    nblk = rows // rb
    return pl.pallas_call(
        _out_kernel,
        grid=(nblk,),
        in_specs=[pl.BlockSpec((rb, D_MODEL), lambda i: (i, 0)),
                  pl.BlockSpec((rb, D_MODEL), lambda i: (i, 0)),
                  pl.BlockSpec((rb, D_MODEL), lambda i: (i, 0)),
                  pl.BlockSpec((1, mrows, D_MODEL), lambda i: (i * gate3.shape[0] // nblk, 0, 0)),
                  pl.BlockSpec((2 * D_MODEL, D_MODEL), lambda i: (0, 0)),
                  pl.BlockSpec((1, D_MODEL), lambda i: (0, 0))],
        out_specs=pl.BlockSpec((rb, D_MODEL), lambda i: (i, 0)),
        out_shape=jax.ShapeDtypeStruct((rows, D_MODEL), F32),
        compiler_params=pltpu.CompilerParams(vmem_limit_bytes=VMEM_LIMIT),
        name="out_projection",
    )(yh, ym, x2d, gate3, w_out, final_g)


def _block_diag_heads(w):
    rows = w.reshape(MLSTM_HEADS, MLSTM_DH, QKV_BLOCK)
    tiled = jnp.tile(rows, (1, 1, MLSTM_DH // QKV_BLOCK))
    rg = lax.broadcasted_iota(jnp.int32, (MLSTM_DH, MLSTM_DH), 0) // QKV_BLOCK
    cg = lax.broadcasted_iota(jnp.int32, (MLSTM_DH, MLSTM_DH), 1) // QKV_BLOCK
    return jnp.where(rg == cg, tiled, 0.0).astype(BF16)


def _mlstm_state_operands(c0, n0, m0, conv0):
    B = c0.shape[0]
    tail0 = jnp.pad(conv0, ((0, 0), (SUBLANES - (CONV_WIDTH - 1), 0), (0, 0)))
    m0p = jnp.pad(m0, ((0, 0), (0, LANES - MLSTM_HEADS))).reshape(B, 1, LANES)
    return (c0, n0.reshape(B, 1, D_MODEL), m0p, tail0)


def _unpack_mlstm_state(B, n_new, m_new):
    return n_new.reshape(B, MLSTM_HEADS, MLSTM_DH), m_new.reshape(B, LANES)[:, :MLSTM_HEADS]


def kernel(x_prompt, x_sample, c_prompt, c_sample, state_hgrn, state_mlstm_C, state_mlstm_n, state_mlstm_m, state_mlstm_conv, w_ada, b_ada, norm_g, w_in, b_in, hgrn_lb_logits, hgrn_norm_g, mlstm_conv_w, mlstm_conv_b, mlstm_wq, mlstm_wk, mlstm_wv, mlstm_norm_g, mlstm_skip, w_out, final_g):
    assert w_in.shape == (1, D_MODEL, N_PROJ) and hgrn_lb_logits.shape == (2, D_MODEL)
    Bp, Tp, _ = x_prompt.shape
    Bs, Ts, _ = x_sample.shape
    assert Tp % PROMPT_BLOCK == 0 and Ts <= SAMPLE_ROWS and Ts >= CONV_WIDTH - 1

    mod = _modulation(jnp.concatenate([c_prompt, c_sample], axis=0), w_ada[0], b_ada[0].reshape(1, -1))
    shift, scale, gate = mod[:, :D_MODEL], mod[:, D_MODEL:2 * D_MODEL], mod[:, 2 * D_MODEL:]
    pad_cols = N_MLSTM_PAD - N_MLSTM
    ng = norm_g[0].reshape(1, -1)
    w_h = w_in[0, :, :N_HGRN].astype(BF16)
    w_m = jnp.pad(w_in[0, :, N_HGRN:].astype(BF16), ((0, 0), (0, pad_cols)))
    b_h = b_in[0, :N_HGRN].reshape(1, -1)
    b_m = jnp.pad(b_in[0, N_HGRN:], (0, pad_cols)).reshape(1, -1)
    hg = hgrn_norm_g[0].reshape(1, -1)
    mw = (mlstm_conv_w[0], mlstm_conv_b[0].reshape(1, -1),
          _block_diag_heads(mlstm_wq[0]), _block_diag_heads(mlstm_wk[0] * KEY_SCALE),
          _block_diag_heads(jnp.swapaxes(mlstm_wk[0], -1, -2) * KEY_SCALE), _block_diag_heads(mlstm_wv[0]),
          mlstm_norm_g[0].reshape(1, -1), mlstm_skip[0].reshape(1, -1))
    wo = w_out[0].astype(BF16)
    fg = final_g.reshape(1, -1)

    xp2 = x_prompt.reshape(Bp * Tp, D_MODEL)
    per_seq = lambda a: a.reshape(-1, 1, D_MODEL)
    sc_p, sh_p = per_seq(scale[:Bp]), per_seq(shift[:Bp])
    yh_p, hg_p, h_next = _hgrn_fused(xp2, sc_p, sh_p, ng, w_h, b_h, hgrn_lb_logits, hg,
                                     B=Bp, T=Tp, TB=PROMPT_BLOCK, L=PROMPT_CHUNK)
    ym_p, c_p, n_p, m_p, tail_p = _mlstm_fused(xp2, sc_p, sh_p, ng, w_m, b_m, h_next, mw,
                                               B=Bp, T=Tp, TB=PROMPT_BLOCK, L=PROMPT_BLOCK)
    n_p, m_p = _unpack_mlstm_state(Bp, n_p, m_p)
    conv_p = tail_p[:, SUBLANES - (CONV_WIDTH - 1):]
    yp = _out_projection(yh_p, ym_p, xp2, per_seq(gate[:Bp]), wo, fg, 1024).reshape(Bp, Tp, D_MODEL)

    rows_s = Bs * SAMPLE_ROWS
    xs2 = jnp.pad(x_sample, ((0, 0), (0, SAMPLE_ROWS - Ts), (0, 0))).reshape(rows_s, D_MODEL)
    per_row = lambda a: jnp.repeat(a, SAMPLE_ROWS, axis=0).reshape(1, rows_s, D_MODEL)
    sc_s, sh_s = per_row(scale[Bp:]), per_row(shift[Bp:])
    proj_h = _in_projection(xs2, sc_s, sh_s, ng, w_h, b_h, rows_s, N_HGRN // 2)
    proj_m = _in_projection(xs2, sc_s, sh_s, ng, w_m, b_m, rows_s, N_MLSTM_PAD // 5)
    yh_s, hg_s = _hgrn(proj_h, state_hgrn[0], hgrn_lb_logits, hg,
                       B=Bs, T=SAMPLE_ROWS, TB=SAMPLE_ROWS, NB=8, L=SAMPLE_ROWS, Tv=Ts)
    states_s = _mlstm_state_operands(state_mlstm_C[0], state_mlstm_n[0], state_mlstm_m[0], state_mlstm_conv[0])
    ym_s, c_s, n_s, m_s = _mlstm(proj_m, states_s, mw,
                                 B=Bs, T=SAMPLE_ROWS, TB=SAMPLE_ROWS, NB=8, L=SAMPLE_ROWS, Tv=Ts)
    n_s, m_s = _unpack_mlstm_state(Bs, n_s, m_s)
    conv_s = proj_m.reshape(Bs, SAMPLE_ROWS, N_MLSTM_PAD)[:, Ts - (CONV_WIDTH - 1):Ts, :D_MODEL]
    ys = _out_projection(yh_s, ym_s, xs2, per_row(gate[Bp:]), wo, fg, rows_s).reshape(Bs, SAMPLE_ROWS, D_MODEL)

    return (yp, ys[:, :Ts], hg_p[None], c_p[None], n_p[None], m_p[None], conv_p[None],
            hg_s[None], c_s[None], n_s[None], m_s[None], conv_s[None])
```

```python
import functools

import jax
import jax.numpy as jnp
from jax import lax
from jax.experimental import pallas as pl
from jax.experimental.pallas import tpu as pltpu

F32 = jnp.float32
BF16 = jnp.bfloat16

D_MODEL = 1024
HGRN_HEADS = 8
HGRN_D = 128
MLSTM_HEADS = 4
MLSTM_DH = 256
KEY_SCALE = MLSTM_DH ** -0.5
CONV_WIDTH = 4
QKV_BLOCK = 4
EPS = 1e-6
N_PROJ = 7176
N_HGRN = 4 * D_MODEL
N_MLSTM = N_PROJ - N_HGRN
N_MLSTM_PAD = 3 * D_MODEL + 128
GATE_TILE = 3 * D_MODEL // 128
MXU_WIDTH = 256
PROMPT_CHUNK = 128
PROMPT_BLOCK = 256
SAMPLE_ROWS = 8
SUBLANES = 8
LANES = 128
VMEM_LIMIT = 56 * 1024 * 1024
FAST_PATH_MIN_LOG_DECAY = -80.0


def _dot(a, b):
    return jnp.dot(a, b, preferred_element_type=F32)


def _dot_nt(a, b):
    return lax.dot_general(a, b, (((1,), (1,)), ((), ())), preferred_element_type=F32)


def _dot_tn(a, b):
    return lax.dot_general(a, b, (((0,), (0,)), ((), ())), preferred_element_type=F32)


def _sigmoid(x):
    return 0.5 * jnp.tanh(0.5 * x) + 0.5


def _silu(x):
    u = 0.5 * x
    return u * jnp.tanh(u) + u


def _levels(L):
    out, s = [], 1
    while s < L:
        out.append(s)
        s *= 2
    return out


def _seg_bcast(W, s, L, r):
    n = W.shape[1]
    if s == 1:
        return jnp.where((r & 1) != 0, pltpu.roll(W, 1, 0), W)
    if s == 2:
        m = r & 3
        return jnp.where(m == 0, pltpu.roll(W, L - 1, 0),
                         jnp.where(m == 1, W,
                                   jnp.where(m == 2, pltpu.roll(W, 1, 0), pltpu.roll(W, 2, 0))))
    pieces = [jnp.broadcast_to(W[b * 2 * s + s - 1:b * 2 * s + s, :], (2 * s, n))
              for b in range(L // (2 * s))]
    return pieces[0] if len(pieces) == 1 else jnp.concatenate(pieces, axis=0)


def _level_factors(x, L, r):
    W = x
    factors = []
    for s in _levels(L):
        Tb = _seg_bcast(W, s, L, r)
        sec = (r & s) != 0
        factors.append(jnp.exp(jnp.where(sec, W, Tb - W)))
        W = W + jnp.where(sec, Tb, 0.0)
    return factors


def _prefix8(x):
    sub = lax.broadcasted_iota(jnp.int32, (SUBLANES, 1), 0)
    y = x + jnp.where(sub >= 1, pltpu.roll(x, 1, 0), 0.0)
    y = y + jnp.where(sub >= 2, pltpu.roll(y, 2, 0), 0.0)
    return y + jnp.where(sub >= 4, pltpu.roll(y, 4, 0), 0.0)


def _chunk_cumsum(x, L):
    outs = []
    for c in range(x.shape[0] // L):
        total = None
        for g in range(L // SUBLANES):
            lo = c * L + g * SUBLANES
            p = _prefix8(x[lo:lo + SUBLANES, :])
            if total is not None:
                p = p + total
            outs.append(p)
            total = p[SUBLANES - 1:SUBLANES, :]
    return outs[0] if len(outs) == 1 else jnp.concatenate(outs, axis=0)


def _rows_bcast(x, row_in_chunk, L):
    n = x.shape[1]
    pieces = [jnp.broadcast_to(x[c * L + row_in_chunk:c * L + row_in_chunk + 1, :], (L, n))
              for c in range(x.shape[0] // L)]
    return pieces[0] if len(pieces) == 1 else jnp.concatenate(pieces, axis=0)


def _rms_mod_bf16(x, scale, shift, g):
    ms = jnp.mean(x * x, axis=-1, keepdims=True)
    return ((x * lax.rsqrt(ms + EPS)) * (g * (1.0 + scale)) + shift).astype(BF16)


def _project_into(dst_ref, h, w_ref, b_ref):
    n = w_ref.shape[1]
    bounds = list(range(0, n - n % D_MODEL, D_MODEL)) or [0]
    for k, lo in enumerate(bounds):
        hi = n if k == len(bounds) - 1 else lo + D_MODEL
        dst_ref[:, lo:hi] = _dot(h, w_ref[:, lo:hi]) + b_ref[:, lo:hi]


def _mod_kernel(c_ref, w_ref, b_ref, o_ref):
    c = c_ref[...]
    a = _silu(c)
    o_ref[...] = _dot(a.astype(BF16), w_ref[...].astype(BF16)) + b_ref[...]


def _modulation(c_all, w_ada, b_ada):
    m = c_all.shape[0]
    n = w_ada.shape[1]
    tn = 512
    return pl.pallas_call(
        _mod_kernel,
        grid=(n // tn,),
        in_specs=[pl.BlockSpec((m, D_MODEL), lambda j: (0, 0)),
                  pl.BlockSpec((D_MODEL, tn), lambda j: (0, j)),
                  pl.BlockSpec((1, tn), lambda j: (0, j))],
        out_specs=pl.BlockSpec((m, tn), lambda j: (0, j)),
        out_shape=jax.ShapeDtypeStruct((m, n), F32),
        name="modulation",
    )(c_all, w_ada, b_ada)


def _split_cast_kernel(w_ref, wh_ref, wm_ref):
    wh_ref[...] = w_ref[:, :N_HGRN].astype(BF16)
    wm_ref[:, :N_MLSTM] = w_ref[:, N_HGRN:].astype(BF16)
    wm_ref[:, N_MLSTM:] = jnp.zeros((w_ref.shape[0], N_MLSTM_PAD - N_MLSTM), BF16)


def _split_cast_weights(w):
    rt = 128
    return pl.pallas_call(
        _split_cast_kernel,
        grid=(D_MODEL // rt,),
        in_specs=[pl.BlockSpec((rt, N_PROJ), lambda i: (i, 0))],
        out_specs=[pl.BlockSpec((rt, N_HGRN), lambda i: (i, 0)), pl.BlockSpec((rt, N_MLSTM_PAD), lambda i: (i, 0))],
        out_shape=[jax.ShapeDtypeStruct((D_MODEL, N_HGRN), BF16), jax.ShapeDtypeStruct((D_MODEL, N_MLSTM_PAD), BF16)],
        name="split_cast_weights",
    )(w)


def _inproj_kernel(x_ref, scale_ref, shift_ref, g_ref, w_ref, b_ref, o_ref, h_scr):
    @pl.when(pl.program_id(1) == 0)
    def _():
        h_scr[...] = _rms_mod_bf16(x_ref[...], scale_ref[0], shift_ref[0], g_ref[...])

    o_ref[...] = _dot(h_scr[...], w_ref[...]) + b_ref[...]


def _in_projection(x2d, scale3, shift3, norm_g, w, b, rb, col_tile):
    rows = x2d.shape[0]
    mrows = scale3.shape[1]
    nblk = rows // rb
    n_cols = w.shape[1]
    return pl.pallas_call(
        _inproj_kernel,
        grid=(nblk, n_cols // col_tile),
        in_specs=[pl.BlockSpec((rb, D_MODEL), lambda i, j: (i, 0)),
                  pl.BlockSpec((1, mrows, D_MODEL), lambda i, j: (i * scale3.shape[0] // nblk, 0, 0)),
                  pl.BlockSpec((1, mrows, D_MODEL), lambda i, j: (i * shift3.shape[0] // nblk, 0, 0)),
                  pl.BlockSpec((1, D_MODEL), lambda i, j: (0, 0)),
                  pl.BlockSpec((D_MODEL, col_tile), lambda i, j: (0, j)),
                  pl.BlockSpec((1, col_tile), lambda i, j: (0, j))],
        out_specs=pl.BlockSpec((rb, col_tile), lambda i, j: (i, j)),
        out_shape=jax.ShapeDtypeStruct((rows, n_cols), F32),
        scratch_shapes=[pltpu.VMEM((rb, D_MODEL), BF16)],
        compiler_params=pltpu.CompilerParams(vmem_limit_bytes=VMEM_LIMIT),
        name="in_projection",
    )(x2d, scale3, shift3, norm_g, w, b)


class _SideWork:
    def __init__(self, pieces=()):
        self._pieces = list(pieces)

    def run(self, n=1):
        for _ in range(n):
            if self._pieces:
                self._pieces.pop(0)()

    def flush(self):
        self.run(len(self._pieces))


def _fused_projection_schedule(i, proj_a, proj_b, x0_ref, sc0_ref, sh0_ref, ng_ref, w_ref, b_ref, next_h, step):
    n = w_ref.shape[1]

    @pl.when(i == 0)
    def _():
        _project_into(proj_a, _rms_mod_bf16(x0_ref[...], sc0_ref[0], sh0_ref[0], ng_ref[...]), w_ref, b_ref)

    def run(cur, nxt):
        h = []

        def norm_piece():
            h.append(next_h())

        def tile_piece(lo):
            hi = min(lo + MXU_WIDTH, n)

            def piece():
                nxt[:, lo:hi] = _dot(h[0], w_ref[:, lo:hi]) + b_ref[:, lo:hi]
            return piece

        step(cur, _SideWork([norm_piece] + [tile_piece(lo) for lo in range(0, n, MXU_WIDTH)]))

    parity = lax.rem(i, 2)
    pl.when(parity == 0)(lambda: run(proj_a, proj_b))
    pl.when(parity == 1)(lambda: run(proj_b, proj_a))


def _fused_projection_specs(n_blocks, nt, tb, n_cols, next_h_given):
    def nxt(i):
        return jnp.minimum(i + 1, n_blocks - 1)
    vec = (1, 1, D_MODEL)
    specs = [pl.BlockSpec((tb, D_MODEL), lambda i: (0, 0)),
             pl.BlockSpec(vec, lambda i: (0, 0, 0)), pl.BlockSpec(vec, lambda i: (0, 0, 0)),
             pl.BlockSpec((1, D_MODEL), lambda i: (0, 0)),
             pl.BlockSpec((D_MODEL, n_cols), lambda i: (0, 0)),
             pl.BlockSpec((1, n_cols), lambda i: (0, 0))]
    if next_h_given:
        return specs + [pl.BlockSpec((tb, D_MODEL), lambda i: (i, 0))]
    return specs + [pl.BlockSpec((tb, D_MODEL), lambda i: (nxt(i), 0)),
                    pl.BlockSpec(vec, lambda i: (nxt(i) // nt, 0, 0)),
                    pl.BlockSpec(vec, lambda i: (nxt(i) // nt, 0, 0))]


def _hgrn_step(load, t, nt, s0_ref, lbl_ref, g_ref, y_ref, s_ref, st_scr, a_scr,
               *, NB, TB, L, Tv, side_work=None):
    H = HGRN_HEADS
    rb = NB * TB
    spn = TB // L
    nseg = rb // L

    @pl.when(t == 0)
    def _():
        for nb in range(NB):
            for h in range(H):
                st_scr[nb, h] = jnp.zeros((HGRN_D, HGRN_D), F32) if s0_ref is None else s0_ref[nb, h].T

    sw = side_work if side_work is not None else _SideWork()
    sw.run(1)

    lg = lbl_ref[...]
    mx = jnp.max(lg, axis=0, keepdims=True)
    e = jnp.exp(lg - mx)
    lb = e[0:1, :] / jnp.sum(e, axis=0, keepdims=True)
    g_norm = g_ref[...]

    hq = load(0)
    sw.run(2)
    c1 = 0.5 * (1.0 - lb)
    p = c1 * jnp.tanh(0.5 * load(1))
    sw.run(2)
    logf = jnp.log((lb + c1) + p)
    sw.run(2)
    kk = c1 - p
    sw.run(2)
    if Tv < L:
        valid = (lax.broadcasted_iota(jnp.int32, (rb, 1), 0) & (L - 1)) < Tv
        logf = jnp.where(valid, logf, 0.0)
        kk = jnp.where(valid, kk, 0.0)
    G = _chunk_cumsum(logf, L)
    sw.run(3)
    GL = [G[s * L + L - 1:s * L + L, :] for s in range(nseg)]
    fast_ok = jnp.min(functools.reduce(jnp.minimum, GL)) >= FAST_PATH_MIN_LOG_DECAY

    ti = lax.broadcasted_iota(jnp.int32, (L, L), 0)
    ji = lax.broadcasted_iota(jnp.int32, (L, L), 1)

    @pl.when(fast_ok)
    def _():
        d = G - _rows_bcast(G, L // 2 - 1, L)
        qt = (hq * jnp.exp(d)).astype(BF16)
        kt = (kk * jnp.exp(-d)).astype(BF16)
        causal = ji <= ti
        for s in range(nseg):
            rows = slice(s * L, (s + 1) * L)
            for h in range(H):
                hs = slice(h * HGRN_D, (h + 1) * HGRN_D)
                a_scr[s * H + h] = jnp.where(causal, _dot_nt(qt[rows, hs], kt[rows, hs]), 0.0)

    @pl.when(jnp.logical_not(fast_ok))
    def _():
        r = lax.broadcasted_iota(jnp.int32, (L, 1), 0)
        xo = ti ^ ji
        diag_mask = ti == ji
        lvl_masks = [(ji < ti) & (xo >= s) & (xo < 2 * s) for s in _levels(L)]
        for s in range(nseg):
            rows = slice(s * L, (s + 1) * L)
            factors = _level_factors(logf[rows, :], L, r)
            for h in range(H):
                hs = slice(h * HGRN_D, (h + 1) * HGRN_D)
                qh = hq[rows, hs]
                kh = kk[rows, hs]
                A = jnp.where(diag_mask, _dot_nt(qh.astype(BF16), kh.astype(BF16)), 0.0)
                for lvl in range(len(factors)):
                    E = factors[lvl][:, hs]
                    A = A + jnp.where(lvl_masks[lvl],
                                      _dot_nt((qh * E).astype(BF16), (kh * E).astype(BF16)), 0.0)
                a_scr[s * H + h] = A

    qg = (hq * jnp.exp(G)).astype(BF16)
    sw.run(2)
    GLb = GL[0] if nseg == 1 and L == rb else jnp.concatenate(
        [jnp.broadcast_to(gl, (L, D_MODEL)) for gl in GL], axis=0)
    kd = (kk * jnp.exp(GLb - G)).astype(BF16)
    sw.run(2)
    hv = load(2)
    vb = hv.astype(BF16)
    hz = load(3)
    zgate = _silu(hz)
    sw.flush()
    merged = L % LANES == 0
    for nb in range(NB):
        st = [st_scr[nb, h] for h in range(H)]
        for c in range(spn):
            s = nb * spn + c
            rows = slice(s * L, (s + 1) * L)
            dS = jnp.exp(GL[s])
            for h in range(H):
                hs = slice(h * HGRN_D, (h + 1) * HGRN_D)
                A = a_scr[s * H + h].astype(BF16)
                if merged:
                    vT = hv[rows, hs].T.astype(BF16)
                    o = _dot_nt(jnp.concatenate([qg[rows, hs], A], axis=1),
                                jnp.concatenate([st[h].astype(BF16), vT], axis=1))
                    st[h] = st[h] * dS[:, hs] + _dot(vT, kd[rows, hs])
                else:
                    o = _dot(A, vb[rows, hs]) + _dot_nt(qg[rows, hs], st[h].astype(BF16))
                    st[h] = st[h] * dS[:, hs] + _dot_tn(vb[rows, hs], kd[rows, hs])
                ms = jnp.mean(o * o, axis=-1, keepdims=True)
                y = o * lax.rsqrt(ms + EPS) * g_norm[:, hs] * zgate[rows, hs]
                y_ref[rows, hs] = y.astype(y_ref.dtype)
        for h in range(H):
            st_scr[nb, h] = st[h]

    @pl.when(t == nt - 1)
    def _():
        for nb in range(NB):
            for h in range(H):
                s_ref[nb, h] = st_scr[nb, h].T


def _hgrn_kernel(q_ref, f_ref, i_ref, z_ref, s0_ref, lbl_ref, g_ref, y_ref, s_ref, st_scr, a_scr,
                 *, nt, **static):
    cols = (q_ref, f_ref, i_ref, z_ref)
    _hgrn_step(lambda c: cols[c][...], lax.rem(pl.program_id(0), nt), nt,
               s0_ref, lbl_ref, g_ref, y_ref, s_ref, st_scr, a_scr, **static)


def _hgrn_fused_kernel(x0_ref, sc0_ref, sh0_ref, ng_ref, w_ref, b_ref, xn_ref, scn_ref, shn_ref,
                       lbl_ref, g_ref, y_ref, s_ref, hn_ref, st_scr, a_scr, proj_a, proj_b,
                       *, nt, **static):
    i = pl.program_id(0)

    def next_h():
        h = _rms_mod_bf16(xn_ref[...], scn_ref[0], shn_ref[0], ng_ref[...])
        hn_ref[...] = h
        return h

    def step(cur, side_work):
        _hgrn_step(lambda c: cur[:, c * D_MODEL:(c + 1) * D_MODEL], lax.rem(i, nt), nt,
                   None, lbl_ref, g_ref, y_ref, s_ref, st_scr, a_scr, side_work=side_work, **static)

    _fused_projection_schedule(i, proj_a, proj_b, x0_ref, sc0_ref, sh0_ref, ng_ref, w_ref, b_ref, next_h, step)


def _hgrn_specs(B, T, TB, NB, L):
    nt = T // TB
    rb = NB * TB
    s_spec = pl.BlockSpec((NB, HGRN_HEADS, HGRN_D, HGRN_D), lambda i: (i // nt, 0, 0, 0))
    in_tail = [pl.BlockSpec((2, D_MODEL), lambda i: (0, 0)), pl.BlockSpec((1, D_MODEL), lambda i: (0, 0))]
    out_specs = [pl.BlockSpec((rb, D_MODEL), lambda i: (i, 0)), s_spec]
    out_shape = [jax.ShapeDtypeStruct((B * T, D_MODEL), BF16),
                 jax.ShapeDtypeStruct((B, HGRN_HEADS, HGRN_D, HGRN_D), F32)]
    scratch = [pltpu.VMEM((NB, HGRN_HEADS, HGRN_D, HGRN_D), F32),
               pltpu.VMEM((rb // L * HGRN_HEADS, L, L), F32)]
    return nt, rb, s_spec, in_tail, out_specs, out_shape, scratch


def _hgrn(proj, s0, lb_logits, norm_g, *, B, T, TB, NB, L, Tv):
    nt, rb, s_spec, in_tail, out_specs, out_shape, scratch = _hgrn_specs(B, T, TB, NB, L)
    assert NB == 1 or nt == 1
    cols = [pl.BlockSpec((rb, D_MODEL), lambda i, c=c: (i, c)) for c in range(4)]
    return pl.pallas_call(
        functools.partial(_hgrn_kernel, nt=nt, NB=NB, TB=TB, L=L, Tv=Tv),
        grid=(B // NB * nt,),
        in_specs=cols + [s_spec] + in_tail, out_specs=out_specs, out_shape=out_shape, scratch_shapes=scratch,
        compiler_params=pltpu.CompilerParams(vmem_limit_bytes=VMEM_LIMIT),
        name="hgrn2",
    )(proj, proj, proj, proj, s0, lb_logits, norm_g)


def _hgrn_fused(x2d, scale3, shift3, norm_g, w, b, lb_logits, hgrn_norm_g, *, B, T, TB, L):
    nt, rb, _, in_tail, out_specs, out_shape, scratch = _hgrn_specs(B, T, TB, 1, L)
    n_blocks = B * nt
    return pl.pallas_call(
        functools.partial(_hgrn_fused_kernel, nt=nt, NB=1, TB=TB, L=L, Tv=L),
        grid=(n_blocks,),
        in_specs=_fused_projection_specs(n_blocks, nt, TB, N_HGRN, False) + in_tail,
        out_specs=out_specs + [pl.BlockSpec((TB, D_MODEL), lambda i: (i, 0))],
        out_shape=out_shape + [jax.ShapeDtypeStruct((B * T, D_MODEL), BF16)],
        scratch_shapes=scratch + [pltpu.VMEM((TB, N_HGRN), F32), pltpu.VMEM((TB, N_HGRN), F32)],
        compiler_params=pltpu.CompilerParams(vmem_limit_bytes=VMEM_LIMIT),
        name="hgrn2_fused",
    )(x2d, scale3, shift3, norm_g, w, b, x2d, scale3, shift3, lb_logits, hgrn_norm_g)


def _mlstm_step(load, t, nt, c0_ref, n0_ref, m0_ref, tail0_ref,
                cw_ref, cb_ref, wq_ref, wk_ref, wkt_ref, wv_ref, ng_ref, skip_ref,
                y_ref, c_ref, n_ref, m_ref, tail_scr, conv_ref,
                *, NB, TB, L, Tv, side_work=None):
    H = MLSTM_HEADS
    rb = NB * TB
    spn = TB // L
    if nt == 1:
        c_in, n_in, m_in, tail_in = c0_ref, n0_ref, m0_ref, tail0_ref
    else:
        c_in, n_in, m_in, tail_in = c_ref, n_ref, m_ref, tail_scr

        @pl.when(t == 0)
        def _():
            for dst, src in ((c_ref, c0_ref), (n_ref, n0_ref), (m_ref, m0_ref), (tail_scr, tail0_ref)):
                dst[...] = jnp.zeros(dst.shape, F32) if src is None else src[...]

    sw = side_work if side_work is not None else _SideWork()
    sw.run(1)

    cw = cw_ref[...]
    ng = ng_ref[...]
    skip = skip_ref[...]
    lane = lax.broadcasted_iota(jnp.int32, (1, LANES), 1)
    ti = lax.broadcasted_iota(jnp.int32, (L, L), 0)
    ji = lax.broadcasted_iota(jnp.int32, (L, L), 1)
    causal = ji <= ti

    mu = load(0)
    xcs = []
    for nb in range(NB):
        u_nb = mu[nb * TB:(nb + 1) * TB, :]
        ext = jnp.concatenate([tail_in[nb], u_nb], axis=0)
        conv = cb_ref[...] + cw[CONV_WIDTH - 1:CONV_WIDTH, :] * u_nb
        for i in range(1, CONV_WIDTH):
            conv = conv + cw[CONV_WIDTH - 1 - i:CONV_WIDTH - i, :] * pltpu.roll(ext, i, 0)[SUBLANES:, :]
            sw.run(1)
        xcs.append(_silu(conv))
        sw.run(1)
        if nt > 1:
            tail_scr[nb] = u_nb[TB - SUBLANES:, :]
    if conv_ref is not None:
        @pl.when(t == nt - 1)
        def _():
            for nb in range(NB):
                conv_ref[nb] = mu[(nb + 1) * TB - SUBLANES:(nb + 1) * TB, :]
    xc = xcs[0] if NB == 1 else jnp.concatenate(xcs, axis=0)
    xcb = xc.astype(BF16)
    mub = mu.astype(BF16)
    mz = load(1)
    zgate = _silu(mz)
    sw.run(2)
    ogate = _sigmoid(load(2))
    sw.run(2)

    gt = load(3)
    lf = jnp.minimum(gt, 0.0) - jnp.log1p(jnp.exp(-jnp.abs(gt)))
    ig = gt
    if Tv < L:
        valid = (lax.broadcasted_iota(jnp.int32, (rb, 1), 0) & (L - 1)) < Tv
        lf = jnp.where(valid, lf, 0.0)
        ig = jnp.where(valid, ig, -jnp.inf)
    bcum = _chunk_cumsum(lf, L)
    sw.run(1)
    comb = jnp.where(lane < H, ig, bcum)
    rpad = -rb % LANES
    if rpad:
        comb = jnp.concatenate([comb, jnp.zeros((rpad, LANES), F32)], axis=0)
    combT = comb.T

    assert spn == 1
    heads = {}

    def project_head(h):
        hs = slice(h * MLSTM_DH, (h + 1) * MLSTM_DH)
        q_all = _dot(xcb[:, hs], wq_ref[h])
        kb_all = _dot(xcb[:, hs], wk_ref[h]).astype(BF16)
        kT_all = _dot_nt(wkt_ref[h], xcb[:, hs])
        vb_all = _dot(mub[:, hs], wv_ref[h]).astype(BF16)
        heads[h] = (hs, q_all, q_all.astype(BF16), kb_all, kT_all, vb_all)
        sw.run(1)

    pairs = [(h, nb) for h in range(H) for nb in range(NB)]
    groups = [pairs] if NB > 1 else [[p] for p in pairs]
    rows_of = lambda nb: slice(nb * L, (nb + 1) * L)

    def run_group(pairs):
        for h in sorted({h for h, _ in pairs}):
            project_head(h)

        gates = []
        for h, nb in pairs:
            rows = rows_of(nb)
            bcol = bcum[rows, H + h:H + h + 1]
            irow = combT[h:h + 1, rows]
            brow = combT[H + h:H + h + 1, rows]
            logD = jnp.where(causal, (bcol - brow) + irow, -jnp.inf)
            m_intra = jnp.max(logD, axis=-1, keepdims=True)
            gates.append((bcol, irow, brow, m_intra, jnp.exp(logD - m_intra)))

        scores = []
        for (h, nb), g in zip(pairs, gates):
            _, _, qb_all, kb_all, _, _ = heads[h]
            rows = rows_of(nb)
            scores.append(_dot_nt(qb_all[rows], kb_all[rows]) * g[4])

        intra = []
        for (h, nb), sc in zip(pairs, scores):
            vb_all = heads[h][5]
            intra.append((jnp.sum(sc, axis=-1, keepdims=True), _dot(sc.astype(BF16), vb_all[rows_of(nb)])))

        writes = []
        for (h, nb), g in zip(pairs, gates):
            _, _, _, kb_all, kT_all, vb_all = heads[h]
            rows = rows_of(nb)
            bcol, irow, brow, m_intra, _ = g
            m_loc = m_intra[L - 1:L, :]
            b_last = bcol[L - 1:L, :]
            wrow = jnp.exp((b_last - brow) + irow - m_loc)
            U = _dot((kT_all[:, rows] * wrow).astype(BF16), vb_all[rows])
            ks = _dot(jnp.broadcast_to(wrow, (SUBLANES, L)).astype(BF16), kb_all[rows])[0:1, :]
            writes.append((m_loc, b_last, U, ks))

        outs = []
        for (h, nb), g, (rs, sv), (m_loc, b_last, U, ks) in zip(pairs, gates, intra, writes):
            hs, q_all, qb_all, _, _, _ = heads[h]
            rows = rows_of(nb)
            bcol, _, _, m_intra, _ = g
            C = c_in[nb, h]
            nh = n_in[nb, :, hs]
            m_prev = m_in[nb][:, h:h + 1]
            m_inter = bcol + m_prev
            m_t = jnp.maximum(m_inter, m_intra)
            inter = jnp.exp(m_inter - m_t)
            scl = jnp.exp(m_intra - m_t)
            den = inter * jnp.sum(q_all[rows] * nh, axis=-1, keepdims=True) + scl * rs
            rden = 1.0 / jnp.maximum(jnp.abs(den), jnp.exp(-m_t))
            hh = (inter * rden) * _dot(qb_all[rows], C.astype(BF16)) + (scl * rden) * sv
            m_new = m_t[L - 1:L, :]
            dec = jnp.exp(b_last + m_prev - m_new)
            scu = jnp.exp(m_loc - m_new)
            c_ref[nb, h] = dec * C + scu * U
            n_ref[nb, :, hs] = dec * nh + scu * ks
            outs.append((hh, m_new))

        for (h, nb), (hh, _) in zip(pairs, outs):
            hs = heads[h][0]
            rows = rows_of(nb)
            hm = ogate[rows, hs] * hh
            ms = jnp.mean(hm * hm, axis=-1, keepdims=True)
            y = (hm * lax.rsqrt(ms + EPS) * ng[:, hs] + skip[:, hs] * xc[rows, hs]) * zgate[rows, hs]
            y_ref[rows, hs] = y.astype(y_ref.dtype)
        return outs

    outs = [o for grp in groups for o in run_group(grp)]

    for nb in range(NB):
        m_row = m_in[nb]
        for h in range(H):
            m_row = jnp.where(lane == h, outs[h * NB + nb][1], m_row)
        m_ref[nb] = m_row
    sw.flush()


N_MLSTM_WEIGHTS = 8


def _mlstm_kernel(u_ref, z_ref, o_ref, gate_ref, *rest, nt, **static):
    cols = (u_ref, z_ref, o_ref, gate_ref)
    ins, (y_ref, c_ref, n_ref, m_ref, tail_scr) = rest[:4 + N_MLSTM_WEIGHTS], rest[4 + N_MLSTM_WEIGHTS:]
    _mlstm_step(lambda c: cols[c][...], lax.rem(pl.program_id(0), nt), nt, *ins,
                y_ref, c_ref, n_ref, m_ref, tail_scr, None, **static)


def _mlstm_fused_kernel(x0_ref, sc0_ref, sh0_ref, ng_ref, w_ref, b_ref, hn_ref, *rest, nt, **static):
    i = pl.program_id(0)
    weights = rest[:N_MLSTM_WEIGHTS]
    y_ref, c_ref, n_ref, m_ref, conv_ref, tail_scr, proj_a, proj_b = rest[N_MLSTM_WEIGHTS:]
    widths = (D_MODEL, D_MODEL, D_MODEL, LANES)

    def step(cur, side_work):
        _mlstm_step(lambda c: cur[:, c * D_MODEL:c * D_MODEL + widths[c]], lax.rem(i, nt), nt,
                    None, None, None, None, *weights,
                    y_ref, c_ref, n_ref, m_ref, tail_scr, conv_ref, side_work=side_work, **static)

    _fused_projection_schedule(i, proj_a, proj_b, x0_ref, sc0_ref, sh0_ref, ng_ref, w_ref, b_ref,
                               lambda: hn_ref[...], step)


def _mlstm_specs(B, T, TB, NB):
    nt = T // TB
    rb = NB * TB

    def full(shape):
        return pl.BlockSpec(shape, lambda i: (0,) * len(shape))

    c_spec = pl.BlockSpec((NB, MLSTM_HEADS, MLSTM_DH, MLSTM_DH), lambda i: (i // nt, 0, 0, 0))
    n_spec = pl.BlockSpec((NB, 1, D_MODEL), lambda i: (i // nt, 0, 0))
    m_spec = pl.BlockSpec((NB, 1, LANES), lambda i: (i // nt, 0, 0))
    t_spec = pl.BlockSpec((NB, SUBLANES, D_MODEL), lambda i: (i // nt, 0, 0))
    head_w = full((MLSTM_HEADS, MLSTM_DH, MLSTM_DH))
    state_specs = [c_spec, n_spec, m_spec, t_spec]
    weight_specs = [full((CONV_WIDTH, D_MODEL)), full((1, D_MODEL)),
                    head_w, head_w, head_w, head_w, full((1, D_MODEL)), full((1, D_MODEL))]
    out_specs = [pl.BlockSpec((rb, D_MODEL), lambda i: (i, 0)), c_spec, n_spec, m_spec]
    out_shape = [jax.ShapeDtypeStruct((B * T, D_MODEL), BF16),
                 jax.ShapeDtypeStruct((B, MLSTM_HEADS, MLSTM_DH, MLSTM_DH), F32),
                 jax.ShapeDtypeStruct((B, 1, D_MODEL), F32),
                 jax.ShapeDtypeStruct((B, 1, LANES), F32)]
    scratch = [pltpu.VMEM((NB, SUBLANES, D_MODEL), F32)]
    return nt, rb, state_specs, weight_specs, out_specs, out_shape, scratch, t_spec


def _mlstm(proj, states, weights, *, B, T, TB, NB, L, Tv):
    nt, rb, state_specs, weight_specs, out_specs, out_shape, scratch, _ = _mlstm_specs(B, T, TB, NB)
    assert NB == 1 or nt == 1
    cols = [pl.BlockSpec((rb, D_MODEL), lambda i, c=c: (i, c)) for c in range(3)]
    cols.append(pl.BlockSpec((rb, LANES), lambda i: (i, GATE_TILE)))
    return pl.pallas_call(
        functools.partial(_mlstm_kernel, nt=nt, NB=NB, TB=TB, L=L, Tv=Tv),
        grid=(B // NB * nt,),
        in_specs=cols + state_specs + weight_specs,
        out_specs=out_specs, out_shape=out_shape, scratch_shapes=scratch,
        compiler_params=pltpu.CompilerParams(vmem_limit_bytes=VMEM_LIMIT),
        name="mlstm",
    )(proj, proj, proj, proj, *states, *weights)


def _mlstm_fused(x2d, scale3, shift3, norm_g, w, b, h_next, weights, *, B, T, TB, L):
    nt, rb, _, weight_specs, out_specs, out_shape, scratch, t_spec = _mlstm_specs(B, T, TB, 1)
    n_blocks = B * nt
    return pl.pallas_call(
        functools.partial(_mlstm_fused_kernel, nt=nt, NB=1, TB=TB, L=L, Tv=L),
        grid=(n_blocks,),
        in_specs=_fused_projection_specs(n_blocks, nt, TB, N_MLSTM_PAD, True) + weight_specs,
        out_specs=out_specs + [t_spec],
        out_shape=out_shape + [jax.ShapeDtypeStruct((B, SUBLANES, D_MODEL), F32)],
        scratch_shapes=scratch + [pltpu.VMEM((TB, N_MLSTM_PAD), F32), pltpu.VMEM((TB, N_MLSTM_PAD), F32)],
        compiler_params=pltpu.CompilerParams(vmem_limit_bytes=VMEM_LIMIT),
        name="mlstm_fused",
    )(x2d, scale3, shift3, norm_g, w, b, h_next, *weights)


def _out_kernel(yh_ref, ym_ref, x_ref, gate_ref, w_ref, fg_ref, o_ref):
    acc = _dot(jnp.concatenate([yh_ref[...], ym_ref[...]], axis=1), w_ref[...])
    out = x_ref[...] + gate_ref[0] * acc
    ms = jnp.mean(out * out, axis=-1, keepdims=True)
    o_ref[...] = out * lax.rsqrt(ms + EPS) * fg_ref[...]


def _out_projection(yh, ym, x2d, gate3, w_out, final_g, rb):
    rows = x2d.shape[0]
    mrows = gate3.shape[1]
    nblk = rows // rb
    return pl.pallas_call(
        _out_kernel,
        grid=(nblk,),
        in_specs=[pl.BlockSpec((rb, D_MODEL), lambda i: (i, 0)),
                  pl.BlockSpec((rb, D_MODEL), lambda i: (i, 0)),
                  pl.BlockSpec((rb, D_MODEL), lambda i: (i, 0)),
                  pl.BlockSpec((1, mrows, D_MODEL), lambda i: (i * gate3.shape[0] // nblk, 0, 0)),
                  pl.BlockSpec((2 * D_MODEL, D_MODEL), lambda i: (0, 0)),
                  pl.BlockSpec((1, D_MODEL), lambda i: (0, 0))],
        out_specs=pl.BlockSpec((rb, D_MODEL), lambda i: (i, 0)),
        out_shape=jax.ShapeDtypeStruct((rows, D_MODEL), F32),
        compiler_params=pltpu.CompilerParams(vmem_limit_bytes=VMEM_LIMIT),
        name="out_projection",
    )(yh, ym, x2d, gate3, w_out, final_g)


def _block_diag_heads(w):
    rows = w.reshape(MLSTM_HEADS, MLSTM_DH, QKV_BLOCK)
    tiled = jnp.tile(rows, (1, 1, MLSTM_DH // QKV_BLOCK))
    rg = lax.broadcasted_iota(jnp.int32, (MLSTM_DH, MLSTM_DH), 0) // QKV_BLOCK
    cg = lax.broadcasted_iota(jnp.int32, (MLSTM_DH, MLSTM_DH), 1) // QKV_BLOCK
    return jnp.where(rg == cg, tiled, 0.0).astype(BF16)


def _mlstm_state_operands(c0, n0, m0, conv0):
    B = c0.shape[0]
    tail0 = jnp.pad(conv0, ((0, 0), (SUBLANES - (CONV_WIDTH - 1), 0), (0, 0)))
    m0p = jnp.pad(m0, ((0, 0), (0, LANES - MLSTM_HEADS))).reshape(B, 1, LANES)
    return (c0, n0.reshape(B, 1, D_MODEL), m0p, tail0)


def _unpack_mlstm_state(B, n_new, m_new):
    return n_new.reshape(B, MLSTM_HEADS, MLSTM_DH), m_new.reshape(B, LANES)[:, :MLSTM_HEADS]


def kernel(x_prompt, x_sample, c_prompt, c_sample, state_hgrn, state_mlstm_C, state_mlstm_n, state_mlstm_m, state_mlstm_conv, w_ada, b_ada, norm_g, w_in, b_in, hgrn_lb_logits, hgrn_norm_g, mlstm_conv_w, mlstm_conv_b, mlstm_wq, mlstm_wk, mlstm_wv, mlstm_norm_g, mlstm_skip, w_out, final_g):
    assert w_in.shape == (1, D_MODEL, N_PROJ) and hgrn_lb_logits.shape == (2, D_MODEL)
    Bp, Tp, _ = x_prompt.shape
    Bs, Ts, _ = x_sample.shape
    assert Tp % PROMPT_BLOCK == 0 and Ts <= SAMPLE_ROWS and Ts >= CONV_WIDTH - 1

    mod = _modulation(jnp.concatenate([c_prompt, c_sample], axis=0), w_ada[0], b_ada[0].reshape(1, -1))
    shift, scale, gate = mod[:, :D_MODEL], mod[:, D_MODEL:2 * D_MODEL], mod[:, 2 * D_MODEL:]
    pad_cols = N_MLSTM_PAD - N_MLSTM
    ng = norm_g[0].reshape(1, -1)
    w_h, w_m = _split_cast_weights(w_in[0])
    b_h = b_in[0, :N_HGRN].reshape(1, -1)
    b_m = jnp.pad(b_in[0, N_HGRN:], (0, pad_cols)).reshape(1, -1)
    hg = hgrn_norm_g[0].reshape(1, -1)
    mw = (mlstm_conv_w[0], mlstm_conv_b[0].reshape(1, -1),
          _block_diag_heads(mlstm_wq[0]), _block_diag_heads(mlstm_wk[0] * KEY_SCALE),
          _block_diag_heads(jnp.swapaxes(mlstm_wk[0], -1, -2) * KEY_SCALE), _block_diag_heads(mlstm_wv[0]),
          mlstm_norm_g[0].reshape(1, -1), mlstm_skip[0].reshape(1, -1))
    wo = w_out[0].astype(BF16)
    fg = final_g.reshape(1, -1)

    xp2 = x_prompt.reshape(Bp * Tp, D_MODEL)
    per_seq = lambda a: a.reshape(-1, 1, D_MODEL)
    sc_p, sh_p = per_seq(scale[:Bp]), per_seq(shift[:Bp])
    yh_p, hg_p, h_next = _hgrn_fused(xp2, sc_p, sh_p, ng, w_h, b_h, hgrn_lb_logits, hg,
                                     B=Bp, T=Tp, TB=PROMPT_BLOCK, L=PROMPT_CHUNK)
    ym_p, c_p, n_p, m_p, tail_p = _mlstm_fused(xp2, sc_p, sh_p, ng, w_m, b_m, h_next, mw,
                                               B=Bp, T=Tp, TB=PROMPT_BLOCK, L=PROMPT_BLOCK)
    n_p, m_p = _unpack_mlstm_state(Bp, n_p, m_p)
    conv_p = tail_p[:, SUBLANES - (CONV_WIDTH - 1):]
    yp = _out_projection(yh_p, ym_p, xp2, per_seq(gate[:Bp]), wo, fg, 1024).reshape(Bp, Tp, D_MODEL)

    rows_s = Bs * SAMPLE_ROWS
    xs2 = jnp.pad(x_sample, ((0, 0), (0, SAMPLE_ROWS - Ts), (0, 0))).reshape(rows_s, D_MODEL)
    per_row = lambda a: jnp.repeat(a, SAMPLE_ROWS, axis=0).reshape(1, rows_s, D_MODEL)
    sc_s, sh_s = per_row(scale[Bp:]), per_row(shift[Bp:])
    proj_h = _in_projection(xs2, sc_s, sh_s, ng, w_h, b_h, rows_s, N_HGRN // 2)
    proj_m = _in_projection(xs2, sc_s, sh_s, ng, w_m, b_m, rows_s, N_MLSTM_PAD // 5)
    yh_s, hg_s = _hgrn(proj_h, state_hgrn[0], hgrn_lb_logits, hg,
                       B=Bs, T=SAMPLE_ROWS, TB=SAMPLE_ROWS, NB=8, L=SAMPLE_ROWS, Tv=Ts)
    states_s = _mlstm_state_operands(state_mlstm_C[0], state_mlstm_n[0], state_mlstm_m[0], state_mlstm_conv[0])
    ym_s, c_s, n_s, m_s = _mlstm(proj_m, states_s, mw,
                                 B=Bs, T=SAMPLE_ROWS, TB=SAMPLE_ROWS, NB=8, L=SAMPLE_ROWS, Tv=Ts)
    n_s, m_s = _unpack_mlstm_state(Bs, n_s, m_s)
    conv_s = proj_m.reshape(Bs, SAMPLE_ROWS, N_MLSTM_PAD)[:, Ts - (CONV_WIDTH - 1):Ts, :D_MODEL]
    ys = _out_projection(yh_s, ym_s, xs2, per_row(gate[Bp:]), wo, fg, rows_s).reshape(Bs, SAMPLE_ROWS, D_MODEL)

    return (yp, ys[:, :Ts], hg_p[None], c_p[None], n_p[None], m_p[None], conv_p[None],
            hg_s[None], c_s[None], n_s[None], m_s[None], conv_s[None])
```

```python
import functools

import jax
import jax.numpy as jnp
from jax import lax
from jax.experimental import pallas as pl
from jax.experimental.pallas import tpu as pltpu

F32 = jnp.float32
BF16 = jnp.bfloat16

D_MODEL = 1024
HGRN_HEADS = 8
HGRN_D = 128
MLSTM_HEADS = 4
MLSTM_DH = 256
KEY_SCALE = MLSTM_DH ** -0.5
CONV_WIDTH = 4
QKV_BLOCK = 4
EPS = 1e-6
N_PROJ = 7176
N_HGRN = 4 * D_MODEL
N_MLSTM = N_PROJ - N_HGRN
N_MLSTM_PAD = 3 * D_MODEL + 128
GATE_TILE = 3 * D_MODEL // 128
MXU_WIDTH = 256
PROMPT_CHUNK = 128
PROMPT_BLOCK = 256
SAMPLE_ROWS = 8
SUBLANES = 8
LANES = 128
VMEM_LIMIT = 56 * 1024 * 1024
FAST_PATH_MIN_LOG_DECAY = -80.0


def _dot(a, b):
    return jnp.dot(a, b, preferred_element_type=F32)


def _dot_nt(a, b):
    return lax.dot_general(a, b, (((1,), (1,)), ((), ())), preferred_element_type=F32)


def _dot_tn(a, b):
    return lax.dot_general(a, b, (((0,), (0,)), ((), ())), preferred_element_type=F32)


def _sigmoid(x):
    return 0.5 * jnp.tanh(0.5 * x) + 0.5


def _silu(x):
    u = 0.5 * x
    return u * jnp.tanh(u) + u


def _levels(L):
    out, s = [], 1
    while s < L:
        out.append(s)
        s *= 2
    return out


def _seg_bcast(W, s, L, r):
    n = W.shape[1]
    if s == 1:
        return jnp.where((r & 1) != 0, pltpu.roll(W, 1, 0), W)
    if s == 2:
        m = r & 3
        return jnp.where(m == 0, pltpu.roll(W, L - 1, 0),
                         jnp.where(m == 1, W,
                                   jnp.where(m == 2, pltpu.roll(W, 1, 0), pltpu.roll(W, 2, 0))))
    pieces = [jnp.broadcast_to(W[b * 2 * s + s - 1:b * 2 * s + s, :], (2 * s, n))
              for b in range(L // (2 * s))]
    return pieces[0] if len(pieces) == 1 else jnp.concatenate(pieces, axis=0)


def _level_factors(x, L, r):
    W = x
    factors = []
    for s in _levels(L):
        Tb = _seg_bcast(W, s, L, r)
        sec = (r & s) != 0
        factors.append(jnp.exp(jnp.where(sec, W, Tb - W)))
        W = W + jnp.where(sec, Tb, 0.0)
    return factors


def _prefix8(x):
    sub = lax.broadcasted_iota(jnp.int32, (SUBLANES, 1), 0)
    y = x + jnp.where(sub >= 1, pltpu.roll(x, 1, 0), 0.0)
    y = y + jnp.where(sub >= 2, pltpu.roll(y, 2, 0), 0.0)
    return y + jnp.where(sub >= 4, pltpu.roll(y, 4, 0), 0.0)


def _chunk_cumsum(x, L):
    outs = []
    for c in range(x.shape[0] // L):
        total = None
        for g in range(L // SUBLANES):
            lo = c * L + g * SUBLANES
            p = _prefix8(x[lo:lo + SUBLANES, :])
            if total is not None:
                p = p + total
            outs.append(p)
            total = p[SUBLANES - 1:SUBLANES, :]
    return outs[0] if len(outs) == 1 else jnp.concatenate(outs, axis=0)


def _rows_bcast(x, row_in_chunk, L):
    n = x.shape[1]
    pieces = [jnp.broadcast_to(x[c * L + row_in_chunk:c * L + row_in_chunk + 1, :], (L, n))
              for c in range(x.shape[0] // L)]
    return pieces[0] if len(pieces) == 1 else jnp.concatenate(pieces, axis=0)


def _rms_mod_bf16(x, scale, shift, g):
    ms = jnp.mean(x * x, axis=-1, keepdims=True)
    return ((x * lax.rsqrt(ms + EPS)) * (g * (1.0 + scale)) + shift).astype(BF16)


def _project_into(dst_ref, h, w_ref, b_ref):
    n = w_ref.shape[1]
    bounds = list(range(0, n - n % D_MODEL, D_MODEL)) or [0]
    for k, lo in enumerate(bounds):
        hi = n if k == len(bounds) - 1 else lo + D_MODEL
        dst_ref[:, lo:hi] = _dot(h, w_ref[:, lo:hi]) + b_ref[:, lo:hi]


def _mod_kernel(c_ref, w_ref, b_ref, o_ref):
    c = c_ref[...]
    a = _silu(c)
    o_ref[...] = _dot(a.astype(BF16), w_ref[...].astype(BF16)) + b_ref[...]


def _modulation(c_all, w_ada, b_ada):
    m = c_all.shape[0]
    n = w_ada.shape[1]
    tn = 512
    return pl.pallas_call(
        _mod_kernel,
        grid=(n // tn,),
        in_specs=[pl.BlockSpec((m, D_MODEL), lambda j: (0, 0)),
                  pl.BlockSpec((D_MODEL, tn), lambda j: (0, j)),
                  pl.BlockSpec((1, tn), lambda j: (0, j))],
        out_specs=pl.BlockSpec((m, tn), lambda j: (0, j)),
        out_shape=jax.ShapeDtypeStruct((m, n), F32),
        name="modulation",
    )(c_all, w_ada, b_ada)


def _inproj_kernel(x_ref, scale_ref, shift_ref, g_ref, w_ref, b_ref, o_ref, h_scr):
    @pl.when(pl.program_id(1) == 0)
    def _():
        h_scr[...] = _rms_mod_bf16(x_ref[...], scale_ref[0], shift_ref[0], g_ref[...])

    o_ref[...] = _dot(h_scr[...], w_ref[...]) + b_ref[...]


def _in_projection(x2d, scale3, shift3, norm_g, w, b, rb, col_tile):
    rows = x2d.shape[0]
    mrows = scale3.shape[1]
    nblk = rows // rb
    n_cols = w.shape[1]
    return pl.pallas_call(
        _inproj_kernel,
        grid=(nblk, n_cols // col_tile),
        in_specs=[pl.BlockSpec((rb, D_MODEL), lambda i, j: (i, 0)),
                  pl.BlockSpec((1, mrows, D_MODEL), lambda i, j: (i * scale3.shape[0] // nblk, 0, 0)),
                  pl.BlockSpec((1, mrows, D_MODEL), lambda i, j: (i * shift3.shape[0] // nblk, 0, 0)),
                  pl.BlockSpec((1, D_MODEL), lambda i, j: (0, 0)),
                  pl.BlockSpec((D_MODEL, col_tile), lambda i, j: (0, j)),
                  pl.BlockSpec((1, col_tile), lambda i, j: (0, j))],
        out_specs=pl.BlockSpec((rb, col_tile), lambda i, j: (i, j)),
        out_shape=jax.ShapeDtypeStruct((rows, n_cols), F32),
        scratch_shapes=[pltpu.VMEM((rb, D_MODEL), BF16)],
        compiler_params=pltpu.CompilerParams(vmem_limit_bytes=VMEM_LIMIT),
        name="in_projection",
    )(x2d, scale3, shift3, norm_g, w, b)


class _SideWork:
    def __init__(self, pieces=()):
        self._pieces = list(pieces)

    def run(self, n=1):
        for _ in range(n):
            if self._pieces:
                self._pieces.pop(0)()

    def flush(self):
        self.run(len(self._pieces))


def _fused_projection_schedule(i, proj_a, proj_b, x0_ref, sc0_ref, sh0_ref, ng_ref, w_ref, b_ref, next_h, step):
    n = w_ref.shape[1]

    @pl.when(i == 0)
    def _():
        _project_into(proj_a, _rms_mod_bf16(x0_ref[...], sc0_ref[0], sh0_ref[0], ng_ref[...]), w_ref, b_ref)

    def run(cur, nxt):
        h = []

        def norm_piece():
            h.append(next_h())

        def tile_piece(lo):
            hi = min(lo + MXU_WIDTH, n)

            def piece():
                nxt[:, lo:hi] = _dot(h[0], w_ref[:, lo:hi]) + b_ref[:, lo:hi]
            return piece

        step(cur, _SideWork([norm_piece] + [tile_piece(lo) for lo in range(0, n, MXU_WIDTH)]))

    parity = lax.rem(i, 2)
    pl.when(parity == 0)(lambda: run(proj_a, proj_b))
    pl.when(parity == 1)(lambda: run(proj_b, proj_a))


def _fused_projection_specs(n_blocks, nt, tb, n_cols, next_h_given):
    def nxt(i):
        return jnp.minimum(i + 1, n_blocks - 1)
    vec = (1, 1, D_MODEL)
    specs = [pl.BlockSpec((tb, D_MODEL), lambda i: (0, 0)),
             pl.BlockSpec(vec, lambda i: (0, 0, 0)), pl.BlockSpec(vec, lambda i: (0, 0, 0)),
             pl.BlockSpec((1, D_MODEL), lambda i: (0, 0)),
             pl.BlockSpec((D_MODEL, n_cols), lambda i: (0, 0)),
             pl.BlockSpec((1, n_cols), lambda i: (0, 0))]
    if next_h_given:
        return specs + [pl.BlockSpec((tb, D_MODEL), lambda i: (i, 0))]
    return specs + [pl.BlockSpec((tb, D_MODEL), lambda i: (nxt(i), 0)),
                    pl.BlockSpec(vec, lambda i: (nxt(i) // nt, 0, 0)),
                    pl.BlockSpec(vec, lambda i: (nxt(i) // nt, 0, 0))]


def _hgrn_step(load, t, nt, s0_ref, lbl_ref, g_ref, y_ref, s_ref, st_scr, a_scr,
               *, NB, TB, L, Tv, side_work=None):
    H = HGRN_HEADS
    rb = NB * TB
    spn = TB // L
    nseg = rb // L

    @pl.when(t == 0)
    def _():
        for nb in range(NB):
            for h in range(H):
                st_scr[nb, h] = jnp.zeros((HGRN_D, HGRN_D), F32) if s0_ref is None else s0_ref[nb, h].T

    sw = side_work if side_work is not None else _SideWork()
    sw.run(1)

    lg = lbl_ref[...]
    mx = jnp.max(lg, axis=0, keepdims=True)
    e = jnp.exp(lg - mx)
    lb = e[0:1, :] / jnp.sum(e, axis=0, keepdims=True)
    g_norm = g_ref[...]

    hq = load(0)
    sw.run(2)
    c1 = 0.5 * (1.0 - lb)
    p = c1 * jnp.tanh(0.5 * load(1))
    sw.run(2)
    logf = jnp.log((lb + c1) + p)
    sw.run(2)
    kk = c1 - p
    sw.run(2)
    if Tv < L:
        valid = (lax.broadcasted_iota(jnp.int32, (rb, 1), 0) & (L - 1)) < Tv
        logf = jnp.where(valid, logf, 0.0)
        kk = jnp.where(valid, kk, 0.0)
    G = _chunk_cumsum(logf, L)
    sw.run(3)
    GL = [G[s * L + L - 1:s * L + L, :] for s in range(nseg)]
    fast_ok = jnp.min(functools.reduce(jnp.minimum, GL)) >= FAST_PATH_MIN_LOG_DECAY

    ti = lax.broadcasted_iota(jnp.int32, (L, L), 0)
    ji = lax.broadcasted_iota(jnp.int32, (L, L), 1)

    @pl.when(fast_ok)
    def _():
        d = G - _rows_bcast(G, L // 2 - 1, L)
        qt = (hq * jnp.exp(d)).astype(BF16)
        kt = (kk * jnp.exp(-d)).astype(BF16)
        causal = ji <= ti
        for s in range(nseg):
            rows = slice(s * L, (s + 1) * L)
            for h in range(H):
                hs = slice(h * HGRN_D, (h + 1) * HGRN_D)
                a_scr[s * H + h] = jnp.where(causal, _dot_nt(qt[rows, hs], kt[rows, hs]), 0.0)

    @pl.when(jnp.logical_not(fast_ok))
    def _():
        r = lax.broadcasted_iota(jnp.int32, (L, 1), 0)
        xo = ti ^ ji
        diag_mask = ti == ji
        lvl_masks = [(ji < ti) & (xo >= s) & (xo < 2 * s) for s in _levels(L)]
        for s in range(nseg):
            rows = slice(s * L, (s + 1) * L)
            factors = _level_factors(logf[rows, :], L, r)
            for h in range(H):
                hs = slice(h * HGRN_D, (h + 1) * HGRN_D)
                qh = hq[rows, hs]
                kh = kk[rows, hs]
                A = jnp.where(diag_mask, _dot_nt(qh.astype(BF16), kh.astype(BF16)), 0.0)
                for lvl in range(len(factors)):
                    E = factors[lvl][:, hs]
                    A = A + jnp.where(lvl_masks[lvl],
                                      _dot_nt((qh * E).astype(BF16), (kh * E).astype(BF16)), 0.0)
                a_scr[s * H + h] = A

    qg = (hq * jnp.exp(G)).astype(BF16)
    sw.run(2)
    GLb = GL[0] if nseg == 1 and L == rb else jnp.concatenate(
        [jnp.broadcast_to(gl, (L, D_MODEL)) for gl in GL], axis=0)
    kd = (kk * jnp.exp(GLb - G)).astype(BF16)
    sw.run(2)
    hv = load(2)
    vb = hv.astype(BF16)
    hz = load(3)
    zgate = _silu(hz)
    sw.flush()
    merged = L % LANES == 0
    for nb in range(NB):
        st = [st_scr[nb, h] for h in range(H)]
        for c in range(spn):
            s = nb * spn + c
            rows = slice(s * L, (s + 1) * L)
            dS = jnp.exp(GL[s])
            for h in range(H):
                hs = slice(h * HGRN_D, (h + 1) * HGRN_D)
                A = a_scr[s * H + h].astype(BF16)
                if merged:
                    vT = hv[rows, hs].T.astype(BF16)
                    o = _dot_nt(jnp.concatenate([qg[rows, hs], A], axis=1),
                                jnp.concatenate([st[h].astype(BF16), vT], axis=1))
                    st[h] = st[h] * dS[:, hs] + _dot(vT, kd[rows, hs])
                else:
                    o = _dot(A, vb[rows, hs]) + _dot_nt(qg[rows, hs], st[h].astype(BF16))
                    st[h] = st[h] * dS[:, hs] + _dot_tn(vb[rows, hs], kd[rows, hs])
                ms = jnp.mean(o * o, axis=-1, keepdims=True)
                y = o * lax.rsqrt(ms + EPS) * g_norm[:, hs] * zgate[rows, hs]
                y_ref[rows, hs] = y.astype(y_ref.dtype)
        for h in range(H):
            st_scr[nb, h] = st[h]

    @pl.when(t == nt - 1)
    def _():
        for nb in range(NB):
            for h in range(H):
                s_ref[nb, h] = st_scr[nb, h].T


def _hgrn_kernel(q_ref, f_ref, i_ref, z_ref, s0_ref, lbl_ref, g_ref, y_ref, s_ref, st_scr, a_scr,
                 *, nt, **static):
    cols = (q_ref, f_ref, i_ref, z_ref)
    _hgrn_step(lambda c: cols[c][...], lax.rem(pl.program_id(0), nt), nt,
               s0_ref, lbl_ref, g_ref, y_ref, s_ref, st_scr, a_scr, **static)


def _hgrn_fused_kernel(x0_ref, sc0_ref, sh0_ref, ng_ref, w_ref, b_ref, xn_ref, scn_ref, shn_ref,
                       lbl_ref, g_ref, y_ref, s_ref, hn_ref, st_scr, a_scr, proj_a, proj_b,
                       *, nt, **static):
    i = pl.program_id(0)

    def next_h():
        h = _rms_mod_bf16(xn_ref[...], scn_ref[0], shn_ref[0], ng_ref[...])
        hn_ref[...] = h
        return h

    def step(cur, side_work):
        _hgrn_step(lambda c: cur[:, c * D_MODEL:(c + 1) * D_MODEL], lax.rem(i, nt), nt,
                   None, lbl_ref, g_ref, y_ref, s_ref, st_scr, a_scr, side_work=side_work, **static)

    _fused_projection_schedule(i, proj_a, proj_b, x0_ref, sc0_ref, sh0_ref, ng_ref, w_ref, b_ref, next_h, step)


def _hgrn_specs(B, T, TB, NB, L):
    nt = T // TB
    rb = NB * TB
    s_spec = pl.BlockSpec((NB, HGRN_HEADS, HGRN_D, HGRN_D), lambda i: (i // nt, 0, 0, 0))
    in_tail = [pl.BlockSpec((2, D_MODEL), lambda i: (0, 0)), pl.BlockSpec((1, D_MODEL), lambda i: (0, 0))]
    out_specs = [pl.BlockSpec((rb, D_MODEL), lambda i: (i, 0)), s_spec]
    out_shape = [jax.ShapeDtypeStruct((B * T, D_MODEL), BF16),
                 jax.ShapeDtypeStruct((B, HGRN_HEADS, HGRN_D, HGRN_D), F32)]
    scratch = [pltpu.VMEM((NB, HGRN_HEADS, HGRN_D, HGRN_D), F32),
               pltpu.VMEM((rb // L * HGRN_HEADS, L, L), F32)]
    return nt, rb, s_spec, in_tail, out_specs, out_shape, scratch


def _hgrn(proj, s0, lb_logits, norm_g, *, B, T, TB, NB, L, Tv):
    nt, rb, s_spec, in_tail, out_specs, out_shape, scratch = _hgrn_specs(B, T, TB, NB, L)
    assert NB == 1 or nt == 1
    cols = [pl.BlockSpec((rb, D_MODEL), lambda i, c=c: (i, c)) for c in range(4)]
    return pl.pallas_call(
        functools.partial(_hgrn_kernel, nt=nt, NB=NB, TB=TB, L=L, Tv=Tv),
        grid=(B // NB * nt,),
        in_specs=cols + [s_spec] + in_tail, out_specs=out_specs, out_shape=out_shape, scratch_shapes=scratch,
        compiler_params=pltpu.CompilerParams(vmem_limit_bytes=VMEM_LIMIT),
        name="hgrn2",
    )(proj, proj, proj, proj, s0, lb_logits, norm_g)


def _hgrn_fused(x2d, scale3, shift3, norm_g, w, b, lb_logits, hgrn_norm_g, *, B, T, TB, L):
    nt, rb, _, in_tail, out_specs, out_shape, scratch = _hgrn_specs(B, T, TB, 1, L)
    n_blocks = B * nt
    return pl.pallas_call(
        functools.partial(_hgrn_fused_kernel, nt=nt, NB=1, TB=TB, L=L, Tv=L),
        grid=(n_blocks,),
        in_specs=_fused_projection_specs(n_blocks, nt, TB, N_HGRN, False) + in_tail,
        out_specs=out_specs + [pl.BlockSpec((TB, D_MODEL), lambda i: (i, 0))],
        out_shape=out_shape + [jax.ShapeDtypeStruct((B * T, D_MODEL), BF16)],
        scratch_shapes=scratch + [pltpu.VMEM((TB, N_HGRN), F32), pltpu.VMEM((TB, N_HGRN), F32)],
        compiler_params=pltpu.CompilerParams(vmem_limit_bytes=VMEM_LIMIT),
        name="hgrn2_fused",
    )(x2d, scale3, shift3, norm_g, w, b, x2d, scale3, shift3, lb_logits, hgrn_norm_g)


def _mlstm_step(load, t, nt, c0_ref, n0_ref, m0_ref, tail0_ref,
                cw_ref, cb_ref, wq_ref, wk_ref, wkt_ref, wv_ref, ng_ref, skip_ref,
                y_ref, c_ref, n_ref, m_ref, tail_scr, conv_ref,
                *, NB, TB, L, Tv, side_work=None):
    H = MLSTM_HEADS
    rb = NB * TB
    spn = TB // L
    if nt == 1:
        c_in, n_in, m_in, tail_in = c0_ref, n0_ref, m0_ref, tail0_ref
    else:
        c_in, n_in, m_in, tail_in = c_ref, n_ref, m_ref, tail_scr

        @pl.when(t == 0)
        def _():
            for dst, src in ((c_ref, c0_ref), (n_ref, n0_ref), (m_ref, m0_ref), (tail_scr, tail0_ref)):
                dst[...] = jnp.zeros(dst.shape, F32) if src is None else src[...]

    sw = side_work if side_work is not None else _SideWork()
    sw.run(1)

    cw = cw_ref[...]
    ng = ng_ref[...]
    skip = skip_ref[...]
    lane = lax.broadcasted_iota(jnp.int32, (1, LANES), 1)
    ti = lax.broadcasted_iota(jnp.int32, (L, L), 0)
    ji = lax.broadcasted_iota(jnp.int32, (L, L), 1)
    causal = ji <= ti

    mu = load(0)
    xcs = []
    for nb in range(NB):
        u_nb = mu[nb * TB:(nb + 1) * TB, :]
        ext = jnp.concatenate([tail_in[nb], u_nb], axis=0)
        conv = cb_ref[...] + cw[CONV_WIDTH - 1:CONV_WIDTH, :] * u_nb
        for i in range(1, CONV_WIDTH):
            conv = conv + cw[CONV_WIDTH - 1 - i:CONV_WIDTH - i, :] * pltpu.roll(ext, i, 0)[SUBLANES:, :]
            sw.run(1)
        xcs.append(_silu(conv))
        sw.run(1)
        if nt > 1:
            tail_scr[nb] = u_nb[TB - SUBLANES:, :]
    if conv_ref is not None:
        @pl.when(t == nt - 1)
        def _():
            for nb in range(NB):
                conv_ref[nb] = mu[(nb + 1) * TB - SUBLANES:(nb + 1) * TB, :]
    xc = xcs[0] if NB == 1 else jnp.concatenate(xcs, axis=0)
    xcb = xc.astype(BF16)
    mub = mu.astype(BF16)
    mz = load(1)
    zgate = _silu(mz)
    sw.run(2)
    ogate = _sigmoid(load(2))
    sw.run(2)

    gt = load(3)
    lf = jnp.minimum(gt, 0.0) - jnp.log1p(jnp.exp(-jnp.abs(gt)))
    ig = gt
    if Tv < L:
        valid = (lax.broadcasted_iota(jnp.int32, (rb, 1), 0) & (L - 1)) < Tv
        lf = jnp.where(valid, lf, 0.0)
        ig = jnp.where(valid, ig, -jnp.inf)
    bcum = _chunk_cumsum(lf, L)
    sw.run(1)
    comb = jnp.where(lane < H, ig, bcum)
    rpad = -rb % LANES
    if rpad:
        comb = jnp.concatenate([comb, jnp.zeros((rpad, LANES), F32)], axis=0)
    combT = comb.T

    assert spn == 1
    heads = {}

    def project_head(h):
        hs = slice(h * MLSTM_DH, (h + 1) * MLSTM_DH)
        q_all = _dot(xcb[:, hs], wq_ref[h])
        kb_all = _dot(xcb[:, hs], wk_ref[h]).astype(BF16)
        kT_all = _dot_nt(wkt_ref[h], xcb[:, hs])
        vb_all = _dot(mub[:, hs], wv_ref[h]).astype(BF16)
        heads[h] = (hs, q_all, q_all.astype(BF16), kb_all, kT_all, vb_all)
        sw.run(1)

    pairs = [(h, nb) for h in range(H) for nb in range(NB)]
    groups = [pairs] if NB > 1 else [[p] for p in pairs]
    rows_of = lambda nb: slice(nb * L, (nb + 1) * L)

    def run_group(pairs):
        for h in sorted({h for h, _ in pairs}):
            project_head(h)

        gates = []
        for h, nb in pairs:
            rows = rows_of(nb)
            bcol = bcum[rows, H + h:H + h + 1]
            irow = combT[h:h + 1, rows]
            brow = combT[H + h:H + h + 1, rows]
            logD = jnp.where(causal, (bcol - brow) + irow, -jnp.inf)
            m_intra = jnp.max(logD, axis=-1, keepdims=True)
            gates.append((bcol, irow, brow, m_intra, jnp.exp(logD - m_intra)))

        scores = []
        for (h, nb), g in zip(pairs, gates):
            _, _, qb_all, kb_all, _, _ = heads[h]
            rows = rows_of(nb)
            scores.append(_dot_nt(qb_all[rows], kb_all[rows]) * g[4])

        intra = []
        for (h, nb), sc in zip(pairs, scores):
            vb_all = heads[h][5]
            intra.append((jnp.sum(sc, axis=-1, keepdims=True), _dot(sc.astype(BF16), vb_all[rows_of(nb)])))

        writes = []
        for (h, nb), g in zip(pairs, gates):
            _, _, _, kb_all, kT_all, vb_all = heads[h]
            rows = rows_of(nb)
            bcol, irow, brow, m_intra, _ = g
            m_loc = m_intra[L - 1:L, :]
            b_last = bcol[L - 1:L, :]
            wrow = jnp.exp((b_last - brow) + irow - m_loc)
            kwT = (kT_all[:, rows] * wrow).astype(BF16)
            ks = _dot(jnp.broadcast_to(wrow, (SUBLANES, L)).astype(BF16), kb_all[rows])[0:1, :]
            writes.append((m_loc, b_last, kwT, ks))

        outs = []
        updates = []
        for (h, nb), g, (rs, sv), (m_loc, b_last, kwT, ks) in zip(pairs, gates, intra, writes):
            hs, q_all, qb_all, _, _, _ = heads[h]
            rows = rows_of(nb)
            bcol, _, _, m_intra, _ = g
            C = c_in[nb, h]
            nh = n_in[nb, :, hs]
            m_prev = m_in[nb][:, h:h + 1]
            m_inter = bcol + m_prev
            m_t = jnp.maximum(m_inter, m_intra)
            inter = jnp.exp(m_inter - m_t)
            scl = jnp.exp(m_intra - m_t)
            den = inter * jnp.sum(q_all[rows] * nh, axis=-1, keepdims=True) + scl * rs
            rden = 1.0 / jnp.maximum(jnp.abs(den), jnp.exp(-m_t))
            hh = (inter * rden) * _dot(qb_all[rows], C.astype(BF16)) + (scl * rden) * sv
            m_new = m_t[L - 1:L, :]
            dec = jnp.exp(b_last + m_prev - m_new)
            scu = jnp.exp(m_loc - m_new)
            n_ref[nb, :, hs] = dec * nh + scu * ks
            outs.append((hh, m_new))
            updates.append((dec, scu, kwT))

        for (h, nb), (dec, scu, kwT) in zip(pairs, updates):
            c_ref[nb, h] = dec * c_in[nb, h] + scu * _dot(kwT, heads[h][5][rows_of(nb)])

        for (h, nb), (hh, _) in zip(pairs, outs):
            hs = heads[h][0]
            rows = rows_of(nb)
            hm = ogate[rows, hs] * hh
            ms = jnp.mean(hm * hm, axis=-1, keepdims=True)
            y = (hm * lax.rsqrt(ms + EPS) * ng[:, hs] + skip[:, hs] * xc[rows, hs]) * zgate[rows, hs]
            y_ref[rows, hs] = y.astype(y_ref.dtype)
        return outs

    outs = [o for grp in groups for o in run_group(grp)]

    for nb in range(NB):
        m_row = m_in[nb]
        for h in range(H):
            m_row = jnp.where(lane == h, outs[h * NB + nb][1], m_row)
        m_ref[nb] = m_row
    sw.flush()


N_MLSTM_WEIGHTS = 8


def _mlstm_kernel(u_ref, z_ref, o_ref, gate_ref, *rest, nt, **static):
    cols = (u_ref, z_ref, o_ref, gate_ref)
    ins, (y_ref, c_ref, n_ref, m_ref, tail_scr) = rest[:4 + N_MLSTM_WEIGHTS], rest[4 + N_MLSTM_WEIGHTS:]
    _mlstm_step(lambda c: cols[c][...], lax.rem(pl.program_id(0), nt), nt, *ins,
                y_ref, c_ref, n_ref, m_ref, tail_scr, None, **static)


def _mlstm_fused_kernel(x0_ref, sc0_ref, sh0_ref, ng_ref, w_ref, b_ref, hn_ref, *rest, nt, **static):
    i = pl.program_id(0)
    weights = rest[:N_MLSTM_WEIGHTS]
    y_ref, c_ref, n_ref, m_ref, conv_ref, tail_scr, proj_a, proj_b = rest[N_MLSTM_WEIGHTS:]
    widths = (D_MODEL, D_MODEL, D_MODEL, LANES)

    def step(cur, side_work):
        _mlstm_step(lambda c: cur[:, c * D_MODEL:c * D_MODEL + widths[c]], lax.rem(i, nt), nt,
                    None, None, None, None, *weights,
                    y_ref, c_ref, n_ref, m_ref, tail_scr, conv_ref, side_work=side_work, **static)

    _fused_projection_schedule(i, proj_a, proj_b, x0_ref, sc0_ref, sh0_ref, ng_ref, w_ref, b_ref,
                               lambda: hn_ref[...], step)


def _mlstm_specs(B, T, TB, NB):
    nt = T // TB
    rb = NB * TB

    def full(shape):
        return pl.BlockSpec(shape, lambda i: (0,) * len(shape))

    c_spec = pl.BlockSpec((NB, MLSTM_HEADS, MLSTM_DH, MLSTM_DH), lambda i: (i // nt, 0, 0, 0))
    n_spec = pl.BlockSpec((NB, 1, D_MODEL), lambda i: (i // nt, 0, 0))
    m_spec = pl.BlockSpec((NB, 1, LANES), lambda i: (i // nt, 0, 0))
    t_spec = pl.BlockSpec((NB, SUBLANES, D_MODEL), lambda i: (i // nt, 0, 0))
    head_w = full((MLSTM_HEADS, MLSTM_DH, MLSTM_DH))
    state_specs = [c_spec, n_spec, m_spec, t_spec]
    weight_specs = [full((CONV_WIDTH, D_MODEL)), full((1, D_MODEL)),
                    head_w, head_w, head_w, head_w, full((1, D_MODEL)), full((1, D_MODEL))]
    out_specs = [pl.BlockSpec((rb, D_MODEL), lambda i: (i, 0)), c_spec, n_spec, m_spec]
    out_shape = [jax.ShapeDtypeStruct((B * T, D_MODEL), BF16),
                 jax.ShapeDtypeStruct((B, MLSTM_HEADS, MLSTM_DH, MLSTM_DH), F32),
                 jax.ShapeDtypeStruct((B, 1, D_MODEL), F32),
                 jax.ShapeDtypeStruct((B, 1, LANES), F32)]
    scratch = [pltpu.VMEM((NB, SUBLANES, D_MODEL), F32)]
    return nt, rb, state_specs, weight_specs, out_specs, out_shape, scratch, t_spec


def _mlstm(proj, states, weights, *, B, T, TB, NB, L, Tv):
    nt, rb, state_specs, weight_specs, out_specs, out_shape, scratch, _ = _mlstm_specs(B, T, TB, NB)
    assert NB == 1 or nt == 1
    cols = [pl.BlockSpec((rb, D_MODEL), lambda i, c=c: (i, c)) for c in range(3)]
    cols.append(pl.BlockSpec((rb, LANES), lambda i: (i, GATE_TILE)))
    return pl.pallas_call(
        functools.partial(_mlstm_kernel, nt=nt, NB=NB, TB=TB, L=L, Tv=Tv),
        grid=(B // NB * nt,),
        in_specs=cols + state_specs + weight_specs,
        out_specs=out_specs, out_shape=out_shape, scratch_shapes=scratch,
        compiler_params=pltpu.CompilerParams(vmem_limit_bytes=VMEM_LIMIT),
        name="mlstm",
    )(proj, proj, proj, proj, *states, *weights)


def _mlstm_fused(x2d, scale3, shift3, norm_g, w, b, h_next, weights, *, B, T, TB, L):
    nt, rb, _, weight_specs, out_specs, out_shape, scratch, t_spec = _mlstm_specs(B, T, TB, 1)
    n_blocks = B * nt
    return pl.pallas_call(
        functools.partial(_mlstm_fused_kernel, nt=nt, NB=1, TB=TB, L=L, Tv=L),
        grid=(n_blocks,),
        in_specs=_fused_projection_specs(n_blocks, nt, TB, N_MLSTM_PAD, True) + weight_specs,
        out_specs=out_specs + [t_spec],
        out_shape=out_shape + [jax.ShapeDtypeStruct((B, SUBLANES, D_MODEL), F32)],
        scratch_shapes=scratch + [pltpu.VMEM((TB, N_MLSTM_PAD), F32), pltpu.VMEM((TB, N_MLSTM_PAD), F32)],
        compiler_params=pltpu.CompilerParams(vmem_limit_bytes=VMEM_LIMIT),
        name="mlstm_fused",
    )(x2d, scale3, shift3, norm_g, w, b, h_next, *weights)


def _out_kernel(yh_ref, ym_ref, x_ref, gate_ref, w_ref, fg_ref, o_ref):
    acc = _dot(jnp.concatenate([yh_ref[...], ym_ref[...]], axis=1), w_ref[...])
    out = x_ref[...] + gate_ref[0] * acc
    ms = jnp.mean(out * out, axis=-1, keepdims=True)
    o_ref[...] = out * lax.rsqrt(ms + EPS) * fg_ref[...]


def _out_projection(yh, ym, x2d, gate3, w_out, final_g, rb):
    rows = x2d.shape[0]
    mrows = gate3.shape[1]
    nblk = rows // rb
    return pl.pallas_call(
        _out_kernel,
        grid=(nblk,),
        in_specs=[pl.BlockSpec((rb, D_MODEL), lambda i: (i, 0)),
                  pl.BlockSpec((rb, D_MODEL), lambda i: (i, 0)),
                  pl.BlockSpec((rb, D_MODEL), lambda i: (i, 0)),
                  pl.BlockSpec((1, mrows, D_MODEL), lambda i: (i * gate3.shape[0] // nblk, 0, 0)),
                  pl.BlockSpec((2 * D_MODEL, D_MODEL), lambda i: (0, 0)),
                  pl.BlockSpec((1, D_MODEL), lambda i: (0, 0))],
        out_specs=pl.BlockSpec((rb, D_MODEL), lambda i: (i, 0)),
        out_shape=jax.ShapeDtypeStruct((rows, D_MODEL), F32),
        compiler_params=pltpu.CompilerParams(vmem_limit_bytes=VMEM_LIMIT),
        name="out_projection",
    )(yh, ym, x2d, gate3, w_out, final_g)


def _block_diag_heads(w):
    rows = w.reshape(MLSTM_HEADS, MLSTM_DH, QKV_BLOCK)
    tiled = jnp.tile(rows, (1, 1, MLSTM_DH // QKV_BLOCK))
    rg = lax.broadcasted_iota(jnp.int32, (MLSTM_DH, MLSTM_DH), 0) // QKV_BLOCK
    cg = lax.broadcasted_iota(jnp.int32, (MLSTM_DH, MLSTM_DH), 1) // QKV_BLOCK
    return jnp.where(rg == cg, tiled, 0.0).astype(BF16)


def _mlstm_state_operands(c0, n0, m0, conv0):
    B = c0.shape[0]
    tail0 = jnp.pad(conv0, ((0, 0), (SUBLANES - (CONV_WIDTH - 1), 0), (0, 0)))
    m0p = jnp.pad(m0, ((0, 0), (0, LANES - MLSTM_HEADS))).reshape(B, 1, LANES)
    return (c0, n0.reshape(B, 1, D_MODEL), m0p, tail0)


def _unpack_mlstm_state(B, n_new, m_new):
    return n_new.reshape(B, MLSTM_HEADS, MLSTM_DH), m_new.reshape(B, LANES)[:, :MLSTM_HEADS]


def kernel(x_prompt, x_sample, c_prompt, c_sample, state_hgrn, state_mlstm_C, state_mlstm_n, state_mlstm_m, state_mlstm_conv, w_ada, b_ada, norm_g, w_in, b_in, hgrn_lb_logits, hgrn_norm_g, mlstm_conv_w, mlstm_conv_b, mlstm_wq, mlstm_wk, mlstm_wv, mlstm_norm_g, mlstm_skip, w_out, final_g):
    assert w_in.shape == (1, D_MODEL, N_PROJ) and hgrn_lb_logits.shape == (2, D_MODEL)
    Bp, Tp, _ = x_prompt.shape
    Bs, Ts, _ = x_sample.shape
    assert Tp % PROMPT_BLOCK == 0 and Ts <= SAMPLE_ROWS and Ts >= CONV_WIDTH - 1

    mod = _modulation(jnp.concatenate([c_prompt, c_sample], axis=0), w_ada[0], b_ada[0].reshape(1, -1))
    shift, scale, gate = mod[:, :D_MODEL], mod[:, D_MODEL:2 * D_MODEL], mod[:, 2 * D_MODEL:]
    pad_cols = N_MLSTM_PAD - N_MLSTM
    ng = norm_g[0].reshape(1, -1)
    w_h = w_in[0, :, :N_HGRN].astype(BF16)
    w_m = jnp.pad(w_in[0, :, N_HGRN:].astype(BF16), ((0, 0), (0, pad_cols)))
    b_h = b_in[0, :N_HGRN].reshape(1, -1)
    b_m = jnp.pad(b_in[0, N_HGRN:], (0, pad_cols)).reshape(1, -1)
    hg = hgrn_norm_g[0].reshape(1, -1)
    mw = (mlstm_conv_w[0], mlstm_conv_b[0].reshape(1, -1),
          _block_diag_heads(mlstm_wq[0]), _block_diag_heads(mlstm_wk[0] * KEY_SCALE),
          _block_diag_heads(jnp.swapaxes(mlstm_wk[0], -1, -2) * KEY_SCALE), _block_diag_heads(mlstm_wv[0]),
          mlstm_norm_g[0].reshape(1, -1), mlstm_skip[0].reshape(1, -1))
    wo = w_out[0].astype(BF16)
    fg = final_g.reshape(1, -1)

    xp2 = x_prompt.reshape(Bp * Tp, D_MODEL)
    per_seq = lambda a: a.reshape(-1, 1, D_MODEL)
    sc_p, sh_p = per_seq(scale[:Bp]), per_seq(shift[:Bp])
    yh_p, hg_p, h_next = _hgrn_fused(xp2, sc_p, sh_p, ng, w_h, b_h, hgrn_lb_logits, hg,
                                     B=Bp, T=Tp, TB=PROMPT_BLOCK, L=PROMPT_CHUNK)
    ym_p, c_p, n_p, m_p, tail_p = _mlstm_fused(xp2, sc_p, sh_p, ng, w_m, b_m, h_next, mw,
                                               B=Bp, T=Tp, TB=PROMPT_BLOCK, L=PROMPT_BLOCK)
    n_p, m_p = _unpack_mlstm_state(Bp, n_p, m_p)
    conv_p = tail_p[:, SUBLANES - (CONV_WIDTH - 1):]
    yp = _out_projection(yh_p, ym_p, xp2, per_seq(gate[:Bp]), wo, fg, 1024).reshape(Bp, Tp, D_MODEL)

    rows_s = Bs * SAMPLE_ROWS
    xs2 = jnp.pad(x_sample, ((0, 0), (0, SAMPLE_ROWS - Ts), (0, 0))).reshape(rows_s, D_MODEL)
    per_row = lambda a: jnp.repeat(a, SAMPLE_ROWS, axis=0).reshape(1, rows_s, D_MODEL)
    sc_s, sh_s = per_row(scale[Bp:]), per_row(shift[Bp:])
    proj_h = _in_projection(xs2, sc_s, sh_s, ng, w_h, b_h, rows_s, N_HGRN // 2)
    proj_m = _in_projection(xs2, sc_s, sh_s, ng, w_m, b_m, rows_s, N_MLSTM_PAD // 5)
    yh_s, hg_s = _hgrn(proj_h, state_hgrn[0], hgrn_lb_logits, hg,
                       B=Bs, T=SAMPLE_ROWS, TB=SAMPLE_ROWS, NB=8, L=SAMPLE_ROWS, Tv=Ts)
    states_s = _mlstm_state_operands(state_mlstm_C[0], state_mlstm_n[0], state_mlstm_m[0], state_mlstm_conv[0])
    ym_s, c_s, n_s, m_s = _mlstm(proj_m, states_s, mw,
                                 B=Bs, T=SAMPLE_ROWS, TB=SAMPLE_ROWS, NB=8, L=SAMPLE_ROWS, Tv=Ts)
    n_s, m_s = _unpack_mlstm_state(Bs, n_s, m_s)
    conv_s = proj_m.reshape(Bs, SAMPLE_ROWS, N_MLSTM_PAD)[:, Ts - (CONV_WIDTH - 1):Ts, :D_MODEL]
    ys = _out_projection(yh_s, ym_s, xs2, per_row(gate[Bp:]), wo, fg, rows_s).reshape(Bs, SAMPLE_ROWS, D_MODEL)

    return (yp, ys[:, :Ts], hg_p[None], c_p[None], n_p[None], m_p[None], conv_p[None],
            hg_s[None], c_s[None], n_s[None], m_s[None], conv_s[None])
```

```python
import functools

import jax
import jax.numpy as jnp
from jax import lax
from jax.experimental import pallas as pl
from jax.experimental.pallas import tpu as pltpu

F32 = jnp.float32
BF16 = jnp.bfloat16

D_MODEL = 1024
HGRN_HEADS = 8
HGRN_D = 128
MLSTM_HEADS = 4
MLSTM_DH = 256
KEY_SCALE = MLSTM_DH ** -0.5
CONV_WIDTH = 4
QKV_BLOCK = 4
EPS = 1e-6
N_PROJ = 7176
N_HGRN = 4 * D_MODEL
N_MLSTM = N_PROJ - N_HGRN
N_MLSTM_PAD = 3 * D_MODEL + 128
GATE_TILE = 3 * D_MODEL // 128
MXU_WIDTH = 256
PROMPT_CHUNK = 128
PROMPT_BLOCK = 256
SAMPLE_ROWS = 8
SUBLANES = 8
LANES = 128
VMEM_LIMIT = 56 * 1024 * 1024
FAST_PATH_MIN_LOG_DECAY = -80.0


def _dot(a, b):
    return jnp.dot(a, b, preferred_element_type=F32)


def _dot_nt(a, b):
    return lax.dot_general(a, b, (((1,), (1,)), ((), ())), preferred_element_type=F32)


def _dot_tn(a, b):
    return lax.dot_general(a, b, (((0,), (0,)), ((), ())), preferred_element_type=F32)


def _sigmoid(x):
    return 0.5 * jnp.tanh(0.5 * x) + 0.5


def _silu(x):
    u = 0.5 * x
    return u * jnp.tanh(u) + u


def _levels(L):
    out, s = [], 1
    while s < L:
        out.append(s)
        s *= 2
    return out


def _seg_bcast(W, s, L, r):
    n = W.shape[1]
    if s == 1:
        return jnp.where((r & 1) != 0, pltpu.roll(W, 1, 0), W)
    if s == 2:
        m = r & 3
        return jnp.where(m == 0, pltpu.roll(W, L - 1, 0),
                         jnp.where(m == 1, W,
                                   jnp.where(m == 2, pltpu.roll(W, 1, 0), pltpu.roll(W, 2, 0))))
    pieces = [jnp.broadcast_to(W[b * 2 * s + s - 1:b * 2 * s + s, :], (2 * s, n))
              for b in range(L // (2 * s))]
    return pieces[0] if len(pieces) == 1 else jnp.concatenate(pieces, axis=0)


def _level_factors(x, L, r):
    W = x
    factors = []
    for s in _levels(L):
        Tb = _seg_bcast(W, s, L, r)
        sec = (r & s) != 0
        factors.append(jnp.exp(jnp.where(sec, W, Tb - W)))
        W = W + jnp.where(sec, Tb, 0.0)
    return factors


def _prefix8(x):
    sub = lax.broadcasted_iota(jnp.int32, (SUBLANES, 1), 0)
    y = x + jnp.where(sub >= 1, pltpu.roll(x, 1, 0), 0.0)
    y = y + jnp.where(sub >= 2, pltpu.roll(y, 2, 0), 0.0)
    return y + jnp.where(sub >= 4, pltpu.roll(y, 4, 0), 0.0)


def _chunk_cumsum(x, L):
    outs = []
    for c in range(x.shape[0] // L):
        total = None
        for g in range(L // SUBLANES):
            lo = c * L + g * SUBLANES
            p = _prefix8(x[lo:lo + SUBLANES, :])
            if total is not None:
                p = p + total
            outs.append(p)
            total = p[SUBLANES - 1:SUBLANES, :]
    return outs[0] if len(outs) == 1 else jnp.concatenate(outs, axis=0)


def _rows_bcast(x, row_in_chunk, L):
    n = x.shape[1]
    pieces = [jnp.broadcast_to(x[c * L + row_in_chunk:c * L + row_in_chunk + 1, :], (L, n))
              for c in range(x.shape[0] // L)]
    return pieces[0] if len(pieces) == 1 else jnp.concatenate(pieces, axis=0)


def _rms_mod_bf16(x, scale, shift, g):
    ms = jnp.mean(x * x, axis=-1, keepdims=True)
    return ((x * lax.rsqrt(ms + EPS)) * (g * (1.0 + scale)) + shift).astype(BF16)


def _project_into(dst_ref, h, w_ref, b_ref):
    n = w_ref.shape[1]
    bounds = list(range(0, n - n % D_MODEL, D_MODEL)) or [0]
    for k, lo in enumerate(bounds):
        hi = n if k == len(bounds) - 1 else lo + D_MODEL
        dst_ref[:, lo:hi] = _dot(h, w_ref[:, lo:hi]) + b_ref[:, lo:hi]


def _mod_kernel(c_ref, w_ref, b_ref, o_ref):
    c = c_ref[...]
    a = _silu(c)
    o_ref[...] = _dot(a.astype(BF16), w_ref[...].astype(BF16)) + b_ref[...]


def _modulation(c_all, w_ada, b_ada):
    m = c_all.shape[0]
    n = w_ada.shape[1]
    tn = 512
    return pl.pallas_call(
        _mod_kernel,
        grid=(n // tn,),
        in_specs=[pl.BlockSpec((m, D_MODEL), lambda j: (0, 0)),
                  pl.BlockSpec((D_MODEL, tn), lambda j: (0, j)),
                  pl.BlockSpec((1, tn), lambda j: (0, j))],
        out_specs=pl.BlockSpec((m, tn), lambda j: (0, j)),
        out_shape=jax.ShapeDtypeStruct((m, n), F32),
        name="modulation",
    )(c_all, w_ada, b_ada)


def _inproj_kernel(x_ref, scale_ref, shift_ref, g_ref, w_ref, b_ref, o_ref, h_scr):
    @pl.when(pl.program_id(1) == 0)
    def _():
        h_scr[...] = _rms_mod_bf16(x_ref[...], scale_ref[0], shift_ref[0], g_ref[...])

    o_ref[...] = _dot(h_scr[...], w_ref[...]) + b_ref[...]


def _in_projection(x2d, scale3, shift3, norm_g, w, b, rb, col_tile):
    rows = x2d.shape[0]
    mrows = scale3.shape[1]
    nblk = rows // rb
    n_cols = w.shape[1]
    return pl.pallas_call(
        _inproj_kernel,
        grid=(nblk, n_cols // col_tile),
        in_specs=[pl.BlockSpec((rb, D_MODEL), lambda i, j: (i, 0)),
                  pl.BlockSpec((1, mrows, D_MODEL), lambda i, j: (i * scale3.shape[0] // nblk, 0, 0)),
                  pl.BlockSpec((1, mrows, D_MODEL), lambda i, j: (i * shift3.shape[0] // nblk, 0, 0)),
                  pl.BlockSpec((1, D_MODEL), lambda i, j: (0, 0)),
                  pl.BlockSpec((D_MODEL, col_tile), lambda i, j: (0, j)),
                  pl.BlockSpec((1, col_tile), lambda i, j: (0, j))],
        out_specs=pl.BlockSpec((rb, col_tile), lambda i, j: (i, j)),
        out_shape=jax.ShapeDtypeStruct((rows, n_cols), F32),
        scratch_shapes=[pltpu.VMEM((rb, D_MODEL), BF16)],
        compiler_params=pltpu.CompilerParams(vmem_limit_bytes=VMEM_LIMIT),
        name="in_projection",
    )(x2d, scale3, shift3, norm_g, w, b)


class _SideWork:
    def __init__(self, pieces=()):
        self._pieces = list(pieces)

    def run(self, n=1):
        for _ in range(n):
            if self._pieces:
                self._pieces.pop(0)()

    def flush(self):
        self.run(len(self._pieces))


def _fused_projection_schedule(i, proj_a, proj_b, x0_ref, sc0_ref, sh0_ref, ng_ref, w_ref, b_ref, next_h, step):
    n = w_ref.shape[1]

    @pl.when(i == 0)
    def _():
        _project_into(proj_a, _rms_mod_bf16(x0_ref[...], sc0_ref[0], sh0_ref[0], ng_ref[...]), w_ref, b_ref)

    def run(cur, nxt):
        h = []

        def norm_piece():
            h.append(next_h())

        def tile_piece(lo):
            hi = min(lo + MXU_WIDTH, n)

            def piece():
                nxt[:, lo:hi] = _dot(h[0], w_ref[:, lo:hi]) + b_ref[:, lo:hi]
            return piece

        step(cur, _SideWork([norm_piece] + [tile_piece(lo) for lo in range(0, n, MXU_WIDTH)]))

    parity = lax.rem(i, 2)
    pl.when(parity == 0)(lambda: run(proj_a, proj_b))
    pl.when(parity == 1)(lambda: run(proj_b, proj_a))


def _fused_projection_specs(n_blocks, nt, tb, n_cols, next_h_given):
    def nxt(i):
        return jnp.minimum(i + 1, n_blocks - 1)
    vec = (1, 1, D_MODEL)
    specs = [pl.BlockSpec((tb, D_MODEL), lambda i: (0, 0)),
             pl.BlockSpec(vec, lambda i: (0, 0, 0)), pl.BlockSpec(vec, lambda i: (0, 0, 0)),
             pl.BlockSpec((1, D_MODEL), lambda i: (0, 0)),
             pl.BlockSpec((D_MODEL, n_cols), lambda i: (0, 0)),
             pl.BlockSpec((1, n_cols), lambda i: (0, 0))]
    if next_h_given:
        return specs + [pl.BlockSpec((tb, D_MODEL), lambda i: (i, 0))]
    return specs + [pl.BlockSpec((tb, D_MODEL), lambda i: (nxt(i), 0)),
                    pl.BlockSpec(vec, lambda i: (nxt(i) // nt, 0, 0)),
                    pl.BlockSpec(vec, lambda i: (nxt(i) // nt, 0, 0))]


def _hgrn_step(load, t, nt, s0_ref, lbl_ref, g_ref, y_ref, s_ref, st_scr, a_scr,
               *, NB, TB, L, Tv, side_work=None):
    H = HGRN_HEADS
    rb = NB * TB
    spn = TB // L
    nseg = rb // L

    @pl.when(t == 0)
    def _():
        for nb in range(NB):
            for h in range(H):
                st_scr[nb, h] = jnp.zeros((HGRN_D, HGRN_D), F32) if s0_ref is None else s0_ref[nb, h].T

    sw = side_work if side_work is not None else _SideWork()
    sw.run(1)

    lg = lbl_ref[...]
    mx = jnp.max(lg, axis=0, keepdims=True)
    e = jnp.exp(lg - mx)
    lb = e[0:1, :] / jnp.sum(e, axis=0, keepdims=True)
    g_norm = g_ref[...]

    hq = load(0)
    sw.run(2)
    c1 = 0.5 * (1.0 - lb)
    p = c1 * jnp.tanh(0.5 * load(1))
    sw.run(2)
    logf = jnp.log((lb + c1) + p)
    sw.run(2)
    kk = c1 - p
    sw.run(2)
    if Tv < L:
        valid = (lax.broadcasted_iota(jnp.int32, (rb, 1), 0) & (L - 1)) < Tv
        logf = jnp.where(valid, logf, 0.0)
        kk = jnp.where(valid, kk, 0.0)
    G = _chunk_cumsum(logf, L)
    sw.run(3)
    GL = [G[s * L + L - 1:s * L + L, :] for s in range(nseg)]
    fast_ok = jnp.min(functools.reduce(jnp.minimum, GL)) >= FAST_PATH_MIN_LOG_DECAY

    ti = lax.broadcasted_iota(jnp.int32, (L, L), 0)
    ji = lax.broadcasted_iota(jnp.int32, (L, L), 1)

    @pl.when(fast_ok)
    def _():
        d = G - _rows_bcast(G, L // 2 - 1, L)
        qt = (hq * jnp.exp(d)).astype(BF16)
        kt = (kk * jnp.exp(-d)).astype(BF16)
        causal = ji <= ti
        for s in range(nseg):
            rows = slice(s * L, (s + 1) * L)
            for h in range(H):
                hs = slice(h * HGRN_D, (h + 1) * HGRN_D)
                a_scr[s * H + h] = jnp.where(causal, _dot_nt(qt[rows, hs], kt[rows, hs]), 0.0)

    @pl.when(jnp.logical_not(fast_ok))
    def _():
        r = lax.broadcasted_iota(jnp.int32, (L, 1), 0)
        xo = ti ^ ji
        diag_mask = ti == ji
        lvl_masks = [(ji < ti) & (xo >= s) & (xo < 2 * s) for s in _levels(L)]
        for s in range(nseg):
            rows = slice(s * L, (s + 1) * L)
            factors = _level_factors(logf[rows, :], L, r)
            for h in range(H):
                hs = slice(h * HGRN_D, (h + 1) * HGRN_D)
                qh = hq[rows, hs]
                kh = kk[rows, hs]
                A = jnp.where(diag_mask, _dot_nt(qh.astype(BF16), kh.astype(BF16)), 0.0)
                for lvl in range(len(factors)):
                    E = factors[lvl][:, hs]
                    A = A + jnp.where(lvl_masks[lvl],
                                      _dot_nt((qh * E).astype(BF16), (kh * E).astype(BF16)), 0.0)
                a_scr[s * H + h] = A

    qg = (hq * jnp.exp(G)).astype(BF16)
    sw.run(2)
    GLb = GL[0] if nseg == 1 and L == rb else jnp.concatenate(
        [jnp.broadcast_to(gl, (L, D_MODEL)) for gl in GL], axis=0)
    kd = (kk * jnp.exp(GLb - G)).astype(BF16)
    sw.run(2)
    hv = load(2)
    vb = hv.astype(BF16)
    hz = load(3)
    zgate = _silu(hz)
    sw.flush()
    merged = L % LANES == 0
    for nb in range(NB):
        st = [st_scr[nb, h] for h in range(H)]
        for c in range(spn):
            s = nb * spn + c
            rows = slice(s * L, (s + 1) * L)
            dS = jnp.exp(GL[s])
            for h in range(H):
                hs = slice(h * HGRN_D, (h + 1) * HGRN_D)
                A = a_scr[s * H + h].astype(BF16)
                if merged:
                    vT = hv[rows, hs].T.astype(BF16)
                    o = _dot_nt(jnp.concatenate([qg[rows, hs], A], axis=1),
                                jnp.concatenate([st[h].astype(BF16), vT], axis=1))
                    st[h] = st[h] * dS[:, hs] + _dot(vT, kd[rows, hs])
                else:
                    o = _dot(A, vb[rows, hs]) + _dot_nt(qg[rows, hs], st[h].astype(BF16))
                    st[h] = st[h] * dS[:, hs] + _dot_tn(vb[rows, hs], kd[rows, hs])
                ms = jnp.mean(o * o, axis=-1, keepdims=True)
                y = o * lax.rsqrt(ms + EPS) * g_norm[:, hs] * zgate[rows, hs]
                y_ref[rows, hs] = y.astype(y_ref.dtype)
        for h in range(H):
            st_scr[nb, h] = st[h]

    @pl.when(t == nt - 1)
    def _():
        for nb in range(NB):
            for h in range(H):
                s_ref[nb, h] = st_scr[nb, h].T


def _pad_sequences(a, NB, Tv, L):
    if Tv == L:
        return a
    zeros = jnp.zeros((L - Tv, a.shape[1]), a.dtype)
    return jnp.concatenate([p for nb in range(NB) for p in (a[nb * Tv:(nb + 1) * Tv, :], zeros)], axis=0)


def _hgrn_kernel(q_ref, f_ref, i_ref, z_ref, s0_ref, lbl_ref, g_ref, y_ref, s_ref, st_scr, a_scr,
                 *, nt, **static):
    cols = (q_ref, f_ref, i_ref, z_ref)
    _hgrn_step(lambda c: _pad_sequences(cols[c][...], static["NB"], static["Tv"], static["L"]),
               lax.rem(pl.program_id(0), nt), nt,
               s0_ref, lbl_ref, g_ref, y_ref, s_ref, st_scr, a_scr, **static)


def _hgrn_fused_kernel(x0_ref, sc0_ref, sh0_ref, ng_ref, w_ref, b_ref, xn_ref, scn_ref, shn_ref,
                       lbl_ref, g_ref, y_ref, s_ref, hn_ref, st_scr, a_scr, proj_a, proj_b,
                       *, nt, **static):
    i = pl.program_id(0)

    def next_h():
        h = _rms_mod_bf16(xn_ref[...], scn_ref[0], shn_ref[0], ng_ref[...])
        hn_ref[...] = h
        return h

    def step(cur, side_work):
        _hgrn_step(lambda c: cur[:, c * D_MODEL:(c + 1) * D_MODEL], lax.rem(i, nt), nt,
                   None, lbl_ref, g_ref, y_ref, s_ref, st_scr, a_scr, side_work=side_work, **static)

    _fused_projection_schedule(i, proj_a, proj_b, x0_ref, sc0_ref, sh0_ref, ng_ref, w_ref, b_ref, next_h, step)


def _hgrn_specs(B, T, TB, NB, L):
    nt = T // TB
    rb = NB * TB
    s_spec = pl.BlockSpec((NB, HGRN_HEADS, HGRN_D, HGRN_D), lambda i: (i // nt, 0, 0, 0))
    in_tail = [pl.BlockSpec((2, D_MODEL), lambda i: (0, 0)), pl.BlockSpec((1, D_MODEL), lambda i: (0, 0))]
    out_specs = [pl.BlockSpec((rb, D_MODEL), lambda i: (i, 0)), s_spec]
    out_shape = [jax.ShapeDtypeStruct((B * T, D_MODEL), BF16),
                 jax.ShapeDtypeStruct((B, HGRN_HEADS, HGRN_D, HGRN_D), F32)]
    scratch = [pltpu.VMEM((NB, HGRN_HEADS, HGRN_D, HGRN_D), F32),
               pltpu.VMEM((rb // L * HGRN_HEADS, L, L), F32)]
    return nt, rb, s_spec, in_tail, out_specs, out_shape, scratch


def _hgrn(proj, s0, lb_logits, norm_g, *, B, T, TB, NB, L, Tv):
    nt, rb, s_spec, in_tail, out_specs, out_shape, scratch = _hgrn_specs(B, T, TB, NB, L)
    assert NB == 1 or nt == 1
    assert TB == L
    cols = [pl.BlockSpec((NB * Tv, D_MODEL), lambda i, c=c: (i, c)) for c in range(4)]
    return pl.pallas_call(
        functools.partial(_hgrn_kernel, nt=nt, NB=NB, TB=TB, L=L, Tv=Tv),
        grid=(B // NB * nt,),
        in_specs=cols + [s_spec] + in_tail, out_specs=out_specs, out_shape=out_shape, scratch_shapes=scratch,
        compiler_params=pltpu.CompilerParams(vmem_limit_bytes=VMEM_LIMIT),
        name="hgrn2",
    )(proj, proj, proj, proj, s0, lb_logits, norm_g)


def _hgrn_fused(x2d, scale3, shift3, norm_g, w, b, lb_logits, hgrn_norm_g, *, B, T, TB, L):
    nt, rb, _, in_tail, out_specs, out_shape, scratch = _hgrn_specs(B, T, TB, 1, L)
    n_blocks = B * nt
    return pl.pallas_call(
        functools.partial(_hgrn_fused_kernel, nt=nt, NB=1, TB=TB, L=L, Tv=L),
        grid=(n_blocks,),
        in_specs=_fused_projection_specs(n_blocks, nt, TB, N_HGRN, False) + in_tail,
        out_specs=out_specs + [pl.BlockSpec((TB, D_MODEL), lambda i: (i, 0))],
        out_shape=out_shape + [jax.ShapeDtypeStruct((B * T, D_MODEL), BF16)],
        scratch_shapes=scratch + [pltpu.VMEM((TB, N_HGRN), F32), pltpu.VMEM((TB, N_HGRN), F32)],
        compiler_params=pltpu.CompilerParams(vmem_limit_bytes=VMEM_LIMIT),
        name="hgrn2_fused",
    )(x2d, scale3, shift3, norm_g, w, b, x2d, scale3, shift3, lb_logits, hgrn_norm_g)


def _mlstm_step(load, t, nt, c0_ref, n0_ref, m0_ref, tail0_ref,
                cw_ref, cb_ref, wq_ref, wk_ref, wkt_ref, wv_ref, ng_ref, skip_ref,
                y_ref, c_ref, n_ref, m_ref, tail_scr, conv_ref,
                *, NB, TB, L, Tv, side_work=None):
    H = MLSTM_HEADS
    rb = NB * TB
    spn = TB // L
    if nt == 1:
        c_in, n_in, m_in, tail_in = c0_ref, n0_ref, m0_ref, tail0_ref
    else:
        c_in, n_in, m_in, tail_in = c_ref, n_ref, m_ref, tail_scr

        @pl.when(t == 0)
        def _():
            for dst, src in ((c_ref, c0_ref), (n_ref, n0_ref), (m_ref, m0_ref), (tail_scr, tail0_ref)):
                dst[...] = jnp.zeros(dst.shape, F32) if src is None else src[...]

    sw = side_work if side_work is not None else _SideWork()
    sw.run(1)

    cw = cw_ref[...]
    ng = ng_ref[...]
    skip = skip_ref[...]
    lane = lax.broadcasted_iota(jnp.int32, (1, LANES), 1)
    ti = lax.broadcasted_iota(jnp.int32, (L, L), 0)
    ji = lax.broadcasted_iota(jnp.int32, (L, L), 1)
    causal = ji <= ti

    mu = load(0)
    xcs = []
    for nb in range(NB):
        u_nb = mu[nb * TB:(nb + 1) * TB, :]
        ext = jnp.concatenate([tail_in[nb], u_nb], axis=0)
        conv = cb_ref[...] + cw[CONV_WIDTH - 1:CONV_WIDTH, :] * u_nb
        for i in range(1, CONV_WIDTH):
            conv = conv + cw[CONV_WIDTH - 1 - i:CONV_WIDTH - i, :] * pltpu.roll(ext, i, 0)[SUBLANES:, :]
            sw.run(1)
        xcs.append(_silu(conv))
        sw.run(1)
        if nt > 1:
            tail_scr[nb] = u_nb[TB - SUBLANES:, :]
    if conv_ref is not None:
        @pl.when(t == nt - 1)
        def _():
            for nb in range(NB):
                conv_ref[nb] = mu[(nb + 1) * TB - SUBLANES:(nb + 1) * TB, :]
    xc = xcs[0] if NB == 1 else jnp.concatenate(xcs, axis=0)
    xcb = xc.astype(BF16)
    mub = mu.astype(BF16)
    mz = load(1)
    zgate = _silu(mz)
    sw.run(2)
    ogate = _sigmoid(load(2))
    sw.run(2)

    gt = load(3)
    lf = jnp.minimum(gt, 0.0) - jnp.log1p(jnp.exp(-jnp.abs(gt)))
    ig = gt
    if Tv < L:
        valid = (lax.broadcasted_iota(jnp.int32, (rb, 1), 0) & (L - 1)) < Tv
        lf = jnp.where(valid, lf, 0.0)
        ig = jnp.where(valid, ig, -jnp.inf)
    bcum = _chunk_cumsum(lf, L)
    sw.run(1)
    comb = jnp.where(lane < H, ig, bcum)
    rpad = -rb % LANES
    if rpad:
        comb = jnp.concatenate([comb, jnp.zeros((rpad, LANES), F32)], axis=0)
    combT = comb.T

    assert spn == 1
    heads = {}

    def project_head(h):
        hs = slice(h * MLSTM_DH, (h + 1) * MLSTM_DH)
        q_all = _dot(xcb[:, hs], wq_ref[h])
        kb_all = _dot(xcb[:, hs], wk_ref[h]).astype(BF16)
        kT_all = _dot_nt(wkt_ref[h], xcb[:, hs])
        vb_all = _dot(mub[:, hs], wv_ref[h]).astype(BF16)
        heads[h] = (hs, q_all, q_all.astype(BF16), kb_all, kT_all, vb_all)
        sw.run(1)

    pairs = [(h, nb) for h in range(H) for nb in range(NB)]
    groups = [pairs] if NB > 1 else [[p] for p in pairs]
    rows_of = lambda nb: slice(nb * L, (nb + 1) * L)

    def run_group(pairs):
        for h in sorted({h for h, _ in pairs}):
            project_head(h)

        gates = []
        for h, nb in pairs:
            rows = rows_of(nb)
            bcol = bcum[rows, H + h:H + h + 1]
            irow = combT[h:h + 1, rows]
            brow = combT[H + h:H + h + 1, rows]
            logD = jnp.where(causal, (bcol - brow) + irow, -jnp.inf)
            m_intra = jnp.max(logD, axis=-1, keepdims=True)
            gates.append((bcol, irow, brow, m_intra, jnp.exp(logD - m_intra)))

        scores = []
        for (h, nb), g in zip(pairs, gates):
            _, _, qb_all, kb_all, _, _ = heads[h]
            rows = rows_of(nb)
            scores.append(_dot_nt(qb_all[rows], kb_all[rows]) * g[4])

        intra = []
        for (h, nb), sc in zip(pairs, scores):
            vb_all = heads[h][5]
            intra.append((jnp.sum(sc, axis=-1, keepdims=True), _dot(sc.astype(BF16), vb_all[rows_of(nb)])))

        writes = []
        for (h, nb), g in zip(pairs, gates):
            _, _, _, kb_all, kT_all, vb_all = heads[h]
            rows = rows_of(nb)
            bcol, irow, brow, m_intra, _ = g
            m_loc = m_intra[L - 1:L, :]
            b_last = bcol[L - 1:L, :]
            wrow = jnp.exp((b_last - brow) + irow - m_loc)
            kwT = (kT_all[:, rows] * wrow).astype(BF16)
            ks = _dot(jnp.broadcast_to(wrow, (SUBLANES, L)).astype(BF16), kb_all[rows])[0:1, :]
            writes.append((m_loc, b_last, kwT if len(pairs) > 1 else _dot(kwT, vb_all[rows]), ks))

        outs = []
        updates = []
        for (h, nb), g, (rs, sv), (m_loc, b_last, kw, ks) in zip(pairs, gates, intra, writes):
            hs, q_all, qb_all, _, _, _ = heads[h]
            rows = rows_of(nb)
            bcol, _, _, m_intra, _ = g
            C = c_in[nb, h]
            nh = n_in[nb, :, hs]
            m_prev = m_in[nb][:, h:h + 1]
            m_inter = bcol + m_prev
            m_t = jnp.maximum(m_inter, m_intra)
            inter = jnp.exp(m_inter - m_t)
            scl = jnp.exp(m_intra - m_t)
            den = inter * jnp.sum(q_all[rows] * nh, axis=-1, keepdims=True) + scl * rs
            rden = 1.0 / jnp.maximum(jnp.abs(den), jnp.exp(-m_t))
            hh = (inter * rden) * _dot(qb_all[rows], C.astype(BF16)) + (scl * rden) * sv
            m_new = m_t[L - 1:L, :]
            dec = jnp.exp(b_last + m_prev - m_new)
            scu = jnp.exp(m_loc - m_new)
            n_ref[nb, :, hs] = dec * nh + scu * ks
            outs.append((hh, m_new))
            if len(pairs) > 1:
                updates.append((dec, scu, kw))
            else:
                c_ref[nb, h] = dec * C + scu * kw

        for (h, nb), (dec, scu, kwT) in zip(pairs, updates):
            c_ref[nb, h] = dec * c_in[nb, h] + scu * _dot(kwT, heads[h][5][rows_of(nb)])

        for (h, nb), (hh, _) in zip(pairs, outs):
            hs = heads[h][0]
            rows = rows_of(nb)
            hm = ogate[rows, hs] * hh
            ms = jnp.mean(hm * hm, axis=-1, keepdims=True)
            y = (hm * lax.rsqrt(ms + EPS) * ng[:, hs] + skip[:, hs] * xc[rows, hs]) * zgate[rows, hs]
            y_ref[rows, hs] = y.astype(y_ref.dtype)
        return outs

    outs = [o for grp in groups for o in run_group(grp)]

    for nb in range(NB):
        m_row = m_in[nb]
        for h in range(H):
            m_row = jnp.where(lane == h, outs[h * NB + nb][1], m_row)
        m_ref[nb] = m_row
    sw.flush()


N_MLSTM_WEIGHTS = 8


def _mlstm_kernel(u_ref, z_ref, o_ref, gate_ref, *rest, nt, **static):
    cols = (u_ref, z_ref, o_ref, gate_ref)
    ins, (y_ref, c_ref, n_ref, m_ref, tail_scr) = rest[:4 + N_MLSTM_WEIGHTS], rest[4 + N_MLSTM_WEIGHTS:]
    _mlstm_step(lambda c: _pad_sequences(cols[c][...], static["NB"], static["Tv"], static["L"]),
                lax.rem(pl.program_id(0), nt), nt, *ins,
                y_ref, c_ref, n_ref, m_ref, tail_scr, None, **static)


def _mlstm_fused_kernel(x0_ref, sc0_ref, sh0_ref, ng_ref, w_ref, b_ref, hn_ref, *rest, nt, **static):
    i = pl.program_id(0)
    weights = rest[:N_MLSTM_WEIGHTS]
    y_ref, c_ref, n_ref, m_ref, conv_ref, tail_scr, proj_a, proj_b = rest[N_MLSTM_WEIGHTS:]
    widths = (D_MODEL, D_MODEL, D_MODEL, LANES)

    def step(cur, side_work):
        _mlstm_step(lambda c: cur[:, c * D_MODEL:c * D_MODEL + widths[c]], lax.rem(i, nt), nt,
                    None, None, None, None, *weights,
                    y_ref, c_ref, n_ref, m_ref, tail_scr, conv_ref, side_work=side_work, **static)

    _fused_projection_schedule(i, proj_a, proj_b, x0_ref, sc0_ref, sh0_ref, ng_ref, w_ref, b_ref,
                               lambda: hn_ref[...], step)


def _mlstm_specs(B, T, TB, NB):
    nt = T // TB
    rb = NB * TB

    def full(shape):
        return pl.BlockSpec(shape, lambda i: (0,) * len(shape))

    c_spec = pl.BlockSpec((NB, MLSTM_HEADS, MLSTM_DH, MLSTM_DH), lambda i: (i // nt, 0, 0, 0))
    n_spec = pl.BlockSpec((NB, 1, D_MODEL), lambda i: (i // nt, 0, 0))
    m_spec = pl.BlockSpec((NB, 1, LANES), lambda i: (i // nt, 0, 0))
    t_spec = pl.BlockSpec((NB, SUBLANES, D_MODEL), lambda i: (i // nt, 0, 0))
    head_w = full((MLSTM_HEADS, MLSTM_DH, MLSTM_DH))
    state_specs = [c_spec, n_spec, m_spec, t_spec]
    weight_specs = [full((CONV_WIDTH, D_MODEL)), full((1, D_MODEL)),
                    head_w, head_w, head_w, head_w, full((1, D_MODEL)), full((1, D_MODEL))]
    out_specs = [pl.BlockSpec((rb, D_MODEL), lambda i: (i, 0)), c_spec, n_spec, m_spec]
    out_shape = [jax.ShapeDtypeStruct((B * T, D_MODEL), BF16),
                 jax.ShapeDtypeStruct((B, MLSTM_HEADS, MLSTM_DH, MLSTM_DH), F32),
                 jax.ShapeDtypeStruct((B, 1, D_MODEL), F32),
                 jax.ShapeDtypeStruct((B, 1, LANES), F32)]
    scratch = [pltpu.VMEM((NB, SUBLANES, D_MODEL), F32)]
    return nt, rb, state_specs, weight_specs, out_specs, out_shape, scratch, t_spec


def _mlstm(proj, states, weights, *, B, T, TB, NB, L, Tv):
    nt, rb, state_specs, weight_specs, out_specs, out_shape, scratch, _ = _mlstm_specs(B, T, TB, NB)
    assert NB == 1 or nt == 1
    assert TB == L
    cols = [pl.BlockSpec((NB * Tv, D_MODEL), lambda i, c=c: (i, c)) for c in range(3)]
    cols.append(pl.BlockSpec((NB * Tv, LANES), lambda i: (i, GATE_TILE)))
    return pl.pallas_call(
        functools.partial(_mlstm_kernel, nt=nt, NB=NB, TB=TB, L=L, Tv=Tv),
        grid=(B // NB * nt,),
        in_specs=cols + state_specs + weight_specs,
        out_specs=out_specs, out_shape=out_shape, scratch_shapes=scratch,
        compiler_params=pltpu.CompilerParams(vmem_limit_bytes=VMEM_LIMIT),
        name="mlstm",
    )(proj, proj, proj, proj, *states, *weights)


def _mlstm_fused(x2d, scale3, shift3, norm_g, w, b, h_next, weights, *, B, T, TB, L):
    nt, rb, _, weight_specs, out_specs, out_shape, scratch, t_spec = _mlstm_specs(B, T, TB, 1)
    n_blocks = B * nt
    return pl.pallas_call(
        functools.partial(_mlstm_fused_kernel, nt=nt, NB=1, TB=TB, L=L, Tv=L),
        grid=(n_blocks,),
        in_specs=_fused_projection_specs(n_blocks, nt, TB, N_MLSTM_PAD, True) + weight_specs,
        out_specs=out_specs + [t_spec],
        out_shape=out_shape + [jax.ShapeDtypeStruct((B, SUBLANES, D_MODEL), F32)],
        scratch_shapes=scratch + [pltpu.VMEM((TB, N_MLSTM_PAD), F32), pltpu.VMEM((TB, N_MLSTM_PAD), F32)],
        compiler_params=pltpu.CompilerParams(vmem_limit_bytes=VMEM_LIMIT),
        name="mlstm_fused",
    )(x2d, scale3, shift3, norm_g, w, b, h_next, *weights)


def _out_kernel(yh_ref, ym_ref, x_ref, gate_ref, w_ref, fg_ref, o_ref):
    acc = _dot(jnp.concatenate([yh_ref[...], ym_ref[...]], axis=1), w_ref[...])
    out = x_ref[...] + gate_ref[0] * acc
    ms = jnp.mean(out * out, axis=-1, keepdims=True)
    o_ref[...] = out * lax.rsqrt(ms + EPS) * fg_ref[...]


def _out_projection(yh, ym, x2d, gate3, w_out, final_g, rb):
    rows = x2d.shape[0]
    mrows = gate3.shape[1]
    nblk = rows // rb
    return pl.pallas_call(
        _out_kernel,
        grid=(nblk,),
        in_specs=[pl.BlockSpec((rb, D_MODEL), lambda i: (i, 0)),
                  pl.BlockSpec((rb, D_MODEL), lambda i: (i, 0)),
                  pl.BlockSpec((rb, D_MODEL), lambda i: (i, 0)),
                  pl.BlockSpec((1, mrows, D_MODEL), lambda i: (i * gate3.shape[0] // nblk, 0, 0)),
                  pl.BlockSpec((2 * D_MODEL, D_MODEL), lambda i: (0, 0)),
                  pl.BlockSpec((1, D_MODEL), lambda i: (0, 0))],
        out_specs=pl.BlockSpec((rb, D_MODEL), lambda i: (i, 0)),
        out_shape=jax.ShapeDtypeStruct((rows, D_MODEL), F32),
        compiler_params=pltpu.CompilerParams(vmem_limit_bytes=VMEM_LIMIT),
        name="out_projection",
    )(yh, ym, x2d, gate3, w_out, final_g)


def _block_diag_heads(w):
    rows = w.reshape(MLSTM_HEADS, MLSTM_DH, QKV_BLOCK)
    tiled = jnp.tile(rows, (1, 1, MLSTM_DH // QKV_BLOCK))
    rg = lax.broadcasted_iota(jnp.int32, (MLSTM_DH, MLSTM_DH), 0) // QKV_BLOCK
    cg = lax.broadcasted_iota(jnp.int32, (MLSTM_DH, MLSTM_DH), 1) // QKV_BLOCK
    return jnp.where(rg == cg, tiled, 0.0).astype(BF16)


def _mlstm_state_operands(c0, n0, m0, conv0):
    B = c0.shape[0]
    tail0 = jnp.pad(conv0, ((0, 0), (SUBLANES - (CONV_WIDTH - 1), 0), (0, 0)))
    m0p = jnp.pad(m0, ((0, 0), (0, LANES - MLSTM_HEADS))).reshape(B, 1, LANES)
    return (c0, n0.reshape(B, 1, D_MODEL), m0p, tail0)


def _unpack_mlstm_state(B, n_new, m_new):
    return n_new.reshape(B, MLSTM_HEADS, MLSTM_DH), m_new.reshape(B, LANES)[:, :MLSTM_HEADS]


def kernel(x_prompt, x_sample, c_prompt, c_sample, state_hgrn, state_mlstm_C, state_mlstm_n, state_mlstm_m, state_mlstm_conv, w_ada, b_ada, norm_g, w_in, b_in, hgrn_lb_logits, hgrn_norm_g, mlstm_conv_w, mlstm_conv_b, mlstm_wq, mlstm_wk, mlstm_wv, mlstm_norm_g, mlstm_skip, w_out, final_g):
    assert w_in.shape == (1, D_MODEL, N_PROJ) and hgrn_lb_logits.shape == (2, D_MODEL)
    Bp, Tp, _ = x_prompt.shape
    Bs, Ts, _ = x_sample.shape
    assert Tp % PROMPT_BLOCK == 0 and Ts <= SAMPLE_ROWS and Ts >= CONV_WIDTH - 1

    mod = _modulation(jnp.concatenate([c_prompt, c_sample], axis=0), w_ada[0], b_ada[0].reshape(1, -1))
    shift, scale, gate = mod[:, :D_MODEL], mod[:, D_MODEL:2 * D_MODEL], mod[:, 2 * D_MODEL:]
    pad_cols = N_MLSTM_PAD - N_MLSTM
    ng = norm_g[0].reshape(1, -1)
    w_h = w_in[0, :, :N_HGRN].astype(BF16)
    w_m = jnp.pad(w_in[0, :, N_HGRN:].astype(BF16), ((0, 0), (0, pad_cols)))
    b_h = b_in[0, :N_HGRN].reshape(1, -1)
    b_m = jnp.pad(b_in[0, N_HGRN:], (0, pad_cols)).reshape(1, -1)
    hg = hgrn_norm_g[0].reshape(1, -1)
    mw = (mlstm_conv_w[0], mlstm_conv_b[0].reshape(1, -1),
          _block_diag_heads(mlstm_wq[0]), _block_diag_heads(mlstm_wk[0] * KEY_SCALE),
          _block_diag_heads(jnp.swapaxes(mlstm_wk[0], -1, -2) * KEY_SCALE), _block_diag_heads(mlstm_wv[0]),
          mlstm_norm_g[0].reshape(1, -1), mlstm_skip[0].reshape(1, -1))
    wo = w_out[0].astype(BF16)
    fg = final_g.reshape(1, -1)

    xp2 = x_prompt.reshape(Bp * Tp, D_MODEL)
    per_seq = lambda a: a.reshape(-1, 1, D_MODEL)
    sc_p, sh_p = per_seq(scale[:Bp]), per_seq(shift[:Bp])
    yh_p, hg_p, h_next = _hgrn_fused(xp2, sc_p, sh_p, ng, w_h, b_h, hgrn_lb_logits, hg,
                                     B=Bp, T=Tp, TB=PROMPT_BLOCK, L=PROMPT_CHUNK)
    ym_p, c_p, n_p, m_p, tail_p = _mlstm_fused(xp2, sc_p, sh_p, ng, w_m, b_m, h_next, mw,
                                               B=Bp, T=Tp, TB=PROMPT_BLOCK, L=PROMPT_BLOCK)
    n_p, m_p = _unpack_mlstm_state(Bp, n_p, m_p)
    conv_p = tail_p[:, SUBLANES - (CONV_WIDTH - 1):]
    yp = _out_projection(yh_p, ym_p, xp2, per_seq(gate[:Bp]), wo, fg, 1024).reshape(Bp, Tp, D_MODEL)

    rows_s = Bs * SAMPLE_ROWS
    xs2 = jnp.pad(x_sample, ((0, 0), (0, SAMPLE_ROWS - Ts), (0, 0))).reshape(rows_s, D_MODEL)
    per_row = lambda a: jnp.repeat(a, SAMPLE_ROWS, axis=0).reshape(1, rows_s, D_MODEL)
    per_tok = lambda a: jnp.repeat(a, Ts, axis=0).reshape(1, Bs * Ts, D_MODEL)
    xs_tok = x_sample.reshape(Bs * Ts, D_MODEL)
    sc_s, sh_s = per_tok(scale[Bp:]), per_tok(shift[Bp:])
    proj_h = _in_projection(xs_tok, sc_s, sh_s, ng, w_h, b_h, Bs * Ts, N_HGRN // 2)
    proj_m = _in_projection(xs_tok, sc_s, sh_s, ng, w_m, b_m, Bs * Ts, N_MLSTM_PAD // 5)
    yh_s, hg_s = _hgrn(proj_h, state_hgrn[0], hgrn_lb_logits, hg,
                       B=Bs, T=SAMPLE_ROWS, TB=SAMPLE_ROWS, NB=8, L=SAMPLE_ROWS, Tv=Ts)
    states_s = _mlstm_state_operands(state_mlstm_C[0], state_mlstm_n[0], state_mlstm_m[0], state_mlstm_conv[0])
    ym_s, c_s, n_s, m_s = _mlstm(proj_m, states_s, mw,
                                 B=Bs, T=SAMPLE_ROWS, TB=SAMPLE_ROWS, NB=8, L=SAMPLE_ROWS, Tv=Ts)
    n_s, m_s = _unpack_mlstm_state(Bs, n_s, m_s)
    conv_s = proj_m.reshape(Bs, Ts, N_MLSTM_PAD)[:, Ts - (CONV_WIDTH - 1):, :D_MODEL]
    ys = _out_projection(yh_s, ym_s, xs2, per_row(gate[Bp:]), wo, fg, rows_s).reshape(Bs, SAMPLE_ROWS, D_MODEL)

    return (yp, ys[:, :Ts], hg_p[None], c_p[None], n_p[None], m_p[None], conv_p[None],
            hg_s[None], c_s[None], n_s[None], m_s[None], conv_s[None])
```

```python
import functools

import jax
import jax.numpy as jnp
from jax import lax
from jax.experimental import pallas as pl
from jax.experimental.pallas import tpu as pltpu

F32 = jnp.float32
BF16 = jnp.bfloat16

D_MODEL = 1024
HGRN_HEADS = 8
HGRN_D = 128
MLSTM_HEADS = 4
MLSTM_DH = 256
KEY_SCALE = MLSTM_DH ** -0.5
CONV_WIDTH = 4
QKV_BLOCK = 4
EPS = 1e-6
N_PROJ = 7176
N_HGRN = 4 * D_MODEL
N_MLSTM = N_PROJ - N_HGRN
N_MLSTM_PAD = 3 * D_MODEL + 128
GATE_TILE = 3 * D_MODEL // 128
MXU_WIDTH = 256
PROMPT_CHUNK = 128
PROMPT_BLOCK = 256
SAMPLE_ROWS = 8
SAMPLE_SEQS_PER_STEP = 8
OUT_PROJ_ROWS = 1024
MOD_COL_TILE = 512
SAMPLE_HGRN_COL_TILE = 2048
SAMPLE_MLSTM_COL_TILE = 640
SUBLANES = 8
LANES = 128
VMEM_LIMIT = 56 * 1024 * 1024
FAST_PATH_MIN_LOG_DECAY = -80.0


def _dot(a, b):
    return jnp.dot(a, b, preferred_element_type=F32)


def _dot_nt(a, b):
    return lax.dot_general(a, b, (((1,), (1,)), ((), ())), preferred_element_type=F32)


def _dot_tn(a, b):
    return lax.dot_general(a, b, (((0,), (0,)), ((), ())), preferred_element_type=F32)


def _sigmoid(x):
    return 0.5 * jnp.tanh(0.5 * x) + 0.5


def _silu(x):
    u = 0.5 * x
    return u * jnp.tanh(u) + u


def _levels(L):
    out, s = [], 1
    while s < L:
        out.append(s)
        s *= 2
    return out


def _seg_bcast(W, s, L, r):
    n = W.shape[1]
    if s == 1:
        return jnp.where((r & 1) != 0, pltpu.roll(W, 1, 0), W)
    if s == 2:
        m = r & 3
        return jnp.where(m == 0, pltpu.roll(W, L - 1, 0),
                         jnp.where(m == 1, W,
                                   jnp.where(m == 2, pltpu.roll(W, 1, 0), pltpu.roll(W, 2, 0))))
    pieces = [jnp.broadcast_to(W[b * 2 * s + s - 1:b * 2 * s + s, :], (2 * s, n))
              for b in range(L // (2 * s))]
    return pieces[0] if len(pieces) == 1 else jnp.concatenate(pieces, axis=0)


def _level_factors(x, L, r):
    W = x
    factors = []
    for s in _levels(L):
        Tb = _seg_bcast(W, s, L, r)
        sec = (r & s) != 0
        factors.append(jnp.exp(jnp.where(sec, W, Tb - W)))
        W = W + jnp.where(sec, Tb, 0.0)
    return factors


def _prefix8(x):
    sub = lax.broadcasted_iota(jnp.int32, (SUBLANES, 1), 0)
    y = x + jnp.where(sub >= 1, pltpu.roll(x, 1, 0), 0.0)
    y = y + jnp.where(sub >= 2, pltpu.roll(y, 2, 0), 0.0)
    return y + jnp.where(sub >= 4, pltpu.roll(y, 4, 0), 0.0)


def _chunk_cumsum(x, L):
    outs = []
    for c in range(x.shape[0] // L):
        total = None
        for g in range(L // SUBLANES):
            lo = c * L + g * SUBLANES
            p = _prefix8(x[lo:lo + SUBLANES, :])
            if total is not None:
                p = p + total
            outs.append(p)
            total = p[SUBLANES - 1:SUBLANES, :]
    return outs[0] if len(outs) == 1 else jnp.concatenate(outs, axis=0)


def _rows_bcast(x, row_in_chunk, L):
    n = x.shape[1]
    pieces = [jnp.broadcast_to(x[c * L + row_in_chunk:c * L + row_in_chunk + 1, :], (L, n))
              for c in range(x.shape[0] // L)]
    return pieces[0] if len(pieces) == 1 else jnp.concatenate(pieces, axis=0)


def _rms_mod_bf16(x, scale, shift, g):
    ms = jnp.mean(x * x, axis=-1, keepdims=True)
    return ((x * lax.rsqrt(ms + EPS)) * (g * (1.0 + scale)) + shift).astype(BF16)


def _project_into(dst_ref, h, w_ref, b_ref):
    n = w_ref.shape[1]
    bounds = list(range(0, n - n % D_MODEL, D_MODEL)) or [0]
    for k, lo in enumerate(bounds):
        hi = n if k == len(bounds) - 1 else lo + D_MODEL
        dst_ref[:, lo:hi] = _dot(h, w_ref[:, lo:hi]) + b_ref[:, lo:hi]


def _mod_kernel(c_ref, w_ref, b_ref, o_ref):
    c = c_ref[...]
    a = _silu(c)
    o_ref[...] = _dot(a.astype(BF16), w_ref[...].astype(BF16)) + b_ref[...]


def _modulation(c_all, w_ada, b_ada):
    m = c_all.shape[0]
    n = w_ada.shape[1]
    tn = MOD_COL_TILE
    return pl.pallas_call(
        _mod_kernel,
        grid=(n // tn,),
        in_specs=[pl.BlockSpec((m, D_MODEL), lambda j: (0, 0)),
                  pl.BlockSpec((D_MODEL, tn), lambda j: (0, j)),
                  pl.BlockSpec((1, tn), lambda j: (0, j))],
        out_specs=pl.BlockSpec((m, tn), lambda j: (0, j)),
        out_shape=jax.ShapeDtypeStruct((m, n), F32),
        name="modulation",
    )(c_all, w_ada, b_ada)


def _inproj_kernel(x_ref, scale_ref, shift_ref, g_ref, w_ref, b_ref, o_ref, h_scr):
    @pl.when(pl.program_id(1) == 0)
    def _():
        h_scr[...] = _rms_mod_bf16(x_ref[...], scale_ref[0], shift_ref[0], g_ref[...])

    o_ref[...] = _dot(h_scr[...], w_ref[...]) + b_ref[...]


def _in_projection(x2d, scale3, shift3, norm_g, w, b, rb, col_tile):
    rows = x2d.shape[0]
    mrows = scale3.shape[1]
    nblk = rows // rb
    n_cols = w.shape[1]
    return pl.pallas_call(
        _inproj_kernel,
        grid=(nblk, n_cols // col_tile),
        in_specs=[pl.BlockSpec((rb, D_MODEL), lambda i, j: (i, 0)),
                  pl.BlockSpec((1, mrows, D_MODEL), lambda i, j: (i * scale3.shape[0] // nblk, 0, 0)),
                  pl.BlockSpec((1, mrows, D_MODEL), lambda i, j: (i * shift3.shape[0] // nblk, 0, 0)),
                  pl.BlockSpec((1, D_MODEL), lambda i, j: (0, 0)),
                  pl.BlockSpec((D_MODEL, col_tile), lambda i, j: (0, j)),
                  pl.BlockSpec((1, col_tile), lambda i, j: (0, j))],
        out_specs=pl.BlockSpec((rb, col_tile), lambda i, j: (i, j)),
        out_shape=jax.ShapeDtypeStruct((rows, n_cols), F32),
        scratch_shapes=[pltpu.VMEM((rb, D_MODEL), BF16)],
        compiler_params=pltpu.CompilerParams(vmem_limit_bytes=VMEM_LIMIT),
        name="in_projection",
    )(x2d, scale3, shift3, norm_g, w, b)


class _SideWork:
    def __init__(self, pieces=()):
        self._pieces = list(pieces)

    def run(self, n=1):
        for _ in range(n):
            if self._pieces:
                self._pieces.pop(0)()

    def flush(self):
        self.run(len(self._pieces))


def _fused_projection_schedule(i, proj_a, proj_b, x0_ref, sc0_ref, sh0_ref, ng_ref, w_ref, b_ref, next_h, step):
    n = w_ref.shape[1]

    @pl.when(i == 0)
    def _():
        _project_into(proj_a, _rms_mod_bf16(x0_ref[...], sc0_ref[0], sh0_ref[0], ng_ref[...]), w_ref, b_ref)

    def run(cur, nxt):
        h = []

        def norm_piece():
            h.append(next_h())

        def tile_piece(lo):
            hi = min(lo + MXU_WIDTH, n)

            def piece():
                nxt[:, lo:hi] = _dot(h[0], w_ref[:, lo:hi]) + b_ref[:, lo:hi]
            return piece

        step(cur, _SideWork([norm_piece] + [tile_piece(lo) for lo in range(0, n, MXU_WIDTH)]))

    parity = lax.rem(i, 2)
    pl.when(parity == 0)(lambda: run(proj_a, proj_b))
    pl.when(parity == 1)(lambda: run(proj_b, proj_a))


def _fused_projection_specs(n_blocks, nt, tb, n_cols, next_h_given):
    def nxt(i):
        return jnp.minimum(i + 1, n_blocks - 1)
    vec = (1, 1, D_MODEL)
    specs = [pl.BlockSpec((tb, D_MODEL), lambda i: (0, 0)),
             pl.BlockSpec(vec, lambda i: (0, 0, 0)), pl.BlockSpec(vec, lambda i: (0, 0, 0)),
             pl.BlockSpec((1, D_MODEL), lambda i: (0, 0)),
             pl.BlockSpec((D_MODEL, n_cols), lambda i: (0, 0)),
             pl.BlockSpec((1, n_cols), lambda i: (0, 0))]
    if next_h_given:
        return specs + [pl.BlockSpec((tb, D_MODEL), lambda i: (i, 0))]
    return specs + [pl.BlockSpec((tb, D_MODEL), lambda i: (nxt(i), 0)),
                    pl.BlockSpec(vec, lambda i: (nxt(i) // nt, 0, 0)),
                    pl.BlockSpec(vec, lambda i: (nxt(i) // nt, 0, 0))]


def _hgrn_step(load, t, nt, s0_ref, lbl_ref, g_ref, y_ref, s_ref, st_scr, a_scr,
               *, NB, TB, L, Tv, side_work=None):
    H = HGRN_HEADS
    rb = NB * TB
    spn = TB // L
    nseg = rb // L

    @pl.when(t == 0)
    def _():
        for nb in range(NB):
            for h in range(H):
                st_scr[nb, h] = jnp.zeros((HGRN_D, HGRN_D), F32) if s0_ref is None else s0_ref[nb, h].T

    sw = side_work if side_work is not None else _SideWork()
    sw.run(1)

    lg = lbl_ref[...]
    mx = jnp.max(lg, axis=0, keepdims=True)
    e = jnp.exp(lg - mx)
    lb = e[0:1, :] / jnp.sum(e, axis=0, keepdims=True)
    g_norm = g_ref[...]

    hq = load(0)
    sw.run(2)
    c1 = 0.5 * (1.0 - lb)
    p = c1 * jnp.tanh(0.5 * load(1))
    sw.run(2)
    logf = jnp.log((lb + c1) + p)
    sw.run(2)
    kk = c1 - p
    sw.run(2)
    if Tv < L:
        valid = (lax.broadcasted_iota(jnp.int32, (rb, 1), 0) & (L - 1)) < Tv
        logf = jnp.where(valid, logf, 0.0)
        kk = jnp.where(valid, kk, 0.0)
    G = _chunk_cumsum(logf, L)
    sw.run(3)
    GL = [G[s * L + L - 1:s * L + L, :] for s in range(nseg)]
    fast_ok = jnp.min(functools.reduce(jnp.minimum, GL)) >= FAST_PATH_MIN_LOG_DECAY

    ti = lax.broadcasted_iota(jnp.int32, (L, L), 0)
    ji = lax.broadcasted_iota(jnp.int32, (L, L), 1)

    @pl.when(fast_ok)
    def _():
        d = G - _rows_bcast(G, L // 2 - 1, L)
        qt = (hq * jnp.exp(d)).astype(BF16)
        kt = (kk * jnp.exp(-d)).astype(BF16)
        causal = ji <= ti
        for s in range(nseg):
            rows = slice(s * L, (s + 1) * L)
            for h in range(H):
                hs = slice(h * HGRN_D, (h + 1) * HGRN_D)
                a_scr[s * H + h] = jnp.where(causal, _dot_nt(qt[rows, hs], kt[rows, hs]), 0.0)

    @pl.when(jnp.logical_not(fast_ok))
    def _():
        r = lax.broadcasted_iota(jnp.int32, (L, 1), 0)
        xo = ti ^ ji
        diag_mask = ti == ji
        lvl_masks = [(ji < ti) & (xo >= s) & (xo < 2 * s) for s in _levels(L)]
        for s in range(nseg):
            rows = slice(s * L, (s + 1) * L)
            factors = _level_factors(logf[rows, :], L, r)
            for h in range(H):
                hs = slice(h * HGRN_D, (h + 1) * HGRN_D)
                qh = hq[rows, hs]
                kh = kk[rows, hs]
                A = jnp.where(diag_mask, _dot_nt(qh.astype(BF16), kh.astype(BF16)), 0.0)
                for lvl in range(len(factors)):
                    E = factors[lvl][:, hs]
                    A = A + jnp.where(lvl_masks[lvl],
                                      _dot_nt((qh * E).astype(BF16), (kh * E).astype(BF16)), 0.0)
                a_scr[s * H + h] = A

    qg = (hq * jnp.exp(G)).astype(BF16)
    sw.run(2)
    GLb = GL[0] if nseg == 1 and L == rb else jnp.concatenate(
        [jnp.broadcast_to(gl, (L, D_MODEL)) for gl in GL], axis=0)
    kd = (kk * jnp.exp(GLb - G)).astype(BF16)
    sw.run(2)
    hv = load(2)
    vb = hv.astype(BF16)
    hz = load(3)
    zgate = _silu(hz)
    sw.flush()
    merged = L % LANES == 0
    for nb in range(NB):
        st = [st_scr[nb, h] for h in range(H)]
        for c in range(spn):
            s = nb * spn + c
            rows = slice(s * L, (s + 1) * L)
            dS = jnp.exp(GL[s])
            for h in range(H):
                hs = slice(h * HGRN_D, (h + 1) * HGRN_D)
                A = a_scr[s * H + h].astype(BF16)
                if merged:
                    vT = hv[rows, hs].T.astype(BF16)
                    o = _dot_nt(jnp.concatenate([qg[rows, hs], A], axis=1),
                                jnp.concatenate([st[h].astype(BF16), vT], axis=1))
                    st[h] = st[h] * dS[:, hs] + _dot(vT, kd[rows, hs])
                else:
                    o = _dot(A, vb[rows, hs]) + _dot_nt(qg[rows, hs], st[h].astype(BF16))
                    st[h] = st[h] * dS[:, hs] + _dot_tn(vb[rows, hs], kd[rows, hs])
                ms = jnp.mean(o * o, axis=-1, keepdims=True)
                y = o * lax.rsqrt(ms + EPS) * g_norm[:, hs] * zgate[rows, hs]
                y_ref[rows, hs] = y.astype(y_ref.dtype)
        for h in range(H):
            st_scr[nb, h] = st[h]

    @pl.when(t == nt - 1)
    def _():
        for nb in range(NB):
            for h in range(H):
                s_ref[nb, h] = st_scr[nb, h].T


def _pad_sequences(a, NB, Tv, L):
    if Tv == L:
        return a
    zeros = jnp.zeros((L - Tv, a.shape[1]), a.dtype)
    return jnp.concatenate([p for nb in range(NB) for p in (a[nb * Tv:(nb + 1) * Tv, :], zeros)], axis=0)


def _hgrn_kernel(q_ref, f_ref, i_ref, z_ref, s0_ref, lbl_ref, g_ref, y_ref, s_ref, st_scr, a_scr,
                 *, nt, **static):
    cols = (q_ref, f_ref, i_ref, z_ref)
    _hgrn_step(lambda c: _pad_sequences(cols[c][...], static["NB"], static["Tv"], static["L"]),
               lax.rem(pl.program_id(0), nt), nt,
               s0_ref, lbl_ref, g_ref, y_ref, s_ref, st_scr, a_scr, **static)


def _hgrn_fused_kernel(x0_ref, sc0_ref, sh0_ref, ng_ref, w_ref, b_ref, xn_ref, scn_ref, shn_ref,
                       lbl_ref, g_ref, y_ref, s_ref, hn_ref, st_scr, a_scr, proj_a, proj_b,
                       *, nt, **static):
    i = pl.program_id(0)

    def next_h():
        h = _rms_mod_bf16(xn_ref[...], scn_ref[0], shn_ref[0], ng_ref[...])
        hn_ref[...] = h
        return h

    def step(cur, side_work):
        _hgrn_step(lambda c: cur[:, c * D_MODEL:(c + 1) * D_MODEL], lax.rem(i, nt), nt,
                   None, lbl_ref, g_ref, y_ref, s_ref, st_scr, a_scr, side_work=side_work, **static)

    _fused_projection_schedule(i, proj_a, proj_b, x0_ref, sc0_ref, sh0_ref, ng_ref, w_ref, b_ref, next_h, step)


def _hgrn_specs(B, T, TB, NB, L):
    nt = T // TB
    rb = NB * TB
    s_spec = pl.BlockSpec((NB, HGRN_HEADS, HGRN_D, HGRN_D), lambda i: (i // nt, 0, 0, 0))
    in_tail = [pl.BlockSpec((2, D_MODEL), lambda i: (0, 0)), pl.BlockSpec((1, D_MODEL), lambda i: (0, 0))]
    out_specs = [pl.BlockSpec((rb, D_MODEL), lambda i: (i, 0)), s_spec]
    out_shape = [jax.ShapeDtypeStruct((B * T, D_MODEL), BF16),
                 jax.ShapeDtypeStruct((B, HGRN_HEADS, HGRN_D, HGRN_D), F32)]
    scratch = [pltpu.VMEM((NB, HGRN_HEADS, HGRN_D, HGRN_D), F32),
               pltpu.VMEM((rb // L * HGRN_HEADS, L, L), F32)]
    return nt, rb, s_spec, in_tail, out_specs, out_shape, scratch


def _hgrn(proj, s0, lb_logits, norm_g, *, B, T, TB, NB, L, Tv):
    nt, rb, s_spec, in_tail, out_specs, out_shape, scratch = _hgrn_specs(B, T, TB, NB, L)
    assert NB == 1 or nt == 1
    assert TB == L
    cols = [pl.BlockSpec((NB * Tv, D_MODEL), lambda i, c=c: (i, c)) for c in range(4)]
    return pl.pallas_call(
        functools.partial(_hgrn_kernel, nt=nt, NB=NB, TB=TB, L=L, Tv=Tv),
        grid=(B // NB * nt,),
        in_specs=cols + [s_spec] + in_tail, out_specs=out_specs, out_shape=out_shape, scratch_shapes=scratch,
        compiler_params=pltpu.CompilerParams(vmem_limit_bytes=VMEM_LIMIT),
        name="hgrn2",
    )(proj, proj, proj, proj, s0, lb_logits, norm_g)


def _hgrn_fused(x2d, scale3, shift3, norm_g, w, b, lb_logits, hgrn_norm_g, *, B, T, TB, L):
    nt, rb, _, in_tail, out_specs, out_shape, scratch = _hgrn_specs(B, T, TB, 1, L)
    n_blocks = B * nt
    return pl.pallas_call(
        functools.partial(_hgrn_fused_kernel, nt=nt, NB=1, TB=TB, L=L, Tv=L),
        grid=(n_blocks,),
        in_specs=_fused_projection_specs(n_blocks, nt, TB, N_HGRN, False) + in_tail,
        out_specs=out_specs + [pl.BlockSpec((TB, D_MODEL), lambda i: (i, 0))],
        out_shape=out_shape + [jax.ShapeDtypeStruct((B * T, D_MODEL), BF16)],
        scratch_shapes=scratch + [pltpu.VMEM((TB, N_HGRN), F32), pltpu.VMEM((TB, N_HGRN), F32)],
        compiler_params=pltpu.CompilerParams(vmem_limit_bytes=VMEM_LIMIT),
        name="hgrn2_fused",
    )(x2d, scale3, shift3, norm_g, w, b, x2d, scale3, shift3, lb_logits, hgrn_norm_g)


def _mlstm_step(load, t, nt, c0_ref, n0_ref, m0_ref, tail0_ref,
                cw_ref, cb_ref, wq_ref, wk_ref, wkt_ref, wv_ref, ng_ref, skip_ref,
                y_ref, c_ref, n_ref, m_ref, tail_scr, conv_ref,
                *, NB, TB, L, Tv, side_work=None):
    H = MLSTM_HEADS
    rb = NB * TB
    spn = TB // L
    if nt == 1:
        c_in, n_in, m_in, tail_in = c0_ref, n0_ref, m0_ref, tail0_ref
    else:
        c_in, n_in, m_in, tail_in = c_ref, n_ref, m_ref, tail_scr

        @pl.when(t == 0)
        def _():
            for dst, src in ((c_ref, c0_ref), (n_ref, n0_ref), (m_ref, m0_ref), (tail_scr, tail0_ref)):
                dst[...] = jnp.zeros(dst.shape, F32) if src is None else src[...]

    sw = side_work if side_work is not None else _SideWork()
    sw.run(1)

    cw = cw_ref[...]
    ng = ng_ref[...]
    skip = skip_ref[...]
    lane = lax.broadcasted_iota(jnp.int32, (1, LANES), 1)
    ti = lax.broadcasted_iota(jnp.int32, (L, L), 0)
    ji = lax.broadcasted_iota(jnp.int32, (L, L), 1)
    causal = ji <= ti

    mu = load(0)
    xcs = []
    for nb in range(NB):
        u_nb = mu[nb * TB:(nb + 1) * TB, :]
        ext = jnp.concatenate([tail_in[nb], u_nb], axis=0)
        conv = cb_ref[...] + cw[CONV_WIDTH - 1:CONV_WIDTH, :] * u_nb
        for i in range(1, CONV_WIDTH):
            conv = conv + cw[CONV_WIDTH - 1 - i:CONV_WIDTH - i, :] * pltpu.roll(ext, i, 0)[SUBLANES:, :]
            sw.run(1)
        xcs.append(_silu(conv))
        sw.run(1)
        if nt > 1:
            tail_scr[nb] = u_nb[TB - SUBLANES:, :]
    if conv_ref is not None:
        @pl.when(t == nt - 1)
        def _():
            for nb in range(NB):
                conv_ref[nb] = mu[(nb + 1) * TB - SUBLANES:(nb + 1) * TB, :]
    xc = xcs[0] if NB == 1 else jnp.concatenate(xcs, axis=0)
    xcb = xc.astype(BF16)
    mub = mu.astype(BF16)
    mz = load(1)
    zgate = _silu(mz)
    sw.run(2)
    ogate = _sigmoid(load(2))
    sw.run(2)

    gt = load(3)
    lf = jnp.minimum(gt, 0.0) - jnp.log1p(jnp.exp(-jnp.abs(gt)))
    ig = gt
    if Tv < L:
        valid = (lax.broadcasted_iota(jnp.int32, (rb, 1), 0) & (L - 1)) < Tv
        lf = jnp.where(valid, lf, 0.0)
        ig = jnp.where(valid, ig, -jnp.inf)
    bcum = _chunk_cumsum(lf, L)
    sw.run(1)
    comb = jnp.where(lane < H, ig, bcum)
    rpad = -rb % LANES
    if rpad:
        comb = jnp.concatenate([comb, jnp.zeros((rpad, LANES), F32)], axis=0)
    combT = comb.T

    assert spn == 1
    heads = {}

    def project_head(h):
        hs = slice(h * MLSTM_DH, (h + 1) * MLSTM_DH)
        q_all = _dot(xcb[:, hs], wq_ref[h])
        kb_all = _dot(xcb[:, hs], wk_ref[h]).astype(BF16)
        kT_all = _dot_nt(wkt_ref[h], xcb[:, hs])
        vb_all = _dot(mub[:, hs], wv_ref[h]).astype(BF16)
        heads[h] = (hs, q_all, q_all.astype(BF16), kb_all, kT_all, vb_all)
        sw.run(1)

    pairs = [(h, nb) for h in range(H) for nb in range(NB)]
    groups = [pairs] if NB > 1 else [[p] for p in pairs]
    rows_of = lambda nb: slice(nb * L, (nb + 1) * L)

    def run_group(pairs):
        for h in sorted({h for h, _ in pairs}):
            project_head(h)

        gates = []
        for h, nb in pairs:
            rows = rows_of(nb)
            bcol = bcum[rows, H + h:H + h + 1]
            irow = combT[h:h + 1, rows]
            brow = combT[H + h:H + h + 1, rows]
            logD = jnp.where(causal, (bcol - brow) + irow, -jnp.inf)
            m_intra = jnp.max(logD, axis=-1, keepdims=True)
            gates.append((bcol, irow, brow, m_intra, jnp.exp(logD - m_intra)))

        scores = []
        for (h, nb), g in zip(pairs, gates):
            _, _, qb_all, kb_all, _, _ = heads[h]
            rows = rows_of(nb)
            scores.append(_dot_nt(qb_all[rows], kb_all[rows]) * g[4])

        intra = []
        for (h, nb), sc in zip(pairs, scores):
            vb_all = heads[h][5]
            intra.append((jnp.sum(sc, axis=-1, keepdims=True), _dot(sc.astype(BF16), vb_all[rows_of(nb)])))

        writes = []
        for (h, nb), g in zip(pairs, gates):
            _, _, _, kb_all, kT_all, vb_all = heads[h]
            rows = rows_of(nb)
            bcol, irow, brow, m_intra, _ = g
            m_loc = m_intra[L - 1:L, :]
            b_last = bcol[L - 1:L, :]
            wrow = jnp.exp((b_last - brow) + irow - m_loc)
            kw = (kT_all[:, rows] * wrow).astype(BF16)
            if len(pairs) == 1:
                kw = _dot(kw, vb_all[rows])
            ks = _dot(jnp.broadcast_to(wrow, (SUBLANES, L)).astype(BF16), kb_all[rows])[0:1, :]
            writes.append((m_loc, b_last, kw, ks))

        outs = []
        updates = []
        for (h, nb), g, (rs, sv), (m_loc, b_last, kw, ks) in zip(pairs, gates, intra, writes):
            hs, q_all, qb_all, _, _, _ = heads[h]
            rows = rows_of(nb)
            bcol, _, _, m_intra, _ = g
            C = c_in[nb, h]
            nh = n_in[nb, :, hs]
            m_prev = m_in[nb][:, h:h + 1]
            m_inter = bcol + m_prev
            m_t = jnp.maximum(m_inter, m_intra)
            inter = jnp.exp(m_inter - m_t)
            scl = jnp.exp(m_intra - m_t)
            den = inter * jnp.sum(q_all[rows] * nh, axis=-1, keepdims=True) + scl * rs
            rden = 1.0 / jnp.maximum(jnp.abs(den), jnp.exp(-m_t))
            hh = (inter * rden) * _dot(qb_all[rows], C.astype(BF16)) + (scl * rden) * sv
            m_new = m_t[L - 1:L, :]
            dec = jnp.exp(b_last + m_prev - m_new)
            scu = jnp.exp(m_loc - m_new)
            n_ref[nb, :, hs] = dec * nh + scu * ks
            outs.append((hh, m_new))
            if len(pairs) > 1:
                updates.append((dec, scu, kw))
            else:
                c_ref[nb, h] = dec * C + scu * kw

        for (h, nb), (dec, scu, kwT) in zip(pairs, updates):
            c_ref[nb, h] = dec * c_in[nb, h] + scu * _dot(kwT, heads[h][5][rows_of(nb)])

        for (h, nb), (hh, _) in zip(pairs, outs):
            hs = heads[h][0]
            rows = rows_of(nb)
            hm = ogate[rows, hs] * hh
            ms = jnp.mean(hm * hm, axis=-1, keepdims=True)
            y = (hm * lax.rsqrt(ms + EPS) * ng[:, hs] + skip[:, hs] * xc[rows, hs]) * zgate[rows, hs]
            y_ref[rows, hs] = y.astype(y_ref.dtype)
        return outs

    outs = [o for grp in groups for o in run_group(grp)]

    for nb in range(NB):
        m_row = m_in[nb]
        for h in range(H):
            m_row = jnp.where(lane == h, outs[h * NB + nb][1], m_row)
        m_ref[nb] = m_row
    sw.flush()


N_MLSTM_WEIGHTS = 8


def _mlstm_kernel(u_ref, z_ref, o_ref, gate_ref, *rest, nt, **static):
    cols = (u_ref, z_ref, o_ref, gate_ref)
    ins, (y_ref, c_ref, n_ref, m_ref, tail_scr) = rest[:4 + N_MLSTM_WEIGHTS], rest[4 + N_MLSTM_WEIGHTS:]
    _mlstm_step(lambda c: _pad_sequences(cols[c][...], static["NB"], static["Tv"], static["L"]),
                lax.rem(pl.program_id(0), nt), nt, *ins,
                y_ref, c_ref, n_ref, m_ref, tail_scr, None, **static)


def _mlstm_fused_kernel(x0_ref, sc0_ref, sh0_ref, ng_ref, w_ref, b_ref, hn_ref, *rest, nt, **static):
    i = pl.program_id(0)
    weights = rest[:N_MLSTM_WEIGHTS]
    y_ref, c_ref, n_ref, m_ref, conv_ref, tail_scr, proj_a, proj_b = rest[N_MLSTM_WEIGHTS:]
    widths = (D_MODEL, D_MODEL, D_MODEL, LANES)

    def step(cur, side_work):
        _mlstm_step(lambda c: cur[:, c * D_MODEL:c * D_MODEL + widths[c]], lax.rem(i, nt), nt,
                    None, None, None, None, *weights,
                    y_ref, c_ref, n_ref, m_ref, tail_scr, conv_ref, side_work=side_work, **static)

    _fused_projection_schedule(i, proj_a, proj_b, x0_ref, sc0_ref, sh0_ref, ng_ref, w_ref, b_ref,
                               lambda: hn_ref[...], step)


def _mlstm_specs(B, T, TB, NB):
    nt = T // TB
    rb = NB * TB

    def full(shape):
        return pl.BlockSpec(shape, lambda i: (0,) * len(shape))

    c_spec = pl.BlockSpec((NB, MLSTM_HEADS, MLSTM_DH, MLSTM_DH), lambda i: (i // nt, 0, 0, 0))
    n_spec = pl.BlockSpec((NB, 1, D_MODEL), lambda i: (i // nt, 0, 0))
    m_spec = pl.BlockSpec((NB, 1, LANES), lambda i: (i // nt, 0, 0))
    t_spec = pl.BlockSpec((NB, SUBLANES, D_MODEL), lambda i: (i // nt, 0, 0))
    head_w = full((MLSTM_HEADS, MLSTM_DH, MLSTM_DH))
    state_specs = [c_spec, n_spec, m_spec, t_spec]
    weight_specs = [full((CONV_WIDTH, D_MODEL)), full((1, D_MODEL)),
                    head_w, head_w, head_w, head_w, full((1, D_MODEL)), full((1, D_MODEL))]
    out_specs = [pl.BlockSpec((rb, D_MODEL), lambda i: (i, 0)), c_spec, n_spec, m_spec]
    out_shape = [jax.ShapeDtypeStruct((B * T, D_MODEL), BF16),
                 jax.ShapeDtypeStruct((B, MLSTM_HEADS, MLSTM_DH, MLSTM_DH), F32),
                 jax.ShapeDtypeStruct((B, 1, D_MODEL), F32),
                 jax.ShapeDtypeStruct((B, 1, LANES), F32)]
    scratch = [pltpu.VMEM((NB, SUBLANES, D_MODEL), F32)]
    return nt, rb, state_specs, weight_specs, out_specs, out_shape, scratch, t_spec


def _mlstm(proj, states, weights, *, B, T, TB, NB, L, Tv):
    nt, rb, state_specs, weight_specs, out_specs, out_shape, scratch, _ = _mlstm_specs(B, T, TB, NB)
    assert NB == 1 or nt == 1
    assert TB == L
    cols = [pl.BlockSpec((NB * Tv, D_MODEL), lambda i, c=c: (i, c)) for c in range(3)]
    cols.append(pl.BlockSpec((NB * Tv, LANES), lambda i: (i, GATE_TILE)))
    return pl.pallas_call(
        functools.partial(_mlstm_kernel, nt=nt, NB=NB, TB=TB, L=L, Tv=Tv),
        grid=(B // NB * nt,),
        in_specs=cols + state_specs + weight_specs,
        out_specs=out_specs, out_shape=out_shape, scratch_shapes=scratch,
        compiler_params=pltpu.CompilerParams(vmem_limit_bytes=VMEM_LIMIT),
        name="mlstm",
    )(proj, proj, proj, proj, *states, *weights)


def _mlstm_fused(x2d, scale3, shift3, norm_g, w, b, h_next, weights, *, B, T, TB, L):
    nt, rb, _, weight_specs, out_specs, out_shape, scratch, t_spec = _mlstm_specs(B, T, TB, 1)
    n_blocks = B * nt
    return pl.pallas_call(
        functools.partial(_mlstm_fused_kernel, nt=nt, NB=1, TB=TB, L=L, Tv=L),
        grid=(n_blocks,),
        in_specs=_fused_projection_specs(n_blocks, nt, TB, N_MLSTM_PAD, True) + weight_specs,
        out_specs=out_specs + [t_spec],
        out_shape=out_shape + [jax.ShapeDtypeStruct((B, SUBLANES, D_MODEL), F32)],
        scratch_shapes=scratch + [pltpu.VMEM((TB, N_MLSTM_PAD), F32), pltpu.VMEM((TB, N_MLSTM_PAD), F32)],
        compiler_params=pltpu.CompilerParams(vmem_limit_bytes=VMEM_LIMIT),
        name="mlstm_fused",
    )(x2d, scale3, shift3, norm_g, w, b, h_next, *weights)


def _out_kernel(yh_ref, ym_ref, x_ref, gate_ref, w_ref, fg_ref, o_ref):
    acc = _dot(jnp.concatenate([yh_ref[...], ym_ref[...]], axis=1), w_ref[...])
    out = x_ref[...] + gate_ref[0] * acc
    ms = jnp.mean(out * out, axis=-1, keepdims=True)
    o_ref[...] = out * lax.rsqrt(ms + EPS) * fg_ref[...]


def _out_projection(yh, ym, x2d, gate3, w_out, final_g, rb):
    rows = x2d.shape[0]
    mrows = gate3.shape[1]
    nblk = rows // rb
    return pl.pallas_call(
        _out_kernel,
        grid=(nblk,),
        in_specs=[pl.BlockSpec((rb, D_MODEL), lambda i: (i, 0)),
                  pl.BlockSpec((rb, D_MODEL), lambda i: (i, 0)),
                  pl.BlockSpec((rb, D_MODEL), lambda i: (i, 0)),
                  pl.BlockSpec((1, mrows, D_MODEL), lambda i: (i * gate3.shape[0] // nblk, 0, 0)),
                  pl.BlockSpec((2 * D_MODEL, D_MODEL), lambda i: (0, 0)),
                  pl.BlockSpec((1, D_MODEL), lambda i: (0, 0))],
        out_specs=pl.BlockSpec((rb, D_MODEL), lambda i: (i, 0)),
        out_shape=jax.ShapeDtypeStruct((rows, D_MODEL), F32),
        compiler_params=pltpu.CompilerParams(vmem_limit_bytes=VMEM_LIMIT),
        name="out_projection",
    )(yh, ym, x2d, gate3, w_out, final_g)


def _block_diag_heads(w):
    rows = w.reshape(MLSTM_HEADS, MLSTM_DH, QKV_BLOCK)
    tiled = jnp.tile(rows, (1, 1, MLSTM_DH // QKV_BLOCK))
    rg = lax.broadcasted_iota(jnp.int32, (MLSTM_DH, MLSTM_DH), 0) // QKV_BLOCK
    cg = lax.broadcasted_iota(jnp.int32, (MLSTM_DH, MLSTM_DH), 1) // QKV_BLOCK
    return jnp.where(rg == cg, tiled, 0.0).astype(BF16)


def _mlstm_state_operands(c0, n0, m0, conv0):
    B = c0.shape[0]
    tail0 = jnp.pad(conv0, ((0, 0), (SUBLANES - (CONV_WIDTH - 1), 0), (0, 0)))
    m0p = jnp.pad(m0, ((0, 0), (0, LANES - MLSTM_HEADS))).reshape(B, 1, LANES)
    return (c0, n0.reshape(B, 1, D_MODEL), m0p, tail0)


def _unpack_mlstm_state(B, n_new, m_new):
    return n_new.reshape(B, MLSTM_HEADS, MLSTM_DH), m_new.reshape(B, LANES)[:, :MLSTM_HEADS]


def kernel(x_prompt, x_sample, c_prompt, c_sample, state_hgrn, state_mlstm_C, state_mlstm_n, state_mlstm_m, state_mlstm_conv, w_ada, b_ada, norm_g, w_in, b_in, hgrn_lb_logits, hgrn_norm_g, mlstm_conv_w, mlstm_conv_b, mlstm_wq, mlstm_wk, mlstm_wv, mlstm_norm_g, mlstm_skip, w_out, final_g):
    assert w_in.shape == (1, D_MODEL, N_PROJ) and hgrn_lb_logits.shape == (2, D_MODEL)
    Bp, Tp, _ = x_prompt.shape
    Bs, Ts, _ = x_sample.shape
    assert Tp % PROMPT_BLOCK == 0 and Ts <= SAMPLE_ROWS and Ts >= CONV_WIDTH - 1

    mod = _modulation(jnp.concatenate([c_prompt, c_sample], axis=0), w_ada[0], b_ada[0].reshape(1, -1))
    shift, scale, gate = mod[:, :D_MODEL], mod[:, D_MODEL:2 * D_MODEL], mod[:, 2 * D_MODEL:]
    pad_cols = N_MLSTM_PAD - N_MLSTM
    ng = norm_g[0].reshape(1, -1)
    w_h = w_in[0, :, :N_HGRN].astype(BF16)
    w_m = jnp.pad(w_in[0, :, N_HGRN:].astype(BF16), ((0, 0), (0, pad_cols)))
    b_h = b_in[0, :N_HGRN].reshape(1, -1)
    b_m = jnp.pad(b_in[0, N_HGRN:], (0, pad_cols)).reshape(1, -1)
    hg = hgrn_norm_g[0].reshape(1, -1)
    mw = (mlstm_conv_w[0], mlstm_conv_b[0].reshape(1, -1),
          _block_diag_heads(mlstm_wq[0]), _block_diag_heads(mlstm_wk[0] * KEY_SCALE),
          _block_diag_heads(jnp.swapaxes(mlstm_wk[0], -1, -2) * KEY_SCALE), _block_diag_heads(mlstm_wv[0]),
          mlstm_norm_g[0].reshape(1, -1), mlstm_skip[0].reshape(1, -1))
    wo = w_out[0].astype(BF16)
    fg = final_g.reshape(1, -1)

    xp2 = x_prompt.reshape(Bp * Tp, D_MODEL)
    per_seq = lambda a: a.reshape(-1, 1, D_MODEL)
    sc_p, sh_p = per_seq(scale[:Bp]), per_seq(shift[:Bp])
    yh_p, hg_p, h_next = _hgrn_fused(xp2, sc_p, sh_p, ng, w_h, b_h, hgrn_lb_logits, hg,
                                     B=Bp, T=Tp, TB=PROMPT_BLOCK, L=PROMPT_CHUNK)
    ym_p, c_p, n_p, m_p, tail_p = _mlstm_fused(xp2, sc_p, sh_p, ng, w_m, b_m, h_next, mw,
                                               B=Bp, T=Tp, TB=PROMPT_BLOCK, L=PROMPT_BLOCK)
    n_p, m_p = _unpack_mlstm_state(Bp, n_p, m_p)
    conv_p = tail_p[:, SUBLANES - (CONV_WIDTH - 1):]
    yp = _out_projection(yh_p, ym_p, xp2, per_seq(gate[:Bp]), wo, fg, OUT_PROJ_ROWS).reshape(Bp, Tp, D_MODEL)

    rows_s = Bs * SAMPLE_ROWS
    xs2 = jnp.pad(x_sample, ((0, 0), (0, SAMPLE_ROWS - Ts), (0, 0))).reshape(rows_s, D_MODEL)
    per_row = lambda a: jnp.repeat(a, SAMPLE_ROWS, axis=0).reshape(1, rows_s, D_MODEL)
    per_tok = lambda a: jnp.repeat(a, Ts, axis=0).reshape(1, Bs * Ts, D_MODEL)
    xs_tok = x_sample.reshape(Bs * Ts, D_MODEL)
    sc_s, sh_s = per_tok(scale[Bp:]), per_tok(shift[Bp:])
    proj_h = _in_projection(xs_tok, sc_s, sh_s, ng, w_h, b_h, Bs * Ts, SAMPLE_HGRN_COL_TILE)
    proj_m = _in_projection(xs_tok, sc_s, sh_s, ng, w_m, b_m, Bs * Ts, SAMPLE_MLSTM_COL_TILE)
    yh_s, hg_s = _hgrn(proj_h, state_hgrn[0], hgrn_lb_logits, hg, B=Bs, T=SAMPLE_ROWS, TB=SAMPLE_ROWS,
                       NB=SAMPLE_SEQS_PER_STEP, L=SAMPLE_ROWS, Tv=Ts)
    states_s = _mlstm_state_operands(state_mlstm_C[0], state_mlstm_n[0], state_mlstm_m[0], state_mlstm_conv[0])
    ym_s, c_s, n_s, m_s = _mlstm(proj_m, states_s, mw, B=Bs, T=SAMPLE_ROWS, TB=SAMPLE_ROWS,
                                 NB=SAMPLE_SEQS_PER_STEP, L=SAMPLE_ROWS, Tv=Ts)
    n_s, m_s = _unpack_mlstm_state(Bs, n_s, m_s)
    conv_s = proj_m.reshape(Bs, Ts, N_MLSTM_PAD)[:, Ts - (CONV_WIDTH - 1):, :D_MODEL]
    ys = _out_projection(yh_s, ym_s, xs2, per_row(gate[Bp:]), wo, fg, rows_s).reshape(Bs, SAMPLE_ROWS, D_MODEL)

    return (yp, ys[:, :Ts], hg_p[None], c_p[None], n_p[None], m_p[None], conv_p[None],
            hg_s[None], c_s[None], n_s[None], m_s[None], conv_s[None])
```

```python
import functools

import jax
import jax.numpy as jnp
from jax import lax
from jax.experimental import pallas as pl
from jax.experimental.pallas import tpu as pltpu

F32 = jnp.float32
BF16 = jnp.bfloat16

D_MODEL = 1024
HGRN_HEADS = 8
HGRN_D = 128
MLSTM_HEADS = 4
MLSTM_DH = 256
KEY_SCALE = MLSTM_DH ** -0.5
CONV_WIDTH = 4
QKV_BLOCK = 4
EPS = 1e-6
N_PROJ = 7176
N_HGRN = 4 * D_MODEL
N_MLSTM = N_PROJ - N_HGRN
N_MLSTM_PAD = 3 * D_MODEL + 128
GATE_TILE = 3 * D_MODEL // 128
MXU_WIDTH = 256
PROMPT_CHUNK = 128
PROMPT_BLOCK = 256
SAMPLE_ROWS = 8
SAMPLE_SEQS_PER_STEP = 8
OUT_PROJ_ROWS = 1024
MOD_COL_TILE = 512
SAMPLE_HGRN_COL_TILE = 2048
SAMPLE_MLSTM_COL_TILE = 640
SUBLANES = 8
LANES = 128
VMEM_LIMIT = 56 * 1024 * 1024
FAST_PATH_MIN_LOG_DECAY = -80.0


def _dot(a, b):
    return jnp.dot(a, b, preferred_element_type=F32)


def _dot_nt(a, b):
    return lax.dot_general(a, b, (((1,), (1,)), ((), ())), preferred_element_type=F32)


def _dot_tn(a, b):
    return lax.dot_general(a, b, (((0,), (0,)), ((), ())), preferred_element_type=F32)


def _sigmoid(x):
    return 0.5 * jnp.tanh(0.5 * x) + 0.5


def _silu(x):
    u = 0.5 * x
    return u * jnp.tanh(u) + u


def _levels(L):
    out, s = [], 1
    while s < L:
        out.append(s)
        s *= 2
    return out


def _seg_bcast(W, s, L, r):
    n = W.shape[1]
    if s == 1:
        return jnp.where((r & 1) != 0, pltpu.roll(W, 1, 0), W)
    if s == 2:
        m = r & 3
        return jnp.where(m == 0, pltpu.roll(W, L - 1, 0),
                         jnp.where(m == 1, W,
                                   jnp.where(m == 2, pltpu.roll(W, 1, 0), pltpu.roll(W, 2, 0))))
    pieces = [jnp.broadcast_to(W[b * 2 * s + s - 1:b * 2 * s + s, :], (2 * s, n))
              for b in range(L // (2 * s))]
    return pieces[0] if len(pieces) == 1 else jnp.concatenate(pieces, axis=0)


def _level_factors(x, L, r):
    W = x
    factors = []
    for s in _levels(L):
        Tb = _seg_bcast(W, s, L, r)
        sec = (r & s) != 0
        factors.append(jnp.exp(jnp.where(sec, W, Tb - W)))
        W = W + jnp.where(sec, Tb, 0.0)
    return factors


def _prefix8(x):
    sub = lax.broadcasted_iota(jnp.int32, (SUBLANES, 1), 0)
    y = x + jnp.where(sub >= 1, pltpu.roll(x, 1, 0), 0.0)
    y = y + jnp.where(sub >= 2, pltpu.roll(y, 2, 0), 0.0)
    return y + jnp.where(sub >= 4, pltpu.roll(y, 4, 0), 0.0)


def _chunk_cumsum(x, L):
    outs = []
    for c in range(x.shape[0] // L):
        total = None
        for g in range(L // SUBLANES):
            lo = c * L + g * SUBLANES
            p = _prefix8(x[lo:lo + SUBLANES, :])
            if total is not None:
                p = p + total
            outs.append(p)
            total = p[SUBLANES - 1:SUBLANES, :]
    return outs[0] if len(outs) == 1 else jnp.concatenate(outs, axis=0)


def _rows_bcast(x, row_in_chunk, L):
    n = x.shape[1]
    pieces = [jnp.broadcast_to(x[c * L + row_in_chunk:c * L + row_in_chunk + 1, :], (L, n))
              for c in range(x.shape[0] // L)]
    return pieces[0] if len(pieces) == 1 else jnp.concatenate(pieces, axis=0)


def _rms_mod_bf16(x, scale, shift, g):
    ms = jnp.mean(x * x, axis=-1, keepdims=True)
    return ((x * lax.rsqrt(ms + EPS)) * (g * (1.0 + scale)) + shift).astype(BF16)


def _output_major(w_shape):
    assert (w_shape[0] == D_MODEL) != (w_shape[1] == D_MODEL)
    return w_shape[1] == D_MODEL


def _n_columns(w_ref):
    return w_ref.shape[0] if _output_major(w_ref.shape) else w_ref.shape[1]


def _project_columns(h, w_ref, lo, hi):
    if _output_major(w_ref.shape):
        return _dot_nt(h, w_ref[lo:hi, :])
    return _dot(h, w_ref[:, lo:hi])


def _project_into(dst_ref, h, w_ref, b_ref):
    n = _n_columns(w_ref)
    bounds = list(range(0, n - n % D_MODEL, D_MODEL)) or [0]
    for k, lo in enumerate(bounds):
        hi = n if k == len(bounds) - 1 else lo + D_MODEL
        dst_ref[:, lo:hi] = _project_columns(h, w_ref, lo, hi) + b_ref[:, lo:hi]


def _mod_kernel(c_ref, w_ref, b_ref, o_ref):
    c = c_ref[...]
    a = _silu(c)
    o_ref[...] = _dot(a.astype(BF16), w_ref[...].astype(BF16)) + b_ref[...]


def _modulation(c_all, w_ada, b_ada):
    m = c_all.shape[0]
    n = w_ada.shape[1]
    tn = MOD_COL_TILE
    return pl.pallas_call(
        _mod_kernel,
        grid=(n // tn,),
        in_specs=[pl.BlockSpec((m, D_MODEL), lambda j: (0, 0)),
                  pl.BlockSpec((D_MODEL, tn), lambda j: (0, j)),
                  pl.BlockSpec((1, tn), lambda j: (0, j))],
        out_specs=pl.BlockSpec((m, tn), lambda j: (0, j)),
        out_shape=jax.ShapeDtypeStruct((m, n), F32),
        name="modulation",
    )(c_all, w_ada, b_ada)


def _inproj_kernel(x_ref, scale_ref, shift_ref, g_ref, w_ref, b_ref, o_ref, h_scr):
    @pl.when(pl.program_id(1) == 0)
    def _():
        h_scr[...] = _rms_mod_bf16(x_ref[...], scale_ref[0], shift_ref[0], g_ref[...])

    o_ref[...] = _project_columns(h_scr[...], w_ref, 0, _n_columns(w_ref)) + b_ref[...]


def _in_projection(x2d, scale3, shift3, norm_g, w, b, rb, col_tile):
    rows = x2d.shape[0]
    mrows = scale3.shape[1]
    nblk = rows // rb
    out_major = _output_major(w.shape)
    n_cols = w.shape[0] if out_major else w.shape[1]
    w_spec = (pl.BlockSpec((col_tile, D_MODEL), lambda i, j: (j, 0)) if out_major
              else pl.BlockSpec((D_MODEL, col_tile), lambda i, j: (0, j)))
    return pl.pallas_call(
        _inproj_kernel,
        grid=(nblk, n_cols // col_tile),
        in_specs=[pl.BlockSpec((rb, D_MODEL), lambda i, j: (i, 0)),
                  pl.BlockSpec((1, mrows, D_MODEL), lambda i, j: (i * scale3.shape[0] // nblk, 0, 0)),
                  pl.BlockSpec((1, mrows, D_MODEL), lambda i, j: (i * shift3.shape[0] // nblk, 0, 0)),
                  pl.BlockSpec((1, D_MODEL), lambda i, j: (0, 0)),
                  w_spec,
                  pl.BlockSpec((1, col_tile), lambda i, j: (0, j))],
        out_specs=pl.BlockSpec((rb, col_tile), lambda i, j: (i, j)),
        out_shape=jax.ShapeDtypeStruct((rows, n_cols), F32),
        scratch_shapes=[pltpu.VMEM((rb, D_MODEL), BF16)],
        compiler_params=pltpu.CompilerParams(vmem_limit_bytes=VMEM_LIMIT),
        name="in_projection",
    )(x2d, scale3, shift3, norm_g, w, b)


class _SideWork:
    def __init__(self, pieces=()):
        self._pieces = list(pieces)

    def run(self, n=1):
        for _ in range(n):
            if self._pieces:
                self._pieces.pop(0)()

    def flush(self):
        self.run(len(self._pieces))


def _fused_projection_schedule(i, proj_a, proj_b, x0_ref, sc0_ref, sh0_ref, ng_ref, w_ref, b_ref, next_h, step):
    n = _n_columns(w_ref)

    @pl.when(i == 0)
    def _():
        _project_into(proj_a, _rms_mod_bf16(x0_ref[...], sc0_ref[0], sh0_ref[0], ng_ref[...]), w_ref, b_ref)

    def run(cur, nxt):
        h = []

        def norm_piece():
            h.append(next_h())

        def tile_piece(lo):
            hi = min(lo + MXU_WIDTH, n)

            def piece():
                nxt[:, lo:hi] = _project_columns(h[0], w_ref, lo, hi) + b_ref[:, lo:hi]
            return piece

        step(cur, _SideWork([norm_piece] + [tile_piece(lo) for lo in range(0, n, MXU_WIDTH)]))

    parity = lax.rem(i, 2)
    pl.when(parity == 0)(lambda: run(proj_a, proj_b))
    pl.when(parity == 1)(lambda: run(proj_b, proj_a))


def _fused_projection_specs(n_blocks, nt, tb, w_shape, next_h_given):
    n_cols = w_shape[0] if _output_major(w_shape) else w_shape[1]
    def nxt(i):
        return jnp.minimum(i + 1, n_blocks - 1)
    vec = (1, 1, D_MODEL)
    specs = [pl.BlockSpec((tb, D_MODEL), lambda i: (0, 0)),
             pl.BlockSpec(vec, lambda i: (0, 0, 0)), pl.BlockSpec(vec, lambda i: (0, 0, 0)),
             pl.BlockSpec((1, D_MODEL), lambda i: (0, 0)),
             pl.BlockSpec(tuple(w_shape), lambda i: (0, 0)),
             pl.BlockSpec((1, n_cols), lambda i: (0, 0))]
    if next_h_given:
        return specs + [pl.BlockSpec((tb, D_MODEL), lambda i: (i, 0))]
    return specs + [pl.BlockSpec((tb, D_MODEL), lambda i: (nxt(i), 0)),
                    pl.BlockSpec(vec, lambda i: (nxt(i) // nt, 0, 0)),
                    pl.BlockSpec(vec, lambda i: (nxt(i) // nt, 0, 0))]


def _hgrn_step(load, t, nt, s0_ref, lbl_ref, g_ref, y_ref, s_ref, st_scr, a_scr,
               *, NB, TB, L, Tv, side_work=None):
    H = HGRN_HEADS
    rb = NB * TB
    spn = TB // L
    nseg = rb // L

    @pl.when(t == 0)
    def _():
        for nb in range(NB):
            for h in range(H):
                st_scr[nb, h] = jnp.zeros((HGRN_D, HGRN_D), F32) if s0_ref is None else s0_ref[nb, h].T

    sw = side_work if side_work is not None else _SideWork()
    sw.run(1)

    lg = lbl_ref[...]
    mx = jnp.max(lg, axis=0, keepdims=True)
    e = jnp.exp(lg - mx)
    lb = e[0:1, :] / jnp.sum(e, axis=0, keepdims=True)
    g_norm = g_ref[...]

    hq = load(0)
    sw.run(2)
    c1 = 0.5 * (1.0 - lb)
    p = c1 * jnp.tanh(0.5 * load(1))
    sw.run(2)
    logf = jnp.log((lb + c1) + p)
    sw.run(2)
    kk = c1 - p
    sw.run(2)
    if Tv < L:
        valid = (lax.broadcasted_iota(jnp.int32, (rb, 1), 0) & (L - 1)) < Tv
        logf = jnp.where(valid, logf, 0.0)
        kk = jnp.where(valid, kk, 0.0)
    G = _chunk_cumsum(logf, L)
    sw.run(3)
    GL = [G[s * L + L - 1:s * L + L, :] for s in range(nseg)]
    fast_ok = jnp.min(functools.reduce(jnp.minimum, GL)) >= FAST_PATH_MIN_LOG_DECAY

    ti = lax.broadcasted_iota(jnp.int32, (L, L), 0)
    ji = lax.broadcasted_iota(jnp.int32, (L, L), 1)

    @pl.when(fast_ok)
    def _():
        d = G - _rows_bcast(G, L // 2 - 1, L)
        qt = (hq * jnp.exp(d)).astype(BF16)
        kt = (kk * jnp.exp(-d)).astype(BF16)
        causal = ji <= ti
        for s in range(nseg):
            rows = slice(s * L, (s + 1) * L)
            for h in range(H):
                hs = slice(h * HGRN_D, (h + 1) * HGRN_D)
                a_scr[s * H + h] = jnp.where(causal, _dot_nt(qt[rows, hs], kt[rows, hs]), 0.0)

    @pl.when(jnp.logical_not(fast_ok))
    def _():
        r = lax.broadcasted_iota(jnp.int32, (L, 1), 0)
        xo = ti ^ ji
        diag_mask = ti == ji
        lvl_masks = [(ji < ti) & (xo >= s) & (xo < 2 * s) for s in _levels(L)]
        for s in range(nseg):
            rows = slice(s * L, (s + 1) * L)
            factors = _level_factors(logf[rows, :], L, r)
            for h in range(H):
                hs = slice(h * HGRN_D, (h + 1) * HGRN_D)
                qh = hq[rows, hs]
                kh = kk[rows, hs]
                A = jnp.where(diag_mask, _dot_nt(qh.astype(BF16), kh.astype(BF16)), 0.0)
                for lvl in range(len(factors)):
                    E = factors[lvl][:, hs]
                    A = A + jnp.where(lvl_masks[lvl],
                                      _dot_nt((qh * E).astype(BF16), (kh * E).astype(BF16)), 0.0)
                a_scr[s * H + h] = A

    qg = (hq * jnp.exp(G)).astype(BF16)
    sw.run(2)
    GLb = GL[0] if nseg == 1 and L == rb else jnp.concatenate(
        [jnp.broadcast_to(gl, (L, D_MODEL)) for gl in GL], axis=0)
    kd = (kk * jnp.exp(GLb - G)).astype(BF16)
    sw.run(2)
    hv = load(2)
    vb = hv.astype(BF16)
    hz = load(3)
    zgate = _silu(hz)
    sw.flush()
    merged = L % LANES == 0
    for nb in range(NB):
        st = [st_scr[nb, h] for h in range(H)]
        for c in range(spn):
            s = nb * spn + c
            rows = slice(s * L, (s + 1) * L)
            dS = jnp.exp(GL[s])
            for h in range(H):
                hs = slice(h * HGRN_D, (h + 1) * HGRN_D)
                A = a_scr[s * H + h].astype(BF16)
                if merged:
                    vT = hv[rows, hs].T.astype(BF16)
                    o = _dot_nt(jnp.concatenate([qg[rows, hs], A], axis=1),
                                jnp.concatenate([st[h].astype(BF16), vT], axis=1))
                    st[h] = st[h] * dS[:, hs] + _dot(vT, kd[rows, hs])
                else:
                    o = _dot(A, vb[rows, hs]) + _dot_nt(qg[rows, hs], st[h].astype(BF16))
                    st[h] = st[h] * dS[:, hs] + _dot_tn(vb[rows, hs], kd[rows, hs])
                ms = jnp.mean(o * o, axis=-1, keepdims=True)
                y = o * lax.rsqrt(ms + EPS) * g_norm[:, hs] * zgate[rows, hs]
                y_ref[rows, hs] = y.astype(y_ref.dtype)
        for h in range(H):
            st_scr[nb, h] = st[h]

    @pl.when(t == nt - 1)
    def _():
        for nb in range(NB):
            for h in range(H):
                s_ref[nb, h] = st_scr[nb, h].T


def _pad_sequences(a, NB, Tv, L):
    if Tv == L:
        return a
    zeros = jnp.zeros((L - Tv, a.shape[1]), a.dtype)
    return jnp.concatenate([p for nb in range(NB) for p in (a[nb * Tv:(nb + 1) * Tv, :], zeros)], axis=0)


def _hgrn_kernel(q_ref, f_ref, i_ref, z_ref, s0_ref, lbl_ref, g_ref, y_ref, s_ref, st_scr, a_scr,
                 *, nt, **static):
    cols = (q_ref, f_ref, i_ref, z_ref)
    _hgrn_step(lambda c: _pad_sequences(cols[c][...], static["NB"], static["Tv"], static["L"]),
               lax.rem(pl.program_id(0), nt), nt,
               s0_ref, lbl_ref, g_ref, y_ref, s_ref, st_scr, a_scr, **static)


def _hgrn_fused_kernel(x0_ref, sc0_ref, sh0_ref, ng_ref, w_ref, b_ref, xn_ref, scn_ref, shn_ref,
                       lbl_ref, g_ref, y_ref, s_ref, hn_ref, st_scr, a_scr, proj_a, proj_b,
                       *, nt, **static):
    i = pl.program_id(0)

    def next_h():
        h = _rms_mod_bf16(xn_ref[...], scn_ref[0], shn_ref[0], ng_ref[...])
        hn_ref[...] = h
        return h

    def step(cur, side_work):
        _hgrn_step(lambda c: cur[:, c * D_MODEL:(c + 1) * D_MODEL], lax.rem(i, nt), nt,
                   None, lbl_ref, g_ref, y_ref, s_ref, st_scr, a_scr, side_work=side_work, **static)

    _fused_projection_schedule(i, proj_a, proj_b, x0_ref, sc0_ref, sh0_ref, ng_ref, w_ref, b_ref, next_h, step)


def _hgrn_specs(B, T, TB, NB, L):
    nt = T // TB
    rb = NB * TB
    s_spec = pl.BlockSpec((NB, HGRN_HEADS, HGRN_D, HGRN_D), lambda i: (i // nt, 0, 0, 0))
    in_tail = [pl.BlockSpec((2, D_MODEL), lambda i: (0, 0)), pl.BlockSpec((1, D_MODEL), lambda i: (0, 0))]
    out_specs = [pl.BlockSpec((rb, D_MODEL), lambda i: (i, 0)), s_spec]
    out_shape = [jax.ShapeDtypeStruct((B * T, D_MODEL), BF16),
                 jax.ShapeDtypeStruct((B, HGRN_HEADS, HGRN_D, HGRN_D), F32)]
    scratch = [pltpu.VMEM((NB, HGRN_HEADS, HGRN_D, HGRN_D), F32),
               pltpu.VMEM((rb // L * HGRN_HEADS, L, L), F32)]
    return nt, rb, s_spec, in_tail, out_specs, out_shape, scratch


def _hgrn(proj, s0, lb_logits, norm_g, *, B, T, TB, NB, L, Tv):
    nt, rb, s_spec, in_tail, out_specs, out_shape, scratch = _hgrn_specs(B, T, TB, NB, L)
    assert NB == 1 or nt == 1
    assert TB == L
    cols = [pl.BlockSpec((NB * Tv, D_MODEL), lambda i, c=c: (i, c)) for c in range(4)]
    return pl.pallas_call(
        functools.partial(_hgrn_kernel, nt=nt, NB=NB, TB=TB, L=L, Tv=Tv),
        grid=(B // NB * nt,),
        in_specs=cols + [s_spec] + in_tail, out_specs=out_specs, out_shape=out_shape, scratch_shapes=scratch,
        compiler_params=pltpu.CompilerParams(vmem_limit_bytes=VMEM_LIMIT),
        name="hgrn2",
    )(proj, proj, proj, proj, s0, lb_logits, norm_g)


def _hgrn_fused(x2d, scale3, shift3, norm_g, w, b, lb_logits, hgrn_norm_g, *, B, T, TB, L):
    nt, rb, _, in_tail, out_specs, out_shape, scratch = _hgrn_specs(B, T, TB, 1, L)
    n_blocks = B * nt
    return pl.pallas_call(
        functools.partial(_hgrn_fused_kernel, nt=nt, NB=1, TB=TB, L=L, Tv=L),
        grid=(n_blocks,),
        in_specs=_fused_projection_specs(n_blocks, nt, TB, w.shape, False) + in_tail,
        out_specs=out_specs + [pl.BlockSpec((TB, D_MODEL), lambda i: (i, 0))],
        out_shape=out_shape + [jax.ShapeDtypeStruct((B * T, D_MODEL), BF16)],
        scratch_shapes=scratch + [pltpu.VMEM((TB, N_HGRN), F32), pltpu.VMEM((TB, N_HGRN), F32)],
        compiler_params=pltpu.CompilerParams(vmem_limit_bytes=VMEM_LIMIT),
        name="hgrn2_fused",
    )(x2d, scale3, shift3, norm_g, w, b, x2d, scale3, shift3, lb_logits, hgrn_norm_g)


def _mlstm_step(load, t, nt, c0_ref, n0_ref, m0_ref, tail0_ref,
                cw_ref, cb_ref, wq_ref, wk_ref, wkt_ref, wv_ref, ng_ref, skip_ref,
                y_ref, c_ref, n_ref, m_ref, tail_scr, conv_ref,
                *, NB, TB, L, Tv, side_work=None):
    H = MLSTM_HEADS
    rb = NB * TB
    spn = TB // L
    if nt == 1:
        c_in, n_in, m_in, tail_in = c0_ref, n0_ref, m0_ref, tail0_ref
    else:
        c_in, n_in, m_in, tail_in = c_ref, n_ref, m_ref, tail_scr

        @pl.when(t == 0)
        def _():
            for dst, src in ((c_ref, c0_ref), (n_ref, n0_ref), (m_ref, m0_ref), (tail_scr, tail0_ref)):
                dst[...] = jnp.zeros(dst.shape, F32) if src is None else src[...]

    sw = side_work if side_work is not None else _SideWork()
    sw.run(1)

    cw = cw_ref[...]
    ng = ng_ref[...]
    skip = skip_ref[...]
    lane = lax.broadcasted_iota(jnp.int32, (1, LANES), 1)
    ti = lax.broadcasted_iota(jnp.int32, (L, L), 0)
    ji = lax.broadcasted_iota(jnp.int32, (L, L), 1)
    causal = ji <= ti

    mu = load(0)
    xcs = []
    for nb in range(NB):
        u_nb = mu[nb * TB:(nb + 1) * TB, :]
        ext = jnp.concatenate([tail_in[nb], u_nb], axis=0)
        conv = cb_ref[...] + cw[CONV_WIDTH - 1:CONV_WIDTH, :] * u_nb
        for i in range(1, CONV_WIDTH):
            conv = conv + cw[CONV_WIDTH - 1 - i:CONV_WIDTH - i, :] * pltpu.roll(ext, i, 0)[SUBLANES:, :]
            sw.run(1)
        xcs.append(_silu(conv))
        sw.run(1)
        if nt > 1:
            tail_scr[nb] = u_nb[TB - SUBLANES:, :]
    if conv_ref is not None:
        @pl.when(t == nt - 1)
        def _():
            for nb in range(NB):
                conv_ref[nb] = mu[(nb + 1) * TB - SUBLANES:(nb + 1) * TB, :]
    xc = xcs[0] if NB == 1 else jnp.concatenate(xcs, axis=0)
    xcb = xc.astype(BF16)
    mub = mu.astype(BF16)
    mz = load(1)
    zgate = _silu(mz)
    sw.run(2)
    ogate = _sigmoid(load(2))
    sw.run(2)

    gt = load(3)
    lf = jnp.minimum(gt, 0.0) - jnp.log1p(jnp.exp(-jnp.abs(gt)))
    ig = gt
    if Tv < L:
        valid = (lax.broadcasted_iota(jnp.int32, (rb, 1), 0) & (L - 1)) < Tv
        lf = jnp.where(valid, lf, 0.0)
        ig = jnp.where(valid, ig, -jnp.inf)
    bcum = _chunk_cumsum(lf, L)
    sw.run(1)
    comb = jnp.where(lane < H, ig, bcum)
    rpad = -rb % LANES
    if rpad:
        comb = jnp.concatenate([comb, jnp.zeros((rpad, LANES), F32)], axis=0)
    combT = comb.T

    assert spn == 1
    heads = {}

    def project_head(h):
        hs = slice(h * MLSTM_DH, (h + 1) * MLSTM_DH)
        q_all = _dot(xcb[:, hs], wq_ref[h])
        kb_all = _dot(xcb[:, hs], wk_ref[h]).astype(BF16)
        kT_all = _dot_nt(wkt_ref[h], xcb[:, hs])
        vb_all = _dot(mub[:, hs], wv_ref[h]).astype(BF16)
        heads[h] = (hs, q_all, q_all.astype(BF16), kb_all, kT_all, vb_all)
        sw.run(1)

    pairs = [(h, nb) for h in range(H) for nb in range(NB)]
    groups = [pairs] if NB > 1 else [[p] for p in pairs]
    rows_of = lambda nb: slice(nb * L, (nb + 1) * L)

    def run_group(pairs):
        for h in sorted({h for h, _ in pairs}):
            project_head(h)

        gates = []
        for h, nb in pairs:
            rows = rows_of(nb)
            bcol = bcum[rows, H + h:H + h + 1]
            irow = combT[h:h + 1, rows]
            brow = combT[H + h:H + h + 1, rows]
            logD = jnp.where(causal, (bcol - brow) + irow, -jnp.inf)
            m_intra = jnp.max(logD, axis=-1, keepdims=True)
            gates.append((bcol, irow, brow, m_intra, jnp.exp(logD - m_intra)))

        scores = []
        for (h, nb), g in zip(pairs, gates):
            _, _, qb_all, kb_all, _, _ = heads[h]
            rows = rows_of(nb)
            scores.append(_dot_nt(qb_all[rows], kb_all[rows]) * g[4])

        intra = []
        for (h, nb), sc in zip(pairs, scores):
            vb_all = heads[h][5]
            intra.append((jnp.sum(sc, axis=-1, keepdims=True), _dot(sc.astype(BF16), vb_all[rows_of(nb)])))

        writes = []
        for (h, nb), g in zip(pairs, gates):
            _, _, _, kb_all, kT_all, vb_all = heads[h]
            rows = rows_of(nb)
            bcol, irow, brow, m_intra, _ = g
            m_loc = m_intra[L - 1:L, :]
            b_last = bcol[L - 1:L, :]
            wrow = jnp.exp((b_last - brow) + irow - m_loc)
            kw = (kT_all[:, rows] * wrow).astype(BF16)
            if len(pairs) == 1:
                kw = _dot(kw, vb_all[rows])
            ks = _dot(jnp.broadcast_to(wrow, (SUBLANES, L)).astype(BF16), kb_all[rows])[0:1, :]
            writes.append((m_loc, b_last, kw, ks))

        outs = []
        updates = []
        for (h, nb), g, (rs, sv), (m_loc, b_last, kw, ks) in zip(pairs, gates, intra, writes):
            hs, q_all, qb_all, _, _, _ = heads[h]
            rows = rows_of(nb)
            bcol, _, _, m_intra, _ = g
            C = c_in[nb, h]
            nh = n_in[nb, :, hs]
            m_prev = m_in[nb][:, h:h + 1]
            m_inter = bcol + m_prev
            m_t = jnp.maximum(m_inter, m_intra)
            inter = jnp.exp(m_inter - m_t)
            scl = jnp.exp(m_intra - m_t)
            den = inter * jnp.sum(q_all[rows] * nh, axis=-1, keepdims=True) + scl * rs
            rden = 1.0 / jnp.maximum(jnp.abs(den), jnp.exp(-m_t))
            hh = (inter * rden) * _dot(qb_all[rows], C.astype(BF16)) + (scl * rden) * sv
            m_new = m_t[L - 1:L, :]
            dec = jnp.exp(b_last + m_prev - m_new)
            scu = jnp.exp(m_loc - m_new)
            n_ref[nb, :, hs] = dec * nh + scu * ks
            outs.append((hh, m_new))
            if len(pairs) > 1:
                updates.append((dec, scu, kw))
            else:
                c_ref[nb, h] = dec * C + scu * kw

        for (h, nb), (dec, scu, kwT) in zip(pairs, updates):
            c_ref[nb, h] = dec * c_in[nb, h] + scu * _dot(kwT, heads[h][5][rows_of(nb)])

        for (h, nb), (hh, _) in zip(pairs, outs):
            hs = heads[h][0]
            rows = rows_of(nb)
            hm = ogate[rows, hs] * hh
            ms = jnp.mean(hm * hm, axis=-1, keepdims=True)
            y = (hm * lax.rsqrt(ms + EPS) * ng[:, hs] + skip[:, hs] * xc[rows, hs]) * zgate[rows, hs]
            y_ref[rows, hs] = y.astype(y_ref.dtype)
        return outs

    outs = [o for grp in groups for o in run_group(grp)]

    for nb in range(NB):
        m_row = m_in[nb]
        for h in range(H):
            m_row = jnp.where(lane == h, outs[h * NB + nb][1], m_row)
        m_ref[nb] = m_row
    sw.flush()


N_MLSTM_WEIGHTS = 8


def _mlstm_kernel(u_ref, z_ref, o_ref, gate_ref, *rest, nt, **static):
    cols = (u_ref, z_ref, o_ref, gate_ref)
    ins, (y_ref, c_ref, n_ref, m_ref, tail_scr) = rest[:4 + N_MLSTM_WEIGHTS], rest[4 + N_MLSTM_WEIGHTS:]
    _mlstm_step(lambda c: _pad_sequences(cols[c][...], static["NB"], static["Tv"], static["L"]),
                lax.rem(pl.program_id(0), nt), nt, *ins,
                y_ref, c_ref, n_ref, m_ref, tail_scr, None, **static)


def _mlstm_fused_kernel(x0_ref, sc0_ref, sh0_ref, ng_ref, w_ref, b_ref, hn_ref, *rest, nt, **static):
    i = pl.program_id(0)
    weights = rest[:N_MLSTM_WEIGHTS]
    y_ref, c_ref, n_ref, m_ref, conv_ref, tail_scr, proj_a, proj_b = rest[N_MLSTM_WEIGHTS:]
    widths = (D_MODEL, D_MODEL, D_MODEL, LANES)

    def step(cur, side_work):
        _mlstm_step(lambda c: cur[:, c * D_MODEL:c * D_MODEL + widths[c]], lax.rem(i, nt), nt,
                    None, None, None, None, *weights,
                    y_ref, c_ref, n_ref, m_ref, tail_scr, conv_ref, side_work=side_work, **static)

    _fused_projection_schedule(i, proj_a, proj_b, x0_ref, sc0_ref, sh0_ref, ng_ref, w_ref, b_ref,
                               lambda: hn_ref[...], step)


def _mlstm_specs(B, T, TB, NB):
    nt = T // TB
    rb = NB * TB

    def full(shape):
        return pl.BlockSpec(shape, lambda i: (0,) * len(shape))

    c_spec = pl.BlockSpec((NB, MLSTM_HEADS, MLSTM_DH, MLSTM_DH), lambda i: (i // nt, 0, 0, 0))
    n_spec = pl.BlockSpec((NB, 1, D_MODEL), lambda i: (i // nt, 0, 0))
    m_spec = pl.BlockSpec((NB, 1, LANES), lambda i: (i // nt, 0, 0))
    t_spec = pl.BlockSpec((NB, SUBLANES, D_MODEL), lambda i: (i // nt, 0, 0))
    head_w = full((MLSTM_HEADS, MLSTM_DH, MLSTM_DH))
    state_specs = [c_spec, n_spec, m_spec, t_spec]
    weight_specs = [full((CONV_WIDTH, D_MODEL)), full((1, D_MODEL)),
                    head_w, head_w, head_w, head_w, full((1, D_MODEL)), full((1, D_MODEL))]
    out_specs = [pl.BlockSpec((rb, D_MODEL), lambda i: (i, 0)), c_spec, n_spec, m_spec]
    out_shape = [jax.ShapeDtypeStruct((B * T, D_MODEL), BF16),
                 jax.ShapeDtypeStruct((B, MLSTM_HEADS, MLSTM_DH, MLSTM_DH), F32),
                 jax.ShapeDtypeStruct((B, 1, D_MODEL), F32),
                 jax.ShapeDtypeStruct((B, 1, LANES), F32)]
    scratch = [pltpu.VMEM((NB, SUBLANES, D_MODEL), F32)]
    return nt, rb, state_specs, weight_specs, out_specs, out_shape, scratch, t_spec


def _mlstm(proj, states, weights, *, B, T, TB, NB, L, Tv):
    nt, rb, state_specs, weight_specs, out_specs, out_shape, scratch, _ = _mlstm_specs(B, T, TB, NB)
    assert NB == 1 or nt == 1
    assert TB == L
    cols = [pl.BlockSpec((NB * Tv, D_MODEL), lambda i, c=c: (i, c)) for c in range(3)]
    cols.append(pl.BlockSpec((NB * Tv, LANES), lambda i: (i, GATE_TILE)))
    return pl.pallas_call(
        functools.partial(_mlstm_kernel, nt=nt, NB=NB, TB=TB, L=L, Tv=Tv),
        grid=(B // NB * nt,),
        in_specs=cols + state_specs + weight_specs,
        out_specs=out_specs, out_shape=out_shape, scratch_shapes=scratch,
        compiler_params=pltpu.CompilerParams(vmem_limit_bytes=VMEM_LIMIT),
        name="mlstm",
    )(proj, proj, proj, proj, *states, *weights)


def _mlstm_fused(x2d, scale3, shift3, norm_g, w, b, h_next, weights, *, B, T, TB, L):
    nt, rb, _, weight_specs, out_specs, out_shape, scratch, t_spec = _mlstm_specs(B, T, TB, 1)
    n_blocks = B * nt
    return pl.pallas_call(
        functools.partial(_mlstm_fused_kernel, nt=nt, NB=1, TB=TB, L=L, Tv=L),
        grid=(n_blocks,),
        in_specs=_fused_projection_specs(n_blocks, nt, TB, w.shape, True) + weight_specs,
        out_specs=out_specs + [t_spec],
        out_shape=out_shape + [jax.ShapeDtypeStruct((B, SUBLANES, D_MODEL), F32)],
        scratch_shapes=scratch + [pltpu.VMEM((TB, N_MLSTM_PAD), F32), pltpu.VMEM((TB, N_MLSTM_PAD), F32)],
        compiler_params=pltpu.CompilerParams(vmem_limit_bytes=VMEM_LIMIT),
        name="mlstm_fused",
    )(x2d, scale3, shift3, norm_g, w, b, h_next, *weights)


def _out_kernel(yh_ref, ym_ref, x_ref, gate_ref, w_ref, fg_ref, o_ref):
    acc = _dot(jnp.concatenate([yh_ref[...], ym_ref[...]], axis=1), w_ref[...])
    out = x_ref[...] + gate_ref[0] * acc
    ms = jnp.mean(out * out, axis=-1, keepdims=True)
    o_ref[...] = out * lax.rsqrt(ms + EPS) * fg_ref[...]


def _out_projection(yh, ym, x2d, gate3, w_out, final_g, rb):
    rows = x2d.shape[0]
    mrows = gate3.shape[1]
    nblk = rows // rb
    return pl.pallas_call(
        _out_kernel,
        grid=(nblk,),
        in_specs=[pl.BlockSpec((rb, D_MODEL), lambda i: (i, 0)),
                  pl.BlockSpec((rb, D_MODEL), lambda i: (i, 0)),
                  pl.BlockSpec((rb, D_MODEL), lambda i: (i, 0)),
                  pl.BlockSpec((1, mrows, D_MODEL), lambda i: (i * gate3.shape[0] // nblk, 0, 0)),
                  pl.BlockSpec((2 * D_MODEL, D_MODEL), lambda i: (0, 0)),
                  pl.BlockSpec((1, D_MODEL), lambda i: (0, 0))],
        out_specs=pl.BlockSpec((rb, D_MODEL), lambda i: (i, 0)),
        out_shape=jax.ShapeDtypeStruct((rows, D_MODEL), F32),
        compiler_params=pltpu.CompilerParams(vmem_limit_bytes=VMEM_LIMIT),
        name="out_projection",
    )(yh, ym, x2d, gate3, w_out, final_g)


def _block_diag_heads(w):
    rows = w.reshape(MLSTM_HEADS, MLSTM_DH, QKV_BLOCK)
    tiled = jnp.tile(rows, (1, 1, MLSTM_DH // QKV_BLOCK))
    rg = lax.broadcasted_iota(jnp.int32, (MLSTM_DH, MLSTM_DH), 0) // QKV_BLOCK
    cg = lax.broadcasted_iota(jnp.int32, (MLSTM_DH, MLSTM_DH), 1) // QKV_BLOCK
    return jnp.where(rg == cg, tiled, 0.0).astype(BF16)


def _mlstm_state_operands(c0, n0, m0, conv0):
    B = c0.shape[0]
    tail0 = jnp.pad(conv0, ((0, 0), (SUBLANES - (CONV_WIDTH - 1), 0), (0, 0)))
    m0p = jnp.pad(m0, ((0, 0), (0, LANES - MLSTM_HEADS))).reshape(B, 1, LANES)
    return (c0, n0.reshape(B, 1, D_MODEL), m0p, tail0)


def _unpack_mlstm_state(B, n_new, m_new):
    return n_new.reshape(B, MLSTM_HEADS, MLSTM_DH), m_new.reshape(B, LANES)[:, :MLSTM_HEADS]


def kernel(x_prompt, x_sample, c_prompt, c_sample, state_hgrn, state_mlstm_C, state_mlstm_n, state_mlstm_m, state_mlstm_conv, w_ada, b_ada, norm_g, w_in, b_in, hgrn_lb_logits, hgrn_norm_g, mlstm_conv_w, mlstm_conv_b, mlstm_wq, mlstm_wk, mlstm_wv, mlstm_norm_g, mlstm_skip, w_out, final_g):
    assert w_in.shape == (1, D_MODEL, N_PROJ) and hgrn_lb_logits.shape == (2, D_MODEL)
    Bp, Tp, _ = x_prompt.shape
    Bs, Ts, _ = x_sample.shape
    assert Tp % PROMPT_BLOCK == 0 and Ts <= SAMPLE_ROWS and Ts >= CONV_WIDTH - 1

    mod = _modulation(jnp.concatenate([c_prompt, c_sample], axis=0), w_ada[0], b_ada[0].reshape(1, -1))
    shift, scale, gate = mod[:, :D_MODEL], mod[:, D_MODEL:2 * D_MODEL], mod[:, 2 * D_MODEL:]
    pad_cols = N_MLSTM_PAD - N_MLSTM
    ng = norm_g[0].reshape(1, -1)
    w_t = jnp.swapaxes(w_in[0], 0, 1).astype(BF16)
    w_h = w_t[:N_HGRN]
    w_m = jnp.pad(w_t[N_HGRN:], ((0, pad_cols), (0, 0)))
    b_h = b_in[0, :N_HGRN].reshape(1, -1)
    b_m = jnp.pad(b_in[0, N_HGRN:], (0, pad_cols)).reshape(1, -1)
    hg = hgrn_norm_g[0].reshape(1, -1)
    mw = (mlstm_conv_w[0], mlstm_conv_b[0].reshape(1, -1),
          _block_diag_heads(mlstm_wq[0]), _block_diag_heads(mlstm_wk[0] * KEY_SCALE),
          _block_diag_heads(jnp.swapaxes(mlstm_wk[0], -1, -2) * KEY_SCALE), _block_diag_heads(mlstm_wv[0]),
          mlstm_norm_g[0].reshape(1, -1), mlstm_skip[0].reshape(1, -1))
    wo = w_out[0].astype(BF16)
    fg = final_g.reshape(1, -1)

    xp2 = x_prompt.reshape(Bp * Tp, D_MODEL)
    per_seq = lambda a: a.reshape(-1, 1, D_MODEL)
    sc_p, sh_p = per_seq(scale[:Bp]), per_seq(shift[:Bp])
    yh_p, hg_p, h_next = _hgrn_fused(xp2, sc_p, sh_p, ng, w_h, b_h, hgrn_lb_logits, hg,
                                     B=Bp, T=Tp, TB=PROMPT_BLOCK, L=PROMPT_CHUNK)
    ym_p, c_p, n_p, m_p, tail_p = _mlstm_fused(xp2, sc_p, sh_p, ng, w_m, b_m, h_next, mw,
                                               B=Bp, T=Tp, TB=PROMPT_BLOCK, L=PROMPT_BLOCK)
    n_p, m_p = _unpack_mlstm_state(Bp, n_p, m_p)
    conv_p = tail_p[:, SUBLANES - (CONV_WIDTH - 1):]
    yp = _out_projection(yh_p, ym_p, xp2, per_seq(gate[:Bp]), wo, fg, OUT_PROJ_ROWS).reshape(Bp, Tp, D_MODEL)

    rows_s = Bs * SAMPLE_ROWS
    xs2 = jnp.pad(x_sample, ((0, 0), (0, SAMPLE_ROWS - Ts), (0, 0))).reshape(rows_s, D_MODEL)
    per_row = lambda a: jnp.repeat(a, SAMPLE_ROWS, axis=0).reshape(1, rows_s, D_MODEL)
    per_tok = lambda a: jnp.repeat(a, Ts, axis=0).reshape(1, Bs * Ts, D_MODEL)
    xs_tok = x_sample.reshape(Bs * Ts, D_MODEL)
    sc_s, sh_s = per_tok(scale[Bp:]), per_tok(shift[Bp:])
    proj_h = _in_projection(xs_tok, sc_s, sh_s, ng, w_h, b_h, Bs * Ts, SAMPLE_HGRN_COL_TILE)
    proj_m = _in_projection(xs_tok, sc_s, sh_s, ng, w_m, b_m, Bs * Ts, SAMPLE_MLSTM_COL_TILE)
    yh_s, hg_s = _hgrn(proj_h, state_hgrn[0], hgrn_lb_logits, hg, B=Bs, T=SAMPLE_ROWS, TB=SAMPLE_ROWS,
                       NB=SAMPLE_SEQS_PER_STEP, L=SAMPLE_ROWS, Tv=Ts)
    states_s = _mlstm_state_operands(state_mlstm_C[0], state_mlstm_n[0], state_mlstm_m[0], state_mlstm_conv[0])
    ym_s, c_s, n_s, m_s = _mlstm(proj_m, states_s, mw, B=Bs, T=SAMPLE_ROWS, TB=SAMPLE_ROWS,
                                 NB=SAMPLE_SEQS_PER_STEP, L=SAMPLE_ROWS, Tv=Ts)
    n_s, m_s = _unpack_mlstm_state(Bs, n_s, m_s)
    conv_s = proj_m.reshape(Bs, Ts, N_MLSTM_PAD)[:, Ts - (CONV_WIDTH - 1):, :D_MODEL]
    ys = _out_projection(yh_s, ym_s, xs2, per_row(gate[Bp:]), wo, fg, rows_s).reshape(Bs, SAMPLE_ROWS, D_MODEL)

    return (yp, ys[:, :Ts], hg_p[None], c_p[None], n_p[None], m_p[None], conv_p[None],
            hg_s[None], c_s[None], n_s[None], m_s[None], conv_s[None])
```

```python
import functools

import jax
import jax.numpy as jnp
from jax import lax
from jax.experimental import pallas as pl
from jax.experimental.pallas import tpu as pltpu

F32 = jnp.float32
BF16 = jnp.bfloat16

D_MODEL = 1024
HGRN_HEADS = 8
HGRN_D = 128
MLSTM_HEADS = 4
MLSTM_DH = 256
KEY_SCALE = MLSTM_DH ** -0.5
CONV_WIDTH = 4
QKV_BLOCK = 4
EPS = 1e-6
N_PROJ = 7176
N_HGRN = 4 * D_MODEL
N_MLSTM = N_PROJ - N_HGRN
N_MLSTM_PAD = 3 * D_MODEL + 128
GATE_TILE = 3 * D_MODEL // 128
MXU_WIDTH = 256
PROMPT_CHUNK = 128
PROMPT_BLOCK = 256
SAMPLE_ROWS = 8
SAMPLE_SEQS_PER_STEP = 8
OUT_PROJ_ROWS = 1024
MOD_COL_TILE = 512
SAMPLE_HGRN_COL_TILE = 2048
SAMPLE_MLSTM_COL_TILE = 640
SUBLANES = 8
LANES = 128
VMEM_LIMIT = 56 * 1024 * 1024
FAST_PATH_MIN_LOG_DECAY = -80.0


def _dot(a, b):
    return jnp.dot(a, b, preferred_element_type=F32)


def _dot_nt(a, b):
    return lax.dot_general(a, b, (((1,), (1,)), ((), ())), preferred_element_type=F32)


def _dot_tn(a, b):
    return lax.dot_general(a, b, (((0,), (0,)), ((), ())), preferred_element_type=F32)


def _sigmoid(x):
    return 0.5 * jnp.tanh(0.5 * x) + 0.5


def _silu(x):
    u = 0.5 * x
    return u * jnp.tanh(u) + u


def _levels(L):
    out, s = [], 1
    while s < L:
        out.append(s)
        s *= 2
    return out


def _seg_bcast(W, s, L, r):
    n = W.shape[1]
    if s == 1:
        return jnp.where((r & 1) != 0, pltpu.roll(W, 1, 0), W)
    if s == 2:
        m = r & 3
        return jnp.where(m == 0, pltpu.roll(W, L - 1, 0),
                         jnp.where(m == 1, W,
                                   jnp.where(m == 2, pltpu.roll(W, 1, 0), pltpu.roll(W, 2, 0))))
    pieces = [jnp.broadcast_to(W[b * 2 * s + s - 1:b * 2 * s + s, :], (2 * s, n))
              for b in range(L // (2 * s))]
    return pieces[0] if len(pieces) == 1 else jnp.concatenate(pieces, axis=0)


def _level_factors(x, L, r):
    W = x
    factors = []
    for s in _levels(L):
        Tb = _seg_bcast(W, s, L, r)
        sec = (r & s) != 0
        factors.append(jnp.exp(jnp.where(sec, W, Tb - W)))
        W = W + jnp.where(sec, Tb, 0.0)
    return factors


def _prefix8(x):
    sub = lax.broadcasted_iota(jnp.int32, (SUBLANES, 1), 0)
    y = x + jnp.where(sub >= 1, pltpu.roll(x, 1, 0), 0.0)
    y = y + jnp.where(sub >= 2, pltpu.roll(y, 2, 0), 0.0)
    return y + jnp.where(sub >= 4, pltpu.roll(y, 4, 0), 0.0)


def _chunk_cumsum(x, L):
    outs = []
    for c in range(x.shape[0] // L):
        total = None
        for g in range(L // SUBLANES):
            lo = c * L + g * SUBLANES
            p = _prefix8(x[lo:lo + SUBLANES, :])
            if total is not None:
                p = p + total
            outs.append(p)
            total = p[SUBLANES - 1:SUBLANES, :]
    return outs[0] if len(outs) == 1 else jnp.concatenate(outs, axis=0)


def _rows_bcast(x, row_in_chunk, L):
    n = x.shape[1]
    pieces = [jnp.broadcast_to(x[c * L + row_in_chunk:c * L + row_in_chunk + 1, :], (L, n))
              for c in range(x.shape[0] // L)]
    return pieces[0] if len(pieces) == 1 else jnp.concatenate(pieces, axis=0)


def _rms_mod_bf16(x, scale, shift, g):
    ms = jnp.mean(x * x, axis=-1, keepdims=True)
    return ((x * lax.rsqrt(ms + EPS)) * (g * (1.0 + scale)) + shift).astype(BF16)


def _store_valid_rows(y_ref, chunk, cols, y, L, Tv):
    y_ref[chunk * Tv:(chunk + 1) * Tv, cols] = y[:Tv].astype(y_ref.dtype)


def _y_out(B, T, NB, TB, L, Tv):
    rows_per_seq, dtype = (T, BF16) if Tv == L else (T // L * Tv, F32)
    return (pl.BlockSpec((NB * rows_per_seq * TB // T, D_MODEL), lambda i: (i, 0)),
            jax.ShapeDtypeStruct((B * rows_per_seq, D_MODEL), dtype))


def _project_into(dst_ref, h, w_ref, b_ref):
    n = w_ref.shape[1]
    bounds = list(range(0, n - n % D_MODEL, D_MODEL)) or [0]
    for k, lo in enumerate(bounds):
        hi = n if k == len(bounds) - 1 else lo + D_MODEL
        dst_ref[:, lo:hi] = _dot(h, w_ref[:, lo:hi]) + b_ref[:, lo:hi]


def _mod_kernel(c_ref, w_ref, b_ref, o_ref):
    c = c_ref[...]
    a = _silu(c)
    o_ref[...] = _dot(a.astype(BF16), w_ref[...].astype(BF16)) + b_ref[...]


def _modulation(c_all, w_ada, b_ada):
    m = c_all.shape[0]
    n = w_ada.shape[1]
    tn = MOD_COL_TILE
    return pl.pallas_call(
        _mod_kernel,
        grid=(n // tn,),
        in_specs=[pl.BlockSpec((m, D_MODEL), lambda j: (0, 0)),
                  pl.BlockSpec((D_MODEL, tn), lambda j: (0, j)),
                  pl.BlockSpec((1, tn), lambda j: (0, j))],
        out_specs=pl.BlockSpec((m, tn), lambda j: (0, j)),
        out_shape=jax.ShapeDtypeStruct((m, n), F32),
        name="modulation",
    )(c_all, w_ada, b_ada)


def _inproj_kernel(x_ref, scale_ref, shift_ref, g_ref, w_ref, b_ref, o_ref, h_scr):
    @pl.when(pl.program_id(1) == 0)
    def _():
        h_scr[...] = _rms_mod_bf16(x_ref[...], scale_ref[0], shift_ref[0], g_ref[...])

    o_ref[...] = _dot(h_scr[...], w_ref[...]) + b_ref[...]


def _in_projection(x2d, scale3, shift3, norm_g, w, b, rb, col_tile):
    rows = x2d.shape[0]
    mrows = scale3.shape[1]
    nblk = rows // rb
    n_cols = w.shape[1]
    return pl.pallas_call(
        _inproj_kernel,
        grid=(nblk, n_cols // col_tile),
        in_specs=[pl.BlockSpec((rb, D_MODEL), lambda i, j: (i, 0)),
                  pl.BlockSpec((1, mrows, D_MODEL), lambda i, j: (i * scale3.shape[0] // nblk, 0, 0)),
                  pl.BlockSpec((1, mrows, D_MODEL), lambda i, j: (i * shift3.shape[0] // nblk, 0, 0)),
                  pl.BlockSpec((1, D_MODEL), lambda i, j: (0, 0)),
                  pl.BlockSpec((D_MODEL, col_tile), lambda i, j: (0, j)),
                  pl.BlockSpec((1, col_tile), lambda i, j: (0, j))],
        out_specs=pl.BlockSpec((rb, col_tile), lambda i, j: (i, j)),
        out_shape=jax.ShapeDtypeStruct((rows, n_cols), F32),
        scratch_shapes=[pltpu.VMEM((rb, D_MODEL), BF16)],
        compiler_params=pltpu.CompilerParams(vmem_limit_bytes=VMEM_LIMIT),
        name="in_projection",
    )(x2d, scale3, shift3, norm_g, w, b)


class _SideWork:
    def __init__(self, pieces=()):
        self._pieces = list(pieces)

    def run(self, n=1):
        for _ in range(n):
            if self._pieces:
                self._pieces.pop(0)()

    def flush(self):
        self.run(len(self._pieces))


def _fused_projection_schedule(i, proj_a, proj_b, x0_ref, sc0_ref, sh0_ref, ng_ref, w_ref, b_ref, next_h, step):
    n = w_ref.shape[1]

    @pl.when(i == 0)
    def _():
        _project_into(proj_a, _rms_mod_bf16(x0_ref[...], sc0_ref[0], sh0_ref[0], ng_ref[...]), w_ref, b_ref)

    def run(cur, nxt):
        h = []

        def norm_piece():
            h.append(next_h())

        def tile_piece(lo):
            hi = min(lo + MXU_WIDTH, n)

            def piece():
                nxt[:, lo:hi] = _dot(h[0], w_ref[:, lo:hi]) + b_ref[:, lo:hi]
            return piece

        step(cur, _SideWork([norm_piece] + [tile_piece(lo) for lo in range(0, n, MXU_WIDTH)]))

    parity = lax.rem(i, 2)
    pl.when(parity == 0)(lambda: run(proj_a, proj_b))
    pl.when(parity == 1)(lambda: run(proj_b, proj_a))


def _fused_projection_specs(n_blocks, nt, tb, n_cols, next_h_given):
    def nxt(i):
        return jnp.minimum(i + 1, n_blocks - 1)
    vec = (1, 1, D_MODEL)
    specs = [pl.BlockSpec((tb, D_MODEL), lambda i: (0, 0)),
             pl.BlockSpec(vec, lambda i: (0, 0, 0)), pl.BlockSpec(vec, lambda i: (0, 0, 0)),
             pl.BlockSpec((1, D_MODEL), lambda i: (0, 0)),
             pl.BlockSpec((D_MODEL, n_cols), lambda i: (0, 0)),
             pl.BlockSpec((1, n_cols), lambda i: (0, 0))]
    if next_h_given:
        return specs + [pl.BlockSpec((tb, D_MODEL), lambda i: (i, 0))]
    return specs + [pl.BlockSpec((tb, D_MODEL), lambda i: (nxt(i), 0)),
                    pl.BlockSpec(vec, lambda i: (nxt(i) // nt, 0, 0)),
                    pl.BlockSpec(vec, lambda i: (nxt(i) // nt, 0, 0))]


def _hgrn_step(load, t, nt, s0_ref, lbl_ref, g_ref, y_ref, s_ref, st_scr, a_scr,
               *, NB, TB, L, Tv, side_work=None):
    H = HGRN_HEADS
    rb = NB * TB
    spn = TB // L
    nseg = rb // L

    @pl.when(t == 0)
    def _():
        for nb in range(NB):
            for h in range(H):
                st_scr[nb, h] = jnp.zeros((HGRN_D, HGRN_D), F32) if s0_ref is None else s0_ref[nb, h].T

    sw = side_work if side_work is not None else _SideWork()
    sw.run(1)

    lg = lbl_ref[...]
    mx = jnp.max(lg, axis=0, keepdims=True)
    e = jnp.exp(lg - mx)
    lb = e[0:1, :] / jnp.sum(e, axis=0, keepdims=True)
    g_norm = g_ref[...]

    hq = load(0)
    sw.run(2)
    c1 = 0.5 * (1.0 - lb)
    p = c1 * jnp.tanh(0.5 * load(1))
    sw.run(2)
    logf = jnp.log((lb + c1) + p)
    sw.run(2)
    kk = c1 - p
    sw.run(2)
    if Tv < L:
        valid = (lax.broadcasted_iota(jnp.int32, (rb, 1), 0) & (L - 1)) < Tv
        logf = jnp.where(valid, logf, 0.0)
        kk = jnp.where(valid, kk, 0.0)
    G = _chunk_cumsum(logf, L)
    sw.run(3)
    GL = [G[s * L + L - 1:s * L + L, :] for s in range(nseg)]
    fast_ok = jnp.min(functools.reduce(jnp.minimum, GL)) >= FAST_PATH_MIN_LOG_DECAY

    ti = lax.broadcasted_iota(jnp.int32, (L, L), 0)
    ji = lax.broadcasted_iota(jnp.int32, (L, L), 1)

    @pl.when(fast_ok)
    def _():
        d = G - _rows_bcast(G, L // 2 - 1, L)
        qt = (hq * jnp.exp(d)).astype(BF16)
        kt = (kk * jnp.exp(-d)).astype(BF16)
        causal = ji <= ti
        for s in range(nseg):
            rows = slice(s * L, (s + 1) * L)
            for h in range(H):
                hs = slice(h * HGRN_D, (h + 1) * HGRN_D)
                a_scr[s * H + h] = jnp.where(causal, _dot_nt(qt[rows, hs], kt[rows, hs]), 0.0)

    @pl.when(jnp.logical_not(fast_ok))
    def _():
        r = lax.broadcasted_iota(jnp.int32, (L, 1), 0)
        xo = ti ^ ji
        diag_mask = ti == ji
        lvl_masks = [(ji < ti) & (xo >= s) & (xo < 2 * s) for s in _levels(L)]
        for s in range(nseg):
            rows = slice(s * L, (s + 1) * L)
            factors = _level_factors(logf[rows, :], L, r)
            for h in range(H):
                hs = slice(h * HGRN_D, (h + 1) * HGRN_D)
                qh = hq[rows, hs]
                kh = kk[rows, hs]
                A = jnp.where(diag_mask, _dot_nt(qh.astype(BF16), kh.astype(BF16)), 0.0)
                for lvl in range(len(factors)):
                    E = factors[lvl][:, hs]
                    A = A + jnp.where(lvl_masks[lvl],
                                      _dot_nt((qh * E).astype(BF16), (kh * E).astype(BF16)), 0.0)
                a_scr[s * H + h] = A

    qg = (hq * jnp.exp(G)).astype(BF16)
    sw.run(2)
    GLb = GL[0] if nseg == 1 and L == rb else jnp.concatenate(
        [jnp.broadcast_to(gl, (L, D_MODEL)) for gl in GL], axis=0)
    kd = (kk * jnp.exp(GLb - G)).astype(BF16)
    sw.run(2)
    hv = load(2)
    vb = hv.astype(BF16)
    hz = load(3)
    zgate = _silu(hz)
    sw.flush()
    merged = L % LANES == 0
    for nb in range(NB):
        st = [st_scr[nb, h] for h in range(H)]
        for c in range(spn):
            s = nb * spn + c
            rows = slice(s * L, (s + 1) * L)
            dS = jnp.exp(GL[s])
            for h in range(H):
                hs = slice(h * HGRN_D, (h + 1) * HGRN_D)
                A = a_scr[s * H + h].astype(BF16)
                if merged:
                    vT = hv[rows, hs].T.astype(BF16)
                    o = _dot_nt(jnp.concatenate([qg[rows, hs], A], axis=1),
                                jnp.concatenate([st[h].astype(BF16), vT], axis=1))
                    st[h] = st[h] * dS[:, hs] + _dot(vT, kd[rows, hs])
                else:
                    o = _dot(A, vb[rows, hs]) + _dot_nt(qg[rows, hs], st[h].astype(BF16))
                    st[h] = st[h] * dS[:, hs] + _dot_tn(vb[rows, hs], kd[rows, hs])
                ms = jnp.mean(o * o, axis=-1, keepdims=True)
                y = o * lax.rsqrt(ms + EPS) * g_norm[:, hs] * zgate[rows, hs]
                _store_valid_rows(y_ref, s, hs, y, L, Tv)
        for h in range(H):
            st_scr[nb, h] = st[h]

    @pl.when(t == nt - 1)
    def _():
        for nb in range(NB):
            for h in range(H):
                s_ref[nb, h] = st_scr[nb, h].T


def _pad_sequences(a, NB, Tv, L):
    if Tv == L:
        return a
    zeros = jnp.zeros((L - Tv, a.shape[1]), a.dtype)
    return jnp.concatenate([p for nb in range(NB) for p in (a[nb * Tv:(nb + 1) * Tv, :], zeros)], axis=0)


def _hgrn_kernel(q_ref, f_ref, i_ref, z_ref, s0_ref, lbl_ref, g_ref, y_ref, s_ref, st_scr, a_scr,
                 *, nt, **static):
    cols = (q_ref, f_ref, i_ref, z_ref)
    _hgrn_step(lambda c: _pad_sequences(cols[c][...], static["NB"], static["Tv"], static["L"]),
               lax.rem(pl.program_id(0), nt), nt,
               s0_ref, lbl_ref, g_ref, y_ref, s_ref, st_scr, a_scr, **static)


def _hgrn_fused_kernel(x0_ref, sc0_ref, sh0_ref, ng_ref, w_ref, b_ref, xn_ref, scn_ref, shn_ref,
                       lbl_ref, g_ref, y_ref, s_ref, hn_ref, st_scr, a_scr, proj_a, proj_b,
                       *, nt, **static):
    i = pl.program_id(0)

    def next_h():
        h = _rms_mod_bf16(xn_ref[...], scn_ref[0], shn_ref[0], ng_ref[...])
        hn_ref[...] = h
        return h

    def step(cur, side_work):
        _hgrn_step(lambda c: cur[:, c * D_MODEL:(c + 1) * D_MODEL], lax.rem(i, nt), nt,
                   None, lbl_ref, g_ref, y_ref, s_ref, st_scr, a_scr, side_work=side_work, **static)

    _fused_projection_schedule(i, proj_a, proj_b, x0_ref, sc0_ref, sh0_ref, ng_ref, w_ref, b_ref, next_h, step)


def _hgrn_specs(B, T, TB, NB, L, Tv):
    nt = T // TB
    rb = NB * TB
    s_spec = pl.BlockSpec((NB, HGRN_HEADS, HGRN_D, HGRN_D), lambda i: (i // nt, 0, 0, 0))
    in_tail = [pl.BlockSpec((2, D_MODEL), lambda i: (0, 0)), pl.BlockSpec((1, D_MODEL), lambda i: (0, 0))]
    y_spec, y_shape = _y_out(B, T, NB, TB, L, Tv)
    out_specs = [y_spec, s_spec]
    out_shape = [y_shape, jax.ShapeDtypeStruct((B, HGRN_HEADS, HGRN_D, HGRN_D), F32)]
    scratch = [pltpu.VMEM((NB, HGRN_HEADS, HGRN_D, HGRN_D), F32),
               pltpu.VMEM((rb // L * HGRN_HEADS, L, L), F32)]
    return nt, rb, s_spec, in_tail, out_specs, out_shape, scratch


def _hgrn(proj, s0, lb_logits, norm_g, *, B, T, TB, NB, L, Tv):
    nt, rb, s_spec, in_tail, out_specs, out_shape, scratch = _hgrn_specs(B, T, TB, NB, L, Tv)
    assert NB == 1 or nt == 1
    assert TB == L
    cols = [pl.BlockSpec((NB * Tv, D_MODEL), lambda i, c=c: (i, c)) for c in range(4)]
    return pl.pallas_call(
        functools.partial(_hgrn_kernel, nt=nt, NB=NB, TB=TB, L=L, Tv=Tv),
        grid=(B // NB * nt,),
        in_specs=cols + [s_spec] + in_tail, out_specs=out_specs, out_shape=out_shape, scratch_shapes=scratch,
        compiler_params=pltpu.CompilerParams(vmem_limit_bytes=VMEM_LIMIT),
        name="hgrn2",
    )(proj, proj, proj, proj, s0, lb_logits, norm_g)


def _hgrn_fused(x2d, scale3, shift3, norm_g, w, b, lb_logits, hgrn_norm_g, *, B, T, TB, L):
    nt, rb, _, in_tail, out_specs, out_shape, scratch = _hgrn_specs(B, T, TB, 1, L, L)
    n_blocks = B * nt
    return pl.pallas_call(
        functools.partial(_hgrn_fused_kernel, nt=nt, NB=1, TB=TB, L=L, Tv=L),
        grid=(n_blocks,),
        in_specs=_fused_projection_specs(n_blocks, nt, TB, N_HGRN, False) + in_tail,
        out_specs=out_specs + [pl.BlockSpec((TB, D_MODEL), lambda i: (i, 0))],
        out_shape=out_shape + [jax.ShapeDtypeStruct((B * T, D_MODEL), BF16)],
        scratch_shapes=scratch + [pltpu.VMEM((TB, N_HGRN), F32), pltpu.VMEM((TB, N_HGRN), F32)],
        compiler_params=pltpu.CompilerParams(vmem_limit_bytes=VMEM_LIMIT),
        name="hgrn2_fused",
    )(x2d, scale3, shift3, norm_g, w, b, x2d, scale3, shift3, lb_logits, hgrn_norm_g)


def _mlstm_step(load, t, nt, c0_ref, n0_ref, m0_ref, tail0_ref,
                cw_ref, cb_ref, wq_ref, wk_ref, wkt_ref, wv_ref, ng_ref, skip_ref,
                y_ref, c_ref, n_ref, m_ref, tail_scr, conv_ref,
                *, NB, TB, L, Tv, side_work=None):
    H = MLSTM_HEADS
    rb = NB * TB
    spn = TB // L
    if nt == 1:
        c_in, n_in, m_in, tail_in = c0_ref, n0_ref, m0_ref, tail0_ref
    else:
        c_in, n_in, m_in, tail_in = c_ref, n_ref, m_ref, tail_scr

        @pl.when(t == 0)
        def _():
            for dst, src in ((c_ref, c0_ref), (n_ref, n0_ref), (m_ref, m0_ref), (tail_scr, tail0_ref)):
                dst[...] = jnp.zeros(dst.shape, F32) if src is None else src[...]

    sw = side_work if side_work is not None else _SideWork()
    sw.run(1)

    cw = cw_ref[...]
    ng = ng_ref[...]
    skip = skip_ref[...]
    lane = lax.broadcasted_iota(jnp.int32, (1, LANES), 1)
    ti = lax.broadcasted_iota(jnp.int32, (L, L), 0)
    ji = lax.broadcasted_iota(jnp.int32, (L, L), 1)
    causal = ji <= ti

    mu = load(0)
    xcs = []
    for nb in range(NB):
        u_nb = mu[nb * TB:(nb + 1) * TB, :]
        ext = jnp.concatenate([tail_in[nb], u_nb], axis=0)
        conv = cb_ref[...] + cw[CONV_WIDTH - 1:CONV_WIDTH, :] * u_nb
        for i in range(1, CONV_WIDTH):
            conv = conv + cw[CONV_WIDTH - 1 - i:CONV_WIDTH - i, :] * pltpu.roll(ext, i, 0)[SUBLANES:, :]
            sw.run(1)
        xcs.append(_silu(conv))
        sw.run(1)
        if nt > 1:
            tail_scr[nb] = u_nb[TB - SUBLANES:, :]
    if conv_ref is not None:
        @pl.when(t == nt - 1)
        def _():
            for nb in range(NB):
                conv_ref[nb] = mu[(nb + 1) * TB - SUBLANES:(nb + 1) * TB, :]
    xc = xcs[0] if NB == 1 else jnp.concatenate(xcs, axis=0)
    xcb = xc.astype(BF16)
    mub = mu.astype(BF16)
    mz = load(1)
    zgate = _silu(mz)
    sw.run(2)
    ogate = _sigmoid(load(2))
    sw.run(2)

    gt = load(3)
    lf = jnp.minimum(gt, 0.0) - jnp.log1p(jnp.exp(-jnp.abs(gt)))
    ig = gt
    if Tv < L:
        valid = (lax.broadcasted_iota(jnp.int32, (rb, 1), 0) & (L - 1)) < Tv
        lf = jnp.where(valid, lf, 0.0)
        ig = jnp.where(valid, ig, -jnp.inf)
    bcum = _chunk_cumsum(lf, L)
    sw.run(1)
    comb = jnp.where(lane < H, ig, bcum)
    rpad = -rb % LANES
    if rpad:
        comb = jnp.concatenate([comb, jnp.zeros((rpad, LANES), F32)], axis=0)
    combT = comb.T

    assert spn == 1
    heads = {}

    def project_head(h):
        hs = slice(h * MLSTM_DH, (h + 1) * MLSTM_DH)
        q_all = _dot(xcb[:, hs], wq_ref[h])
        kb_all = _dot(xcb[:, hs], wk_ref[h]).astype(BF16)
        kT_all = _dot_nt(wkt_ref[h], xcb[:, hs])
        vb_all = _dot(mub[:, hs], wv_ref[h]).astype(BF16)
        heads[h] = (hs, q_all, q_all.astype(BF16), kb_all, kT_all, vb_all)
        sw.run(1)

    pairs = [(h, nb) for h in range(H) for nb in range(NB)]
    groups = [pairs] if NB > 1 else [[p] for p in pairs]
    rows_of = lambda nb: slice(nb * L, (nb + 1) * L)

    def run_group(pairs):
        for h in sorted({h for h, _ in pairs}):
            project_head(h)

        gates = []
        for h, nb in pairs:
            rows = rows_of(nb)
            bcol = bcum[rows, H + h:H + h + 1]
            irow = combT[h:h + 1, rows]
            brow = combT[H + h:H + h + 1, rows]
            logD = jnp.where(causal, (bcol - brow) + irow, -jnp.inf)
            m_intra = jnp.max(logD, axis=-1, keepdims=True)
            gates.append((bcol, irow, brow, m_intra, jnp.exp(logD - m_intra)))

        scores = []
        for (h, nb), g in zip(pairs, gates):
            _, _, qb_all, kb_all, _, _ = heads[h]
            rows = rows_of(nb)
            scores.append(_dot_nt(qb_all[rows], kb_all[rows]) * g[4])

        intra = []
        for (h, nb), sc in zip(pairs, scores):
            vb_all = heads[h][5]
            intra.append((jnp.sum(sc, axis=-1, keepdims=True), _dot(sc.astype(BF16), vb_all[rows_of(nb)])))

        writes = []
        for (h, nb), g in zip(pairs, gates):
            _, _, _, kb_all, kT_all, vb_all = heads[h]
            rows = rows_of(nb)
            bcol, irow, brow, m_intra, _ = g
            m_loc = m_intra[L - 1:L, :]
            b_last = bcol[L - 1:L, :]
            wrow = jnp.exp((b_last - brow) + irow - m_loc)
            kw = (kT_all[:, rows] * wrow).astype(BF16)
            if len(pairs) == 1:
                kw = _dot(kw, vb_all[rows])
            ks = _dot(jnp.broadcast_to(wrow, (SUBLANES, L)).astype(BF16), kb_all[rows])[0:1, :]
            writes.append((m_loc, b_last, kw, ks))

        outs = []
        updates = []
        for (h, nb), g, (rs, sv), (m_loc, b_last, kw, ks) in zip(pairs, gates, intra, writes):
            hs, q_all, qb_all, _, _, _ = heads[h]
            rows = rows_of(nb)
            bcol, _, _, m_intra, _ = g
            C = c_in[nb, h]
            nh = n_in[nb, :, hs]
            m_prev = m_in[nb][:, h:h + 1]
            m_inter = bcol + m_prev
            m_t = jnp.maximum(m_inter, m_intra)
            inter = jnp.exp(m_inter - m_t)
            scl = jnp.exp(m_intra - m_t)
            den = inter * jnp.sum(q_all[rows] * nh, axis=-1, keepdims=True) + scl * rs
            rden = 1.0 / jnp.maximum(jnp.abs(den), jnp.exp(-m_t))
            hh = (inter * rden) * _dot(qb_all[rows], C.astype(BF16)) + (scl * rden) * sv
            m_new = m_t[L - 1:L, :]
            dec = jnp.exp(b_last + m_prev - m_new)
            scu = jnp.exp(m_loc - m_new)
            n_ref[nb, :, hs] = dec * nh + scu * ks
            outs.append((hh, m_new))
            if len(pairs) > 1:
                updates.append((dec, scu, kw))
            else:
                c_ref[nb, h] = dec * C + scu * kw

        for (h, nb), (dec, scu, kwT) in zip(pairs, updates):
            c_ref[nb, h] = dec * c_in[nb, h] + scu * _dot(kwT, heads[h][5][rows_of(nb)])

        for (h, nb), (hh, _) in zip(pairs, outs):
            hs = heads[h][0]
            rows = rows_of(nb)
            hm = ogate[rows, hs] * hh
            ms = jnp.mean(hm * hm, axis=-1, keepdims=True)
            y = (hm * lax.rsqrt(ms + EPS) * ng[:, hs] + skip[:, hs] * xc[rows, hs]) * zgate[rows, hs]
            _store_valid_rows(y_ref, nb, hs, y, L, Tv)
        return outs

    outs = [o for grp in groups for o in run_group(grp)]

    for nb in range(NB):
        m_row = m_in[nb]
        for h in range(H):
            m_row = jnp.where(lane == h, outs[h * NB + nb][1], m_row)
        m_ref[nb] = m_row
    sw.flush()


N_MLSTM_WEIGHTS = 8


def _mlstm_kernel(u_ref, z_ref, o_ref, gate_ref, *rest, nt, **static):
    cols = (u_ref, z_ref, o_ref, gate_ref)
    ins, (y_ref, c_ref, n_ref, m_ref, tail_scr) = rest[:4 + N_MLSTM_WEIGHTS], rest[4 + N_MLSTM_WEIGHTS:]
    _mlstm_step(lambda c: _pad_sequences(cols[c][...], static["NB"], static["Tv"], static["L"]),
                lax.rem(pl.program_id(0), nt), nt, *ins,
                y_ref, c_ref, n_ref, m_ref, tail_scr, None, **static)


def _mlstm_fused_kernel(x0_ref, sc0_ref, sh0_ref, ng_ref, w_ref, b_ref, hn_ref, *rest, nt, **static):
    i = pl.program_id(0)
    weights = rest[:N_MLSTM_WEIGHTS]
    y_ref, c_ref, n_ref, m_ref, conv_ref, tail_scr, proj_a, proj_b = rest[N_MLSTM_WEIGHTS:]
    widths = (D_MODEL, D_MODEL, D_MODEL, LANES)

    def step(cur, side_work):
        _mlstm_step(lambda c: cur[:, c * D_MODEL:c * D_MODEL + widths[c]], lax.rem(i, nt), nt,
                    None, None, None, None, *weights,
                    y_ref, c_ref, n_ref, m_ref, tail_scr, conv_ref, side_work=side_work, **static)

    _fused_projection_schedule(i, proj_a, proj_b, x0_ref, sc0_ref, sh0_ref, ng_ref, w_ref, b_ref,
                               lambda: hn_ref[...], step)


def _mlstm_specs(B, T, TB, NB, L, Tv):
    nt = T // TB
    rb = NB * TB

    def full(shape):
        return pl.BlockSpec(shape, lambda i: (0,) * len(shape))

    c_spec = pl.BlockSpec((NB, MLSTM_HEADS, MLSTM_DH, MLSTM_DH), lambda i: (i // nt, 0, 0, 0))
    n_spec = pl.BlockSpec((NB, 1, D_MODEL), lambda i: (i // nt, 0, 0))
    m_spec = pl.BlockSpec((NB, 1, LANES), lambda i: (i // nt, 0, 0))
    t_spec = pl.BlockSpec((NB, SUBLANES, D_MODEL), lambda i: (i // nt, 0, 0))
    head_w = full((MLSTM_HEADS, MLSTM_DH, MLSTM_DH))
    state_specs = [c_spec, n_spec, m_spec, t_spec]
    weight_specs = [full((CONV_WIDTH, D_MODEL)), full((1, D_MODEL)),
                    head_w, head_w, head_w, head_w, full((1, D_MODEL)), full((1, D_MODEL))]
    y_spec, y_shape = _y_out(B, T, NB, TB, L, Tv)
    out_specs = [y_spec, c_spec, n_spec, m_spec]
    out_shape = [y_shape,
                 jax.ShapeDtypeStruct((B, MLSTM_HEADS, MLSTM_DH, MLSTM_DH), F32),
                 jax.ShapeDtypeStruct((B, 1, D_MODEL), F32),
                 jax.ShapeDtypeStruct((B, 1, LANES), F32)]
    scratch = [pltpu.VMEM((NB, SUBLANES, D_MODEL), F32)]
    return nt, rb, state_specs, weight_specs, out_specs, out_shape, scratch, t_spec


def _mlstm(proj, states, weights, *, B, T, TB, NB, L, Tv):
    nt, rb, state_specs, weight_specs, out_specs, out_shape, scratch, _ = _mlstm_specs(B, T, TB, NB, L, Tv)
    assert NB == 1 or nt == 1
    assert TB == L
    cols = [pl.BlockSpec((NB * Tv, D_MODEL), lambda i, c=c: (i, c)) for c in range(3)]
    cols.append(pl.BlockSpec((NB * Tv, LANES), lambda i: (i, GATE_TILE)))
    return pl.pallas_call(
        functools.partial(_mlstm_kernel, nt=nt, NB=NB, TB=TB, L=L, Tv=Tv),
        grid=(B // NB * nt,),
        in_specs=cols + state_specs + weight_specs,
        out_specs=out_specs, out_shape=out_shape, scratch_shapes=scratch,
        compiler_params=pltpu.CompilerParams(vmem_limit_bytes=VMEM_LIMIT),
        name="mlstm",
    )(proj, proj, proj, proj, *states, *weights)


def _mlstm_fused(x2d, scale3, shift3, norm_g, w, b, h_next, weights, *, B, T, TB, L):
    nt, rb, _, weight_specs, out_specs, out_shape, scratch, t_spec = _mlstm_specs(B, T, TB, 1, L, L)
    n_blocks = B * nt
    return pl.pallas_call(
        functools.partial(_mlstm_fused_kernel, nt=nt, NB=1, TB=TB, L=L, Tv=L),
        grid=(n_blocks,),
        in_specs=_fused_projection_specs(n_blocks, nt, TB, N_MLSTM_PAD, True) + weight_specs,
        out_specs=out_specs + [t_spec],
        out_shape=out_shape + [jax.ShapeDtypeStruct((B, SUBLANES, D_MODEL), F32)],
        scratch_shapes=scratch + [pltpu.VMEM((TB, N_MLSTM_PAD), F32), pltpu.VMEM((TB, N_MLSTM_PAD), F32)],
        compiler_params=pltpu.CompilerParams(vmem_limit_bytes=VMEM_LIMIT),
        name="mlstm_fused",
    )(x2d, scale3, shift3, norm_g, w, b, h_next, *weights)


def _out_kernel(yh_ref, ym_ref, x_ref, gate_ref, w_ref, fg_ref, o_ref):
    acc = _dot(jnp.concatenate([yh_ref[...], ym_ref[...]], axis=1).astype(BF16), w_ref[...])
    out = x_ref[...] + gate_ref[0] * acc
    ms = jnp.mean(out * out, axis=-1, keepdims=True)
    o_ref[...] = out * lax.rsqrt(ms + EPS) * fg_ref[...]


def _out_projection(yh, ym, x2d, gate3, w_out, final_g, rb):
    rows = x2d.shape[0]
    mrows = gate3.shape[1]
    nblk = rows // rb
    return pl.pallas_call(
        _out_kernel,
        grid=(nblk,),
        in_specs=[pl.BlockSpec((rb, D_MODEL), lambda i: (i, 0)),
                  pl.BlockSpec((rb, D_MODEL), lambda i: (i, 0)),
                  pl.BlockSpec((rb, D_MODEL), lambda i: (i, 0)),
                  pl.BlockSpec((1, mrows, D_MODEL), lambda i: (i * gate3.shape[0] // nblk, 0, 0)),
                  pl.BlockSpec((2 * D_MODEL, D_MODEL), lambda i: (0, 0)),
                  pl.BlockSpec((1, D_MODEL), lambda i: (0, 0))],
        out_specs=pl.BlockSpec((rb, D_MODEL), lambda i: (i, 0)),
        out_shape=jax.ShapeDtypeStruct((rows, D_MODEL), F32),
        compiler_params=pltpu.CompilerParams(vmem_limit_bytes=VMEM_LIMIT),
        name="out_projection",
    )(yh, ym, x2d, gate3, w_out, final_g)


def _block_diag_heads(w):
    rows = w.reshape(MLSTM_HEADS, MLSTM_DH, QKV_BLOCK)
    tiled = jnp.tile(rows, (1, 1, MLSTM_DH // QKV_BLOCK))
    rg = lax.broadcasted_iota(jnp.int32, (MLSTM_DH, MLSTM_DH), 0) // QKV_BLOCK
    cg = lax.broadcasted_iota(jnp.int32, (MLSTM_DH, MLSTM_DH), 1) // QKV_BLOCK
    return jnp.where(rg == cg, tiled, 0.0).astype(BF16)


def _mlstm_state_operands(c0, n0, m0, conv0):
    B = c0.shape[0]
    tail0 = jnp.pad(conv0, ((0, 0), (SUBLANES - (CONV_WIDTH - 1), 0), (0, 0)))
    m0p = jnp.pad(m0, ((0, 0), (0, LANES - MLSTM_HEADS))).reshape(B, 1, LANES)
    return (c0, n0.reshape(B, 1, D_MODEL), m0p, tail0)


def _unpack_mlstm_state(B, n_new, m_new):
    return n_new.reshape(B, MLSTM_HEADS, MLSTM_DH), m_new.reshape(B, LANES)[:, :MLSTM_HEADS]


def kernel(x_prompt, x_sample, c_prompt, c_sample, state_hgrn, state_mlstm_C, state_mlstm_n, state_mlstm_m, state_mlstm_conv, w_ada, b_ada, norm_g, w_in, b_in, hgrn_lb_logits, hgrn_norm_g, mlstm_conv_w, mlstm_conv_b, mlstm_wq, mlstm_wk, mlstm_wv, mlstm_norm_g, mlstm_skip, w_out, final_g):
    assert w_in.shape == (1, D_MODEL, N_PROJ) and hgrn_lb_logits.shape == (2, D_MODEL)
    Bp, Tp, _ = x_prompt.shape
    Bs, Ts, _ = x_sample.shape
    assert Tp % PROMPT_BLOCK == 0 and Ts <= SAMPLE_ROWS and Ts >= CONV_WIDTH - 1

    mod = _modulation(jnp.concatenate([c_prompt, c_sample], axis=0), w_ada[0], b_ada[0].reshape(1, -1))
    shift, scale, gate = mod[:, :D_MODEL], mod[:, D_MODEL:2 * D_MODEL], mod[:, 2 * D_MODEL:]
    pad_cols = N_MLSTM_PAD - N_MLSTM
    ng = norm_g[0].reshape(1, -1)
    w_h = w_in[0, :, :N_HGRN].astype(BF16)
    w_m = jnp.pad(w_in[0, :, N_HGRN:].astype(BF16), ((0, 0), (0, pad_cols)))
    b_h = b_in[0, :N_HGRN].reshape(1, -1)
    b_m = jnp.pad(b_in[0, N_HGRN:], (0, pad_cols)).reshape(1, -1)
    hg = hgrn_norm_g[0].reshape(1, -1)
    mw = (mlstm_conv_w[0], mlstm_conv_b[0].reshape(1, -1),
          _block_diag_heads(mlstm_wq[0]), _block_diag_heads(mlstm_wk[0] * KEY_SCALE),
          _block_diag_heads(jnp.swapaxes(mlstm_wk[0], -1, -2) * KEY_SCALE), _block_diag_heads(mlstm_wv[0]),
          mlstm_norm_g[0].reshape(1, -1), mlstm_skip[0].reshape(1, -1))
    wo = w_out[0].astype(BF16)
    fg = final_g.reshape(1, -1)

    xp2 = x_prompt.reshape(Bp * Tp, D_MODEL)
    per_seq = lambda a: a.reshape(-1, 1, D_MODEL)
    sc_p, sh_p = per_seq(scale[:Bp]), per_seq(shift[:Bp])
    yh_p, hg_p, h_next = _hgrn_fused(xp2, sc_p, sh_p, ng, w_h, b_h, hgrn_lb_logits, hg,
                                     B=Bp, T=Tp, TB=PROMPT_BLOCK, L=PROMPT_CHUNK)
    ym_p, c_p, n_p, m_p, tail_p = _mlstm_fused(xp2, sc_p, sh_p, ng, w_m, b_m, h_next, mw,
                                               B=Bp, T=Tp, TB=PROMPT_BLOCK, L=PROMPT_BLOCK)
    n_p, m_p = _unpack_mlstm_state(Bp, n_p, m_p)
    conv_p = tail_p[:, SUBLANES - (CONV_WIDTH - 1):]
    yp = _out_projection(yh_p, ym_p, xp2, per_seq(gate[:Bp]), wo, fg, OUT_PROJ_ROWS).reshape(Bp, Tp, D_MODEL)

    per_tok = lambda a: jnp.repeat(a, Ts, axis=0).reshape(1, Bs * Ts, D_MODEL)
    xs_tok = x_sample.reshape(Bs * Ts, D_MODEL)
    sc_s, sh_s = per_tok(scale[Bp:]), per_tok(shift[Bp:])
    proj_h = _in_projection(xs_tok, sc_s, sh_s, ng, w_h, b_h, Bs * Ts, SAMPLE_HGRN_COL_TILE)
    proj_m = _in_projection(xs_tok, sc_s, sh_s, ng, w_m, b_m, Bs * Ts, SAMPLE_MLSTM_COL_TILE)
    yh_s, hg_s = _hgrn(proj_h, state_hgrn[0], hgrn_lb_logits, hg, B=Bs, T=SAMPLE_ROWS, TB=SAMPLE_ROWS,
                       NB=SAMPLE_SEQS_PER_STEP, L=SAMPLE_ROWS, Tv=Ts)
    states_s = _mlstm_state_operands(state_mlstm_C[0], state_mlstm_n[0], state_mlstm_m[0], state_mlstm_conv[0])
    ym_s, c_s, n_s, m_s = _mlstm(proj_m, states_s, mw, B=Bs, T=SAMPLE_ROWS, TB=SAMPLE_ROWS,
                                 NB=SAMPLE_SEQS_PER_STEP, L=SAMPLE_ROWS, Tv=Ts)
    n_s, m_s = _unpack_mlstm_state(Bs, n_s, m_s)
    conv_s = proj_m.reshape(Bs, Ts, N_MLSTM_PAD)[:, Ts - (CONV_WIDTH - 1):, :D_MODEL]
    ys = _out_projection(yh_s, ym_s, xs_tok, per_tok(gate[Bp:]), wo, fg, Bs * Ts).reshape(Bs, Ts, D_MODEL)

    return (yp, ys, hg_p[None], c_p[None], n_p[None], m_p[None], conv_p[None],
            hg_s[None], c_s[None], n_s[None], m_s[None], conv_s[None])
```

```python
import functools

import jax
import jax.numpy as jnp
from jax import lax
from jax.experimental import pallas as pl
from jax.experimental.pallas import tpu as pltpu

F32 = jnp.float32
BF16 = jnp.bfloat16

D_MODEL = 1024
HGRN_HEADS = 8
HGRN_D = 128
MLSTM_HEADS = 4
MLSTM_DH = 256
KEY_SCALE = MLSTM_DH ** -0.5
CONV_WIDTH = 4
QKV_BLOCK = 4
EPS = 1e-6
N_PROJ = 7176
N_HGRN = 4 * D_MODEL
N_MLSTM = N_PROJ - N_HGRN
N_MLSTM_PAD = 3 * D_MODEL + 128
GATE_TILE = 3 * D_MODEL // 128
MXU_WIDTH = 256
PROMPT_CHUNK = 128
PROMPT_BLOCK = 256
SAMPLE_ROWS = 8
SAMPLE_SEQS_PER_STEP = 8
OUT_PROJ_ROWS = 1024
MOD_COL_TILE = 512
SAMPLE_HGRN_COL_TILE = 2048
SAMPLE_MLSTM_COL_TILE = 640
SUBLANES = 8
LANES = 128
VMEM_LIMIT = 56 * 1024 * 1024
FAST_PATH_MIN_LOG_DECAY = -80.0


def _dot(a, b):
    return jnp.dot(a, b, preferred_element_type=F32)


def _dot_nt(a, b):
    return lax.dot_general(a, b, (((1,), (1,)), ((), ())), preferred_element_type=F32)


def _dot_tn(a, b):
    return lax.dot_general(a, b, (((0,), (0,)), ((), ())), preferred_element_type=F32)


def _sigmoid(x):
    return 0.5 * jnp.tanh(0.5 * x) + 0.5


def _silu(x):
    u = 0.5 * x
    return u * jnp.tanh(u) + u


def _levels(L):
    out, s = [], 1
    while s < L:
        out.append(s)
        s *= 2
    return out


def _seg_bcast(W, s, L, r):
    n = W.shape[1]
    if s == 1:
        return jnp.where((r & 1) != 0, pltpu.roll(W, 1, 0), W)
    if s == 2:
        m = r & 3
        return jnp.where(m == 0, pltpu.roll(W, L - 1, 0),
                         jnp.where(m == 1, W,
                                   jnp.where(m == 2, pltpu.roll(W, 1, 0), pltpu.roll(W, 2, 0))))
    pieces = [jnp.broadcast_to(W[b * 2 * s + s - 1:b * 2 * s + s, :], (2 * s, n))
              for b in range(L // (2 * s))]
    return pieces[0] if len(pieces) == 1 else jnp.concatenate(pieces, axis=0)


def _level_factors(x, L, r):
    W = x
    factors = []
    for s in _levels(L):
        Tb = _seg_bcast(W, s, L, r)
        sec = (r & s) != 0
        factors.append(jnp.exp(jnp.where(sec, W, Tb - W)))
        W = W + jnp.where(sec, Tb, 0.0)
    return factors


def _prefix8(x):
    sub = lax.broadcasted_iota(jnp.int32, (SUBLANES, 1), 0)
    y = x + jnp.where(sub >= 1, pltpu.roll(x, 1, 0), 0.0)
    y = y + jnp.where(sub >= 2, pltpu.roll(y, 2, 0), 0.0)
    return y + jnp.where(sub >= 4, pltpu.roll(y, 4, 0), 0.0)


def _chunk_cumsum(x, L):
    outs = []
    for c in range(x.shape[0] // L):
        total = None
        for g in range(L // SUBLANES):
            lo = c * L + g * SUBLANES
            p = _prefix8(x[lo:lo + SUBLANES, :])
            if total is not None:
                p = p + total
            outs.append(p)
            total = p[SUBLANES - 1:SUBLANES, :]
    return outs[0] if len(outs) == 1 else jnp.concatenate(outs, axis=0)


def _rows_bcast(x, row_in_chunk, L):
    n = x.shape[1]
    pieces = [jnp.broadcast_to(x[c * L + row_in_chunk:c * L + row_in_chunk + 1, :], (L, n))
              for c in range(x.shape[0] // L)]
    return pieces[0] if len(pieces) == 1 else jnp.concatenate(pieces, axis=0)


def _rms_mod_bf16(x, scale, shift, g):
    ms = jnp.mean(x * x, axis=-1, keepdims=True)
    return ((x * lax.rsqrt(ms + EPS)) * (g * (1.0 + scale)) + shift).astype(BF16)


def _store_valid_rows(y_ref, chunk, cols, y, L, Tv):
    y_ref[chunk * Tv:(chunk + 1) * Tv, cols] = y[:Tv].astype(y_ref.dtype)


def _y_out(B, T, NB, TB, L, Tv):
    rows_per_seq, dtype = (T, BF16) if Tv == L else (T // L * Tv, F32)
    return (pl.BlockSpec((NB * rows_per_seq * TB // T, D_MODEL), lambda i: (i, 0)),
            jax.ShapeDtypeStruct((B * rows_per_seq, D_MODEL), dtype))


def _project_into(dst_ref, h, w_ref, b_ref):
    n = w_ref.shape[1]
    bounds = list(range(0, n - n % D_MODEL, D_MODEL)) or [0]
    for k, lo in enumerate(bounds):
        hi = n if k == len(bounds) - 1 else lo + D_MODEL
        dst_ref[:, lo:hi] = _dot(h, w_ref[:, lo:hi]) + b_ref[:, lo:hi]


def _mod_kernel(c_ref, w_ref, b_ref, o_ref):
    c = c_ref[...]
    a = _silu(c)
    o_ref[...] = _dot(a.astype(BF16), w_ref[...].astype(BF16)) + b_ref[...]


def _modulation(c_all, w_ada, b_ada):
    m = c_all.shape[0]
    n = w_ada.shape[1]
    tn = MOD_COL_TILE
    return pl.pallas_call(
        _mod_kernel,
        grid=(n // tn,),
        in_specs=[pl.BlockSpec((m, D_MODEL), lambda j: (0, 0)),
                  pl.BlockSpec((D_MODEL, tn), lambda j: (0, j)),
                  pl.BlockSpec((1, tn), lambda j: (0, j))],
        out_specs=pl.BlockSpec((m, tn), lambda j: (0, j)),
        out_shape=jax.ShapeDtypeStruct((m, n), F32),
        name="modulation",
    )(c_all, w_ada, b_ada)


def _inproj_kernel(x_ref, scale_ref, shift_ref, g_ref, w_ref, b_ref, o_ref, h_scr):
    @pl.when(pl.program_id(1) == 0)
    def _():
        h_scr[...] = _rms_mod_bf16(x_ref[...], scale_ref[0], shift_ref[0], g_ref[...])

    o_ref[...] = _dot(h_scr[...], w_ref[...]) + b_ref[...]


def _in_projection(x2d, scale3, shift3, norm_g, w, b, rb, col_tile):
    rows = x2d.shape[0]
    mrows = scale3.shape[1]
    nblk = rows // rb
    n_cols = w.shape[1]
    return pl.pallas_call(
        _inproj_kernel,
        grid=(nblk, n_cols // col_tile),
        in_specs=[pl.BlockSpec((rb, D_MODEL), lambda i, j: (i, 0)),
                  pl.BlockSpec((1, mrows, D_MODEL), lambda i, j: (i * scale3.shape[0] // nblk, 0, 0)),
                  pl.BlockSpec((1, mrows, D_MODEL), lambda i, j: (i * shift3.shape[0] // nblk, 0, 0)),
                  pl.BlockSpec((1, D_MODEL), lambda i, j: (0, 0)),
                  pl.BlockSpec((D_MODEL, col_tile), lambda i, j: (0, j)),
                  pl.BlockSpec((1, col_tile), lambda i, j: (0, j))],
        out_specs=pl.BlockSpec((rb, col_tile), lambda i, j: (i, j)),
        out_shape=jax.ShapeDtypeStruct((rows, n_cols), F32),
        scratch_shapes=[pltpu.VMEM((rb, D_MODEL), BF16)],
        compiler_params=pltpu.CompilerParams(vmem_limit_bytes=VMEM_LIMIT),
        name="in_projection",
    )(x2d, scale3, shift3, norm_g, w, b)


class _SideWork:
    def __init__(self, pieces=()):
        self._pieces = list(pieces)

    def run(self, n=1):
        for _ in range(n):
            if self._pieces:
                self._pieces.pop(0)()

    def flush(self):
        self.run(len(self._pieces))


def _fused_projection_schedule(i, proj_a, proj_b, x0_ref, sc0_ref, sh0_ref, ng_ref, w_ref, b_ref, next_h, step):
    n = w_ref.shape[1]

    @pl.when(i == 0)
    def _():
        _project_into(proj_a, _rms_mod_bf16(x0_ref[...], sc0_ref[0], sh0_ref[0], ng_ref[...]), w_ref, b_ref)

    def run(cur, nxt):
        h = []

        def norm_piece():
            h.append(next_h())

        def tile_piece(lo):
            hi = min(lo + MXU_WIDTH, n)

            def piece():
                nxt[:, lo:hi] = _dot(h[0], w_ref[:, lo:hi]) + b_ref[:, lo:hi]
            return piece

        step(cur, _SideWork([norm_piece] + [tile_piece(lo) for lo in range(0, n, MXU_WIDTH)]))

    parity = lax.rem(i, 2)
    pl.when(parity == 0)(lambda: run(proj_a, proj_b))
    pl.when(parity == 1)(lambda: run(proj_b, proj_a))


def _fused_projection_specs(n_blocks, nt, tb, n_cols, next_h_given):
    def nxt(i):
        return jnp.minimum(i + 1, n_blocks - 1)
    vec = (1, 1, D_MODEL)
    specs = [pl.BlockSpec((tb, D_MODEL), lambda i: (0, 0)),
             pl.BlockSpec(vec, lambda i: (0, 0, 0)), pl.BlockSpec(vec, lambda i: (0, 0, 0)),
             pl.BlockSpec((1, D_MODEL), lambda i: (0, 0)),
             pl.BlockSpec((D_MODEL, n_cols), lambda i: (0, 0)),
             pl.BlockSpec((1, n_cols), lambda i: (0, 0))]
    if next_h_given:
        return specs + [pl.BlockSpec((tb, D_MODEL), lambda i: (i, 0))]
    return specs + [pl.BlockSpec((tb, D_MODEL), lambda i: (nxt(i), 0)),
                    pl.BlockSpec(vec, lambda i: (nxt(i) // nt, 0, 0)),
                    pl.BlockSpec(vec, lambda i: (nxt(i) // nt, 0, 0))]


def _hgrn_step(load, t, nt, s0_ref, lbl_ref, g_ref, y_ref, s_ref, st_scr, a_scr,
               *, NB, TB, L, Tv, side_work=None):
    H = HGRN_HEADS
    rb = NB * TB
    spn = TB // L
    nseg = rb // L

    one_step = nt == 1 and spn == 1
    if not one_step:
        @pl.when(t == 0)
        def _():
            for nb in range(NB):
                for h in range(H):
                    st_scr[nb, h] = jnp.zeros((HGRN_D, HGRN_D), F32) if s0_ref is None else s0_ref[nb, h].T

    sw = side_work if side_work is not None else _SideWork()
    sw.run(1)

    lg = lbl_ref[...]
    mx = jnp.max(lg, axis=0, keepdims=True)
    e = jnp.exp(lg - mx)
    lb = e[0:1, :] / jnp.sum(e, axis=0, keepdims=True)
    g_norm = g_ref[...]

    hq = load(0)
    sw.run(2)
    c1 = 0.5 * (1.0 - lb)
    p = c1 * jnp.tanh(0.5 * load(1))
    sw.run(2)
    logf = jnp.log((lb + c1) + p)
    sw.run(2)
    kk = c1 - p
    sw.run(2)
    if Tv < L:
        valid = (lax.broadcasted_iota(jnp.int32, (rb, 1), 0) & (L - 1)) < Tv
        logf = jnp.where(valid, logf, 0.0)
        kk = jnp.where(valid, kk, 0.0)
    G = _chunk_cumsum(logf, L)
    sw.run(3)
    GL = [G[s * L + L - 1:s * L + L, :] for s in range(nseg)]
    fast_ok = jnp.min(functools.reduce(jnp.minimum, GL)) >= FAST_PATH_MIN_LOG_DECAY

    ti = lax.broadcasted_iota(jnp.int32, (L, L), 0)
    ji = lax.broadcasted_iota(jnp.int32, (L, L), 1)

    @pl.when(fast_ok)
    def _():
        d = G - _rows_bcast(G, L // 2 - 1, L)
        qt = (hq * jnp.exp(d)).astype(BF16)
        kt = (kk * jnp.exp(-d)).astype(BF16)
        causal = ji <= ti
        for s in range(nseg):
            rows = slice(s * L, (s + 1) * L)
            for h in range(H):
                hs = slice(h * HGRN_D, (h + 1) * HGRN_D)
                a_scr[s * H + h] = jnp.where(causal, _dot_nt(qt[rows, hs], kt[rows, hs]), 0.0)

    @pl.when(jnp.logical_not(fast_ok))
    def _():
        r = lax.broadcasted_iota(jnp.int32, (L, 1), 0)
        xo = ti ^ ji
        diag_mask = ti == ji
        lvl_masks = [(ji < ti) & (xo >= s) & (xo < 2 * s) for s in _levels(L)]
        for s in range(nseg):
            rows = slice(s * L, (s + 1) * L)
            factors = _level_factors(logf[rows, :], L, r)
            for h in range(H):
                hs = slice(h * HGRN_D, (h + 1) * HGRN_D)
                qh = hq[rows, hs]
                kh = kk[rows, hs]
                A = jnp.where(diag_mask, _dot_nt(qh.astype(BF16), kh.astype(BF16)), 0.0)
                for lvl in range(len(factors)):
                    E = factors[lvl][:, hs]
                    A = A + jnp.where(lvl_masks[lvl],
                                      _dot_nt((qh * E).astype(BF16), (kh * E).astype(BF16)), 0.0)
                a_scr[s * H + h] = A

    qg = (hq * jnp.exp(G)).astype(BF16)
    sw.run(2)
    GLb = GL[0] if nseg == 1 and L == rb else jnp.concatenate(
        [jnp.broadcast_to(gl, (L, D_MODEL)) for gl in GL], axis=0)
    kd = (kk * jnp.exp(GLb - G)).astype(BF16)
    sw.run(2)
    hv = load(2)
    vb = hv.astype(BF16)
    hz = load(3)
    zgate = _silu(hz)
    sw.flush()
    merged = L % LANES == 0
    if one_step:
        for nb in range(NB):
            rows = slice(nb * L, (nb + 1) * L)
            dS = jnp.exp(GL[nb])
            for h in range(H):
                hs = slice(h * HGRN_D, (h + 1) * HGRN_D)
                S = s0_ref[nb, h]
                decay = jnp.broadcast_to(dS[:, hs], (HGRN_D, HGRN_D)).T
                o = _dot(a_scr[nb * H + h].astype(BF16), vb[rows, hs]) + _dot(qg[rows, hs], S.astype(BF16))
                s_ref[nb, h] = S * decay + _dot_tn(kd[rows, hs], vb[rows, hs])
                ms = jnp.mean(o * o, axis=-1, keepdims=True)
                y = o * lax.rsqrt(ms + EPS) * g_norm[:, hs] * zgate[rows, hs]
                _store_valid_rows(y_ref, nb, hs, y, L, Tv)
        return

    for nb in range(NB):
        st = [st_scr[nb, h] for h in range(H)]
        for c in range(spn):
            s = nb * spn + c
            rows = slice(s * L, (s + 1) * L)
            dS = jnp.exp(GL[s])
            for h in range(H):
                hs = slice(h * HGRN_D, (h + 1) * HGRN_D)
                A = a_scr[s * H + h].astype(BF16)
                if merged:
                    vT = hv[rows, hs].T.astype(BF16)
                    o = _dot_nt(jnp.concatenate([qg[rows, hs], A], axis=1),
                                jnp.concatenate([st[h].astype(BF16), vT], axis=1))
                    st[h] = st[h] * dS[:, hs] + _dot(vT, kd[rows, hs])
                else:
                    o = _dot(A, vb[rows, hs]) + _dot_nt(qg[rows, hs], st[h].astype(BF16))
                    st[h] = st[h] * dS[:, hs] + _dot_tn(vb[rows, hs], kd[rows, hs])
                ms = jnp.mean(o * o, axis=-1, keepdims=True)
                y = o * lax.rsqrt(ms + EPS) * g_norm[:, hs] * zgate[rows, hs]
                _store_valid_rows(y_ref, s, hs, y, L, Tv)
        for h in range(H):
            st_scr[nb, h] = st[h]

    @pl.when(t == nt - 1)
    def _():
        for nb in range(NB):
            for h in range(H):
                s_ref[nb, h] = st_scr[nb, h].T


def _pad_sequences(a, NB, Tv, L):
    if Tv == L:
        return a
    zeros = jnp.zeros((L - Tv, a.shape[1]), a.dtype)
    return jnp.concatenate([p for nb in range(NB) for p in (a[nb * Tv:(nb + 1) * Tv, :], zeros)], axis=0)


def _hgrn_kernel(q_ref, f_ref, i_ref, z_ref, s0_ref, lbl_ref, g_ref, y_ref, s_ref, st_scr, a_scr,
                 *, nt, **static):
    cols = (q_ref, f_ref, i_ref, z_ref)
    _hgrn_step(lambda c: _pad_sequences(cols[c][...], static["NB"], static["Tv"], static["L"]),
               lax.rem(pl.program_id(0), nt), nt,
               s0_ref, lbl_ref, g_ref, y_ref, s_ref, st_scr, a_scr, **static)


def _hgrn_fused_kernel(x0_ref, sc0_ref, sh0_ref, ng_ref, w_ref, b_ref, xn_ref, scn_ref, shn_ref,
                       lbl_ref, g_ref, y_ref, s_ref, hn_ref, st_scr, a_scr, proj_a, proj_b,
                       *, nt, **static):
    i = pl.program_id(0)

    def next_h():
        h = _rms_mod_bf16(xn_ref[...], scn_ref[0], shn_ref[0], ng_ref[...])
        hn_ref[...] = h
        return h

    def step(cur, side_work):
        _hgrn_step(lambda c: cur[:, c * D_MODEL:(c + 1) * D_MODEL], lax.rem(i, nt), nt,
                   None, lbl_ref, g_ref, y_ref, s_ref, st_scr, a_scr, side_work=side_work, **static)

    _fused_projection_schedule(i, proj_a, proj_b, x0_ref, sc0_ref, sh0_ref, ng_ref, w_ref, b_ref, next_h, step)


def _hgrn_specs(B, T, TB, NB, L, Tv):
    nt = T // TB
    rb = NB * TB
    s_spec = pl.BlockSpec((NB, HGRN_HEADS, HGRN_D, HGRN_D), lambda i: (i // nt, 0, 0, 0))
    in_tail = [pl.BlockSpec((2, D_MODEL), lambda i: (0, 0)), pl.BlockSpec((1, D_MODEL), lambda i: (0, 0))]
    y_spec, y_shape = _y_out(B, T, NB, TB, L, Tv)
    out_specs = [y_spec, s_spec]
    out_shape = [y_shape, jax.ShapeDtypeStruct((B, HGRN_HEADS, HGRN_D, HGRN_D), F32)]
    scratch = [pltpu.VMEM((NB, HGRN_HEADS, HGRN_D, HGRN_D), F32),
               pltpu.VMEM((rb // L * HGRN_HEADS, L, L), F32)]
    return nt, rb, s_spec, in_tail, out_specs, out_shape, scratch


def _hgrn(proj, s0, lb_logits, norm_g, *, B, T, TB, NB, L, Tv):
    nt, rb, s_spec, in_tail, out_specs, out_shape, scratch = _hgrn_specs(B, T, TB, NB, L, Tv)
    assert NB == 1 or nt == 1
    assert TB == L
    cols = [pl.BlockSpec((NB * Tv, D_MODEL), lambda i, c=c: (i, c)) for c in range(4)]
    return pl.pallas_call(
        functools.partial(_hgrn_kernel, nt=nt, NB=NB, TB=TB, L=L, Tv=Tv),
        grid=(B // NB * nt,),
        in_specs=cols + [s_spec] + in_tail, out_specs=out_specs, out_shape=out_shape, scratch_shapes=scratch,
        compiler_params=pltpu.CompilerParams(vmem_limit_bytes=VMEM_LIMIT),
        name="hgrn2",
    )(proj, proj, proj, proj, s0, lb_logits, norm_g)


def _hgrn_fused(x2d, scale3, shift3, norm_g, w, b, lb_logits, hgrn_norm_g, *, B, T, TB, L):
    nt, rb, _, in_tail, out_specs, out_shape, scratch = _hgrn_specs(B, T, TB, 1, L, L)
    n_blocks = B * nt
    return pl.pallas_call(
        functools.partial(_hgrn_fused_kernel, nt=nt, NB=1, TB=TB, L=L, Tv=L),
        grid=(n_blocks,),
        in_specs=_fused_projection_specs(n_blocks, nt, TB, N_HGRN, False) + in_tail,
        out_specs=out_specs + [pl.BlockSpec((TB, D_MODEL), lambda i: (i, 0))],
        out_shape=out_shape + [jax.ShapeDtypeStruct((B * T, D_MODEL), BF16)],
        scratch_shapes=scratch + [pltpu.VMEM((TB, N_HGRN), F32), pltpu.VMEM((TB, N_HGRN), F32)],
        compiler_params=pltpu.CompilerParams(vmem_limit_bytes=VMEM_LIMIT),
        name="hgrn2_fused",
    )(x2d, scale3, shift3, norm_g, w, b, x2d, scale3, shift3, lb_logits, hgrn_norm_g)


def _mlstm_step(load, t, nt, c0_ref, n0_ref, m0_ref, tail0_ref,
                cw_ref, cb_ref, wq_ref, wk_ref, wkt_ref, wv_ref, ng_ref, skip_ref,
                y_ref, c_ref, n_ref, m_ref, tail_scr, conv_ref,
                *, NB, TB, L, Tv, side_work=None):
    H = MLSTM_HEADS
    rb = NB * TB
    spn = TB // L
    if nt == 1:
        c_in, n_in, m_in, tail_in = c0_ref, n0_ref, m0_ref, tail0_ref
    else:
        c_in, n_in, m_in, tail_in = c_ref, n_ref, m_ref, tail_scr

        @pl.when(t == 0)
        def _():
            for dst, src in ((c_ref, c0_ref), (n_ref, n0_ref), (m_ref, m0_ref), (tail_scr, tail0_ref)):
                dst[...] = jnp.zeros(dst.shape, F32) if src is None else src[...]

    sw = side_work if side_work is not None else _SideWork()
    sw.run(1)

    cw = cw_ref[...]
    ng = ng_ref[...]
    skip = skip_ref[...]
    lane = lax.broadcasted_iota(jnp.int32, (1, LANES), 1)
    ti = lax.broadcasted_iota(jnp.int32, (L, L), 0)
    ji = lax.broadcasted_iota(jnp.int32, (L, L), 1)
    causal = ji <= ti

    mu = load(0)
    xcs = []
    for nb in range(NB):
        u_nb = mu[nb * TB:(nb + 1) * TB, :]
        ext = jnp.concatenate([tail_in[nb], u_nb], axis=0)
        conv = cb_ref[...] + cw[CONV_WIDTH - 1:CONV_WIDTH, :] * u_nb
        for i in range(1, CONV_WIDTH):
            conv = conv + cw[CONV_WIDTH - 1 - i:CONV_WIDTH - i, :] * pltpu.roll(ext, i, 0)[SUBLANES:, :]
            sw.run(1)
        xcs.append(_silu(conv))
        sw.run(1)
        if nt > 1:
            tail_scr[nb] = u_nb[TB - SUBLANES:, :]
    if conv_ref is not None:
        @pl.when(t == nt - 1)
        def _():
            for nb in range(NB):
                conv_ref[nb] = mu[(nb + 1) * TB - SUBLANES:(nb + 1) * TB, :]
    xc = xcs[0] if NB == 1 else jnp.concatenate(xcs, axis=0)
    xcb = xc.astype(BF16)
    mub = mu.astype(BF16)
    mz = load(1)
    zgate = _silu(mz)
    sw.run(2)
    ogate = _sigmoid(load(2))
    sw.run(2)

    gt = load(3)
    lf = jnp.minimum(gt, 0.0) - jnp.log1p(jnp.exp(-jnp.abs(gt)))
    ig = gt
    if Tv < L:
        valid = (lax.broadcasted_iota(jnp.int32, (rb, 1), 0) & (L - 1)) < Tv
        lf = jnp.where(valid, lf, 0.0)
        ig = jnp.where(valid, ig, -jnp.inf)
    bcum = _chunk_cumsum(lf, L)
    sw.run(1)
    comb = jnp.where(lane < H, ig, bcum)
    rpad = -rb % LANES
    if rpad:
        comb = jnp.concatenate([comb, jnp.zeros((rpad, LANES), F32)], axis=0)
    combT = comb.T

    assert spn == 1
    heads = {}

    def project_head(h):
        hs = slice(h * MLSTM_DH, (h + 1) * MLSTM_DH)
        q_all = _dot(xcb[:, hs], wq_ref[h])
        kb_all = _dot(xcb[:, hs], wk_ref[h]).astype(BF16)
        kT_all = _dot_nt(wkt_ref[h], xcb[:, hs])
        vb_all = _dot(mub[:, hs], wv_ref[h]).astype(BF16)
        heads[h] = (hs, q_all, q_all.astype(BF16), kb_all, kT_all, vb_all)
        sw.run(1)

    pairs = [(h, nb) for h in range(H) for nb in range(NB)]
    groups = [pairs] if NB > 1 else [[p] for p in pairs]
    rows_of = lambda nb: slice(nb * L, (nb + 1) * L)

    def run_group(pairs):
        for h in sorted({h for h, _ in pairs}):
            project_head(h)

        gates = []
        for h, nb in pairs:
            rows = rows_of(nb)
            bcol = bcum[rows, H + h:H + h + 1]
            irow = combT[h:h + 1, rows]
            brow = combT[H + h:H + h + 1, rows]
            logD = jnp.where(causal, (bcol - brow) + irow, -jnp.inf)
            m_intra = jnp.max(logD, axis=-1, keepdims=True)
            gates.append((bcol, irow, brow, m_intra, jnp.exp(logD - m_intra)))

        scores = []
        for (h, nb), g in zip(pairs, gates):
            _, _, qb_all, kb_all, _, _ = heads[h]
            rows = rows_of(nb)
            scores.append(_dot_nt(qb_all[rows], kb_all[rows]) * g[4])

        intra = []
        for (h, nb), sc in zip(pairs, scores):
            vb_all = heads[h][5]
            intra.append((jnp.sum(sc, axis=-1, keepdims=True), _dot(sc.astype(BF16), vb_all[rows_of(nb)])))

        writes = []
        for (h, nb), g in zip(pairs, gates):
            _, _, _, kb_all, kT_all, vb_all = heads[h]
            rows = rows_of(nb)
            bcol, irow, brow, m_intra, _ = g
            m_loc = m_intra[L - 1:L, :]
            b_last = bcol[L - 1:L, :]
            wrow = jnp.exp((b_last - brow) + irow - m_loc)
            kw = (kT_all[:, rows] * wrow).astype(BF16)
            if len(pairs) == 1:
                kw = _dot(kw, vb_all[rows])
            ks = _dot(jnp.broadcast_to(wrow, (SUBLANES, L)).astype(BF16), kb_all[rows])[0:1, :]
            writes.append((m_loc, b_last, kw, ks))

        outs = []
        updates = []
        for (h, nb), g, (rs, sv), (m_loc, b_last, kw, ks) in zip(pairs, gates, intra, writes):
            hs, q_all, qb_all, _, _, _ = heads[h]
            rows = rows_of(nb)
            bcol, _, _, m_intra, _ = g
            C = c_in[nb, h]
            nh = n_in[nb, :, hs]
            m_prev = m_in[nb][:, h:h + 1]
            m_inter = bcol + m_prev
            m_t = jnp.maximum(m_inter, m_intra)
            inter = jnp.exp(m_inter - m_t)
            scl = jnp.exp(m_intra - m_t)
            den = inter * jnp.sum(q_all[rows] * nh, axis=-1, keepdims=True) + scl * rs
            rden = 1.0 / jnp.maximum(jnp.abs(den), jnp.exp(-m_t))
            hh = (inter * rden) * _dot(qb_all[rows], C.astype(BF16)) + (scl * rden) * sv
            m_new = m_t[L - 1:L, :]
            dec = jnp.exp(b_last + m_prev - m_new)
            scu = jnp.exp(m_loc - m_new)
            n_ref[nb, :, hs] = dec * nh + scu * ks
            outs.append((hh, m_new))
            if len(pairs) > 1:
                updates.append((dec, scu, kw))
            else:
                c_ref[nb, h] = dec * C + scu * kw

        for (h, nb), (dec, scu, kwT) in zip(pairs, updates):
            c_ref[nb, h] = dec * c_in[nb, h] + scu * _dot(kwT, heads[h][5][rows_of(nb)])

        for (h, nb), (hh, _) in zip(pairs, outs):
            hs = heads[h][0]
            rows = rows_of(nb)
            hm = ogate[rows, hs] * hh
            ms = jnp.mean(hm * hm, axis=-1, keepdims=True)
            y = (hm * lax.rsqrt(ms + EPS) * ng[:, hs] + skip[:, hs] * xc[rows, hs]) * zgate[rows, hs]
            _store_valid_rows(y_ref, nb, hs, y, L, Tv)
        return outs

    outs = [o for grp in groups for o in run_group(grp)]

    for nb in range(NB):
        m_row = m_in[nb]
        for h in range(H):
            m_row = jnp.where(lane == h, outs[h * NB + nb][1], m_row)
        m_ref[nb] = m_row
    sw.flush()


N_MLSTM_WEIGHTS = 8


def _mlstm_kernel(u_ref, z_ref, o_ref, gate_ref, *rest, nt, **static):
    cols = (u_ref, z_ref, o_ref, gate_ref)
    ins, (y_ref, c_ref, n_ref, m_ref, tail_scr) = rest[:4 + N_MLSTM_WEIGHTS], rest[4 + N_MLSTM_WEIGHTS:]
    _mlstm_step(lambda c: _pad_sequences(cols[c][...], static["NB"], static["Tv"], static["L"]),
                lax.rem(pl.program_id(0), nt), nt, *ins,
                y_ref, c_ref, n_ref, m_ref, tail_scr, None, **static)


def _mlstm_fused_kernel(x0_ref, sc0_ref, sh0_ref, ng_ref, w_ref, b_ref, hn_ref, *rest, nt, **static):
    i = pl.program_id(0)
    weights = rest[:N_MLSTM_WEIGHTS]
    y_ref, c_ref, n_ref, m_ref, conv_ref, tail_scr, proj_a, proj_b = rest[N_MLSTM_WEIGHTS:]
    widths = (D_MODEL, D_MODEL, D_MODEL, LANES)

    def step(cur, side_work):
        _mlstm_step(lambda c: cur[:, c * D_MODEL:c * D_MODEL + widths[c]], lax.rem(i, nt), nt,
                    None, None, None, None, *weights,
                    y_ref, c_ref, n_ref, m_ref, tail_scr, conv_ref, side_work=side_work, **static)

    _fused_projection_schedule(i, proj_a, proj_b, x0_ref, sc0_ref, sh0_ref, ng_ref, w_ref, b_ref,
                               lambda: hn_ref[...], step)


def _mlstm_specs(B, T, TB, NB, L, Tv):
    nt = T // TB
    rb = NB * TB

    def full(shape):
        return pl.BlockSpec(shape, lambda i: (0,) * len(shape))

    c_spec = pl.BlockSpec((NB, MLSTM_HEADS, MLSTM_DH, MLSTM_DH), lambda i: (i // nt, 0, 0, 0))
    n_spec = pl.BlockSpec((NB, 1, D_MODEL), lambda i: (i // nt, 0, 0))
    m_spec = pl.BlockSpec((NB, 1, LANES), lambda i: (i // nt, 0, 0))
    t_spec = pl.BlockSpec((NB, SUBLANES, D_MODEL), lambda i: (i // nt, 0, 0))
    head_w = full((MLSTM_HEADS, MLSTM_DH, MLSTM_DH))
    state_specs = [c_spec, n_spec, m_spec, t_spec]
    weight_specs = [full((CONV_WIDTH, D_MODEL)), full((1, D_MODEL)),
                    head_w, head_w, head_w, head_w, full((1, D_MODEL)), full((1, D_MODEL))]
    y_spec, y_shape = _y_out(B, T, NB, TB, L, Tv)
    out_specs = [y_spec, c_spec, n_spec, m_spec]
    out_shape = [y_shape,
                 jax.ShapeDtypeStruct((B, MLSTM_HEADS, MLSTM_DH, MLSTM_DH), F32),
                 jax.ShapeDtypeStruct((B, 1, D_MODEL), F32),
                 jax.ShapeDtypeStruct((B, 1, LANES), F32)]
    scratch = [pltpu.VMEM((NB, SUBLANES, D_MODEL), F32)]
    return nt, rb, state_specs, weight_specs, out_specs, out_shape, scratch, t_spec


def _mlstm(proj, states, weights, *, B, T, TB, NB, L, Tv):
    nt, rb, state_specs, weight_specs, out_specs, out_shape, scratch, _ = _mlstm_specs(B, T, TB, NB, L, Tv)
    assert NB == 1 or nt == 1
    assert TB == L
    cols = [pl.BlockSpec((NB * Tv, D_MODEL), lambda i, c=c: (i, c)) for c in range(3)]
    cols.append(pl.BlockSpec((NB * Tv, LANES), lambda i: (i, GATE_TILE)))
    return pl.pallas_call(
        functools.partial(_mlstm_kernel, nt=nt, NB=NB, TB=TB, L=L, Tv=Tv),
        grid=(B // NB * nt,),
        in_specs=cols + state_specs + weight_specs,
        out_specs=out_specs, out_shape=out_shape, scratch_shapes=scratch,
        compiler_params=pltpu.CompilerParams(vmem_limit_bytes=VMEM_LIMIT),
        name="mlstm",
    )(proj, proj, proj, proj, *states, *weights)


def _mlstm_fused(x2d, scale3, shift3, norm_g, w, b, h_next, weights, *, B, T, TB, L):
    nt, rb, _, weight_specs, out_specs, out_shape, scratch, t_spec = _mlstm_specs(B, T, TB, 1, L, L)
    n_blocks = B * nt
    return pl.pallas_call(
        functools.partial(_mlstm_fused_kernel, nt=nt, NB=1, TB=TB, L=L, Tv=L),
        grid=(n_blocks,),
        in_specs=_fused_projection_specs(n_blocks, nt, TB, N_MLSTM_PAD, True) + weight_specs,
        out_specs=out_specs + [t_spec],
        out_shape=out_shape + [jax.ShapeDtypeStruct((B, SUBLANES, D_MODEL), F32)],
        scratch_shapes=scratch + [pltpu.VMEM((TB, N_MLSTM_PAD), F32), pltpu.VMEM((TB, N_MLSTM_PAD), F32)],
        compiler_params=pltpu.CompilerParams(vmem_limit_bytes=VMEM_LIMIT),
        name="mlstm_fused",
    )(x2d, scale3, shift3, norm_g, w, b, h_next, *weights)


def _out_kernel(yh_ref, ym_ref, x_ref, gate_ref, w_ref, fg_ref, o_ref):
    acc = _dot(jnp.concatenate([yh_ref[...], ym_ref[...]], axis=1).astype(BF16), w_ref[...])
    out = x_ref[...] + gate_ref[0] * acc
    ms = jnp.mean(out * out, axis=-1, keepdims=True)
    o_ref[...] = out * lax.rsqrt(ms + EPS) * fg_ref[...]


def _out_projection(yh, ym, x2d, gate3, w_out, final_g, rb):
    rows = x2d.shape[0]
    mrows = gate3.shape[1]
    nblk = rows // rb
    return pl.pallas_call(
        _out_kernel,
        grid=(nblk,),
        in_specs=[pl.BlockSpec((rb, D_MODEL), lambda i: (i, 0)),
                  pl.BlockSpec((rb, D_MODEL), lambda i: (i, 0)),
                  pl.BlockSpec((rb, D_MODEL), lambda i: (i, 0)),
                  pl.BlockSpec((1, mrows, D_MODEL), lambda i: (i * gate3.shape[0] // nblk, 0, 0)),
                  pl.BlockSpec((2 * D_MODEL, D_MODEL), lambda i: (0, 0)),
                  pl.BlockSpec((1, D_MODEL), lambda i: (0, 0))],
        out_specs=pl.BlockSpec((rb, D_MODEL), lambda i: (i, 0)),
        out_shape=jax.ShapeDtypeStruct((rows, D_MODEL), F32),
        compiler_params=pltpu.CompilerParams(vmem_limit_bytes=VMEM_LIMIT),
        name="out_projection",
    )(yh, ym, x2d, gate3, w_out, final_g)


def _block_diag_heads(w):
    rows = w.reshape(MLSTM_HEADS, MLSTM_DH, QKV_BLOCK)
    tiled = jnp.tile(rows, (1, 1, MLSTM_DH // QKV_BLOCK))
    rg = lax.broadcasted_iota(jnp.int32, (MLSTM_DH, MLSTM_DH), 0) // QKV_BLOCK
    cg = lax.broadcasted_iota(jnp.int32, (MLSTM_DH, MLSTM_DH), 1) // QKV_BLOCK
    return jnp.where(rg == cg, tiled, 0.0).astype(BF16)


def _mlstm_state_operands(c0, n0, m0, conv0):
    B = c0.shape[0]
    tail0 = jnp.pad(conv0, ((0, 0), (SUBLANES - (CONV_WIDTH - 1), 0), (0, 0)))
    m0p = jnp.pad(m0, ((0, 0), (0, LANES - MLSTM_HEADS))).reshape(B, 1, LANES)
    return (c0, n0.reshape(B, 1, D_MODEL), m0p, tail0)


def _unpack_mlstm_state(B, n_new, m_new):
    return n_new.reshape(B, MLSTM_HEADS, MLSTM_DH), m_new.reshape(B, LANES)[:, :MLSTM_HEADS]


def kernel(x_prompt, x_sample, c_prompt, c_sample, state_hgrn, state_mlstm_C, state_mlstm_n, state_mlstm_m, state_mlstm_conv, w_ada, b_ada, norm_g, w_in, b_in, hgrn_lb_logits, hgrn_norm_g, mlstm_conv_w, mlstm_conv_b, mlstm_wq, mlstm_wk, mlstm_wv, mlstm_norm_g, mlstm_skip, w_out, final_g):
    assert w_in.shape == (1, D_MODEL, N_PROJ) and hgrn_lb_logits.shape == (2, D_MODEL)
    Bp, Tp, _ = x_prompt.shape
    Bs, Ts, _ = x_sample.shape
    assert Tp % PROMPT_BLOCK == 0 and Ts <= SAMPLE_ROWS and Ts >= CONV_WIDTH - 1

    mod = _modulation(jnp.concatenate([c_prompt, c_sample], axis=0), w_ada[0], b_ada[0].reshape(1, -1))
    shift, scale, gate = mod[:, :D_MODEL], mod[:, D_MODEL:2 * D_MODEL], mod[:, 2 * D_MODEL:]
    pad_cols = N_MLSTM_PAD - N_MLSTM
    ng = norm_g[0].reshape(1, -1)
    w_h = w_in[0, :, :N_HGRN].astype(BF16)
    w_m = jnp.pad(w_in[0, :, N_HGRN:].astype(BF16), ((0, 0), (0, pad_cols)))
    b_h = b_in[0, :N_HGRN].reshape(1, -1)
    b_m = jnp.pad(b_in[0, N_HGRN:], (0, pad_cols)).reshape(1, -1)
    hg = hgrn_norm_g[0].reshape(1, -1)
    mw = (mlstm_conv_w[0], mlstm_conv_b[0].reshape(1, -1),
          _block_diag_heads(mlstm_wq[0]), _block_diag_heads(mlstm_wk[0] * KEY_SCALE),
          _block_diag_heads(jnp.swapaxes(mlstm_wk[0], -1, -2) * KEY_SCALE), _block_diag_heads(mlstm_wv[0]),
          mlstm_norm_g[0].reshape(1, -1), mlstm_skip[0].reshape(1, -1))
    wo = w_out[0].astype(BF16)
    fg = final_g.reshape(1, -1)

    xp2 = x_prompt.reshape(Bp * Tp, D_MODEL)
    per_seq = lambda a: a.reshape(-1, 1, D_MODEL)
    sc_p, sh_p = per_seq(scale[:Bp]), per_seq(shift[:Bp])
    yh_p, hg_p, h_next = _hgrn_fused(xp2, sc_p, sh_p, ng, w_h, b_h, hgrn_lb_logits, hg,
                                     B=Bp, T=Tp, TB=PROMPT_BLOCK, L=PROMPT_CHUNK)
    ym_p, c_p, n_p, m_p, tail_p = _mlstm_fused(xp2, sc_p, sh_p, ng, w_m, b_m, h_next, mw,
                                               B=Bp, T=Tp, TB=PROMPT_BLOCK, L=PROMPT_BLOCK)
    n_p, m_p = _unpack_mlstm_state(Bp, n_p, m_p)
    conv_p = tail_p[:, SUBLANES - (CONV_WIDTH - 1):]
    yp = _out_projection(yh_p, ym_p, xp2, per_seq(gate[:Bp]), wo, fg, OUT_PROJ_ROWS).reshape(Bp, Tp, D_MODEL)

    per_tok = lambda a: jnp.repeat(a, Ts, axis=0).reshape(1, Bs * Ts, D_MODEL)
    xs_tok = x_sample.reshape(Bs * Ts, D_MODEL)
    sc_s, sh_s = per_tok(scale[Bp:]), per_tok(shift[Bp:])
    proj_h = _in_projection(xs_tok, sc_s, sh_s, ng, w_h, b_h, Bs * Ts, SAMPLE_HGRN_COL_TILE)
    proj_m = _in_projection(xs_tok, sc_s, sh_s, ng, w_m, b_m, Bs * Ts, SAMPLE_MLSTM_COL_TILE)
    yh_s, hg_s = _hgrn(proj_h, state_hgrn[0], hgrn_lb_logits, hg, B=Bs, T=SAMPLE_ROWS, TB=SAMPLE_ROWS,
                       NB=SAMPLE_SEQS_PER_STEP, L=SAMPLE_ROWS, Tv=Ts)
    states_s = _mlstm_state_operands(state_mlstm_C[0], state_mlstm_n[0], state_mlstm_m[0], state_mlstm_conv[0])
    ym_s, c_s, n_s, m_s = _mlstm(proj_m, states_s, mw, B=Bs, T=SAMPLE_ROWS, TB=SAMPLE_ROWS,
                                 NB=SAMPLE_SEQS_PER_STEP, L=SAMPLE_ROWS, Tv=Ts)
    n_s, m_s = _unpack_mlstm_state(Bs, n_s, m_s)
    conv_s = proj_m.reshape(Bs, Ts, N_MLSTM_PAD)[:, Ts - (CONV_WIDTH - 1):, :D_MODEL]
    ys = _out_projection(yh_s, ym_s, xs_tok, per_tok(gate[Bp:]), wo, fg, Bs * Ts).reshape(Bs, Ts, D_MODEL)

    return (yp, ys, hg_p[None], c_p[None], n_p[None], m_p[None], conv_p[None],
            hg_s[None], c_s[None], n_s[None], m_s[None], conv_s[None])
```

```python
import functools

import jax
import jax.numpy as jnp
from jax import lax
from jax.experimental import pallas as pl
from jax.experimental.pallas import tpu as pltpu

F32 = jnp.float32
BF16 = jnp.bfloat16

D_MODEL = 1024
HGRN_HEADS = 8
HGRN_D = 128
MLSTM_HEADS = 4
MLSTM_DH = 256
KEY_SCALE = MLSTM_DH ** -0.5
CONV_WIDTH = 4
QKV_BLOCK = 4
EPS = 1e-6
N_PROJ = 7176
N_HGRN = 4 * D_MODEL
N_MLSTM = N_PROJ - N_HGRN
N_MLSTM_PAD = 3 * D_MODEL + 128
GATE_TILE = 3 * D_MODEL // 128
MXU_WIDTH = 256
PROMPT_CHUNK = 128
PROMPT_BLOCK = 256
SAMPLE_ROWS = 8
SAMPLE_SEQS_PER_STEP = 8
OUT_PROJ_ROWS = 1024
MOD_COL_TILE = 512
SAMPLE_HGRN_COL_TILE = 2048
SAMPLE_MLSTM_COL_TILE = 640
SUBLANES = 8
LANES = 128
VMEM_LIMIT = 56 * 1024 * 1024
FAST_PATH_MIN_LOG_DECAY = -80.0


def _dot(a, b):
    return jnp.dot(a, b, preferred_element_type=F32)


def _dot_nt(a, b):
    return lax.dot_general(a, b, (((1,), (1,)), ((), ())), preferred_element_type=F32)


def _dot_tn(a, b):
    return lax.dot_general(a, b, (((0,), (0,)), ((), ())), preferred_element_type=F32)


def _sigmoid(x):
    return 0.5 * jnp.tanh(0.5 * x) + 0.5


def _silu(x):
    u = 0.5 * x
    return u * jnp.tanh(u) + u


def _levels(L):
    out, s = [], 1
    while s < L:
        out.append(s)
        s *= 2
    return out


def _seg_bcast(W, s, L, r):
    n = W.shape[1]
    if s == 1:
        return jnp.where((r & 1) != 0, pltpu.roll(W, 1, 0), W)
    if s == 2:
        m = r & 3
        return jnp.where(m == 0, pltpu.roll(W, L - 1, 0),
                         jnp.where(m == 1, W,
                                   jnp.where(m == 2, pltpu.roll(W, 1, 0), pltpu.roll(W, 2, 0))))
    pieces = [jnp.broadcast_to(W[b * 2 * s + s - 1:b * 2 * s + s, :], (2 * s, n))
              for b in range(L // (2 * s))]
    return pieces[0] if len(pieces) == 1 else jnp.concatenate(pieces, axis=0)


def _level_factors(x, L, r):
    W = x
    factors = []
    for s in _levels(L):
        Tb = _seg_bcast(W, s, L, r)
        sec = (r & s) != 0
        factors.append(jnp.exp(jnp.where(sec, W, Tb - W)))
        W = W + jnp.where(sec, Tb, 0.0)
    return factors


def _prefix8(x):
    sub = lax.broadcasted_iota(jnp.int32, (SUBLANES, 1), 0)
    y = x + jnp.where(sub >= 1, pltpu.roll(x, 1, 0), 0.0)
    y = y + jnp.where(sub >= 2, pltpu.roll(y, 2, 0), 0.0)
    return y + jnp.where(sub >= 4, pltpu.roll(y, 4, 0), 0.0)


def _chunk_cumsum(x, L):
    outs = []
    for c in range(x.shape[0] // L):
        total = None
        for g in range(L // SUBLANES):
            lo = c * L + g * SUBLANES
            p = _prefix8(x[lo:lo + SUBLANES, :])
            if total is not None:
                p = p + total
            outs.append(p)
            total = p[SUBLANES - 1:SUBLANES, :]
    return outs[0] if len(outs) == 1 else jnp.concatenate(outs, axis=0)


def _rows_bcast(x, row_in_chunk, L):
    n = x.shape[1]
    pieces = [jnp.broadcast_to(x[c * L + row_in_chunk:c * L + row_in_chunk + 1, :], (L, n))
              for c in range(x.shape[0] // L)]
    return pieces[0] if len(pieces) == 1 else jnp.concatenate(pieces, axis=0)


def _rms_mod_bf16(x, scale, shift, g):
    ms = jnp.mean(x * x, axis=-1, keepdims=True)
    return ((x * lax.rsqrt(ms + EPS)) * (g * (1.0 + scale)) + shift).astype(BF16)


def _store_valid_rows(y_ref, chunk, cols, y, L, Tv):
    y_ref[chunk * Tv:(chunk + 1) * Tv, cols] = y[:Tv].astype(y_ref.dtype)


def _y_out(B, T, NB, TB, L, Tv):
    rows_per_seq, dtype = (T, BF16) if Tv == L else (T // L * Tv, F32)
    return (pl.BlockSpec((NB * rows_per_seq * TB // T, D_MODEL), lambda i: (i, 0)),
            jax.ShapeDtypeStruct((B * rows_per_seq, D_MODEL), dtype))


def _project_into(dst_ref, h, w_ref, b_ref):
    n = w_ref.shape[1]
    bounds = list(range(0, n - n % D_MODEL, D_MODEL)) or [0]
    for k, lo in enumerate(bounds):
        hi = n if k == len(bounds) - 1 else lo + D_MODEL
        dst_ref[:, lo:hi] = _dot(h, w_ref[:, lo:hi]) + b_ref[:, lo:hi]


def _mod_kernel(c_ref, w_ref, b_ref, o_ref):
    c = c_ref[...]
    a = _silu(c)
    o_ref[...] = _dot(a.astype(BF16), w_ref[...].astype(BF16)) + b_ref[...]


def _modulation(c_all, w_ada, b_ada):
    m = c_all.shape[0]
    n = w_ada.shape[1]
    tn = MOD_COL_TILE
    return pl.pallas_call(
        _mod_kernel,
        grid=(n // tn,),
        in_specs=[pl.BlockSpec((m, D_MODEL), lambda j: (0, 0)),
                  pl.BlockSpec((D_MODEL, tn), lambda j: (0, j)),
                  pl.BlockSpec((1, tn), lambda j: (0, j))],
        out_specs=pl.BlockSpec((m, tn), lambda j: (0, j)),
        out_shape=jax.ShapeDtypeStruct((m, n), F32),
        name="modulation",
    )(c_all, w_ada, b_ada)


def _inproj_kernel(x_ref, scale_ref, shift_ref, g_ref, w_ref, b_ref, o_ref, h_scr):
    @pl.when(pl.program_id(1) == 0)
    def _():
        h_scr[...] = _rms_mod_bf16(x_ref[...], scale_ref[0], shift_ref[0], g_ref[...])

    o_ref[...] = _dot(h_scr[...], w_ref[...]) + b_ref[...]


def _in_projection(x2d, scale3, shift3, norm_g, w, b, rb, col_tile):
    rows = x2d.shape[0]
    mrows = scale3.shape[1]
    nblk = rows // rb
    n_cols = w.shape[1]
    return pl.pallas_call(
        _inproj_kernel,
        grid=(nblk, n_cols // col_tile),
        in_specs=[pl.BlockSpec((rb, D_MODEL), lambda i, j: (i, 0)),
                  pl.BlockSpec((1, mrows, D_MODEL), lambda i, j: (i * scale3.shape[0] // nblk, 0, 0)),
                  pl.BlockSpec((1, mrows, D_MODEL), lambda i, j: (i * shift3.shape[0] // nblk, 0, 0)),
                  pl.BlockSpec((1, D_MODEL), lambda i, j: (0, 0)),
                  pl.BlockSpec((D_MODEL, col_tile), lambda i, j: (0, j)),
                  pl.BlockSpec((1, col_tile), lambda i, j: (0, j))],
        out_specs=pl.BlockSpec((rb, col_tile), lambda i, j: (i, j)),
        out_shape=jax.ShapeDtypeStruct((rows, n_cols), F32),
        scratch_shapes=[pltpu.VMEM((rb, D_MODEL), BF16)],
        compiler_params=pltpu.CompilerParams(vmem_limit_bytes=VMEM_LIMIT),
        name="in_projection",
    )(x2d, scale3, shift3, norm_g, w, b)


class _SideWork:
    def __init__(self, pieces=()):
        self._pieces = list(pieces)

    def run(self, n=1):
        for _ in range(n):
            if self._pieces:
                self._pieces.pop(0)()

    def flush(self):
        self.run(len(self._pieces))


def _fused_projection_schedule(i, proj_a, proj_b, x0_ref, sc0_ref, sh0_ref, ng_ref, w_ref, b_ref, next_h, step):
    n = w_ref.shape[1]

    @pl.when(i == 0)
    def _():
        _project_into(proj_a, _rms_mod_bf16(x0_ref[...], sc0_ref[0], sh0_ref[0], ng_ref[...]), w_ref, b_ref)

    def run(cur, nxt):
        h = []

        def norm_piece():
            h.append(next_h())

        def tile_piece(lo):
            hi = min(lo + MXU_WIDTH, n)

            def piece():
                nxt[:, lo:hi] = _dot(h[0], w_ref[:, lo:hi]) + b_ref[:, lo:hi]
            return piece

        step(cur, _SideWork([norm_piece] + [tile_piece(lo) for lo in range(0, n, MXU_WIDTH)]))

    parity = lax.rem(i, 2)
    pl.when(parity == 0)(lambda: run(proj_a, proj_b))
    pl.when(parity == 1)(lambda: run(proj_b, proj_a))


def _fused_projection_specs(n_blocks, nt, tb, n_cols, next_h_given):
    def nxt(i):
        return jnp.minimum(i + 1, n_blocks - 1)
    vec = (1, 1, D_MODEL)
    specs = [pl.BlockSpec((tb, D_MODEL), lambda i: (0, 0)),
             pl.BlockSpec(vec, lambda i: (0, 0, 0)), pl.BlockSpec(vec, lambda i: (0, 0, 0)),
             pl.BlockSpec((1, D_MODEL), lambda i: (0, 0)),
             pl.BlockSpec((D_MODEL, n_cols), lambda i: (0, 0)),
             pl.BlockSpec((1, n_cols), lambda i: (0, 0))]
    if next_h_given:
        return specs + [pl.BlockSpec((tb, D_MODEL), lambda i: (i, 0))]
    return specs + [pl.BlockSpec((tb, D_MODEL), lambda i: (nxt(i), 0)),
                    pl.BlockSpec(vec, lambda i: (nxt(i) // nt, 0, 0)),
                    pl.BlockSpec(vec, lambda i: (nxt(i) // nt, 0, 0))]


def _hgrn_step(load, t, nt, s0_ref, lbl_ref, g_ref, y_ref, s_ref, st_scr, a_scr,
               *, NB, TB, L, Tv, side_work=None):
    H = HGRN_HEADS
    rb = NB * TB
    spn = TB // L
    nseg = rb // L

    one_step = nt == 1 and spn == 1
    if not one_step:
        @pl.when(t == 0)
        def _():
            for nb in range(NB):
                for h in range(H):
                    st_scr[nb, h] = jnp.zeros((HGRN_D, HGRN_D), F32) if s0_ref is None else s0_ref[nb, h].T

    sw = side_work if side_work is not None else _SideWork()
    sw.run(1)

    lg = lbl_ref[...]
    mx = jnp.max(lg, axis=0, keepdims=True)
    e = jnp.exp(lg - mx)
    lb = e[0:1, :] / jnp.sum(e, axis=0, keepdims=True)
    g_norm = g_ref[...]

    hq = load(0)
    sw.run(2)
    c1 = 0.5 * (1.0 - lb)
    p = c1 * jnp.tanh(0.5 * load(1))
    sw.run(2)
    logf = jnp.log((lb + c1) + p)
    sw.run(2)
    kk = c1 - p
    sw.run(2)
    if Tv < L:
        valid = (lax.broadcasted_iota(jnp.int32, (rb, 1), 0) & (L - 1)) < Tv
        logf = jnp.where(valid, logf, 0.0)
        kk = jnp.where(valid, kk, 0.0)
    G = _chunk_cumsum(logf, L)
    sw.run(3)
    GL = [G[s * L + L - 1:s * L + L, :] for s in range(nseg)]
    fast_ok = jnp.min(functools.reduce(jnp.minimum, GL)) >= FAST_PATH_MIN_LOG_DECAY

    ti = lax.broadcasted_iota(jnp.int32, (L, L), 0)
    ji = lax.broadcasted_iota(jnp.int32, (L, L), 1)

    @pl.when(fast_ok)
    def _():
        d = G - _rows_bcast(G, L // 2 - 1, L)
        qt = (hq * jnp.exp(d)).astype(BF16)
        kt = (kk * jnp.exp(-d)).astype(BF16)
        causal = ji <= ti
        for s in range(nseg):
            rows = slice(s * L, (s + 1) * L)
            for h in range(H):
                hs = slice(h * HGRN_D, (h + 1) * HGRN_D)
                a_scr[s * H + h] = jnp.where(causal, _dot_nt(qt[rows, hs], kt[rows, hs]), 0.0)

    @pl.when(jnp.logical_not(fast_ok))
    def _():
        r = lax.broadcasted_iota(jnp.int32, (L, 1), 0)
        xo = ti ^ ji
        diag_mask = ti == ji
        lvl_masks = [(ji < ti) & (xo >= s) & (xo < 2 * s) for s in _levels(L)]
        for s in range(nseg):
            rows = slice(s * L, (s + 1) * L)
            factors = _level_factors(logf[rows, :], L, r)
            for h in range(H):
                hs = slice(h * HGRN_D, (h + 1) * HGRN_D)
                qh = hq[rows, hs]
                kh = kk[rows, hs]
                A = jnp.where(diag_mask, _dot_nt(qh.astype(BF16), kh.astype(BF16)), 0.0)
                for lvl in range(len(factors)):
                    E = factors[lvl][:, hs]
                    A = A + jnp.where(lvl_masks[lvl],
                                      _dot_nt((qh * E).astype(BF16), (kh * E).astype(BF16)), 0.0)
                a_scr[s * H + h] = A

    qg = (hq * jnp.exp(G)).astype(BF16)
    sw.run(2)
    GLb = GL[0] if nseg == 1 and L == rb else jnp.concatenate(
        [jnp.broadcast_to(gl, (L, D_MODEL)) for gl in GL], axis=0)
    kd = (kk * jnp.exp(GLb - G)).astype(BF16)
    sw.run(2)
    hv = load(2)
    vb = hv.astype(BF16)
    hz = load(3)
    zgate = _silu(hz)
    sw.flush()
    merged = L % LANES == 0
    if one_step:
        for nb in range(NB):
            rows = slice(nb * L, (nb + 1) * L)
            dS = jnp.exp(GL[nb])
            for h in range(H):
                hs = slice(h * HGRN_D, (h + 1) * HGRN_D)
                S = s0_ref[nb, h]
                decay = jnp.broadcast_to(dS[:, hs], (HGRN_D, HGRN_D)).T
                o = _dot(a_scr[nb * H + h].astype(BF16), vb[rows, hs]) + _dot(qg[rows, hs], S.astype(BF16))
                s_ref[nb, h] = S * decay + _dot_tn(kd[rows, hs], vb[rows, hs])
                ms = jnp.mean(o * o, axis=-1, keepdims=True)
                y = o * lax.rsqrt(ms + EPS) * g_norm[:, hs] * zgate[rows, hs]
                _store_valid_rows(y_ref, nb, hs, y, L, Tv)
        return

    for nb in range(NB):
        st = [st_scr[nb, h] for h in range(H)]
        for c in range(spn):
            s = nb * spn + c
            rows = slice(s * L, (s + 1) * L)
            dS = jnp.exp(GL[s])
            for h in range(H):
                hs = slice(h * HGRN_D, (h + 1) * HGRN_D)
                A = a_scr[s * H + h].astype(BF16)
                if merged:
                    vT = hv[rows, hs].T.astype(BF16)
                    o = _dot_nt(jnp.concatenate([qg[rows, hs], A], axis=1),
                                jnp.concatenate([st[h].astype(BF16), vT], axis=1))
                    st[h] = st[h] * dS[:, hs] + _dot(vT, kd[rows, hs])
                else:
                    o = _dot(A, vb[rows, hs]) + _dot_nt(qg[rows, hs], st[h].astype(BF16))
                    st[h] = st[h] * dS[:, hs] + _dot_tn(vb[rows, hs], kd[rows, hs])
                ms = jnp.mean(o * o, axis=-1, keepdims=True)
                y = o * lax.rsqrt(ms + EPS) * g_norm[:, hs] * zgate[rows, hs]
                _store_valid_rows(y_ref, s, hs, y, L, Tv)
        for h in range(H):
            st_scr[nb, h] = st[h]

    @pl.when(t == nt - 1)
    def _():
        for nb in range(NB):
            for h in range(H):
                s_ref[nb, h] = st_scr[nb, h].T


def _pad_sequences(a, NB, Tv, L):
    if Tv == L:
        return a
    zeros = jnp.zeros((L - Tv, a.shape[1]), a.dtype)
    return jnp.concatenate([p for nb in range(NB) for p in (a[nb * Tv:(nb + 1) * Tv, :], zeros)], axis=0)


def _hgrn_kernel(q_ref, f_ref, i_ref, z_ref, s0_ref, lbl_ref, g_ref, y_ref, s_ref, st_scr, a_scr,
                 *, nt, **static):
    cols = (q_ref, f_ref, i_ref, z_ref)
    _hgrn_step(lambda c: _pad_sequences(cols[c][...], static["NB"], static["Tv"], static["L"]),
               lax.rem(pl.program_id(0), nt), nt,
               s0_ref, lbl_ref, g_ref, y_ref, s_ref, st_scr, a_scr, **static)


def _hgrn_fused_kernel(x0_ref, sc0_ref, sh0_ref, ng_ref, w_ref, b_ref, xn_ref, scn_ref, shn_ref,
                       lbl_ref, g_ref, y_ref, s_ref, hn_ref, st_scr, a_scr, proj_a, proj_b,
                       *, nt, **static):
    i = pl.program_id(0)

    def next_h():
        h = _rms_mod_bf16(xn_ref[...], scn_ref[0], shn_ref[0], ng_ref[...])
        hn_ref[...] = h
        return h

    def step(cur, side_work):
        _hgrn_step(lambda c: cur[:, c * D_MODEL:(c + 1) * D_MODEL], lax.rem(i, nt), nt,
                   None, lbl_ref, g_ref, y_ref, s_ref, st_scr, a_scr, side_work=side_work, **static)

    _fused_projection_schedule(i, proj_a, proj_b, x0_ref, sc0_ref, sh0_ref, ng_ref, w_ref, b_ref, next_h, step)


def _hgrn_specs(B, T, TB, NB, L, Tv):
    nt = T // TB
    rb = NB * TB
    s_spec = pl.BlockSpec((NB, HGRN_HEADS, HGRN_D, HGRN_D), lambda i: (i // nt, 0, 0, 0))
    in_tail = [pl.BlockSpec((2, D_MODEL), lambda i: (0, 0)), pl.BlockSpec((1, D_MODEL), lambda i: (0, 0))]
    y_spec, y_shape = _y_out(B, T, NB, TB, L, Tv)
    out_specs = [y_spec, s_spec]
    out_shape = [y_shape, jax.ShapeDtypeStruct((B, HGRN_HEADS, HGRN_D, HGRN_D), F32)]
    scratch = [pltpu.VMEM((NB, HGRN_HEADS, HGRN_D, HGRN_D), F32),
               pltpu.VMEM((rb // L * HGRN_HEADS, L, L), F32)]
    return nt, rb, s_spec, in_tail, out_specs, out_shape, scratch


def _hgrn(proj, s0, lb_logits, norm_g, *, B, T, TB, NB, L, Tv):
    nt, rb, s_spec, in_tail, out_specs, out_shape, scratch = _hgrn_specs(B, T, TB, NB, L, Tv)
    assert NB == 1 or nt == 1
    assert TB == L
    cols = [pl.BlockSpec((NB * Tv, D_MODEL), lambda i, c=c: (i, c)) for c in range(4)]
    return pl.pallas_call(
        functools.partial(_hgrn_kernel, nt=nt, NB=NB, TB=TB, L=L, Tv=Tv),
        grid=(B // NB * nt,),
        in_specs=cols + [s_spec] + in_tail, out_specs=out_specs, out_shape=out_shape, scratch_shapes=scratch,
        compiler_params=pltpu.CompilerParams(vmem_limit_bytes=VMEM_LIMIT),
        name="hgrn2",
    )(proj, proj, proj, proj, s0, lb_logits, norm_g)


def _hgrn_fused(x2d, scale3, shift3, norm_g, w, b, lb_logits, hgrn_norm_g, *, B, T, TB, L):
    nt, rb, _, in_tail, out_specs, out_shape, scratch = _hgrn_specs(B, T, TB, 1, L, L)
    n_blocks = B * nt
    return pl.pallas_call(
        functools.partial(_hgrn_fused_kernel, nt=nt, NB=1, TB=TB, L=L, Tv=L),
        grid=(n_blocks,),
        in_specs=_fused_projection_specs(n_blocks, nt, TB, N_HGRN, False) + in_tail,
        out_specs=out_specs + [pl.BlockSpec((TB, D_MODEL), lambda i: (i, 0))],
        out_shape=out_shape + [jax.ShapeDtypeStruct((B * T, D_MODEL), BF16)],
        scratch_shapes=scratch + [pltpu.VMEM((TB, N_HGRN), F32), pltpu.VMEM((TB, N_HGRN), F32)],
        compiler_params=pltpu.CompilerParams(vmem_limit_bytes=VMEM_LIMIT),
        name="hgrn2_fused",
    )(x2d, scale3, shift3, norm_g, w, b, x2d, scale3, shift3, lb_logits, hgrn_norm_g)


def _mlstm_step(load, t, nt, c0_ref, n0_ref, m0_ref, tail0_ref,
                cw_ref, cb_ref, wq_ref, wk_ref, wkt_ref, wv_ref, ng_ref, skip_ref,
                y_ref, c_ref, n_ref, m_ref, tail_scr, conv_ref,
                *, NB, TB, L, Tv, side_work=None):
    H = MLSTM_HEADS
    rb = NB * TB
    spn = TB // L
    if nt == 1:
        c_in, n_in, m_in, tail_in = c0_ref, n0_ref, m0_ref, tail0_ref
    else:
        c_in, n_in, m_in, tail_in = c_ref, n_ref, m_ref, tail_scr

        @pl.when(t == 0)
        def _():
            for dst, src in ((c_ref, c0_ref), (n_ref, n0_ref), (m_ref, m0_ref), (tail_scr, tail0_ref)):
                dst[...] = jnp.zeros(dst.shape, F32) if src is None else src[...]

    sw = side_work if side_work is not None else _SideWork()
    sw.run(1)

    cw = cw_ref[...]
    ng = ng_ref[...]
    skip = skip_ref[...]
    lane = lax.broadcasted_iota(jnp.int32, (1, LANES), 1)
    ti = lax.broadcasted_iota(jnp.int32, (L, L), 0)
    ji = lax.broadcasted_iota(jnp.int32, (L, L), 1)
    causal = ji <= ti

    mu = load(0)
    xcs = []
    for nb in range(NB):
        u_nb = mu[nb * TB:(nb + 1) * TB, :]
        ext = jnp.concatenate([tail_in[nb], u_nb], axis=0)
        conv = cb_ref[...] + cw[CONV_WIDTH - 1:CONV_WIDTH, :] * u_nb
        for i in range(1, CONV_WIDTH):
            conv = conv + cw[CONV_WIDTH - 1 - i:CONV_WIDTH - i, :] * pltpu.roll(ext, i, 0)[SUBLANES:, :]
            sw.run(1)
        xcs.append(_silu(conv))
        sw.run(1)
        if nt > 1:
            tail_scr[nb] = u_nb[TB - SUBLANES:, :]
    if conv_ref is not None:
        @pl.when(t == nt - 1)
        def _():
            for nb in range(NB):
                conv_ref[nb] = mu[(nb + 1) * TB - SUBLANES:(nb + 1) * TB, :]
    xc = xcs[0] if NB == 1 else jnp.concatenate(xcs, axis=0)
    xcb = xc.astype(BF16)
    mub = mu.astype(BF16)
    mz = load(1)
    zgate = _silu(mz)
    sw.run(2)
    ogate = _sigmoid(load(2))
    sw.run(2)

    gt = load(3)
    lf = jnp.minimum(gt, 0.0) - jnp.log1p(jnp.exp(-jnp.abs(gt)))
    ig = gt
    if Tv < L:
        valid = (lax.broadcasted_iota(jnp.int32, (rb, 1), 0) & (L - 1)) < Tv
        lf = jnp.where(valid, lf, 0.0)
        ig = jnp.where(valid, ig, -jnp.inf)
    bcum = _chunk_cumsum(lf, L)
    sw.run(1)
    comb = jnp.where(lane < H, ig, bcum)
    rpad = -rb % LANES
    if rpad:
        comb = jnp.concatenate([comb, jnp.zeros((rpad, LANES), F32)], axis=0)
    combT = comb.T

    assert spn == 1
    heads = {}

    def project_head(h):
        hs = slice(h * MLSTM_DH, (h + 1) * MLSTM_DH)
        q_all = _dot(xcb[:, hs], wq_ref[h])
        kb_all = _dot(xcb[:, hs], wk_ref[h]).astype(BF16)
        kT_all = _dot_nt(wkt_ref[h], xcb[:, hs])
        vb_all = _dot(mub[:, hs], wv_ref[h]).astype(BF16)
        heads[h] = (hs, q_all, q_all.astype(BF16), kb_all, kT_all, vb_all)
        sw.run(1)

    pairs = [(h, nb) for h in range(H) for nb in range(NB)]
    groups = [pairs] if NB > 1 else [[p] for p in pairs]
    rows_of = lambda nb: slice(nb * L, (nb + 1) * L)

    def run_group(pairs):
        for h in sorted({h for h, _ in pairs}):
            project_head(h)

        gates = []
        for h, nb in pairs:
            rows = rows_of(nb)
            bcol = bcum[rows, H + h:H + h + 1]
            irow = combT[h:h + 1, rows]
            brow = combT[H + h:H + h + 1, rows]
            logD = jnp.where(causal, (bcol - brow) + irow, -jnp.inf)
            m_intra = jnp.max(logD, axis=-1, keepdims=True)
            gates.append((bcol, irow, brow, m_intra, jnp.exp(logD - m_intra)))

        scores = []
        for (h, nb), g in zip(pairs, gates):
            _, _, qb_all, kb_all, _, _ = heads[h]
            rows = rows_of(nb)
            scores.append(_dot_nt(qb_all[rows], kb_all[rows]) * g[4])

        intra = []
        for (h, nb), sc in zip(pairs, scores):
            vb_all = heads[h][5]
            intra.append((jnp.sum(sc, axis=-1, keepdims=True), _dot(sc.astype(BF16), vb_all[rows_of(nb)])))

        writes = []
        for (h, nb), g in zip(pairs, gates):
            _, _, _, kb_all, kT_all, vb_all = heads[h]
            rows = rows_of(nb)
            bcol, irow, brow, m_intra, _ = g
            m_loc = m_intra[L - 1:L, :]
            b_last = bcol[L - 1:L, :]
            wrow = jnp.exp((b_last - brow) + irow - m_loc)
            kw = (kT_all[:, rows] * wrow).astype(BF16)
            if len(pairs) == 1:
                kw = _dot(kw, vb_all[rows])
            ks = _dot(jnp.broadcast_to(wrow, (SUBLANES, L)).astype(BF16), kb_all[rows])[0:1, :]
            writes.append((m_loc, b_last, kw, ks))

        outs = []
        updates = []
        for (h, nb), g, (rs, sv), (m_loc, b_last, kw, ks) in zip(pairs, gates, intra, writes):
            hs, q_all, qb_all, _, _, _ = heads[h]
            rows = rows_of(nb)
            bcol, _, _, m_intra, _ = g
            C = c_in[nb, h]
            nh = n_in[nb, :, hs]
            m_prev = m_in[nb][:, h:h + 1]
            m_inter = bcol + m_prev
            m_t = jnp.maximum(m_inter, m_intra)
            inter = jnp.exp(m_inter - m_t)
            scl = jnp.exp(m_intra - m_t)
            den = inter * jnp.sum(q_all[rows] * nh, axis=-1, keepdims=True) + scl * rs
            rden = 1.0 / jnp.maximum(jnp.abs(den), jnp.exp(-m_t))
            hh = (inter * rden) * _dot(qb_all[rows], C.astype(BF16)) + (scl * rden) * sv
            m_new = m_t[L - 1:L, :]
            dec = jnp.exp(b_last + m_prev - m_new)
            scu = jnp.exp(m_loc - m_new)
            n_ref[nb, :, hs] = dec * nh + scu * ks
            outs.append((hh, m_new))
            if len(pairs) > 1:
                updates.append((dec, scu, kw))
            else:
                c_ref[nb, h] = dec * C + scu * kw

        for (h, nb), (dec, scu, kwT) in zip(pairs, updates):
            c_ref[nb, h] = dec * c_in[nb, h] + scu * _dot(kwT, heads[h][5][rows_of(nb)])

        for (h, nb), (hh, _) in zip(pairs, outs):
            hs = heads[h][0]
            rows = rows_of(nb)
            hm = ogate[rows, hs] * hh
            ms = jnp.mean(hm * hm, axis=-1, keepdims=True)
            y = (hm * lax.rsqrt(ms + EPS) * ng[:, hs] + skip[:, hs] * xc[rows, hs]) * zgate[rows, hs]
            _store_valid_rows(y_ref, nb, hs, y, L, Tv)
        return outs

    outs = [o for grp in groups for o in run_group(grp)]

    for nb in range(NB):
        m_row = m_in[nb]
        for h in range(H):
            m_row = jnp.where(lane == h, outs[h * NB + nb][1], m_row)
        m_ref[nb] = m_row
    sw.flush()


N_MLSTM_WEIGHTS = 8
STATE_RING = 3


def _mlstm_kernel(u_ref, z_ref, o_ref, gate_ref, *rest, nt, **static):
    cols = (u_ref, z_ref, o_ref, gate_ref)
    ins, (y_ref, c_ref, n_ref, m_ref, tail_scr) = rest[:4 + N_MLSTM_WEIGHTS], rest[4 + N_MLSTM_WEIGHTS:]
    _mlstm_step(lambda c: _pad_sequences(cols[c][...], static["NB"], static["Tv"], static["L"]),
                lax.rem(pl.program_id(0), nt), nt, *ins,
                y_ref, c_ref, n_ref, m_ref, tail_scr, None, **static)


def _mlstm_ring_kernel(u_ref, z_ref, o_ref, gate_ref, c_hbm, *rest, nt, **static):
    cols = (u_ref, z_ref, o_ref, gate_ref)
    ins = rest[:3 + N_MLSTM_WEIGHTS]
    y_ref, c_ref, n_ref, m_ref, tail_scr, ring, sem = rest[3 + N_MLSTM_WEIGHTS:]
    NB = static["NB"]
    i = pl.program_id(0)
    n = pl.num_programs(0)

    def copy(j):
        slot = lax.rem(j, STATE_RING)
        return pltpu.make_async_copy(c_hbm.at[pl.ds(j * NB, NB)], ring.at[slot], sem.at[slot])

    @pl.when(i == 0)
    def _():
        for j in range(STATE_RING - 1):
            copy(j).start()

    @pl.when(i + (STATE_RING - 1) < n)
    def _():
        copy(i + (STATE_RING - 1)).start()

    copy(i).wait()
    _mlstm_step(lambda c: _pad_sequences(cols[c][...], NB, static["Tv"], static["L"]),
                lax.rem(i, nt), nt, ring.at[lax.rem(i, STATE_RING)], *ins,
                y_ref, c_ref, n_ref, m_ref, tail_scr, None, **static)


def _mlstm_fused_kernel(x0_ref, sc0_ref, sh0_ref, ng_ref, w_ref, b_ref, hn_ref, *rest, nt, **static):
    i = pl.program_id(0)
    weights = rest[:N_MLSTM_WEIGHTS]
    y_ref, c_ref, n_ref, m_ref, conv_ref, tail_scr, proj_a, proj_b = rest[N_MLSTM_WEIGHTS:]
    widths = (D_MODEL, D_MODEL, D_MODEL, LANES)

    def step(cur, side_work):
        _mlstm_step(lambda c: cur[:, c * D_MODEL:c * D_MODEL + widths[c]], lax.rem(i, nt), nt,
                    None, None, None, None, *weights,
                    y_ref, c_ref, n_ref, m_ref, tail_scr, conv_ref, side_work=side_work, **static)

    _fused_projection_schedule(i, proj_a, proj_b, x0_ref, sc0_ref, sh0_ref, ng_ref, w_ref, b_ref,
                               lambda: hn_ref[...], step)


def _mlstm_specs(B, T, TB, NB, L, Tv):
    nt = T // TB
    rb = NB * TB

    def full(shape):
        return pl.BlockSpec(shape, lambda i: (0,) * len(shape))

    c_spec = pl.BlockSpec((NB, MLSTM_HEADS, MLSTM_DH, MLSTM_DH), lambda i: (i // nt, 0, 0, 0))
    n_spec = pl.BlockSpec((NB, 1, D_MODEL), lambda i: (i // nt, 0, 0))
    m_spec = pl.BlockSpec((NB, 1, LANES), lambda i: (i // nt, 0, 0))
    t_spec = pl.BlockSpec((NB, SUBLANES, D_MODEL), lambda i: (i // nt, 0, 0))
    head_w = full((MLSTM_HEADS, MLSTM_DH, MLSTM_DH))
    state_specs = [c_spec, n_spec, m_spec, t_spec]
    weight_specs = [full((CONV_WIDTH, D_MODEL)), full((1, D_MODEL)),
                    head_w, head_w, head_w, head_w, full((1, D_MODEL)), full((1, D_MODEL))]
    y_spec, y_shape = _y_out(B, T, NB, TB, L, Tv)
    out_specs = [y_spec, c_spec, n_spec, m_spec]
    out_shape = [y_shape,
                 jax.ShapeDtypeStruct((B, MLSTM_HEADS, MLSTM_DH, MLSTM_DH), F32),
                 jax.ShapeDtypeStruct((B, 1, D_MODEL), F32),
                 jax.ShapeDtypeStruct((B, 1, LANES), F32)]
    scratch = [pltpu.VMEM((NB, SUBLANES, D_MODEL), F32)]
    return nt, rb, state_specs, weight_specs, out_specs, out_shape, scratch, t_spec


def _mlstm(proj, states, weights, *, B, T, TB, NB, L, Tv):
    nt, rb, state_specs, weight_specs, out_specs, out_shape, scratch, _ = _mlstm_specs(B, T, TB, NB, L, Tv)
    assert NB == 1 or nt == 1
    assert TB == L
    cols = [pl.BlockSpec((NB * Tv, D_MODEL), lambda i, c=c: (i, c)) for c in range(3)]
    cols.append(pl.BlockSpec((NB * Tv, LANES), lambda i: (i, GATE_TILE)))
    if nt == 1 and B // NB >= STATE_RING:
        ring = [pltpu.VMEM((STATE_RING, NB, MLSTM_HEADS, MLSTM_DH, MLSTM_DH), F32), pltpu.SemaphoreType.DMA((STATE_RING,))]
        return pl.pallas_call(
            functools.partial(_mlstm_ring_kernel, nt=nt, NB=NB, TB=TB, L=L, Tv=Tv),
            grid=(B // NB,),
            in_specs=cols + [pl.BlockSpec(memory_space=pl.ANY)] + state_specs[1:] + weight_specs,
            out_specs=out_specs, out_shape=out_shape, scratch_shapes=scratch + ring,
            compiler_params=pltpu.CompilerParams(vmem_limit_bytes=VMEM_LIMIT),
            name="mlstm",
        )(proj, proj, proj, proj, *states, *weights)
    return pl.pallas_call(
        functools.partial(_mlstm_kernel, nt=nt, NB=NB, TB=TB, L=L, Tv=Tv),
        grid=(B // NB * nt,),
        in_specs=cols + state_specs + weight_specs,
        out_specs=out_specs, out_shape=out_shape, scratch_shapes=scratch,
        compiler_params=pltpu.CompilerParams(vmem_limit_bytes=VMEM_LIMIT),
        name="mlstm",
    )(proj, proj, proj, proj, *states, *weights)


def _mlstm_fused(x2d, scale3, shift3, norm_g, w, b, h_next, weights, *, B, T, TB, L):
    nt, rb, _, weight_specs, out_specs, out_shape, scratch, t_spec = _mlstm_specs(B, T, TB, 1, L, L)
    n_blocks = B * nt
    return pl.pallas_call(
        functools.partial(_mlstm_fused_kernel, nt=nt, NB=1, TB=TB, L=L, Tv=L),
        grid=(n_blocks,),
        in_specs=_fused_projection_specs(n_blocks, nt, TB, N_MLSTM_PAD, True) + weight_specs,
        out_specs=out_specs + [t_spec],
        out_shape=out_shape + [jax.ShapeDtypeStruct((B, SUBLANES, D_MODEL), F32)],
        scratch_shapes=scratch + [pltpu.VMEM((TB, N_MLSTM_PAD), F32), pltpu.VMEM((TB, N_MLSTM_PAD), F32)],
        compiler_params=pltpu.CompilerParams(vmem_limit_bytes=VMEM_LIMIT),
        name="mlstm_fused",
    )(x2d, scale3, shift3, norm_g, w, b, h_next, *weights)


def _out_kernel(yh_ref, ym_ref, x_ref, gate_ref, w_ref, fg_ref, o_ref):
    acc = _dot(jnp.concatenate([yh_ref[...], ym_ref[...]], axis=1).astype(BF16), w_ref[...])
    out = x_ref[...] + gate_ref[0] * acc
    ms = jnp.mean(out * out, axis=-1, keepdims=True)
    o_ref[...] = out * lax.rsqrt(ms + EPS) * fg_ref[...]


def _out_projection(yh, ym, x2d, gate3, w_out, final_g, rb):
    rows = x2d.shape[0]
    mrows = gate3.shape[1]
    nblk = rows // rb
    return pl.pallas_call(
        _out_kernel,
        grid=(nblk,),
        in_specs=[pl.BlockSpec((rb, D_MODEL), lambda i: (i, 0)),
                  pl.BlockSpec((rb, D_MODEL), lambda i: (i, 0)),
                  pl.BlockSpec((rb, D_MODEL), lambda i: (i, 0)),
                  pl.BlockSpec((1, mrows, D_MODEL), lambda i: (i * gate3.shape[0] // nblk, 0, 0)),
                  pl.BlockSpec((2 * D_MODEL, D_MODEL), lambda i: (0, 0)),
                  pl.BlockSpec((1, D_MODEL), lambda i: (0, 0))],
        out_specs=pl.BlockSpec((rb, D_MODEL), lambda i: (i, 0)),
        out_shape=jax.ShapeDtypeStruct((rows, D_MODEL), F32),
        compiler_params=pltpu.CompilerParams(vmem_limit_bytes=VMEM_LIMIT),
        name="out_projection",
    )(yh, ym, x2d, gate3, w_out, final_g)


def _block_diag_heads(w):
    rows = w.reshape(MLSTM_HEADS, MLSTM_DH, QKV_BLOCK)
    tiled = jnp.tile(rows, (1, 1, MLSTM_DH // QKV_BLOCK))
    rg = lax.broadcasted_iota(jnp.int32, (MLSTM_DH, MLSTM_DH), 0) // QKV_BLOCK
    cg = lax.broadcasted_iota(jnp.int32, (MLSTM_DH, MLSTM_DH), 1) // QKV_BLOCK
    return jnp.where(rg == cg, tiled, 0.0).astype(BF16)


def _mlstm_state_operands(c0, n0, m0, conv0):
    B = c0.shape[0]
    tail0 = jnp.pad(conv0, ((0, 0), (SUBLANES - (CONV_WIDTH - 1), 0), (0, 0)))
    m0p = jnp.pad(m0, ((0, 0), (0, LANES - MLSTM_HEADS))).reshape(B, 1, LANES)
    return (c0, n0.reshape(B, 1, D_MODEL), m0p, tail0)


def _unpack_mlstm_state(B, n_new, m_new):
    return n_new.reshape(B, MLSTM_HEADS, MLSTM_DH), m_new.reshape(B, LANES)[:, :MLSTM_HEADS]


def kernel(x_prompt, x_sample, c_prompt, c_sample, state_hgrn, state_mlstm_C, state_mlstm_n, state_mlstm_m, state_mlstm_conv, w_ada, b_ada, norm_g, w_in, b_in, hgrn_lb_logits, hgrn_norm_g, mlstm_conv_w, mlstm_conv_b, mlstm_wq, mlstm_wk, mlstm_wv, mlstm_norm_g, mlstm_skip, w_out, final_g):
    assert w_in.shape == (1, D_MODEL, N_PROJ) and hgrn_lb_logits.shape == (2, D_MODEL)
    Bp, Tp, _ = x_prompt.shape
    Bs, Ts, _ = x_sample.shape
    assert Tp % PROMPT_BLOCK == 0 and Ts <= SAMPLE_ROWS and Ts >= CONV_WIDTH - 1

    mod = _modulation(jnp.concatenate([c_prompt, c_sample], axis=0), w_ada[0], b_ada[0].reshape(1, -1))
    shift, scale, gate = mod[:, :D_MODEL], mod[:, D_MODEL:2 * D_MODEL], mod[:, 2 * D_MODEL:]
    pad_cols = N_MLSTM_PAD - N_MLSTM
    ng = norm_g[0].reshape(1, -1)
    w_h = w_in[0, :, :N_HGRN].astype(BF16)
    w_m = jnp.pad(w_in[0, :, N_HGRN:].astype(BF16), ((0, 0), (0, pad_cols)))
    b_h = b_in[0, :N_HGRN].reshape(1, -1)
    b_m = jnp.pad(b_in[0, N_HGRN:], (0, pad_cols)).reshape(1, -1)
    hg = hgrn_norm_g[0].reshape(1, -1)
    mw = (mlstm_conv_w[0], mlstm_conv_b[0].reshape(1, -1),
          _block_diag_heads(mlstm_wq[0]), _block_diag_heads(mlstm_wk[0] * KEY_SCALE),
          _block_diag_heads(jnp.swapaxes(mlstm_wk[0], -1, -2) * KEY_SCALE), _block_diag_heads(mlstm_wv[0]),
          mlstm_norm_g[0].reshape(1, -1), mlstm_skip[0].reshape(1, -1))
    wo = w_out[0].astype(BF16)
    fg = final_g.reshape(1, -1)

    xp2 = x_prompt.reshape(Bp * Tp, D_MODEL)
    per_seq = lambda a: a.reshape(-1, 1, D_MODEL)
    sc_p, sh_p = per_seq(scale[:Bp]), per_seq(shift[:Bp])
    yh_p, hg_p, h_next = _hgrn_fused(xp2, sc_p, sh_p, ng, w_h, b_h, hgrn_lb_logits, hg,
                                     B=Bp, T=Tp, TB=PROMPT_BLOCK, L=PROMPT_CHUNK)
    ym_p, c_p, n_p, m_p, tail_p = _mlstm_fused(xp2, sc_p, sh_p, ng, w_m, b_m, h_next, mw,
                                               B=Bp, T=Tp, TB=PROMPT_BLOCK, L=PROMPT_BLOCK)
    n_p, m_p = _unpack_mlstm_state(Bp, n_p, m_p)
    conv_p = tail_p[:, SUBLANES - (CONV_WIDTH - 1):]
    yp = _out_projection(yh_p, ym_p, xp2, per_seq(gate[:Bp]), wo, fg, OUT_PROJ_ROWS).reshape(Bp, Tp, D_MODEL)

    per_tok = lambda a: jnp.repeat(a, Ts, axis=0).reshape(1, Bs * Ts, D_MODEL)
    xs_tok = x_sample.reshape(Bs * Ts, D_MODEL)
    sc_s, sh_s = per_tok(scale[Bp:]), per_tok(shift[Bp:])
    proj_h = _in_projection(xs_tok, sc_s, sh_s, ng, w_h, b_h, Bs * Ts, SAMPLE_HGRN_COL_TILE)
    proj_m = _in_projection(xs_tok, sc_s, sh_s, ng, w_m, b_m, Bs * Ts, SAMPLE_MLSTM_COL_TILE)
    yh_s, hg_s = _hgrn(proj_h, state_hgrn[0], hgrn_lb_logits, hg, B=Bs, T=SAMPLE_ROWS, TB=SAMPLE_ROWS,
                       NB=SAMPLE_SEQS_PER_STEP, L=SAMPLE_ROWS, Tv=Ts)
    states_s = _mlstm_state_operands(state_mlstm_C[0], state_mlstm_n[0], state_mlstm_m[0], state_mlstm_conv[0])
    ym_s, c_s, n_s, m_s = _mlstm(proj_m, states_s, mw, B=Bs, T=SAMPLE_ROWS, TB=SAMPLE_ROWS,
                                 NB=SAMPLE_SEQS_PER_STEP, L=SAMPLE_ROWS, Tv=Ts)
    n_s, m_s = _unpack_mlstm_state(Bs, n_s, m_s)
    conv_s = proj_m.reshape(Bs, Ts, N_MLSTM_PAD)[:, Ts - (CONV_WIDTH - 1):, :D_MODEL]
    ys = _out_projection(yh_s, ym_s, xs_tok, per_tok(gate[Bp:]), wo, fg, Bs * Ts).reshape(Bs, Ts, D_MODEL)

    return (yp, ys, hg_p[None], c_p[None], n_p[None], m_p[None], conv_p[None],
            hg_s[None], c_s[None], n_s[None], m_s[None], conv_s[None])
```
